```python
import jax, jax.numpy as jnp
from jax import lax
import numpy as np

D_MODEL = 1024
BATCH = 8
SEQ = 2048
DEPTH = 2
DEC_BATCH = 32
DEC_SEQ = 1
PAST_LEN = 8192
PAGE_SIZE = 128

N_EVEN = (DEPTH + 1) // 2
N_ODD = DEPTH // 2
H_A = 4
DK_A = 128
DV_A = 128
CHUNK_A = 32
H_B = 8
DH_B = 64
Q_BLOCK = 128
FOX_BIAS_INIT = 5.0
D_MIX_EVEN = H_A * DV_A + H_B * DH_B
EVEN_WIDTHS = (H_A * DK_A, H_A * DK_A, H_A * DV_A, H_A * DV_A, H_B * DH_B, H_B * DH_B, H_B * DH_B, H_B)
EVEN_SPLITS = tuple(int(s) for s in np.cumsum(EVEN_WIDTHS)[:-1])
IN_EVEN = int(sum(EVEN_WIDTHS))
D_INNER = 2 * D_MODEL
HEADDIM_C = 64
H_C = D_INNER // HEADDIM_C
N_GROUPS_C = 4
D_STATE_C = 128
CONV_W = 4
CONV_DIM = D_INNER + 2 * N_GROUPS_C * D_STATE_C
IN_ODD = D_INNER + CONV_DIM + H_C
CHUNK_C = 64
N_EXPERTS = 32
TOP_K = 4
D_FF = D_MODEL
SWIGLU_LIMIT = 7.0
SWIGLU_ALPHA = 1.702
MOE_BLOCK = 128
DN_ALPHA = (2 * DEPTH) ** 0.25
DN_BETA = (8 * DEPTH) ** -0.25
LN_EPS = 1e-5
RMS_EPS = 1e-6

kernel_name = 'hybrid_hgrn2_fox_ssd_moe_step'


def layer_norm(x, g, b):
    xf = x.astype(jnp.float32)
    xc = xf - jnp.mean(xf, axis=-1, keepdims=True)
    var = jnp.mean(xc * xc, axis=-1, keepdims=True)
    return (xc * lax.rsqrt(var + LN_EPS) * g.astype(jnp.float32) + b.astype(jnp.float32)).astype(x.dtype)


def rms_normalize(x):
    return x * lax.rsqrt(jnp.mean(x * x, axis=-1, keepdims=True) + RMS_EPS)


def gather_pages(pool, page_table):
    g = pool[page_table]
    return g.reshape((g.shape[0], g.shape[1] * g.shape[2]) + g.shape[3:])


def hgrn2_chunked(q, k, v, logf, s0):
    bsz, t_len, nh, dk = q.shape
    dv = v.shape[-1]
    c = CHUNK_A if t_len % CHUNK_A == 0 else t_len
    nc = t_len // c
    q, k, logf = (a.reshape(bsz, nc, c, nh, dk) for a in (q, k, logf))
    v = v.reshape(bsz, nc, c, nh, dv)
    b = jnp.cumsum(logf, axis=2)
    g = b[:, :, -1]
    q_d = q * jnp.exp(b)
    k_d = k * jnp.exp(-b)
    k_end = k * jnp.exp(g[:, :, None] - b)
    causal = jnp.tril(jnp.ones((c, c), bool))
    scores = jnp.where(causal, jnp.einsum('bcthk,bcshk->bchts', q_d, k_d), 0.0)
    o_intra = jnp.einsum('bchts,bcshv->bcthv', scores, v)
    u = jnp.einsum('bcshk,bcshv->bchkv', k_end, v)

    def step(s, inp):
        u_c, g_c = inp
        return jnp.exp(g_c)[..., None] * s + u_c, s

    s_last, s_prev = lax.scan(step, s0, (jnp.moveaxis(u, 1, 0), jnp.moveaxis(g, 1, 0)))
    s_prev = jnp.moveaxis(s_prev, 0, 1)
    o_inter = jnp.einsum('bcthk,bchkv->bcthv', q_d, s_prev)
    return (o_intra + o_inter).reshape(bsz, t_len, nh, dv), s_last


def fox_prompt(q, k, v, logf):
    bsz, t_len, nh, dh = q.shape
    scale = dh ** -0.5
    c_t = jnp.swapaxes(jnp.cumsum(logf, axis=1), 1, 2)
    nb = t_len // Q_BLOCK
    q_blocks = jnp.moveaxis(q.reshape(bsz, nb, Q_BLOCK, nh, dh), 1, 0)
    c_blocks = jnp.moveaxis(c_t.reshape(bsz, nh, nb, Q_BLOCK), 2, 0)
    q_pos = jnp.arange(t_len).reshape(nb, Q_BLOCK)
    k_pos = jnp.arange(t_len)

    def block(args):
        q_blk, c_blk, p_blk = args
        logits = jnp.einsum('bqhd,bkhd->bhqk', q_blk, k) * scale + c_blk[..., None] - c_t[:, :, None, :]
        logits = jnp.where(k_pos[None, :] <= p_blk[:, None], logits, -jnp.inf)
        p = jax.nn.softmax(logits, axis=-1)
        return jnp.einsum('bhqk,bkhd->bqhd', p, v)

    o = lax.map(block, (q_blocks, c_blocks, q_pos))
    return jnp.moveaxis(o, 0, 1).reshape(bsz, t_len, nh, dh)


def fox_sample(q, k, v, logf, k_past, v_past, logf_past):
    scale = q.shape[-1] ** -0.5
    n_past = k_past.shape[1]
    t_new = q.shape[1]
    suffix = lax.cumsum(logf_past, axis=1, reverse=True) - logf_past
    cn_t = jnp.swapaxes(jnp.cumsum(logf, axis=1), 1, 2)
    s_past = (jnp.einsum('bqhd,bkhd->bhqk', q, k_past) * scale
              + cn_t[..., None] + jnp.swapaxes(suffix, 1, 2)[:, :, None, :])
    causal = jnp.tril(jnp.ones((t_new, t_new), bool))
    s_new = jnp.where(causal, jnp.einsum('bqhd,bkhd->bhqk', q, k) * scale
                      + cn_t[..., None] - cn_t[:, :, None, :], -jnp.inf)
    p = jax.nn.softmax(jnp.concatenate([s_past, s_new], axis=-1), axis=-1)
    return (jnp.einsum('bhqk,bkhd->bqhd', p[..., :n_past], v_past)
            + jnp.einsum('bhqk,bkhd->bqhd', p[..., n_past:], v))


def even_mixer(x, w_in, lb, a_norm_w, f_bias, w_out, s0, past):
    bsz, t_len, _ = x.shape
    h = (x @ w_in).astype(jnp.float32)
    qa, fa, ia, ga, qb, kb, vb, fb = jnp.split(h, EVEN_SPLITS, axis=-1)
    qa = jax.nn.silu(qa).reshape(bsz, t_len, H_A, DK_A)
    za = fa.reshape(bsz, t_len, H_A, DK_A)
    log_fa = jnp.log(lb + (1.0 - lb) * jax.nn.sigmoid(za))
    ka = (1.0 - lb) * jax.nn.sigmoid(-za)
    oa, s_new = hgrn2_chunked(qa, ka, ia.reshape(bsz, t_len, H_A, DV_A), log_fa, s0)
    oa = rms_normalize(oa) * a_norm_w.astype(jnp.float32) * jax.nn.silu(ga.reshape(bsz, t_len, H_A, DV_A))
    qb = qb.reshape(bsz, t_len, H_B, DH_B)
    kb = kb.reshape(bsz, t_len, H_B, DH_B)
    vb = vb.reshape(bsz, t_len, H_B, DH_B)
    log_fb = jax.nn.log_sigmoid(fb + f_bias.astype(jnp.float32))
    if past is None:
        ob = fox_prompt(qb, kb, vb, log_fb)
    else:
        ob = fox_sample(qb, kb, vb, log_fb, past[0], past[1], past[2])
    mix = jnp.concatenate([oa.reshape(bsz, t_len, -1), ob.reshape(bsz, t_len, -1)], axis=-1).astype(x.dtype)
    return mix @ w_out, kb, vb, log_fb, s_new


def ssd_chunked(x, dt, a, b_in, c_in, h0):
    bsz, t_len, nh, hp = x.shape
    ng, ns = b_in.shape[2], b_in.shape[3]
    r = nh // ng
    c = CHUNK_C if t_len % CHUNK_C == 0 else t_len
    nc = t_len // c
    x = x.reshape(bsz, nc, c, ng, r, hp)
    dt = dt.reshape(bsz, nc, c, ng, r)
    b_in = b_in.reshape(bsz, nc, c, ng, ns)
    c_in = c_in.reshape(bsz, nc, c, ng, ns)
    cum = jnp.cumsum(dt * a.reshape(ng, r), axis=2)
    causal = jnp.tril(jnp.ones((c, c), bool))[:, :, None, None]
    decay = jnp.exp(jnp.where(causal, cum[:, :, :, None] - cum[:, :, None, :], -jnp.inf))
    cb = jnp.einsum('bctgn,bcsgn->bctsg', c_in, b_in)
    w = cb[..., None] * decay * dt[:, :, None]
    y_intra = jnp.einsum('bctsgr,bcsgrp->bctgrp', w, x)
    to_end = jnp.exp(cum[:, :, -1:] - cum) * dt
    states = jnp.einsum('bcsgn,bcsgr,bcsgrp->bcgrpn', b_in, to_end, x)

    def step(hs, inp):
        st, dec = inp
        return jnp.exp(dec)[..., None, None] * hs + st, hs

    h_last, h_prev = lax.scan(step, h0.reshape(bsz, ng, r, hp, ns),
                              (jnp.moveaxis(states, 1, 0), jnp.moveaxis(cum[:, :, -1], 1, 0)))
    h_prev = jnp.moveaxis(h_prev, 0, 1)
    y_inter = jnp.einsum('bctgn,bcgrpn,bctgr->bctgrp', c_in, h_prev, jnp.exp(cum))
    return (y_intra + y_inter).reshape(bsz, t_len, nh, hp), h_last.reshape(bsz, nh, hp, ns)


def mamba2_mixer(x, conv_state, ssm_state, w_in, conv_w, conv_b, dt_bias, a_log, d_skip, norm_w, w_out):
    bsz, t_len, _ = x.shape
    h = (x @ w_in).astype(jnp.float32)
    z, xbc, dt = jnp.split(h, (D_INNER, D_INNER + CONV_DIM), axis=-1)
    ext = jnp.concatenate([conv_state.astype(jnp.float32), xbc], axis=1)
    cw = conv_w.astype(jnp.float32)
    conv = conv_b.astype(jnp.float32) + sum(ext[:, j:j + t_len] * cw[j] for j in range(CONV_W))
    new_conv = ext[:, t_len:]
    xbc = jax.nn.silu(conv)
    xs, b_in, c_in = jnp.split(xbc, (D_INNER, D_INNER + N_GROUPS_C * D_STATE_C), axis=-1)
    xs = xs.reshape(bsz, t_len, H_C, HEADDIM_C)
    b_in = b_in.reshape(bsz, t_len, N_GROUPS_C, D_STATE_C)
    c_in = c_in.reshape(bsz, t_len, N_GROUPS_C, D_STATE_C)
    dt = jax.nn.softplus(dt + dt_bias.astype(jnp.float32))
    a = -jnp.exp(a_log.astype(jnp.float32))
    y, h_new = ssd_chunked(xs, dt, a, b_in, c_in, ssm_state.astype(jnp.float32))
    y = y + d_skip.astype(jnp.float32)[:, None] * xs
    y = y.reshape(bsz, t_len, D_INNER) * jax.nn.silu(z)
    y = rms_normalize(y.reshape(bsz, t_len, N_GROUPS_C, D_INNER // N_GROUPS_C)).reshape(bsz, t_len, D_INNER)
    y = y * norm_w.astype(jnp.float32)
    return y.astype(x.dtype) @ w_out, h_new, new_conv


def moe(x, router_w, router_b, w1, b1, w2, b2):
    bsz, t_len, d = x.shape
    xt = x.reshape(-1, d)
    n_tok = xt.shape[0]
    logits = (xt @ router_w + router_b).astype(jnp.float32)
    top_v, top_i = lax.top_k(logits, TOP_K)
    gates = jax.nn.softmax(top_v, axis=-1).astype(x.dtype)
    n_assign = n_tok * TOP_K
    e_flat = top_i.reshape(-1)
    order = jnp.argsort(e_flat)
    e_sorted = e_flat[order]
    tok_sorted = order // TOP_K
    counts = jnp.zeros((N_EXPERTS,), jnp.int32).at[e_flat].add(1)
    padded = (counts + MOE_BLOCK - 1) // MOE_BLOCK * MOE_BLOCK
    ends = jnp.cumsum(padded)
    start = jnp.cumsum(counts) - counts
    pos = ends[e_sorted] - padded[e_sorted] + jnp.arange(n_assign) - start[e_sorted]
    n_blocks = -(-n_assign // MOE_BLOCK) + N_EXPERTS
    row_tok = jnp.full((n_blocks * MOE_BLOCK,), n_tok, jnp.int32).at[pos].set(tok_sorted)
    block_expert = jnp.minimum(jnp.searchsorted(ends, jnp.arange(n_blocks) * MOE_BLOCK, side='right'), N_EXPERTS - 1)
    x_rows = jnp.concatenate([xt, jnp.zeros((1, d), xt.dtype)], axis=0)[row_tok].reshape(n_blocks, MOE_BLOCK, d)

    def expert_block(args):
        xb, e = args
        hb = xb @ w1[e] + b1[e]
        gate = jnp.minimum(hb[:, 0::2], SWIGLU_LIMIT)
        up = jnp.clip(hb[:, 1::2], -SWIGLU_LIMIT, SWIGLU_LIMIT)
        act = (up + 1.0) * gate * jax.nn.sigmoid(SWIGLU_ALPHA * gate)
        return act @ w2[e] + b2[e]

    y_rows = lax.map(expert_block, (x_rows, block_expert)).reshape(-1, d)
    y_assign = jnp.zeros((n_assign, d), y_rows.dtype).at[order].set(y_rows[pos])
    y = jnp.einsum('tkd,tk->td', y_assign.reshape(n_tok, TOP_K, d), gates)
    return y.reshape(bsz, t_len, d)


def setup_inputs(seed: int = 0) -> dict:
    key = jax.random.key(seed)
    ks = jax.random.split(key, 40)
    f32 = jnp.float32

    def nrm(k, shape, scale):
        return jax.random.normal(k, shape, f32) * scale

    n_pages = PAST_LEN // PAGE_SIZE
    n_used = DEC_BATCH * n_pages
    n_pool = n_used + max(1, n_used // 4)
    page_table = jax.random.permutation(ks[0], n_pool)[:n_used].reshape(DEC_BATCH, n_pages).astype(jnp.int32)
    dt0 = jnp.exp(jax.random.uniform(ks[16], (N_ODD, H_C), f32, np.log(1e-3), np.log(1e-1)))
    return {
        'x_prompt': nrm(ks[1], (BATCH, SEQ, D_MODEL), 1.0),
        'x_sample': nrm(ks[2], (DEC_BATCH, DEC_SEQ, D_MODEL), 1.0),
        'cache_k': nrm(ks[3], (N_EVEN, n_pool, PAGE_SIZE, H_B, DH_B), 1.0),
        'cache_v': nrm(ks[4], (N_EVEN, n_pool, PAGE_SIZE, H_B, DH_B), 1.0),
        'cache_logf': jax.nn.log_sigmoid(nrm(ks[5], (N_EVEN, n_pool, PAGE_SIZE, H_B), 0.5) + 8.0),
        'page_table': page_table,
        'state_hgrn': nrm(ks[6], (N_EVEN, DEC_BATCH, H_A, DK_A, DV_A), 0.5),
        'state_ssm': nrm(ks[7], (N_ODD, DEC_BATCH, H_C, HEADDIM_C, D_STATE_C), 0.1),
        'state_conv': nrm(ks[8], (N_ODD, DEC_BATCH, CONV_W - 1, CONV_DIM), 1.0),
        'w_in_even': nrm(ks[9], (N_EVEN, D_MODEL, IN_EVEN), D_MODEL ** -0.5),
        'hgrn_lower_bound': nrm(ks[10], (N_EVEN + 1, H_A * DK_A), 0.1),
        'hgrn_norm_w': 1.0 + nrm(ks[11], (N_EVEN, DV_A), 0.01),
        'fox_f_bias': FOX_BIAS_INIT + nrm(ks[12], (N_EVEN, H_B), 0.5),
        'w_out_even': nrm(ks[13], (N_EVEN, D_MIX_EVEN, D_MODEL), D_MIX_EVEN ** -0.5 * DN_BETA),
        'w_in_odd': nrm(ks[14], (N_ODD, D_MODEL, IN_ODD), D_MODEL ** -0.5),
        'conv_w': nrm(ks[15], (N_ODD, CONV_W, CONV_DIM), CONV_W ** -0.5),
        'conv_b': nrm(ks[17], (N_ODD, CONV_DIM), 0.01),
        'dt_bias': dt0 + jnp.log(-jnp.expm1(-dt0)),
        'a_log': jnp.log(jax.random.uniform(ks[18], (N_ODD, H_C), f32, 1.0, 16.0)),
        'd_skip': 1.0 + nrm(ks[19], (N_ODD, H_C), 0.01),
        'ssm_norm_w': 1.0 + nrm(ks[20], (N_ODD, D_INNER), 0.01),
        'w_out_odd': nrm(ks[21], (N_ODD, D_INNER, D_MODEL), D_INNER ** -0.5 * DN_BETA),
        'ln1_g': 1.0 + nrm(ks[22], (DEPTH, D_MODEL), 0.01),
        'ln1_b': nrm(ks[23], (DEPTH, D_MODEL), 0.01),
        'ln2_g': 1.0 + nrm(ks[24], (DEPTH, D_MODEL), 0.01),
        'ln2_b': nrm(ks[25], (DEPTH, D_MODEL), 0.01),
        'router_w': nrm(ks[26], (DEPTH, D_MODEL, N_EXPERTS), D_MODEL ** -0.5),
        'router_b': nrm(ks[27], (DEPTH, N_EXPERTS), 0.01),
        'exp_w1': nrm(ks[28], (DEPTH, N_EXPERTS, D_MODEL, 2 * D_FF), D_MODEL ** -0.5),
        'exp_b1': nrm(ks[29], (DEPTH, N_EXPERTS, 2 * D_FF), 0.01),
        'exp_w2': nrm(ks[30], (DEPTH, N_EXPERTS, D_FF, D_MODEL), D_FF ** -0.5 * DN_BETA),
        'exp_b2': nrm(ks[31], (DEPTH, N_EXPERTS, D_MODEL), 0.01),
    }


def reference(x_prompt, x_sample, cache_k, cache_v, cache_logf, page_table, state_hgrn, state_ssm, state_conv,
              w_in_even, hgrn_lower_bound, hgrn_norm_w, fox_f_bias, w_out_even,
              w_in_odd, conv_w, conv_b, dt_bias, a_log, d_skip, ssm_norm_w, w_out_odd,
              ln1_g, ln1_b, ln2_g, ln2_b, router_w, router_b, exp_w1, exp_b1, exp_w2, exp_b2):
    f32 = jnp.float32
    lb_all = jnp.cumsum(jax.nn.softmax(hgrn_lower_bound.astype(f32), axis=0), axis=0)
    xp, xs = x_prompt, x_sample
    bp, bd = xp.shape[0], xs.shape[0]
    kp_l, ks_l, vp_l, vs_l, lfp_l, lfs_l, hgp_l, hgs_l = [], [], [], [], [], [], [], []
    ssp_l, sss_l, cvp_l, cvs_l = [], [], [], []
    for l in range(DEPTH):
        i = l // 2
        if l % 2 == 0:
            lb = lb_all[i].reshape(H_A, DK_A)
            s0p = jnp.zeros((bp, H_A, DK_A, DV_A), f32)
            mp, kp, vp, lfp, hgp = even_mixer(xp, w_in_even[i], lb, hgrn_norm_w[i], fox_f_bias[i], w_out_even[i], s0p, None)
            past = (gather_pages(cache_k[i], page_table).astype(f32),
                    gather_pages(cache_v[i], page_table).astype(f32),
                    gather_pages(cache_logf[i], page_table).astype(f32))
            ms, k_s, v_s, lfs, hgs = even_mixer(xs, w_in_even[i], lb, hgrn_norm_w[i], fox_f_bias[i], w_out_even[i],
                                                state_hgrn[i].astype(f32), past)
            kp_l.append(kp.astype(xp.dtype)); ks_l.append(k_s.astype(xs.dtype))
            vp_l.append(vp.astype(xp.dtype)); vs_l.append(v_s.astype(xs.dtype))
            lfp_l.append(lfp.astype(xp.dtype)); lfs_l.append(lfs.astype(xs.dtype))
            hgp_l.append(hgp.astype(xp.dtype)); hgs_l.append(hgs.astype(xs.dtype))
        else:
            mp, ssp, cvp = mamba2_mixer(xp, jnp.zeros((bp, CONV_W - 1, CONV_DIM), f32),
                                        jnp.zeros((bp, H_C, HEADDIM_C, D_STATE_C), f32),
                                        w_in_odd[i], conv_w[i], conv_b[i], dt_bias[i], a_log[i], d_skip[i],
                                        ssm_norm_w[i], w_out_odd[i])
            ms, sss, cvs = mamba2_mixer(xs, state_conv[i], state_ssm[i],
                                        w_in_odd[i], conv_w[i], conv_b[i], dt_bias[i], a_log[i], d_skip[i],
                                        ssm_norm_w[i], w_out_odd[i])
            ssp_l.append(ssp.astype(xp.dtype)); sss_l.append(sss.astype(xs.dtype))
            cvp_l.append(cvp.astype(xp.dtype)); cvs_l.append(cvs.astype(xs.dtype))
        xp = layer_norm(DN_ALPHA * xp + mp, ln1_g[l], ln1_b[l])
        xs = layer_norm(DN_ALPHA * xs + ms, ln1_g[l], ln1_b[l])
        xp = layer_norm(DN_ALPHA * xp + moe(xp, router_w[l], router_b[l], exp_w1[l], exp_b1[l], exp_w2[l], exp_b2[l]),
                        ln2_g[l], ln2_b[l])
        xs = layer_norm(DN_ALPHA * xs + moe(xs, router_w[l], router_b[l], exp_w1[l], exp_b1[l], exp_w2[l], exp_b2[l]),
                        ln2_g[l], ln2_b[l])
    y_prompt, y_sample = xp, xs
    k_prompt, k_sample = jnp.stack(kp_l), jnp.stack(ks_l)
    v_prompt, v_sample = jnp.stack(vp_l), jnp.stack(vs_l)
    logf_prompt, logf_sample = jnp.stack(lfp_l), jnp.stack(lfs_l)
    hgrn_prompt, hgrn_sample = jnp.stack(hgp_l), jnp.stack(hgs_l)
    ssm_prompt, ssm_sample = jnp.stack(ssp_l), jnp.stack(sss_l)
    conv_prompt, conv_sample = jnp.stack(cvp_l), jnp.stack(cvs_l)
    return (y_prompt, y_sample, k_prompt, k_sample, v_prompt, v_sample, logf_prompt, logf_sample,
            hgrn_prompt, hgrn_sample, ssm_prompt, ssm_sample, conv_prompt, conv_sample)
```

```python
import functools

import jax
import jax.numpy as jnp
import numpy as np
from jax import lax
from jax.experimental import pallas as pl
from jax.experimental.pallas import tpu as pltpu

F32 = jnp.float32
BF16 = jnp.bfloat16

D_MODEL = 1024
DEPTH = 2
PAGE_SIZE = 128
H_A, DK_A, DV_A, CHUNK_A = 4, 128, 128, 32
H_B, DH_B, Q_BLOCK = 8, 64, 128
EVEN_WIDTHS = (H_A * DK_A, H_A * DK_A, H_A * DV_A, H_A * DV_A, H_B * DH_B, H_B * DH_B, H_B * DH_B, H_B)
EVEN_SPLITS = tuple(int(s) for s in np.cumsum(EVEN_WIDTHS)[:-1])
D_INNER = 2 * D_MODEL
HEADDIM_C = 64
H_C = D_INNER // HEADDIM_C
N_GROUPS_C = 4
D_STATE_C = 128
CONV_W = 4
CONV_DIM = D_INNER + 2 * N_GROUPS_C * D_STATE_C
CHUNK_C = 64
N_EXPERTS = 32
TOP_K = 4
D_FF = D_MODEL
SWIGLU_LIMIT = 7.0
SWIGLU_ALPHA = 1.702
DN_ALPHA = (2 * DEPTH) ** 0.25
LN_EPS = 1e-5
RMS_EPS = 1e-6

V7X_LANES = 128
VMEM_LIMIT = 56 * 1024 * 1024


def _mm_kernel(x_ref, w_ref, o_ref):
    o_ref[...] = jnp.dot(x_ref[...].astype(BF16), w_ref[...].astype(BF16), preferred_element_type=F32)


def _matmul(x, w, tm=512, tn=512):
    m, k = x.shape
    n = w.shape[1]
    n_pad = -(-n // V7X_LANES) * V7X_LANES
    if n_pad != n:
        w = jnp.pad(w, ((0, 0), (0, n_pad - n)))
    tm = min(tm, m)
    tn = tn if n_pad % tn == 0 else (256 if n_pad % 256 == 0 else V7X_LANES)
    out = pl.pallas_call(
        _mm_kernel,
        grid=(pl.cdiv(m, tm), n_pad // tn),
        in_specs=[pl.BlockSpec((tm, k), lambda i, j: (i, 0)),
                  pl.BlockSpec((k, tn), lambda i, j: (0, j))],
        out_specs=pl.BlockSpec((tm, tn), lambda i, j: (i, j)),
        out_shape=jax.ShapeDtypeStruct((m, n_pad), F32),
        compiler_params=pltpu.CompilerParams(dimension_semantics=("parallel", "parallel"),
                                             vmem_limit_bytes=VMEM_LIMIT),
        name="dense_matmul",
    )(x, w)
    return out[:, :n] if n_pad != n else out


def _proj(x, w, n_main):
    lead = x.shape[:-1]
    x2 = x.reshape(-1, x.shape[-1])
    main = _matmul(x2, w[:, :n_main])
    if n_main == w.shape[1]:
        return main.reshape(lead + (n_main,))
    tail = _matmul(x2, w[:, n_main:])
    return jnp.concatenate([main, tail], axis=-1).reshape(lead + (w.shape[1],))


def _ffn_kernel(te_ref, tv_ref, x_ref, w1g_ref, w1u_ref, b1g_ref, b1u_ref, w2_ref, b2_ref, o_ref):
    i = pl.program_id(0)

    @pl.when(tv_ref[i] != 0)
    def _():
        x = x_ref[...].astype(BF16)
        hg = jnp.dot(x, w1g_ref[0], preferred_element_type=F32) + b1g_ref[0]
        hu = jnp.dot(x, w1u_ref[0], preferred_element_type=F32) + b1u_ref[0]
        gate = jnp.minimum(hg, SWIGLU_LIMIT)
        up = jnp.clip(hu, -SWIGLU_LIMIT, SWIGLU_LIMIT)
        act = (up + 1.0) * gate * jax.nn.sigmoid(SWIGLU_ALPHA * gate)
        o_ref[...] = jnp.dot(act.astype(BF16), w2_ref[0], preferred_element_type=F32) + b2_ref[0]

    @pl.when(tv_ref[i] == 0)
    def _():
        o_ref[...] = jnp.zeros_like(o_ref)


def _expert_ffn(x_rows, tile_expert, tile_valid, w1g, w1u, b1g, b1u, w2, b2, tm):
    rows, d = x_rows.shape
    n_tiles = rows // tm
    wmap = lambda i, te, tv: (te[i], 0, 0)
    return pl.pallas_call(
        _ffn_kernel,
        grid_spec=pltpu.PrefetchScalarGridSpec(
            num_scalar_prefetch=2,
            grid=(n_tiles,),
            in_specs=[pl.BlockSpec((tm, d), lambda i, te, tv: (i, 0)),
                      pl.BlockSpec((1, d, D_FF), wmap),
                      pl.BlockSpec((1, d, D_FF), wmap),
                      pl.BlockSpec((1, 1, D_FF), wmap),
                      pl.BlockSpec((1, 1, D_FF), wmap),
                      pl.BlockSpec((1, D_FF, d), wmap),
                      pl.BlockSpec((1, 1, d), wmap)],
            out_specs=pl.BlockSpec((tm, d), lambda i, te, tv: (i, 0)),
        ),
        out_shape=jax.ShapeDtypeStruct((rows, d), F32),
        compiler_params=pltpu.CompilerParams(dimension_semantics=("arbitrary",),
                                             vmem_limit_bytes=VMEM_LIMIT),
        name="moe_expert_ffn",
    )(tile_expert, tile_valid, x_rows, w1g, w1u, b1g, b1u, w2, b2)


def _moe(x, router_w, router_b, w1g, w1u, b1g, b1u, w2, b2, tm):
    bsz, t_len, d = x.shape
    xt = x.reshape(-1, d)
    n_tok = xt.shape[0]
    logits = (_matmul(xt, router_w) + router_b).astype(F32)
    top_v, top_i = lax.top_k(logits, TOP_K)
    gates = jax.nn.softmax(top_v, axis=-1)
    n_assign = n_tok * TOP_K
    e_flat = top_i.reshape(-1)
    order = jnp.argsort(e_flat)
    e_sorted = e_flat[order]
    tok_sorted = order // TOP_K
    counts = jnp.zeros((N_EXPERTS,), jnp.int32).at[e_flat].add(1)
    padded = (counts + tm - 1) // tm * tm
    ends = jnp.cumsum(padded)
    start = jnp.cumsum(counts) - counts
    pos = ends[e_sorted] - padded[e_sorted] + jnp.arange(n_assign) - start[e_sorted]
    n_tiles = -(-n_assign // tm) + N_EXPERTS
    row_tok = jnp.full((n_tiles * tm,), n_tok, jnp.int32).at[pos].set(tok_sorted)
    tile_start = jnp.arange(n_tiles) * tm
    tile_expert = jnp.minimum(jnp.searchsorted(ends, tile_start, side='right'), N_EXPERTS - 1).astype(jnp.int32)
    tile_valid = (tile_start < ends[-1]).astype(jnp.int32)
    x_rows = jnp.concatenate([xt, jnp.zeros((1, d), xt.dtype)], axis=0)[row_tok]
    y_rows = _expert_ffn(x_rows, tile_expert, tile_valid, w1g, w1u, b1g, b1u, w2, b2, tm)
    y_assign = jnp.zeros((n_assign, d), y_rows.dtype).at[order].set(y_rows[pos])
    y = jnp.einsum('tkd,tk->td', y_assign.reshape(n_tok, TOP_K, d), gates)
    return y.reshape(bsz, t_len, d)


def _layer_norm(x, g, b):
    xc = x - jnp.mean(x, axis=-1, keepdims=True)
    var = jnp.mean(xc * xc, axis=-1, keepdims=True)
    return xc * lax.rsqrt(var + LN_EPS) * g + b


def _rms_normalize(x):
    return x * lax.rsqrt(jnp.mean(x * x, axis=-1, keepdims=True) + RMS_EPS)


def _gather_pages(pool, page_table):
    g = pool[page_table]
    return g.reshape((g.shape[0], g.shape[1] * g.shape[2]) + g.shape[3:])


def _hgrn2_chunked(q, k, v, logf, s0):
    bsz, t_len, nh, dk = q.shape
    dv = v.shape[-1]
    c = CHUNK_A if t_len % CHUNK_A == 0 else t_len
    nc = t_len // c
    q, k, logf = (a.reshape(bsz, nc, c, nh, dk) for a in (q, k, logf))
    v = v.reshape(bsz, nc, c, nh, dv)
    b = jnp.cumsum(logf, axis=2)
    g = b[:, :, -1]
    q_d = q * jnp.exp(b)
    k_d = k * jnp.exp(-b)
    k_end = k * jnp.exp(g[:, :, None] - b)
    causal = jnp.tril(jnp.ones((c, c), bool))
    scores = jnp.where(causal, jnp.einsum('bcthk,bcshk->bchts', q_d, k_d), 0.0)
    o_intra = jnp.einsum('bchts,bcshv->bcthv', scores, v)
    u = jnp.einsum('bcshk,bcshv->bchkv', k_end, v)

    def step(s, inp):
        u_c, g_c = inp
        return jnp.exp(g_c)[..., None] * s + u_c, s

    s_last, s_prev = lax.scan(step, s0, (jnp.moveaxis(u, 1, 0), jnp.moveaxis(g, 1, 0)))
    s_prev = jnp.moveaxis(s_prev, 0, 1)
    o_inter = jnp.einsum('bcthk,bchkv->bcthv', q_d, s_prev)
    return (o_intra + o_inter).reshape(bsz, t_len, nh, dv), s_last


def _fox_prompt(q, k, v, logf):
    bsz, t_len, nh, dh = q.shape
    scale = dh ** -0.5
    c_t = jnp.swapaxes(jnp.cumsum(logf, axis=1), 1, 2)
    nb = t_len // Q_BLOCK
    q_blocks = jnp.moveaxis(q.reshape(bsz, nb, Q_BLOCK, nh, dh), 1, 0)
    c_blocks = jnp.moveaxis(c_t.reshape(bsz, nh, nb, Q_BLOCK), 2, 0)
    q_pos = jnp.arange(t_len).reshape(nb, Q_BLOCK)
    k_pos = jnp.arange(t_len)

    def block(args):
        q_blk, c_blk, p_blk = args
        logits = jnp.einsum('bqhd,bkhd->bhqk', q_blk, k) * scale + c_blk[..., None] - c_t[:, :, None, :]
        logits = jnp.where(k_pos[None, :] <= p_blk[:, None], logits, -jnp.inf)
        p = jax.nn.softmax(logits, axis=-1)
        return jnp.einsum('bhqk,bkhd->bqhd', p, v)

    o = lax.map(block, (q_blocks, c_blocks, q_pos))
    return jnp.moveaxis(o, 0, 1).reshape(bsz, t_len, nh, dh)


def _fox_sample(q, k, v, logf, k_past, v_past, logf_past):
    scale = q.shape[-1] ** -0.5
    n_past = k_past.shape[1]
    t_new = q.shape[1]
    suffix = lax.cumsum(logf_past, axis=1, reverse=True) - logf_past
    cn_t = jnp.swapaxes(jnp.cumsum(logf, axis=1), 1, 2)
    s_past = (jnp.einsum('bqhd,bkhd->bhqk', q, k_past) * scale
              + cn_t[..., None] + jnp.swapaxes(suffix, 1, 2)[:, :, None, :])
    causal = jnp.tril(jnp.ones((t_new, t_new), bool))
    s_new = jnp.where(causal, jnp.einsum('bqhd,bkhd->bhqk', q, k) * scale
                      + cn_t[..., None] - cn_t[:, :, None, :], -jnp.inf)
    p = jax.nn.softmax(jnp.concatenate([s_past, s_new], axis=-1), axis=-1)
    return (jnp.einsum('bhqk,bkhd->bqhd', p[..., :n_past], v_past)
            + jnp.einsum('bhqk,bkhd->bqhd', p[..., n_past:], v))


def _even_mixer(x, w_in, lb, a_norm_w, f_bias, w_out, s0, past):
    bsz, t_len, _ = x.shape
    h = _proj(x, w_in, EVEN_SPLITS[-1])
    qa, fa, ia, ga, qb, kb, vb, fb = jnp.split(h, EVEN_SPLITS, axis=-1)
    qa = jax.nn.silu(qa).reshape(bsz, t_len, H_A, DK_A)
    za = fa.reshape(bsz, t_len, H_A, DK_A)
    log_fa = jnp.log(lb + (1.0 - lb) * jax.nn.sigmoid(za))
    ka = (1.0 - lb) * jax.nn.sigmoid(-za)
    oa, s_new = _hgrn2_chunked(qa, ka, ia.reshape(bsz, t_len, H_A, DV_A), log_fa, s0)
    oa = _rms_normalize(oa) * a_norm_w * jax.nn.silu(ga.reshape(bsz, t_len, H_A, DV_A))
    qb = qb.reshape(bsz, t_len, H_B, DH_B)
    kb = kb.reshape(bsz, t_len, H_B, DH_B)
    vb = vb.reshape(bsz, t_len, H_B, DH_B)
    log_fb = jax.nn.log_sigmoid(fb + f_bias)
    if past is None:
        ob = _fox_prompt(qb, kb, vb, log_fb)
    else:
        ob = _fox_sample(qb, kb, vb, log_fb, past[0], past[1], past[2])
    mix = jnp.concatenate([oa.reshape(bsz, t_len, -1), ob.reshape(bsz, t_len, -1)], axis=-1)
    return _proj(mix, w_out, w_out.shape[1]), kb, vb, log_fb, s_new


def _ssd_chunked(x, dt, a, b_in, c_in, h0):
    bsz, t_len, nh, hp = x.shape
    ng, ns = b_in.shape[2], b_in.shape[3]
    r = nh // ng
    c = CHUNK_C if t_len % CHUNK_C == 0 else t_len
    nc = t_len // c
    x = x.reshape(bsz, nc, c, ng, r, hp)
    dt = dt.reshape(bsz, nc, c, ng, r)
    b_in = b_in.reshape(bsz, nc, c, ng, ns)
    c_in = c_in.reshape(bsz, nc, c, ng, ns)
    cum = jnp.cumsum(dt * a.reshape(ng, r), axis=2)
    causal = jnp.tril(jnp.ones((c, c), bool))[:, :, None, None]
    decay = jnp.exp(jnp.where(causal, cum[:, :, :, None] - cum[:, :, None, :], -jnp.inf))
    cb = jnp.einsum('bctgn,bcsgn->bctsg', c_in, b_in)
    w = cb[..., None] * decay * dt[:, :, None]
    y_intra = jnp.einsum('bctsgr,bcsgrp->bctgrp', w, x)
    to_end = jnp.exp(cum[:, :, -1:] - cum) * dt
    states = jnp.einsum('bcsgn,bcsgr,bcsgrp->bcgrpn', b_in, to_end, x)

    def step(hs, inp):
        st, dec = inp
        return jnp.exp(dec)[..., None, None] * hs + st, hs

    h_last, h_prev = lax.scan(step, h0.reshape(bsz, ng, r, hp, ns),
                              (jnp.moveaxis(states, 1, 0), jnp.moveaxis(cum[:, :, -1], 1, 0)))
    h_prev = jnp.moveaxis(h_prev, 0, 1)
    y_inter = jnp.einsum('bctgn,bcgrpn,bctgr->bctgrp', c_in, h_prev, jnp.exp(cum))
    return (y_intra + y_inter).reshape(bsz, t_len, nh, hp), h_last.reshape(bsz, nh, hp, ns)


def _mamba2_mixer(x, conv_state, ssm_state, w_in, conv_w, conv_b, dt_bias, a_log, d_skip, norm_w, w_out):
    bsz, t_len, _ = x.shape
    h = _proj(x, w_in, D_INNER + CONV_DIM)
    z, xbc, dt = jnp.split(h, (D_INNER, D_INNER + CONV_DIM), axis=-1)
    ext = jnp.concatenate([conv_state, xbc], axis=1)
    conv = conv_b + sum(ext[:, j:j + t_len] * conv_w[j] for j in range(CONV_W))
    new_conv = ext[:, t_len:]
    xbc = jax.nn.silu(conv)
    xs, b_in, c_in = jnp.split(xbc, (D_INNER, D_INNER + N_GROUPS_C * D_STATE_C), axis=-1)
    xs = xs.reshape(bsz, t_len, H_C, HEADDIM_C)
    b_in = b_in.reshape(bsz, t_len, N_GROUPS_C, D_STATE_C)
    c_in = c_in.reshape(bsz, t_len, N_GROUPS_C, D_STATE_C)
    dt = jax.nn.softplus(dt + dt_bias)
    a = -jnp.exp(a_log)
    y, h_new = _ssd_chunked(xs, dt, a, b_in, c_in, ssm_state)
    y = y + d_skip[:, None] * xs
    y = y.reshape(bsz, t_len, D_INNER) * jax.nn.silu(z)
    y = _rms_normalize(y.reshape(bsz, t_len, N_GROUPS_C, D_INNER // N_GROUPS_C)).reshape(bsz, t_len, D_INNER)
    y = y * norm_w
    return _proj(y, w_out, w_out.shape[1]), h_new, new_conv


def kernel(x_prompt, x_sample, cache_k, cache_v, cache_logf, page_table, state_hgrn, state_ssm, state_conv,
           w_in_even, hgrn_lower_bound, hgrn_norm_w, fox_f_bias, w_out_even,
           w_in_odd, conv_w, conv_b, dt_bias, a_log, d_skip, ssm_norm_w, w_out_odd,
           ln1_g, ln1_b, ln2_g, ln2_b, router_w, router_b, exp_w1, exp_b1, exp_w2, exp_b2):
    lb_all = jnp.cumsum(jax.nn.softmax(hgrn_lower_bound, axis=0), axis=0)
    xp, xs = x_prompt, x_sample
    bp, bd = xp.shape[0], xs.shape[0]
    outs = {}
    for l in range(DEPTH):
        i = l // 2
        if l % 2 == 0:
            lb = lb_all[i].reshape(H_A, DK_A)
            s0p = jnp.zeros((bp, H_A, DK_A, DV_A), F32)
            mp, kp, vp, lfp, hgp = _even_mixer(xp, w_in_even[i], lb, hgrn_norm_w[i], fox_f_bias[i], w_out_even[i],
                                               s0p, None)
            past = (_gather_pages(cache_k[i], page_table), _gather_pages(cache_v[i], page_table),
                    _gather_pages(cache_logf[i], page_table))
            ms, k_s, v_s, lfs, hgs = _even_mixer(xs, w_in_even[i], lb, hgrn_norm_w[i], fox_f_bias[i], w_out_even[i],
                                                 state_hgrn[i], past)
            for name, val in (("kp", kp), ("ks", k_s), ("vp", vp), ("vs", v_s), ("lfp", lfp), ("lfs", lfs),
                              ("hgp", hgp), ("hgs", hgs)):
                outs.setdefault(name, []).append(val)
        else:
            mp, ssp, cvp = _mamba2_mixer(xp, jnp.zeros((bp, CONV_W - 1, CONV_DIM), F32),
                                         jnp.zeros((bp, H_C, HEADDIM_C, D_STATE_C), F32),
                                         w_in_odd[i], conv_w[i], conv_b[i], dt_bias[i], a_log[i], d_skip[i],
                                         ssm_norm_w[i], w_out_odd[i])
            ms, sss, cvs = _mamba2_mixer(xs, state_conv[i], state_ssm[i],
                                         w_in_odd[i], conv_w[i], conv_b[i], dt_bias[i], a_log[i], d_skip[i],
                                         ssm_norm_w[i], w_out_odd[i])
            for name, val in (("ssp", ssp), ("sss", sss), ("cvp", cvp), ("cvs", cvs)):
                outs.setdefault(name, []).append(val)
        xp = _layer_norm(DN_ALPHA * xp + mp, ln1_g[l], ln1_b[l])
        xs = _layer_norm(DN_ALPHA * xs + ms, ln1_g[l], ln1_b[l])
        w1g = exp_w1[l][:, :, 0::2].astype(BF16)
        w1u = exp_w1[l][:, :, 1::2].astype(BF16)
        b1g = exp_b1[l][:, None, 0::2]
        b1u = exp_b1[l][:, None, 1::2]
        w2 = exp_w2[l].astype(BF16)
        b2 = exp_b2[l][:, None, :]
        moe_p = _moe(xp, router_w[l], router_b[l], w1g, w1u, b1g, b1u, w2, b2, 512)
        moe_s = _moe(xs, router_w[l], router_b[l], w1g, w1u, b1g, b1u, w2, b2, 128)
        xp = _layer_norm(DN_ALPHA * xp + moe_p, ln2_g[l], ln2_b[l])
        xs = _layer_norm(DN_ALPHA * xs + moe_s, ln2_g[l], ln2_b[l])
    st = {k: jnp.stack(v) for k, v in outs.items()}
    return (xp, xs, st["kp"], st["ks"], st["vp"], st["vs"], st["lfp"], st["lfs"],
            st["hgp"], st["hgs"], st["ssp"], st["sss"], st["cvp"], st["cvs"])
```

```python
import functools

import jax
import jax.numpy as jnp
import numpy as np
from jax import lax
from jax.experimental import pallas as pl
from jax.experimental.pallas import tpu as pltpu

F32 = jnp.float32
BF16 = jnp.bfloat16
I32 = jnp.int32
HIGHEST = lax.Precision.HIGHEST

D_MODEL = 1024
DEPTH = 2
PAGE_SIZE = 128
H_A, DK_A, DV_A, CHUNK_A = 4, 128, 128, 32
H_B, DH_B, Q_BLOCK = 8, 64, 128
EVEN_WIDTHS = (H_A * DK_A, H_A * DK_A, H_A * DV_A, H_A * DV_A, H_B * DH_B, H_B * DH_B, H_B * DH_B, H_B)
EVEN_SPLITS = tuple(int(s) for s in np.cumsum(EVEN_WIDTHS)[:-1])
D_INNER = 2 * D_MODEL
HEADDIM_C = 64
H_C = D_INNER // HEADDIM_C
N_GROUPS_C = 4
D_STATE_C = 128
CONV_W = 4
CONV_DIM = D_INNER + 2 * N_GROUPS_C * D_STATE_C
CHUNK_C = 64
N_EXPERTS = 32
TOP_K = 4
D_FF = D_MODEL
SWIGLU_LIMIT = 7.0
SWIGLU_ALPHA = 1.702
DN_ALPHA = (2 * DEPTH) ** 0.25
LN_EPS = 1e-5
RMS_EPS = 1e-6

V7X_LANES = 128
V7X_MXU = 256
VMEM_LIMIT = 56 * 1024 * 1024
EXPERT_TILE = 512


def _row_tile(m, cap):
    best = 0
    for t in range(16, cap + 1, 16):
        if m % t == 0:
            best = t
    assert best, (m, cap)
    return best


def _params(*sem):
    return pltpu.CompilerParams(dimension_semantics=sem, vmem_limit_bytes=VMEM_LIMIT)


def _mm_kernel(x_ref, w_ref, o_ref):
    o_ref[...] = jnp.dot(x_ref[...].astype(BF16), w_ref[...].astype(BF16), preferred_element_type=F32)


def _matmul(x, w, tm_cap=1024, tn=512):
    m, k = x.shape
    n = w.shape[1]
    n_pad = -(-n // V7X_LANES) * V7X_LANES
    if n_pad != n:
        w = jnp.pad(w, ((0, 0), (0, n_pad - n)))
    tm = _row_tile(m, tm_cap)
    tn = tn if n_pad % tn == 0 else (V7X_MXU if n_pad % V7X_MXU == 0 else V7X_LANES)
    out = pl.pallas_call(
        _mm_kernel,
        grid=(m // tm, n_pad // tn),
        in_specs=[pl.BlockSpec((tm, k), lambda i, j: (i, 0)),
                  pl.BlockSpec((k, tn), lambda i, j: (0, j))],
        out_specs=pl.BlockSpec((tm, tn), lambda i, j: (i, j)),
        out_shape=jax.ShapeDtypeStruct((m, n_pad), F32),
        compiler_params=_params("parallel", "parallel"),
        name="dense_matmul",
    )(x, w)
    return out[:, :n] if n_pad != n else out


def _proj(x2, w, n_main):
    main = _matmul(x2, w[:, :n_main])
    if n_main == w.shape[1]:
        return main
    return jnp.concatenate([main, _matmul(x2, w[:, n_main:])], axis=-1)


def _post_mixer_kernel(x_ref, lhs_ref, w_ref, g_ref, b_ref, rw_ref, rb_ref, x1_ref, eid_ref, gate_ref):
    acc = DN_ALPHA * x_ref[...] + jnp.dot(lhs_ref[...].astype(BF16), w_ref[...], preferred_element_type=F32)
    xc = acc - jnp.mean(acc, axis=-1, keepdims=True)
    var = jnp.mean(xc * xc, axis=-1, keepdims=True)
    x1 = xc * lax.rsqrt(var + LN_EPS) * g_ref[...] + b_ref[...]
    x1_ref[...] = x1
    logits = jnp.dot(x1, rw_ref[...], precision=HIGHEST, preferred_element_type=F32) + rb_ref[...]
    lane = lax.broadcasted_iota(I32, logits.shape, 1)
    eids = jnp.zeros(logits.shape, I32)
    vals = []
    for k in range(TOP_K):
        top = jnp.max(logits, axis=-1, keepdims=True)
        idx = jnp.min(jnp.where(logits == top, lane, V7X_LANES), axis=-1, keepdims=True)
        vals.append(top)
        eids = jnp.where(lane == k, idx, eids)
        logits = jnp.where(lane == idx, -jnp.inf, logits)
    exps = [jnp.exp(v - vals[0]) for v in vals]
    denom = exps[0] + exps[1] + exps[2] + exps[3]
    gates = jnp.zeros(logits.shape, F32)
    for k in range(TOP_K):
        gates = jnp.where(lane == k, exps[k] / denom, gates)
    eid_ref[...] = eids
    gate_ref[...] = gates


def _post_mixer(x, lhs, w_bf16, g, b, rw, rb):
    m, d = x.shape
    k = lhs.shape[1]
    tm = _row_tile(m, 640)
    row = lambda i: (i, 0)
    fixed = lambda i: (0, 0)
    return pl.pallas_call(
        _post_mixer_kernel,
        grid=(m // tm,),
        in_specs=[pl.BlockSpec((tm, d), row), pl.BlockSpec((tm, k), row), pl.BlockSpec((k, d), fixed),
                  pl.BlockSpec((1, d), fixed), pl.BlockSpec((1, d), fixed),
                  pl.BlockSpec((d, V7X_LANES), fixed), pl.BlockSpec((1, V7X_LANES), fixed)],
        out_specs=[pl.BlockSpec((tm, d), row), pl.BlockSpec((tm, V7X_LANES), row),
                   pl.BlockSpec((tm, V7X_LANES), row)],
        out_shape=[jax.ShapeDtypeStruct((m, d), F32), jax.ShapeDtypeStruct((m, V7X_LANES), I32),
                   jax.ShapeDtypeStruct((m, V7X_LANES), F32)],
        compiler_params=_params("parallel"),
        name="post_mixer",
    )(x, lhs, w_bf16, g, b, rw, rb)


def _rank_kernel(eid_ref, rank_ref, cnt_ref, carry_ref):
    i = pl.program_id(0)

    @pl.when(i == 0)
    def _():
        carry_ref[...] = jnp.zeros_like(carry_ref)

    eids = eid_ref[...]
    tm = eids.shape[0]
    lane = lax.broadcasted_iota(I32, eids.shape, 1)
    sel = [jnp.sum(jnp.where(lane == k, eids, 0), axis=-1, keepdims=True) for k in range(TOP_K)]
    onehot = jnp.zeros(eids.shape, F32)
    for k in range(TOP_K):
        onehot = onehot + (lane == sel[k]).astype(F32)
    r = lax.broadcasted_iota(I32, (tm, tm), 0)
    c = lax.broadcasted_iota(I32, (tm, tm), 1)
    before = (c < r).astype(BF16)
    prior = jnp.dot(before, onehot.astype(BF16), preferred_element_type=F32) + carry_ref[...]
    ranks = jnp.zeros(eids.shape, F32)
    for k in range(TOP_K):
        rk = jnp.sum(jnp.where(lane == sel[k], prior, 0.0), axis=-1, keepdims=True)
        ranks = jnp.where(lane == k, rk, ranks)
    rank_ref[...] = ranks.astype(I32)
    total = carry_ref[...] + jnp.sum(onehot, axis=0, keepdims=True)
    carry_ref[...] = total
    cnt_ref[...] = total.astype(I32)


def _route_ranks(eids):
    m = eids.shape[0]
    tm = _row_tile(m, 640)
    return pl.pallas_call(
        _rank_kernel,
        grid=(m // tm,),
        in_specs=[pl.BlockSpec((tm, V7X_LANES), lambda i: (i, 0))],
        out_specs=[pl.BlockSpec((tm, V7X_LANES), lambda i: (i, 0)), pl.BlockSpec((1, V7X_LANES), lambda i: (0, 0))],
        out_shape=[jax.ShapeDtypeStruct((m, V7X_LANES), I32), jax.ShapeDtypeStruct((1, V7X_LANES), I32)],
        scratch_shapes=[pltpu.VMEM((1, V7X_LANES), F32)],
        compiler_params=_params("arbitrary"),
        name="route_ranks",
    )(eids)


def _row_copy(src_ref, s, dst_ref, d, sem):
    return pltpu.make_async_copy(src_ref.at[pl.ds(s, 1)], dst_ref.at[pl.ds(d, 1)], sem)


def _dispatch_kernel(pos_ref, x_ref, init_ref, out_ref, sem):
    del init_ref
    i = pl.program_id(0)
    tm = x_ref.shape[0]
    base = i * tm * TOP_K

    def start(t, carry):
        for k in range(TOP_K):
            _row_copy(x_ref, t, out_ref, pos_ref[base + t * TOP_K + k], sem).start()
        return carry

    def wait(t, carry):
        for k in range(TOP_K):
            _row_copy(x_ref, t, out_ref, pos_ref[base + t * TOP_K + k], sem).wait()
        return carry

    lax.fori_loop(0, tm, start, 0)
    lax.fori_loop(0, tm, wait, 0)


def _dispatch_rows(x1, pos_flat, n_rows):
    m, d = x1.shape
    tm = _row_tile(m, 1024)
    init = jnp.zeros((n_rows, d), x1.dtype)
    return pl.pallas_call(
        _dispatch_kernel,
        grid_spec=pltpu.PrefetchScalarGridSpec(
            num_scalar_prefetch=1,
            grid=(m // tm,),
            in_specs=[pl.BlockSpec((tm, d), lambda i, pos: (i, 0)), pl.BlockSpec(memory_space=pl.ANY)],
            out_specs=pl.BlockSpec(memory_space=pl.ANY),
            scratch_shapes=[pltpu.SemaphoreType.DMA(())],
        ),
        out_shape=jax.ShapeDtypeStruct((n_rows, d), x1.dtype),
        input_output_aliases={2: 0},
        compiler_params=_params("arbitrary"),
        name="moe_dispatch",
    )(pos_flat, x1, init)


def _w1_prep_kernel(w_ref, g_ref, u_ref):
    r = lax.broadcasted_iota(I32, (V7X_MXU, V7X_MXU), 0)
    c = lax.broadcasted_iota(I32, (V7X_MXU, V7X_MXU), 1)
    half = V7X_MXU // 2
    src = jnp.where(c < half, 2 * c, 2 * (c - half) + 1)
    perm = jnp.where(r == src, 1.0, 0.0).astype(BF16)
    for j in range(w_ref.shape[2] // V7X_MXU):
        blk = w_ref[0, :, j * V7X_MXU:(j + 1) * V7X_MXU].astype(BF16)
        res = jnp.dot(blk, perm, preferred_element_type=F32).astype(BF16)
        g_ref[0, :, j * half:(j + 1) * half] = res[:, :half]
        u_ref[0, :, j * half:(j + 1) * half] = res[:, half:]


def _w1_prep(w1):
    e, k, n2 = w1.shape
    tk = 512
    out = jax.ShapeDtypeStruct((e, k, n2 // 2), BF16)
    return pl.pallas_call(
        _w1_prep_kernel,
        grid=(e, k // tk),
        in_specs=[pl.BlockSpec((1, tk, n2), lambda i, j: (i, j, 0))],
        out_specs=[pl.BlockSpec((1, tk, n2 // 2), lambda i, j: (i, j, 0))] * 2,
        out_shape=[out, out],
        compiler_params=_params("parallel", "parallel"),
        name="w1_prep",
    )(w1)


def _ffn_kernel(te_ref, nv_ref, x_ref, w1g_ref, w1u_ref, b1g_ref, b1u_ref, w2_ref, b2_ref, o_ref):
    @pl.when(pl.program_id(0) < nv_ref[0])
    def _():
        x = x_ref[...].astype(BF16)
        hg = jnp.dot(x, w1g_ref[0], preferred_element_type=F32) + b1g_ref[0]
        hu = jnp.dot(x, w1u_ref[0], preferred_element_type=F32) + b1u_ref[0]
        gate = jnp.minimum(hg, SWIGLU_LIMIT)
        up = jnp.clip(hu, -SWIGLU_LIMIT, SWIGLU_LIMIT)
        act = (up + 1.0) * gate * jax.nn.sigmoid(SWIGLU_ALPHA * gate)
        o_ref[...] = jnp.dot(act.astype(BF16), w2_ref[0], preferred_element_type=F32) + b2_ref[0]


def _expert_ffn(x_rows, tile_expert, n_valid, w1g, w1u, b1g, b1u, w2, b2):
    rows, d = x_rows.shape
    tm = EXPERT_TILE
    rmap = lambda i, te, nv: (jnp.minimum(i, nv[0] - 1), 0)
    wmap = lambda i, te, nv: (te[i], 0, 0)
    return pl.pallas_call(
        _ffn_kernel,
        grid_spec=pltpu.PrefetchScalarGridSpec(
            num_scalar_prefetch=2,
            grid=(rows // tm,),
            in_specs=[pl.BlockSpec((tm, d), rmap),
                      pl.BlockSpec((1, d, D_FF), wmap), pl.BlockSpec((1, d, D_FF), wmap),
                      pl.BlockSpec((1, 1, D_FF), wmap), pl.BlockSpec((1, 1, D_FF), wmap),
                      pl.BlockSpec((1, D_FF, d), wmap), pl.BlockSpec((1, 1, d), wmap)],
            out_specs=pl.BlockSpec((tm, d), rmap),
        ),
        out_shape=jax.ShapeDtypeStruct((rows, d), F32),
        compiler_params=_params("arbitrary"),
        name="moe_expert_ffn",
    )(tile_expert, n_valid, x_rows, w1g, w1u, b1g, b1u, w2, b2)


def _combine_kernel(pos_ref, y_ref, gate_ref, x_ref, g_ref, b_ref, o_ref, buf_ref, sem):
    i = pl.program_id(0)
    tm = x_ref.shape[0]
    base = i * tm * TOP_K

    def start(t, carry):
        for k in range(TOP_K):
            _row_copy(y_ref, pos_ref[base + t * TOP_K + k], buf_ref.at[k], t, sem).start()
        return carry

    def wait(t, carry):
        for k in range(TOP_K):
            _row_copy(y_ref, pos_ref[base + t * TOP_K + k], buf_ref.at[k], t, sem).wait()
        return carry

    lax.fori_loop(0, tm, start, 0)
    lax.fori_loop(0, tm, wait, 0)
    gates = gate_ref[...]
    acc = DN_ALPHA * x_ref[...]
    for k in range(TOP_K):
        acc = acc + gates[:, k:k + 1] * buf_ref[k]
    xc = acc - jnp.mean(acc, axis=-1, keepdims=True)
    var = jnp.mean(xc * xc, axis=-1, keepdims=True)
    o_ref[...] = xc * lax.rsqrt(var + LN_EPS) * g_ref[...] + b_ref[...]


def _combine_ln(y_rows, pos_flat, gates, x1, g, b):
    m, d = x1.shape
    tm = _row_tile(m, 320)
    row = lambda i, pos: (i, 0)
    fixed = lambda i, pos: (0, 0)
    return pl.pallas_call(
        _combine_kernel,
        grid_spec=pltpu.PrefetchScalarGridSpec(
            num_scalar_prefetch=1,
            grid=(m // tm,),
            in_specs=[pl.BlockSpec(memory_space=pl.ANY), pl.BlockSpec((tm, V7X_LANES), row),
                      pl.BlockSpec((tm, d), row), pl.BlockSpec((1, d), fixed), pl.BlockSpec((1, d), fixed)],
            out_specs=pl.BlockSpec((tm, d), row),
            scratch_shapes=[pltpu.VMEM((TOP_K, tm, d), F32), pltpu.SemaphoreType.DMA(())],
        ),
        out_shape=jax.ShapeDtypeStruct((m, d), F32),
        compiler_params=_params("arbitrary"),
        name="moe_combine_ln",
    )(pos_flat, y_rows, gates, x1, g, b)


def _moe_ln(x1, eids, gates, w1g, w1u, b1g, b1u, w2, b2, g, b):
    m, d = x1.shape
    ranks, counts = _route_ranks(eids)
    counts = counts[0, :N_EXPERTS]
    padded = (counts + EXPERT_TILE - 1) // EXPERT_TILE * EXPERT_TILE
    ends = jnp.cumsum(padded)
    gstart = ends - padded
    n_tiles = -(-m * TOP_K // EXPERT_TILE) + N_EXPERTS
    tile_start = jnp.arange(n_tiles, dtype=I32) * EXPERT_TILE
    tile_expert = jnp.minimum(jnp.searchsorted(ends, tile_start, side='right'), N_EXPERTS - 1).astype(I32)
    n_valid = (ends[-1:] // EXPERT_TILE).astype(I32)
    pos_flat = (gstart[eids[:, :TOP_K]] + ranks[:, :TOP_K]).reshape(-1).astype(I32)
    x_rows = _dispatch_rows(x1, pos_flat, n_tiles * EXPERT_TILE)
    y_rows = _expert_ffn(x_rows, tile_expert, n_valid, w1g, w1u, b1g, b1u, w2, b2)
    return _combine_ln(y_rows, pos_flat, gates, x1, g, b)


def _rms_normalize(x):
    return x * lax.rsqrt(jnp.mean(x * x, axis=-1, keepdims=True) + RMS_EPS)


def _gather_pages(pool, page_table):
    g = pool[page_table]
    return g.reshape((g.shape[0], g.shape[1] * g.shape[2]) + g.shape[3:])


def _hgrn2_chunked(q, k, v, logf, s0):
    bsz, t_len, nh, dk = q.shape
    dv = v.shape[-1]
    c = CHUNK_A if t_len % CHUNK_A == 0 else t_len
    nc = t_len // c
    q, k, logf = (a.reshape(bsz, nc, c, nh, dk) for a in (q, k, logf))
    v = v.reshape(bsz, nc, c, nh, dv)
    b = jnp.cumsum(logf, axis=2)
    g = b[:, :, -1]
    q_d = q * jnp.exp(b)
    k_d = k * jnp.exp(-b)
    k_end = k * jnp.exp(g[:, :, None] - b)
    causal = jnp.tril(jnp.ones((c, c), bool))
    scores = jnp.where(causal, jnp.einsum('bcthk,bcshk->bchts', q_d, k_d), 0.0)
    o_intra = jnp.einsum('bchts,bcshv->bcthv', scores, v)
    u = jnp.einsum('bcshk,bcshv->bchkv', k_end, v)

    def step(s, inp):
        u_c, g_c = inp
        return jnp.exp(g_c)[..., None] * s + u_c, s

    s_last, s_prev = lax.scan(step, s0, (jnp.moveaxis(u, 1, 0), jnp.moveaxis(g, 1, 0)))
    s_prev = jnp.moveaxis(s_prev, 0, 1)
    o_inter = jnp.einsum('bcthk,bchkv->bcthv', q_d, s_prev)
    return (o_intra + o_inter).reshape(bsz, t_len, nh, dv), s_last


def _fox_prompt(q, k, v, logf):
    bsz, t_len, nh, dh = q.shape
    scale = dh ** -0.5
    c_t = jnp.swapaxes(jnp.cumsum(logf, axis=1), 1, 2)
    nb = t_len // Q_BLOCK
    q_blocks = jnp.moveaxis(q.reshape(bsz, nb, Q_BLOCK, nh, dh), 1, 0)
    c_blocks = jnp.moveaxis(c_t.reshape(bsz, nh, nb, Q_BLOCK), 2, 0)
    q_pos = jnp.arange(t_len).reshape(nb, Q_BLOCK)
    k_pos = jnp.arange(t_len)

    def block(args):
        q_blk, c_blk, p_blk = args
        logits = jnp.einsum('bqhd,bkhd->bhqk', q_blk, k) * scale + c_blk[..., None] - c_t[:, :, None, :]
        logits = jnp.where(k_pos[None, :] <= p_blk[:, None], logits, -jnp.inf)
        p = jax.nn.softmax(logits, axis=-1)
        return jnp.einsum('bhqk,bkhd->bqhd', p, v)

    o = lax.map(block, (q_blocks, c_blocks, q_pos))
    return jnp.moveaxis(o, 0, 1).reshape(bsz, t_len, nh, dh)


def _fox_sample(q, k, v, logf, k_past, v_past, logf_past):
    scale = q.shape[-1] ** -0.5
    n_past = k_past.shape[1]
    t_new = q.shape[1]
    suffix = lax.cumsum(logf_past, axis=1, reverse=True) - logf_past
    cn_t = jnp.swapaxes(jnp.cumsum(logf, axis=1), 1, 2)
    s_past = (jnp.einsum('bqhd,bkhd->bhqk', q, k_past) * scale
              + cn_t[..., None] + jnp.swapaxes(suffix, 1, 2)[:, :, None, :])
    causal = jnp.tril(jnp.ones((t_new, t_new), bool))
    s_new = jnp.where(causal, jnp.einsum('bqhd,bkhd->bhqk', q, k) * scale
                      + cn_t[..., None] - cn_t[:, :, None, :], -jnp.inf)
    p = jax.nn.softmax(jnp.concatenate([s_past, s_new], axis=-1), axis=-1)
    return (jnp.einsum('bhqk,bkhd->bqhd', p[..., :n_past], v_past)
            + jnp.einsum('bhqk,bkhd->bqhd', p[..., n_past:], v))


def _even_mixer(h, lb, a_norm_w, f_bias, s0, past):
    bsz, t_len, _ = h.shape
    qa, fa, ia, ga, qb, kb, vb, fb = jnp.split(h, EVEN_SPLITS, axis=-1)
    qa = jax.nn.silu(qa).reshape(bsz, t_len, H_A, DK_A)
    za = fa.reshape(bsz, t_len, H_A, DK_A)
    log_fa = jnp.log(lb + (1.0 - lb) * jax.nn.sigmoid(za))
    ka = (1.0 - lb) * jax.nn.sigmoid(-za)
    oa, s_new = _hgrn2_chunked(qa, ka, ia.reshape(bsz, t_len, H_A, DV_A), log_fa, s0)
    oa = _rms_normalize(oa) * a_norm_w * jax.nn.silu(ga.reshape(bsz, t_len, H_A, DV_A))
    qb = qb.reshape(bsz, t_len, H_B, DH_B)
    kb = kb.reshape(bsz, t_len, H_B, DH_B)
    vb = vb.reshape(bsz, t_len, H_B, DH_B)
    log_fb = jax.nn.log_sigmoid(fb + f_bias)
    if past is None:
        ob = _fox_prompt(qb, kb, vb, log_fb)
    else:
        ob = _fox_sample(qb, kb, vb, log_fb, past[0], past[1], past[2])
    mix = jnp.concatenate([oa.reshape(bsz, t_len, -1), ob.reshape(bsz, t_len, -1)], axis=-1)
    return mix, kb, vb, log_fb, s_new


def _ssd_chunked(x, dt, a, b_in, c_in, h0):
    bsz, t_len, nh, hp = x.shape
    ng, ns = b_in.shape[2], b_in.shape[3]
    r = nh // ng
    c = CHUNK_C if t_len % CHUNK_C == 0 else t_len
    nc = t_len // c
    x = x.reshape(bsz, nc, c, ng, r, hp)
    dt = dt.reshape(bsz, nc, c, ng, r)
    b_in = b_in.reshape(bsz, nc, c, ng, ns)
    c_in = c_in.reshape(bsz, nc, c, ng, ns)
    cum = jnp.cumsum(dt * a.reshape(ng, r), axis=2)
    causal = jnp.tril(jnp.ones((c, c), bool))[:, :, None, None]
    decay = jnp.exp(jnp.where(causal, cum[:, :, :, None] - cum[:, :, None, :], -jnp.inf))
    cb = jnp.einsum('bctgn,bcsgn->bctsg', c_in, b_in)
    w = cb[..., None] * decay * dt[:, :, None]
    y_intra = jnp.einsum('bctsgr,bcsgrp->bctgrp', w, x)
    to_end = jnp.exp(cum[:, :, -1:] - cum) * dt
    states = jnp.einsum('bcsgn,bcsgr,bcsgrp->bcgrpn', b_in, to_end, x)

    def step(hs, inp):
        st, dec = inp
        return jnp.exp(dec)[..., None, None] * hs + st, hs

    h_last, h_prev = lax.scan(step, h0.reshape(bsz, ng, r, hp, ns),
                              (jnp.moveaxis(states, 1, 0), jnp.moveaxis(cum[:, :, -1], 1, 0)))
    h_prev = jnp.moveaxis(h_prev, 0, 1)
    y_inter = jnp.einsum('bctgn,bcgrpn,bctgr->bctgrp', c_in, h_prev, jnp.exp(cum))
    return (y_intra + y_inter).reshape(bsz, t_len, nh, hp), h_last.reshape(bsz, nh, hp, ns)


def _mamba2_mixer(h, conv_state, ssm_state, conv_w, conv_b, dt_bias, a_log, d_skip, norm_w):
    bsz, t_len, _ = h.shape
    z, xbc, dt = jnp.split(h, (D_INNER, D_INNER + CONV_DIM), axis=-1)
    ext = jnp.concatenate([conv_state, xbc], axis=1)
    conv = conv_b + sum(ext[:, j:j + t_len] * conv_w[j] for j in range(CONV_W))
    new_conv = ext[:, t_len:]
    xbc = jax.nn.silu(conv)
    xs, b_in, c_in = jnp.split(xbc, (D_INNER, D_INNER + N_GROUPS_C * D_STATE_C), axis=-1)
    xs = xs.reshape(bsz, t_len, H_C, HEADDIM_C)
    b_in = b_in.reshape(bsz, t_len, N_GROUPS_C, D_STATE_C)
    c_in = c_in.reshape(bsz, t_len, N_GROUPS_C, D_STATE_C)
    dt = jax.nn.softplus(dt + dt_bias)
    a = -jnp.exp(a_log)
    y, h_new = _ssd_chunked(xs, dt, a, b_in, c_in, ssm_state)
    y = y + d_skip[:, None] * xs
    y = y.reshape(bsz, t_len, D_INNER) * jax.nn.silu(z)
    y = _rms_normalize(y.reshape(bsz, t_len, N_GROUPS_C, D_INNER // N_GROUPS_C)).reshape(bsz, t_len, D_INNER)
    return y * norm_w, h_new, new_conv


def kernel(x_prompt, x_sample, cache_k, cache_v, cache_logf, page_table, state_hgrn, state_ssm, state_conv,
           w_in_even, hgrn_lower_bound, hgrn_norm_w, fox_f_bias, w_out_even,
           w_in_odd, conv_w, conv_b, dt_bias, a_log, d_skip, ssm_norm_w, w_out_odd,
           ln1_g, ln1_b, ln2_g, ln2_b, router_w, router_b, exp_w1, exp_b1, exp_w2, exp_b2):
    lb_all = jnp.cumsum(jax.nn.softmax(hgrn_lower_bound, axis=0), axis=0)
    bp, t_p, d = x_prompt.shape
    bd, t_d, _ = x_sample.shape
    n_p = bp * t_p
    x_all = jnp.concatenate([x_prompt.reshape(n_p, d), x_sample.reshape(bd * t_d, d)], axis=0)
    outs = {}
    for l in range(DEPTH):
        i = l // 2
        if l % 2 == 0:
            h = _proj(x_all, w_in_even[i], EVEN_SPLITS[-1])
            hp, hs = h[:n_p].reshape(bp, t_p, -1), h[n_p:].reshape(bd, t_d, -1)
            lb = lb_all[i].reshape(H_A, DK_A)
            mp, kp, vp, lfp, hgp = _even_mixer(hp, lb, hgrn_norm_w[i], fox_f_bias[i],
                                               jnp.zeros((bp, H_A, DK_A, DV_A), F32), None)
            past = (_gather_pages(cache_k[i], page_table), _gather_pages(cache_v[i], page_table),
                    _gather_pages(cache_logf[i], page_table))
            ms, k_s, v_s, lfs, hgs = _even_mixer(hs, lb, hgrn_norm_w[i], fox_f_bias[i], state_hgrn[i], past)
            for name, val in (("kp", kp), ("ks", k_s), ("vp", vp), ("vs", v_s), ("lfp", lfp), ("lfs", lfs),
                              ("hgp", hgp), ("hgs", hgs)):
                outs.setdefault(name, []).append(val)
            w_out = w_out_even[i]
        else:
            h = _proj(x_all, w_in_odd[i], D_INNER + CONV_DIM)
            hp, hs = h[:n_p].reshape(bp, t_p, -1), h[n_p:].reshape(bd, t_d, -1)
            mp, ssp, cvp = _mamba2_mixer(hp, jnp.zeros((bp, CONV_W - 1, CONV_DIM), F32),
                                         jnp.zeros((bp, H_C, HEADDIM_C, D_STATE_C), F32),
                                         conv_w[i], conv_b[i], dt_bias[i], a_log[i], d_skip[i], ssm_norm_w[i])
            ms, sss, cvs = _mamba2_mixer(hs, state_conv[i], state_ssm[i],
                                         conv_w[i], conv_b[i], dt_bias[i], a_log[i], d_skip[i], ssm_norm_w[i])
            for name, val in (("ssp", ssp), ("sss", sss), ("cvp", cvp), ("cvs", cvs)):
                outs.setdefault(name, []).append(val)
            w_out = w_out_odd[i]
        mix = jnp.concatenate([mp.reshape(n_p, -1), ms.reshape(bd * t_d, -1)], axis=0)
        rw = jnp.pad(router_w[l], ((0, 0), (0, V7X_LANES - N_EXPERTS)))
        rb = jnp.concatenate([router_b[l], jnp.full((V7X_LANES - N_EXPERTS,), -jnp.inf, F32)])[None]
        x1, eids, gates = _post_mixer(x_all, mix, w_out.astype(BF16), ln1_g[l][None], ln1_b[l][None], rw, rb)
        w1g, w1u = _w1_prep(exp_w1[l])
        b1g = exp_b1[l][:, None, 0::2]
        b1u = exp_b1[l][:, None, 1::2]
        x_all = _moe_ln(x1, eids, gates, w1g, w1u, b1g, b1u, exp_w2[l].astype(BF16), exp_b2[l][:, None, :],
                        ln2_g[l][None], ln2_b[l][None])
    st = {k: jnp.stack(v) for k, v in outs.items()}
    return (x_all[:n_p].reshape(bp, t_p, d), x_all[n_p:].reshape(bd, t_d, d),
            st["kp"], st["ks"], st["vp"], st["vs"], st["lfp"], st["lfs"],
            st["hgp"], st["hgs"], st["ssp"], st["sss"], st["cvp"], st["cvs"])
```

```python
import functools

import jax
import jax.numpy as jnp
import numpy as np
from jax import lax
from jax.experimental import pallas as pl
from jax.experimental.pallas import tpu as pltpu

F32 = jnp.float32
BF16 = jnp.bfloat16
I32 = jnp.int32
HIGHEST = lax.Precision.HIGHEST

D_MODEL = 1024
DEPTH = 2
PAGE_SIZE = 128
H_A, DK_A, DV_A, CHUNK_A = 4, 128, 128, 32
H_B, DH_B = 8, 64
D_A = H_A * DK_A
D_B = H_B * DH_B
EVEN_MAIN = 4 * D_A + 3 * D_B
D_INNER = 2 * D_MODEL
HEADDIM_C = 64
H_C = D_INNER // HEADDIM_C
N_GROUPS_C = 4
D_STATE_C = 128
D_BC = N_GROUPS_C * D_STATE_C
GROUP_W = D_INNER // N_GROUPS_C
CONV_W = 4
CONV_DIM = D_INNER + 2 * D_BC
ODD_MAIN = D_INNER + CONV_DIM
N_EXPERTS = 32
TOP_K = 4
D_FF = D_MODEL
SWIGLU_LIMIT = 7.0
SWIGLU_ALPHA = 1.702
DN_ALPHA = (2 * DEPTH) ** 0.25
LN_EPS = 1e-5
RMS_EPS = 1e-6

V7X_LANES = 128
V7X_SUBLANES = 8
V7X_MXU = 256
VMEM_LIMIT = 56 * 1024 * 1024
EXPERT_TILE = 512
ATTN_BLOCK = 256
SSD_BLOCK = 128
DECODE_PAGES = 8

NT_DIMS = (((1,), (1,)), ((), ()))
TN_DIMS = (((0,), (0,)), ((), ()))


def _row_tile(m, cap):
    best = 0
    for t in range(16, cap + 1, 16):
        if m % t == 0:
            best = t
    assert best, (m, cap)
    return best


def _params(*sem):
    return pltpu.CompilerParams(dimension_semantics=sem, vmem_limit_bytes=VMEM_LIMIT)


def _silu(x):
    return x * jax.nn.sigmoid(x)


def _softplus(x):
    return jnp.maximum(x, 0.0) + jnp.log1p(jnp.exp(-jnp.abs(x)))


def _log_sigmoid(x):
    return -_softplus(-x)


def _iota2(shape):
    return lax.broadcasted_iota(I32, shape, 0), lax.broadcasted_iota(I32, shape, 1)


def _to_column(row_vec):
    n = row_vec.shape[1]
    r, c = _iota2((n, n))
    return jnp.sum(jnp.where(r == c, jnp.broadcast_to(row_vec, (n, n)), 0.0), axis=1, keepdims=True)


def _to_row(col_vec):
    n = col_vec.shape[0]
    r, c = _iota2((n, n))
    return jnp.sum(jnp.where(r == c, jnp.broadcast_to(col_vec, (n, n)), 0.0), axis=0, keepdims=True)


def _expand_heads(v, e_bf16):
    hi = v.astype(BF16)
    r1 = v - hi.astype(F32)
    mid = r1.astype(BF16)
    lo = (r1 - mid.astype(F32)).astype(BF16)
    dot = lambda a: jnp.dot(a, e_bf16, preferred_element_type=F32)
    return dot(hi) + dot(mid) + dot(lo)


def _mm_kernel(x_ref, w_ref, o_ref):
    o_ref[...] = jnp.dot(x_ref[...].astype(BF16), w_ref[...].astype(BF16), preferred_element_type=F32)


def _matmul(x, w, tm_cap=1024, tn=512):
    m, k = x.shape
    n = w.shape[1]
    n_pad = -(-n // V7X_LANES) * V7X_LANES
    if n_pad != n:
        w = jnp.pad(w, ((0, 0), (0, n_pad - n)))
    tm = _row_tile(m, tm_cap)
    tn = tn if n_pad % tn == 0 else (V7X_MXU if n_pad % V7X_MXU == 0 else V7X_LANES)
    return pl.pallas_call(
        _mm_kernel,
        grid=(m // tm, n_pad // tn),
        in_specs=[pl.BlockSpec((tm, k), lambda i, j: (i, 0)),
                  pl.BlockSpec((k, tn), lambda i, j: (0, j))],
        out_specs=pl.BlockSpec((tm, tn), lambda i, j: (i, j)),
        out_shape=jax.ShapeDtypeStruct((m, n_pad), F32),
        compiler_params=_params("parallel", "parallel"),
        name="dense_matmul",
    )(x, w)


def _post_mixer_kernel(x_ref, lhs_ref, w_ref, g_ref, b_ref, rw_ref, rb_ref, x1_ref, eid_ref, gate_ref):
    acc = DN_ALPHA * x_ref[...] + jnp.dot(lhs_ref[...].astype(BF16), w_ref[...], preferred_element_type=F32)
    xc = acc - jnp.mean(acc, axis=-1, keepdims=True)
    var = jnp.mean(xc * xc, axis=-1, keepdims=True)
    x1 = xc * lax.rsqrt(var + LN_EPS) * g_ref[...] + b_ref[...]
    x1_ref[...] = x1
    logits = jnp.dot(x1, rw_ref[...], precision=HIGHEST, preferred_element_type=F32) + rb_ref[...]
    lane = lax.broadcasted_iota(I32, logits.shape, 1)
    eids = jnp.zeros(logits.shape, I32)
    vals = []
    for k in range(TOP_K):
        top = jnp.max(logits, axis=-1, keepdims=True)
        idx = jnp.min(jnp.where(logits == top, lane, V7X_LANES), axis=-1, keepdims=True)
        vals.append(top)
        eids = jnp.where(lane == k, idx, eids)
        logits = jnp.where(lane == idx, -jnp.inf, logits)
    exps = [jnp.exp(v - vals[0]) for v in vals]
    denom = exps[0] + exps[1] + exps[2] + exps[3]
    gates = jnp.zeros(logits.shape, F32)
    for k in range(TOP_K):
        gates = jnp.where(lane == k, exps[k] / denom, gates)
    eid_ref[...] = eids
    gate_ref[...] = gates


def _post_mixer(x, lhs, w_bf16, g, b, rw, rb):
    m, d = x.shape
    k = lhs.shape[1]
    tm = _row_tile(m, 640)
    row = lambda i: (i, 0)
    fixed = lambda i: (0, 0)
    return pl.pallas_call(
        _post_mixer_kernel,
        grid=(m // tm,),
        in_specs=[pl.BlockSpec((tm, d), row), pl.BlockSpec((tm, k), row), pl.BlockSpec((k, d), fixed),
                  pl.BlockSpec((1, d), fixed), pl.BlockSpec((1, d), fixed),
                  pl.BlockSpec((d, V7X_LANES), fixed), pl.BlockSpec((1, V7X_LANES), fixed)],
        out_specs=[pl.BlockSpec((tm, d), row), pl.BlockSpec((tm, V7X_LANES), row),
                   pl.BlockSpec((tm, V7X_LANES), row)],
        out_shape=[jax.ShapeDtypeStruct((m, d), F32), jax.ShapeDtypeStruct((m, V7X_LANES), I32),
                   jax.ShapeDtypeStruct((m, V7X_LANES), F32)],
        compiler_params=_params("parallel"),
        name="post_mixer",
    )(x, lhs, w_bf16, g, b, rw, rb)


def _rank_kernel(eid_ref, rank_ref, cnt_ref, carry_ref):
    i = pl.program_id(0)

    @pl.when(i == 0)
    def _():
        carry_ref[...] = jnp.zeros_like(carry_ref)

    eids = eid_ref[...]
    tm = eids.shape[0]
    lane = lax.broadcasted_iota(I32, eids.shape, 1)
    sel = [jnp.sum(jnp.where(lane == k, eids, 0), axis=-1, keepdims=True) for k in range(TOP_K)]
    onehot = jnp.zeros(eids.shape, F32)
    for k in range(TOP_K):
        onehot = onehot + (lane == sel[k]).astype(F32)
    r, c = _iota2((tm, tm))
    before = (c < r).astype(BF16)
    prior = jnp.dot(before, onehot.astype(BF16), preferred_element_type=F32) + carry_ref[...]
    ranks = jnp.zeros(eids.shape, F32)
    for k in range(TOP_K):
        rk = jnp.sum(jnp.where(lane == sel[k], prior, 0.0), axis=-1, keepdims=True)
        ranks = jnp.where(lane == k, rk, ranks)
    rank_ref[...] = ranks.astype(I32)
    total = carry_ref[...] + jnp.sum(onehot, axis=0, keepdims=True)
    carry_ref[...] = total
    cnt_ref[...] = total.astype(I32)


def _route_ranks(eids):
    m = eids.shape[0]
    tm = _row_tile(m, 640)
    return pl.pallas_call(
        _rank_kernel,
        grid=(m // tm,),
        in_specs=[pl.BlockSpec((tm, V7X_LANES), lambda i: (i, 0))],
        out_specs=[pl.BlockSpec((tm, V7X_LANES), lambda i: (i, 0)), pl.BlockSpec((1, V7X_LANES), lambda i: (0, 0))],
        out_shape=[jax.ShapeDtypeStruct((m, V7X_LANES), I32), jax.ShapeDtypeStruct((1, V7X_LANES), I32)],
        scratch_shapes=[pltpu.VMEM((1, V7X_LANES), F32)],
        compiler_params=_params("arbitrary"),
        name="route_ranks",
    )(eids)


def _row_copy(src_ref, s, dst_ref, d, sem):
    return pltpu.make_async_copy(src_ref.at[pl.ds(s, 1)], dst_ref.at[pl.ds(d, 1)], sem)


def _dispatch_kernel(pos_ref, x_ref, init_ref, out_ref, sem):
    del init_ref
    i = pl.program_id(0)
    tm = x_ref.shape[0]
    base = i * tm * TOP_K

    def start(t, carry):
        for k in range(TOP_K):
            _row_copy(x_ref, t, out_ref, pos_ref[base + t * TOP_K + k], sem).start()
        return carry

    def wait(t, carry):
        for k in range(TOP_K):
            _row_copy(x_ref, t, out_ref, pos_ref[base + t * TOP_K + k], sem).wait()
        return carry

    lax.fori_loop(0, tm, start, 0)
    lax.fori_loop(0, tm, wait, 0)


def _dispatch_rows(x1, pos_flat, n_rows):
    m, d = x1.shape
    tm = _row_tile(m, 1024)
    init = jnp.zeros((n_rows, d), x1.dtype)
    return pl.pallas_call(
        _dispatch_kernel,
        grid_spec=pltpu.PrefetchScalarGridSpec(
            num_scalar_prefetch=1,
            grid=(m // tm,),
            in_specs=[pl.BlockSpec((tm, d), lambda i, pos: (i, 0)), pl.BlockSpec(memory_space=pl.ANY)],
            out_specs=pl.BlockSpec(memory_space=pl.ANY),
            scratch_shapes=[pltpu.SemaphoreType.DMA(())],
        ),
        out_shape=jax.ShapeDtypeStruct((n_rows, d), x1.dtype),
        input_output_aliases={2: 0},
        compiler_params=_params("arbitrary"),
        name="moe_dispatch",
    )(pos_flat, x1, init)


def _w1_prep_kernel(w_ref, g_ref, u_ref):
    r, c = _iota2((V7X_MXU, V7X_MXU))
    half = V7X_MXU // 2
    src = jnp.where(c < half, 2 * c, 2 * (c - half) + 1)
    perm = jnp.where(r == src, 1.0, 0.0).astype(BF16)
    for j in range(w_ref.shape[2] // V7X_MXU):
        blk = w_ref[0, :, j * V7X_MXU:(j + 1) * V7X_MXU].astype(BF16)
        res = jnp.dot(blk, perm, preferred_element_type=F32).astype(BF16)
        g_ref[0, :, j * half:(j + 1) * half] = res[:, :half]
        u_ref[0, :, j * half:(j + 1) * half] = res[:, half:]


def _w1_prep(w1):
    e, k, n2 = w1.shape
    tk = 512
    out = jax.ShapeDtypeStruct((e, k, n2 // 2), BF16)
    return pl.pallas_call(
        _w1_prep_kernel,
        grid=(e, k // tk),
        in_specs=[pl.BlockSpec((1, tk, n2), lambda i, j: (i, j, 0))],
        out_specs=[pl.BlockSpec((1, tk, n2 // 2), lambda i, j: (i, j, 0))] * 2,
        out_shape=[out, out],
        compiler_params=_params("parallel", "parallel"),
        name="w1_prep",
    )(w1)


def _ffn_kernel(te_ref, nv_ref, x_ref, w1g_ref, w1u_ref, b1g_ref, b1u_ref, w2_ref, b2_ref, o_ref):
    @pl.when(pl.program_id(0) < nv_ref[0])
    def _():
        x = x_ref[...].astype(BF16)
        hg = jnp.dot(x, w1g_ref[0], preferred_element_type=F32) + b1g_ref[0]
        hu = jnp.dot(x, w1u_ref[0], preferred_element_type=F32) + b1u_ref[0]
        gate = jnp.minimum(hg, SWIGLU_LIMIT)
        up = jnp.clip(hu, -SWIGLU_LIMIT, SWIGLU_LIMIT)
        act = (up + 1.0) * gate * jax.nn.sigmoid(SWIGLU_ALPHA * gate)
        o_ref[...] = jnp.dot(act.astype(BF16), w2_ref[0], preferred_element_type=F32) + b2_ref[0]


def _expert_ffn(x_rows, tile_expert, n_valid, w1g, w1u, b1g, b1u, w2, b2):
    rows, d = x_rows.shape
    tm = EXPERT_TILE
    rmap = lambda i, te, nv: (jnp.minimum(i, nv[0] - 1), 0)
    wmap = lambda i, te, nv: (te[i], 0, 0)
    return pl.pallas_call(
        _ffn_kernel,
        grid_spec=pltpu.PrefetchScalarGridSpec(
            num_scalar_prefetch=2,
            grid=(rows // tm,),
            in_specs=[pl.BlockSpec((tm, d), rmap),
                      pl.BlockSpec((1, d, D_FF), wmap), pl.BlockSpec((1, d, D_FF), wmap),
                      pl.BlockSpec((1, 1, D_FF), wmap), pl.BlockSpec((1, 1, D_FF), wmap),
                      pl.BlockSpec((1, D_FF, d), wmap), pl.BlockSpec((1, 1, d), wmap)],
            out_specs=pl.BlockSpec((tm, d), rmap),
        ),
        out_shape=jax.ShapeDtypeStruct((rows, d), F32),
        compiler_params=_params("arbitrary"),
        name="moe_expert_ffn",
    )(tile_expert, n_valid, x_rows, w1g, w1u, b1g, b1u, w2, b2)


def _combine_kernel(pos_ref, y_ref, gate_ref, x_ref, g_ref, b_ref, o_ref, buf_ref, sem):
    i = pl.program_id(0)
    tm = x_ref.shape[0]
    base = i * tm * TOP_K

    def start(t, carry):
        for k in range(TOP_K):
            _row_copy(y_ref, pos_ref[base + t * TOP_K + k], buf_ref.at[k], t, sem).start()
        return carry

    def wait(t, carry):
        for k in range(TOP_K):
            _row_copy(y_ref, pos_ref[base + t * TOP_K + k], buf_ref.at[k], t, sem).wait()
        return carry

    lax.fori_loop(0, tm, start, 0)
    lax.fori_loop(0, tm, wait, 0)
    gates = gate_ref[...]
    acc = DN_ALPHA * x_ref[...]
    for k in range(TOP_K):
        acc = acc + gates[:, k:k + 1] * buf_ref[k]
    xc = acc - jnp.mean(acc, axis=-1, keepdims=True)
    var = jnp.mean(xc * xc, axis=-1, keepdims=True)
    o_ref[...] = xc * lax.rsqrt(var + LN_EPS) * g_ref[...] + b_ref[...]


def _combine_ln(y_rows, pos_flat, gates, x1, g, b):
    m, d = x1.shape
    tm = _row_tile(m, 320)
    row = lambda i, pos: (i, 0)
    fixed = lambda i, pos: (0, 0)
    return pl.pallas_call(
        _combine_kernel,
        grid_spec=pltpu.PrefetchScalarGridSpec(
            num_scalar_prefetch=1,
            grid=(m // tm,),
            in_specs=[pl.BlockSpec(memory_space=pl.ANY), pl.BlockSpec((tm, V7X_LANES), row),
                      pl.BlockSpec((tm, d), row), pl.BlockSpec((1, d), fixed), pl.BlockSpec((1, d), fixed)],
            out_specs=pl.BlockSpec((tm, d), row),
            scratch_shapes=[pltpu.VMEM((TOP_K, tm, d), F32), pltpu.SemaphoreType.DMA(())],
        ),
        out_shape=jax.ShapeDtypeStruct((m, d), F32),
        compiler_params=_params("arbitrary"),
        name="moe_combine_ln",
    )(pos_flat, y_rows, gates, x1, g, b)


def _moe_ln(x1, eids, gates, w1g, w1u, b1g, b1u, w2, b2, g, b):
    m, d = x1.shape
    ranks, counts = _route_ranks(eids)
    counts = counts[0, :N_EXPERTS]
    padded = (counts + EXPERT_TILE - 1) // EXPERT_TILE * EXPERT_TILE
    ends = jnp.cumsum(padded)
    gstart = ends - padded
    n_tiles = -(-m * TOP_K // EXPERT_TILE) + N_EXPERTS
    tile_start = jnp.arange(n_tiles, dtype=I32) * EXPERT_TILE
    tile_expert = jnp.minimum(jnp.searchsorted(ends, tile_start, side='right'), N_EXPERTS - 1).astype(I32)
    n_valid = (ends[-1:] // EXPERT_TILE).astype(I32)
    pos_flat = (gstart[eids[:, :TOP_K]] + ranks[:, :TOP_K]).reshape(-1).astype(I32)
    x_rows = _dispatch_rows(x1, pos_flat, n_tiles * EXPERT_TILE)
    y_rows = _expert_ffn(x_rows, tile_expert, n_valid, w1g, w1u, b1g, b1u, w2, b2)
    return _combine_ln(y_rows, pos_flat, gates, x1, g, b)


def _fox_prep_kernel(t_ref, bias_ref, lf_ref, ccol_ref, crow_ref):
    t_len = t_ref.shape[0]
    r, c = _iota2((V7X_LANES, V7X_LANES))
    tril = jnp.where(c <= r, 1.0, 0.0)
    carry = jnp.zeros((1, V7X_LANES), F32)
    for blk in range(t_len // V7X_LANES):
        rows = slice(blk * V7X_LANES, (blk + 1) * V7X_LANES)
        lf = _log_sigmoid(t_ref[rows, :] + bias_ref[...])
        lf_ref[rows, :] = lf
        cs = jnp.dot(tril, lf, precision=HIGHEST, preferred_element_type=F32) + carry
        ccol_ref[rows, :] = cs
        crow_ref[0, :, rows] = cs.T
        carry = cs[V7X_LANES - 1:, :]


def _fox_prep(tail, bias_row, bsz, t_len):
    n_p = bsz * t_len
    return pl.pallas_call(
        _fox_prep_kernel,
        grid=(bsz,),
        in_specs=[pl.BlockSpec((t_len, V7X_LANES), lambda b: (b, 0)), pl.BlockSpec((1, V7X_LANES), lambda b: (0, 0))],
        out_specs=[pl.BlockSpec((t_len, V7X_LANES), lambda b: (b, 0)), pl.BlockSpec((t_len, V7X_LANES), lambda b: (b, 0)),
                   pl.BlockSpec((1, V7X_LANES, t_len), lambda b: (b, 0, 0))],
        out_shape=[jax.ShapeDtypeStruct((n_p, V7X_LANES), F32), jax.ShapeDtypeStruct((n_p, V7X_LANES), F32),
                   jax.ShapeDtypeStruct((bsz, V7X_LANES, t_len), F32)],
        compiler_params=_params("parallel"),
        name="fox_prep",
    )(tail, bias_row)


def _fox_attn_kernel(q_ref, k_ref, v_ref, ccol_ref, crow_ref, o_ref):
    pair = pl.program_id(1)
    qi = pl.program_id(2)
    tq = q_ref.shape[0]
    q = q_ref[...] * (DH_B ** -0.5)
    ccol = ccol_ref[...]
    lane = lax.broadcasted_iota(I32, (tq, V7X_LANES), 1)
    row_id, col_id = _iota2((tq, tq))
    out = jnp.zeros((tq, V7X_LANES), F32)
    for j in range(2):
        head = 2 * pair + j
        in_head = (lane // DH_B) == j
        qj = jnp.where(in_head, q, 0.0).astype(BF16)
        cq = jnp.sum(jnp.where(lane == head, ccol, 0.0), axis=-1, keepdims=True)

        def block(kb, carry, masked):
            m, l, acc = carry
            start = pl.multiple_of(kb * tq, tq)
            k = k_ref[pl.ds(start, tq), :].astype(BF16)
            v = v_ref[pl.ds(start, tq), :].astype(BF16)
            ck = crow_ref[0, pl.ds(head, 1), pl.ds(start, tq)]
            s = lax.dot_general(qj, k, NT_DIMS, preferred_element_type=F32) + (cq - ck)
            if masked:
                s = jnp.where(col_id <= row_id, s, -jnp.inf)
            m_new = jnp.maximum(m, jnp.max(s, axis=-1, keepdims=True))
            alpha = jnp.exp(m - m_new)
            pe = jnp.exp(s - m_new)
            l = alpha * l + jnp.sum(pe, axis=-1, keepdims=True)
            acc = alpha * acc + jnp.dot(pe.astype(BF16), v, preferred_element_type=F32)
            return m_new, l, acc

        init = (jnp.full((tq, 1), -jnp.inf, F32), jnp.zeros((tq, 1), F32), jnp.zeros((tq, V7X_LANES), F32))
        carry = lax.fori_loop(0, qi, lambda kb, cr: block(kb, cr, False), init)
        m, l, acc = block(qi, carry, True)
        out = jnp.where(in_head, acc / l, out)
    o_ref[...] = out.astype(o_ref.dtype)


def _fox_attention(h_main, ccol, crow, bsz, t_len):
    n_p = bsz * t_len
    tq = ATTN_BLOCK
    nq = t_len // tq
    qcol, kcol, vcol = (4 * D_A) // V7X_LANES, (4 * D_A + D_B) // V7X_LANES, (4 * D_A + 2 * D_B) // V7X_LANES
    return pl.pallas_call(
        _fox_attn_kernel,
        grid=(bsz, H_B // 2, nq),
        in_specs=[pl.BlockSpec((tq, V7X_LANES), lambda b, p, qi: (b * nq + qi, qcol + p)),
                  pl.BlockSpec((t_len, V7X_LANES), lambda b, p, qi: (b, kcol + p)),
                  pl.BlockSpec((t_len, V7X_LANES), lambda b, p, qi: (b, vcol + p)),
                  pl.BlockSpec((tq, V7X_LANES), lambda b, p, qi: (b * nq + qi, 0)),
                  pl.BlockSpec((1, V7X_LANES, t_len), lambda b, p, qi: (b, 0, 0))],
        out_specs=pl.BlockSpec((tq, V7X_LANES), lambda b, p, qi: (b * nq + qi, p)),
        out_shape=jax.ShapeDtypeStruct((n_p, D_B), BF16),
        compiler_params=_params("parallel", "parallel", "arbitrary"),
        name="fox_attention",
    )(h_main, h_main, h_main, ccol, crow)


def _fox_decode_kernel(pt_ref, q_ref, kn_ref, vn_ref, t_ref, bias_ref, *rest):
    np_ = DECODE_PAGES
    k_refs, v_refs, lf_refs = rest[:np_], rest[np_:2 * np_], rest[2 * np_:3 * np_]
    o_ref, lfo_ref, m_ref, l_ref, acc_ref, carry_ref = rest[3 * np_:]
    j = pl.program_id(1)
    scale = DH_B ** -0.5
    q = q_ref[0]
    hrow, hlane = _iota2((H_B, D_B))
    qmat = jnp.where(hlane // DH_B == hrow, jnp.broadcast_to(q, (H_B, D_B)), 0.0) * scale
    lf_new = _log_sigmoid(t_ref[0] + bias_ref[...])

    @pl.when(j == 0)
    def _():
        m_ref[...] = jnp.sum(qmat * kn_ref[0], axis=-1, keepdims=True)
        l_ref[...] = jnp.ones_like(l_ref)
        acc_ref[...] = jnp.broadcast_to(vn_ref[0], acc_ref.shape)
        carry_ref[...] = jnp.zeros_like(carry_ref)
        lfo_ref[0] = lf_new

    r8, c8 = _iota2((H_B, H_B))
    eye8 = jnp.where(r8 == c8, 1.0, 0.0)
    cn = jnp.sum(jnp.where(r8 == c8, jnp.broadcast_to(lf_new[:, :H_B], (H_B, H_B)), 0.0), axis=-1, keepdims=True)
    pr, pc = _iota2((PAGE_SIZE, PAGE_SIZE))
    later = jnp.where(pr > pc, 1.0, 0.0)
    qb = qmat.astype(BF16)
    for i in reversed(range(np_)):
        lft = lax.dot_general(eye8, lf_refs[i][0], NT_DIMS, precision=HIGHEST, preferred_element_type=F32)
        suffix = jnp.dot(lft, later, precision=HIGHEST, preferred_element_type=F32) + carry_ref[...]
        carry_ref[...] = carry_ref[...] + jnp.sum(lft, axis=-1, keepdims=True)
        s = lax.dot_general(qb, k_refs[i][0].astype(BF16), NT_DIMS, preferred_element_type=F32) + cn + suffix
        m_new = jnp.maximum(m_ref[...], jnp.max(s, axis=-1, keepdims=True))
        alpha = jnp.exp(m_ref[...] - m_new)
        pe = jnp.exp(s - m_new)
        l_ref[...] = alpha * l_ref[...] + jnp.sum(pe, axis=-1, keepdims=True)
        acc_ref[...] = alpha * acc_ref[...] + jnp.dot(pe.astype(BF16), v_refs[i][0].astype(BF16),
                                                      preferred_element_type=F32)
        m_ref[...] = m_new

    @pl.when(j == pl.num_programs(1) - 1)
    def _():
        o = jnp.where(hlane // DH_B == hrow, acc_ref[...] / l_ref[...], 0.0)
        o_ref[0] = jnp.sum(o, axis=0, keepdims=True).astype(o_ref.dtype)


def _fox_decode(hs_main, hs_tail, bias_row, cache_k, cache_v, cache_logf, page_table):
    bd = hs_main.shape[0]
    n_pages = page_table.shape[1]
    n_pool = cache_k.shape[0]
    steps = n_pages // DECODE_PAGES
    ck = cache_k.reshape(n_pool, PAGE_SIZE, D_B)
    cv = cache_v.reshape(n_pool, PAGE_SIZE, D_B)
    qcol, kcol, vcol = (4 * D_A) // D_B, (4 * D_A + D_B) // D_B, (4 * D_A + 2 * D_B) // D_B

    def page(i):
        return lambda b, j, pt: (pt[b * n_pages + (steps - 1 - j) * DECODE_PAGES + i], 0, 0)

    tok = lambda col: pl.BlockSpec((1, 1, D_B), lambda b, j, pt: (b, 0, col))
    in_specs = [tok(qcol), tok(kcol), tok(vcol),
                pl.BlockSpec((1, 1, V7X_LANES), lambda b, j, pt: (b, 0, 0)),
                pl.BlockSpec((1, V7X_LANES), lambda b, j, pt: (0, 0))]
    in_specs += [pl.BlockSpec((1, PAGE_SIZE, D_B), page(i)) for i in range(DECODE_PAGES)]
    in_specs += [pl.BlockSpec((1, PAGE_SIZE, D_B), page(i)) for i in range(DECODE_PAGES)]
    in_specs += [pl.BlockSpec((1, PAGE_SIZE, H_B), page(i)) for i in range(DECODE_PAGES)]
    return pl.pallas_call(
        _fox_decode_kernel,
        grid_spec=pltpu.PrefetchScalarGridSpec(
            num_scalar_prefetch=1,
            grid=(bd, steps),
            in_specs=in_specs,
            out_specs=[pl.BlockSpec((1, 1, D_B), lambda b, j, pt: (b, 0, 0)),
                       pl.BlockSpec((1, 1, V7X_LANES), lambda b, j, pt: (b, 0, 0))],
            scratch_shapes=[pltpu.VMEM((H_B, 1), F32), pltpu.VMEM((H_B, 1), F32), pltpu.VMEM((H_B, D_B), F32),
                            pltpu.VMEM((H_B, 1), F32)],
        ),
        out_shape=[jax.ShapeDtypeStruct((bd, 1, D_B), BF16), jax.ShapeDtypeStruct((bd, 1, V7X_LANES), F32)],
        compiler_params=_params("parallel", "arbitrary"),
        name="fox_decode",
    )(page_table.reshape(-1), hs_main, hs_main, hs_main, hs_tail, bias_row,
      *([ck] * DECODE_PAGES), *([cv] * DECODE_PAGES), *([cache_logf] * DECODE_PAGES))


def _hgrn_gates(q, z, lb):
    qa = _silu(q)
    logf = jnp.log(lb + (1.0 - lb) * jax.nn.sigmoid(z))
    ka = (1.0 - lb) * jax.nn.sigmoid(-z)
    return qa, ka, logf


def _hgrn_out(o, g, nw):
    o = o * lax.rsqrt(jnp.mean(o * o, axis=-1, keepdims=True) + RMS_EPS)
    return o * nw * _silu(g)


def _hgrn_kernel(q_ref, f_ref, i_ref, g_ref, lb_ref, nw_ref, o_ref, s_ref, qd_ref, kd_ref, ke_ref, gf_ref, oc_ref,
                 st_ref):
    t_len = q_ref.shape[0]
    cs = CHUNK_A
    lb = lb_ref[0]
    r, c = _iota2((V7X_LANES, V7X_LANES))
    same = (r // cs) == (c // cs)
    incl = jnp.where(same & (c <= r), 1.0, 0.0)
    whole = jnp.where(same, 1.0, 0.0)
    for blk in range(t_len // V7X_LANES):
        rows = slice(blk * V7X_LANES, (blk + 1) * V7X_LANES)
        qa, ka, logf = _hgrn_gates(q_ref[rows, :], f_ref[rows, :], lb)
        b = jnp.dot(incl, logf, precision=HIGHEST, preferred_element_type=F32)
        gtot = jnp.dot(whole, logf, precision=HIGHEST, preferred_element_type=F32)
        qd_ref[rows, :] = qa * jnp.exp(b)
        kd_ref[rows, :] = ka * jnp.exp(-b)
        ke_ref[rows, :] = ka * jnp.exp(gtot - b)
        gf_ref[rows, :] = gtot
    st_ref[...] = jnp.zeros_like(st_ref)
    cr, cc = _iota2((cs, cs))
    causal = cc <= cr

    def chunk(ci, carry):
        start = pl.multiple_of(ci * cs, cs)
        rows = pl.ds(start, cs)
        qd = qd_ref[rows, :].astype(BF16)
        kd = kd_ref[rows, :].astype(BF16)
        ke = ke_ref[rows, :].astype(BF16)
        v = i_ref[rows, :].astype(BF16)
        sc = jnp.where(causal, lax.dot_general(qd, kd, NT_DIMS, preferred_element_type=F32), 0.0)
        st = st_ref[...]
        o = (jnp.dot(sc.astype(BF16), v, preferred_element_type=F32)
             + lax.dot_general(qd, st.astype(BF16), NT_DIMS, preferred_element_type=F32))
        oc_ref[rows, :] = o
        u_t = lax.dot_general(v, ke, TN_DIMS, preferred_element_type=F32)
        st_ref[...] = st * jnp.exp(gf_ref[pl.ds(start, 1), :]) + u_t
        return carry

    lax.fori_loop(0, t_len // cs, chunk, 0, unroll=2)
    for blk in range(t_len // V7X_LANES):
        rows = slice(blk * V7X_LANES, (blk + 1) * V7X_LANES)
        o_ref[rows, :] = _hgrn_out(oc_ref[rows, :], g_ref[rows, :], nw_ref[...]).astype(o_ref.dtype)
    s_ref[0, 0] = st_ref[...].T


def _hgrn_prompt(h_main, lb3, nw_row, bsz, t_len):
    n_p = bsz * t_len
    col = lambda grp: (lambda b, h: (b, grp * H_A + h))
    scr = pltpu.VMEM((t_len, V7X_LANES), F32)
    return pl.pallas_call(
        _hgrn_kernel,
        grid=(bsz, H_A),
        in_specs=[pl.BlockSpec((t_len, DK_A), col(0)), pl.BlockSpec((t_len, DK_A), col(1)),
                  pl.BlockSpec((t_len, DV_A), col(2)), pl.BlockSpec((t_len, DV_A), col(3)),
                  pl.BlockSpec((1, 1, DK_A), lambda b, h: (h, 0, 0)), pl.BlockSpec((1, DV_A), lambda b, h: (0, 0))],
        out_specs=[pl.BlockSpec((t_len, DV_A), lambda b, h: (b, h)),
                   pl.BlockSpec((1, 1, DK_A, DV_A), lambda b, h: (b, h, 0, 0))],
        out_shape=[jax.ShapeDtypeStruct((n_p, D_A), BF16), jax.ShapeDtypeStruct((bsz, H_A, DK_A, DV_A), F32)],
        scratch_shapes=[scr, scr, scr, scr, scr, pltpu.VMEM((DV_A, DK_A), F32)],
        compiler_params=_params("parallel", "parallel"),
        name="hgrn_prompt",
    )(h_main, h_main, h_main, h_main, lb3, nw_row)


def _hgrn_step_kernel(h_ref, lb_ref, nw_ref, s0_ref, o_ref, s_ref):
    for h in range(H_A):
        grp = lambda g: h_ref[0, :, g * D_A + h * DK_A: g * D_A + (h + 1) * DK_A]
        qa, ka, logf = _hgrn_gates(grp(0), grp(1), lb_ref[h])
        v = grp(2)
        s_new = _to_column(jnp.exp(logf)) * s0_ref[0, h] + _to_column(ka) * v
        s_ref[0, h] = s_new
        o = jnp.sum(_to_column(qa) * s_new, axis=0, keepdims=True)
        o_ref[0, :, h * DV_A:(h + 1) * DV_A] = _hgrn_out(o, grp(3), nw_ref[...]).astype(o_ref.dtype)


def _hgrn_step(hs_main, lb3, nw_row, s0):
    bd = hs_main.shape[0]
    return pl.pallas_call(
        _hgrn_step_kernel,
        grid=(bd,),
        in_specs=[pl.BlockSpec((1, 1, 4 * D_A), lambda b: (b, 0, 0)), pl.BlockSpec((H_A, 1, DK_A), lambda b: (0, 0, 0)),
                  pl.BlockSpec((1, DV_A), lambda b: (0, 0)), pl.BlockSpec((1, H_A, DK_A, DV_A), lambda b: (b, 0, 0, 0))],
        out_specs=[pl.BlockSpec((1, 1, D_A), lambda b: (b, 0, 0)),
                   pl.BlockSpec((1, H_A, DK_A, DV_A), lambda b: (b, 0, 0, 0))],
        out_shape=[jax.ShapeDtypeStruct((bd, 1, D_A), BF16), jax.ShapeDtypeStruct(s0.shape, F32)],
        compiler_params=_params("parallel"),
        name="hgrn_step",
    )(hs_main, lb3, nw_row, s0)


def _conv_silu(cur, prev, w, b):
    row8 = lax.broadcasted_iota(I32, prev.shape, 0)
    acc = b + cur * w[CONV_W - 1:CONV_W, :]
    for s in range(1, CONV_W):
        sh = pltpu.roll(cur, s, 0)
        head = jnp.where(row8 < s, pltpu.roll(prev, s, 0), sh[:V7X_SUBLANES, :])
        shifted = jnp.concatenate([head, sh[V7X_SUBLANES:, :]], axis=0)
        acc = acc + shifted * w[CONV_W - 1 - s:CONV_W - s, :]
    return _silu(acc)


def _gated_group_norm(y, z, nw):
    y = y * _silu(z)
    parts = []
    for g in range(N_GROUPS_C):
        seg = y[:, g * GROUP_W:(g + 1) * GROUP_W]
        parts.append(seg * lax.rsqrt(jnp.mean(seg * seg, axis=-1, keepdims=True) + RMS_EPS))
    return jnp.concatenate(parts, axis=-1) * nw


def _ssd_kernel(z_ref, x_ref, bc_ref, dt_ref, cwx_ref, cbx_ref, cwbc_ref, cbbc_ref, dtb_ref, alog_ref, dsk_ref,
                nw_ref, y_ref, hs_ref, tailx_ref, tailbc_ref, ht_ref, e_ref, yacc_ref, xw_ref):
    tb = pl.program_id(1)
    tt = x_ref.shape[0]
    pair_w = 2 * HEADDIM_C
    heads_per_group = H_C // N_GROUPS_C

    @pl.when(tb == 0)
    def _():
        tailx_ref[...] = jnp.zeros_like(tailx_ref)
        tailbc_ref[...] = jnp.zeros_like(tailbc_ref)
        ht_ref[...] = jnp.zeros_like(ht_ref)
        er, ec = _iota2(e_ref.shape)
        e_ref[...] = jnp.where(ec // HEADDIM_C == er, 1.0, 0.0).astype(BF16)

    x_raw = x_ref[...]
    bc_raw = bc_ref[...]
    xs = _conv_silu(x_raw, tailx_ref[...], cwx_ref[...], cbx_ref[...])
    bcv = _conv_silu(bc_raw, tailbc_ref[...], cwbc_ref[...], cbbc_ref[...])
    tailx_ref[...] = x_raw[tt - V7X_SUBLANES:, :]
    tailbc_ref[...] = bc_raw[tt - V7X_SUBLANES:, :]

    lane = lax.broadcasted_iota(I32, (tt, V7X_LANES), 1)
    dt = jnp.where(lane < H_C, _softplus(dt_ref[...] + dtb_ref[...]), 0.0)
    a = -jnp.exp(alog_ref[...])
    r, c = _iota2((tt, tt))
    causal = c <= r
    cum = jnp.dot(jnp.where(causal, 1.0, 0.0), dt * a, precision=HIGHEST, preferred_element_type=F32)
    cum_t = cum.T
    xdt = xs * _expand_heads(dt, e_ref[...])
    low = (lane % pair_w) < HEADDIM_C

    for g in range(N_GROUPS_C):
        b_g = bcv[:, g * D_STATE_C:(g + 1) * D_STATE_C]
        c_g = bcv[:, D_BC + g * D_STATE_C:D_BC + (g + 1) * D_STATE_C].astype(BF16)
        cb = lax.dot_general(c_g, b_g.astype(BF16), NT_DIMS, preferred_element_type=F32)
        y_inter = jnp.dot(c_g, ht_ref[g].astype(BF16), preferred_element_type=F32)
        decs = []
        for pr in range(heads_per_group // 2):
            slab = slice(g * GROUP_W + pr * pair_w, g * GROUP_W + (pr + 1) * pair_w)
            xdt_slab = xdt[:, slab]
            ys, es, tes = [], [], []
            for j in range(2):
                head = g * heads_per_group + pr * 2 + j
                colb = jnp.broadcast_to(cum[:, head:head + 1], (tt, tt))
                decay = jnp.exp(jnp.where(causal, colb - cum_t[head:head + 1, :], -jnp.inf))
                ys.append(jnp.dot((cb * decay).astype(BF16), xdt_slab.astype(BF16), preferred_element_type=F32))
                es.append(jnp.exp(colb))
                tes.append(jnp.exp(colb[tt - 1:, :] - colb))
            e_pair = jnp.where(low, es[0], es[1])
            yacc_ref[:, slab] = (jnp.where(low, ys[0], ys[1]) + e_pair * y_inter[:, pr * pair_w:(pr + 1) * pair_w]
                                 + dsk_ref[:, slab] * xs[:, slab])
            xw_ref[:, pr * pair_w:(pr + 1) * pair_w] = xdt_slab * jnp.where(low, tes[0], tes[1])
            decs.append(e_pair[tt - 1:, :])
        dec_row = jnp.concatenate(decs, axis=-1)
        ht_ref[g] = ht_ref[g] * dec_row + jnp.dot(b_g.T.astype(BF16), xw_ref[...].astype(BF16),
                                                  preferred_element_type=F32)

    y_ref[...] = _gated_group_norm(yacc_ref[...], z_ref[...], nw_ref[...]).astype(y_ref.dtype)

    @pl.when(tb == pl.num_programs(1) - 1)
    def _():
        for g in range(N_GROUPS_C):
            for q in range(GROUP_W // V7X_LANES):
                rows = slice(g * GROUP_W + q * V7X_LANES, g * GROUP_W + (q + 1) * V7X_LANES)
                hs_ref[0, rows, :] = ht_ref[g][:, q * V7X_LANES:(q + 1) * V7X_LANES].T


def _ssd_prompt(h_main, h_tail, conv_w, conv_b, dtb_row, alog_row, dsk_row, nw_row, bsz, t_len):
    n_p = bsz * t_len
    tt = SSD_BLOCK
    nt = t_len // tt
    rowmap = lambda col: (lambda b, t: (b * nt + t, col))
    fixed = lambda b, t: (0, 0)
    cwx, cwbc = conv_w[:, :D_INNER], conv_w[:, D_INNER:]
    cbx, cbbc = conv_b[None, :D_INNER], conv_b[None, D_INNER:]
    return pl.pallas_call(
        _ssd_kernel,
        grid=(bsz, nt),
        in_specs=[pl.BlockSpec((tt, D_INNER), rowmap(0)), pl.BlockSpec((tt, D_INNER), rowmap(1)),
                  pl.BlockSpec((tt, 2 * D_BC), rowmap(2 * D_INNER // (2 * D_BC))),
                  pl.BlockSpec((tt, V7X_LANES), rowmap(0)),
                  pl.BlockSpec((CONV_W, D_INNER), fixed), pl.BlockSpec((1, D_INNER), fixed),
                  pl.BlockSpec((CONV_W, 2 * D_BC), fixed), pl.BlockSpec((1, 2 * D_BC), fixed),
                  pl.BlockSpec((1, V7X_LANES), fixed), pl.BlockSpec((1, V7X_LANES), fixed),
                  pl.BlockSpec((1, D_INNER), fixed), pl.BlockSpec((1, D_INNER), fixed)],
        out_specs=[pl.BlockSpec((tt, D_INNER), rowmap(0)),
                   pl.BlockSpec((1, D_INNER, D_STATE_C), lambda b, t: (b, 0, 0))],
        out_shape=[jax.ShapeDtypeStruct((n_p, D_INNER), BF16), jax.ShapeDtypeStruct((bsz, D_INNER, D_STATE_C), F32)],
        scratch_shapes=[pltpu.VMEM((V7X_SUBLANES, D_INNER), F32), pltpu.VMEM((V7X_SUBLANES, 2 * D_BC), F32),
                        pltpu.VMEM((N_GROUPS_C, D_STATE_C, GROUP_W), F32), pltpu.VMEM((V7X_LANES, D_INNER), BF16),
                        pltpu.VMEM((tt, D_INNER), F32), pltpu.VMEM((tt, GROUP_W), F32)],
        compiler_params=_params("parallel", "arbitrary"),
        name="ssd_prompt",
    )(h_main, h_main, h_main, h_tail, cwx, cbx, cwbc, cbbc, dtb_row, alog_row, dsk_row, nw_row)


def _ssd_step_kernel(h_ref, t_ref, cs_ref, cw_ref, cb_ref, dtb_ref, alog_ref, dsk_ref, nw_ref, h0_ref,
                     y_ref, hn_ref):
    z = h_ref[0, :, :D_INNER]
    xbc_new = h_ref[0, :, D_INNER:]
    cw = cw_ref[...]
    conv = cb_ref[...] + xbc_new * cw[CONV_W - 1:CONV_W, :]
    for j in range(CONV_W - 1):
        conv = conv + cs_ref[0, j:j + 1, :] * cw[j:j + 1, :]
    xbc = _silu(conv)
    xs = xbc[:, :D_INNER]
    lane = lax.broadcasted_iota(I32, (1, V7X_LANES), 1)
    dt = jnp.where(lane < H_C, _softplus(t_ref[0] + dtb_ref[...]), 0.0)
    da = jnp.exp(dt * -jnp.exp(alog_ref[...]))
    er, ec = _iota2((V7X_LANES, D_INNER))
    expand = jnp.where(ec // HEADDIM_C == er, 1.0, 0.0).astype(BF16)
    rows8 = lambda v: jnp.broadcast_to(v, (V7X_SUBLANES, V7X_LANES))
    dt_x = _expand_heads(rows8(dt), expand)[:1, :]
    da_x = _expand_heads(rows8(da), expand)[:1, :]
    xdt = xs * dt_x
    y_parts = []
    for q in range(D_INNER // V7X_LANES):
        g = q // (GROUP_W // V7X_LANES)
        lanes = slice(q * V7X_LANES, (q + 1) * V7X_LANES)
        b_g = xbc[:, D_INNER + g * D_STATE_C:D_INNER + (g + 1) * D_STATE_C]
        c_g = xbc[:, D_INNER + D_BC + g * D_STATE_C:D_INNER + D_BC + (g + 1) * D_STATE_C]
        h_new = _to_column(da_x[:, lanes]) * h0_ref[0, lanes, :] + _to_column(xdt[:, lanes]) * b_g
        hn_ref[0, lanes, :] = h_new
        y_parts.append(_to_row(jnp.sum(h_new * c_g, axis=-1, keepdims=True)))
    y = jnp.concatenate(y_parts, axis=-1) + dsk_ref[...] * xs
    y_ref[0] = _gated_group_norm(y, z, nw_ref[...]).astype(y_ref.dtype)


def _ssd_step(hs_main, hs_tail, conv_state, conv_w, conv_b, dtb_row, alog_row, dsk_row, nw_row, h0):
    bd = hs_main.shape[0]
    fixed = lambda b: (0, 0)
    tok = lambda b: (b, 0, 0)
    return pl.pallas_call(
        _ssd_step_kernel,
        grid=(bd,),
        in_specs=[pl.BlockSpec((1, 1, ODD_MAIN), tok), pl.BlockSpec((1, 1, V7X_LANES), tok),
                  pl.BlockSpec((1, CONV_W - 1, CONV_DIM), tok),
                  pl.BlockSpec((CONV_W, CONV_DIM), fixed), pl.BlockSpec((1, CONV_DIM), fixed),
                  pl.BlockSpec((1, V7X_LANES), fixed), pl.BlockSpec((1, V7X_LANES), fixed),
                  pl.BlockSpec((1, D_INNER), fixed), pl.BlockSpec((1, D_INNER), fixed),
                  pl.BlockSpec((1, D_INNER, D_STATE_C), tok)],
        out_specs=[pl.BlockSpec((1, 1, D_INNER), tok), pl.BlockSpec((1, D_INNER, D_STATE_C), tok)],
        out_shape=[jax.ShapeDtypeStruct((bd, 1, D_INNER), BF16), jax.ShapeDtypeStruct((bd, D_INNER, D_STATE_C), F32)],
        compiler_params=_params("parallel"),
        name="ssd_step",
    )(hs_main, hs_tail, conv_state, conv_w, conv_b[None], dtb_row, alog_row, dsk_row, nw_row, h0)


def _pad_lanes(v):
    return jnp.pad(v, (0, V7X_LANES - v.shape[0]))[None]


def kernel(x_prompt, x_sample, cache_k, cache_v, cache_logf, page_table, state_hgrn, state_ssm, state_conv,
           w_in_even, hgrn_lower_bound, hgrn_norm_w, fox_f_bias, w_out_even,
           w_in_odd, conv_w, conv_b, dt_bias, a_log, d_skip, ssm_norm_w, w_out_odd,
           ln1_g, ln1_b, ln2_g, ln2_b, router_w, router_b, exp_w1, exp_b1, exp_w2, exp_b2):
    lb_all = jnp.cumsum(jax.nn.softmax(hgrn_lower_bound, axis=0), axis=0)
    bp, t_p, d = x_prompt.shape
    bd, t_d, _ = x_sample.shape
    assert t_d == 1
    n_p = bp * t_p
    x_all = jnp.concatenate([x_prompt.reshape(n_p, d), x_sample.reshape(bd, d)], axis=0)
    outs = {}
    for l in range(DEPTH):
        i = l // 2
        if l % 2 == 0:
            h_main = _matmul(x_all, w_in_even[i][:, :EVEN_MAIN])
            h_tail = _matmul(x_all, w_in_even[i][:, EVEN_MAIN:])
            hs_main, hs_tail = h_main[n_p:, None, :], h_tail[n_p:, None, :]
            lb3 = lb_all[i].reshape(H_A, 1, DK_A)
            nw_row = hgrn_norm_w[i][None]
            bias_row = _pad_lanes(fox_f_bias[i])
            lf_p, ccol, crow = _fox_prep(h_tail, bias_row, bp, t_p)
            ob_p = _fox_attention(h_main, ccol, crow, bp, t_p)
            oa_p, hg_p = _hgrn_prompt(h_main, lb3, nw_row, bp, t_p)
            oa_s, hg_s = _hgrn_step(hs_main, lb3, nw_row, state_hgrn[i])
            ob_s, lf_s = _fox_decode(hs_main, hs_tail, bias_row, cache_k[i], cache_v[i], cache_logf[i], page_table)
            mix = jnp.concatenate([jnp.concatenate([oa_p, ob_p], axis=1),
                                   jnp.concatenate([oa_s[:, 0], ob_s[:, 0]], axis=1)], axis=0)
            kcol, vcol = 4 * D_A + D_B, 4 * D_A + 2 * D_B
            vals = (("kp", h_main[:n_p, kcol:kcol + D_B].reshape(bp, t_p, H_B, DH_B)),
                    ("ks", h_main[n_p:, kcol:kcol + D_B].reshape(bd, t_d, H_B, DH_B)),
                    ("vp", h_main[:n_p, vcol:vcol + D_B].reshape(bp, t_p, H_B, DH_B)),
                    ("vs", h_main[n_p:, vcol:vcol + D_B].reshape(bd, t_d, H_B, DH_B)),
                    ("lfp", lf_p[:, :H_B].reshape(bp, t_p, H_B)), ("lfs", lf_s[:, :, :H_B]),
                    ("hgp", hg_p), ("hgs", hg_s))
            w_out = w_out_even[i]
        else:
            h_main = _matmul(x_all, w_in_odd[i][:, :ODD_MAIN])
            h_tail = _matmul(x_all, w_in_odd[i][:, ODD_MAIN:])
            hs_main, hs_tail = h_main[n_p:, None, :], h_tail[n_p:, None, :]
            dtb_row, alog_row = _pad_lanes(dt_bias[i]), _pad_lanes(a_log[i])
            dsk_row = jnp.repeat(d_skip[i], HEADDIM_C)[None]
            nw_row = ssm_norm_w[i][None]
            y_p, ss_p = _ssd_prompt(h_main, h_tail, conv_w[i], conv_b[i], dtb_row, alog_row, dsk_row, nw_row, bp, t_p)
            y_s, ss_s = _ssd_step(hs_main, hs_tail, state_conv[i], conv_w[i], conv_b[i], dtb_row, alog_row, dsk_row,
                                  nw_row, state_ssm[i].reshape(bd, D_INNER, D_STATE_C))
            mix = jnp.concatenate([y_p, y_s[:, 0]], axis=0)
            xbc_p = h_main[:n_p, D_INNER:].reshape(bp, t_p, CONV_DIM)
            vals = (("ssp", ss_p.reshape(bp, H_C, HEADDIM_C, D_STATE_C)),
                    ("sss", ss_s.reshape(bd, H_C, HEADDIM_C, D_STATE_C)),
                    ("cvp", xbc_p[:, t_p - (CONV_W - 1):]),
                    ("cvs", jnp.concatenate([state_conv[i][:, 1:], hs_main[:, :, D_INNER:]], axis=1)))
            w_out = w_out_odd[i]
        for name, val in vals:
            outs.setdefault(name, []).append(val)
        rw = jnp.pad(router_w[l], ((0, 0), (0, V7X_LANES - N_EXPERTS)))
        rb = jnp.concatenate([router_b[l], jnp.full((V7X_LANES - N_EXPERTS,), -jnp.inf, F32)])[None]
        x1, eids, gates = _post_mixer(x_all, mix, w_out.astype(BF16), ln1_g[l][None], ln1_b[l][None], rw, rb)
        w1g, w1u = _w1_prep(exp_w1[l])
        b1g = exp_b1[l][:, None, 0::2]
        b1u = exp_b1[l][:, None, 1::2]
        x_all = _moe_ln(x1, eids, gates, w1g, w1u, b1g, b1u, exp_w2[l].astype(BF16), exp_b2[l][:, None, :],
                        ln2_g[l][None], ln2_b[l][None])
    st = {k: jnp.stack(v) for k, v in outs.items()}
    return (x_all[:n_p].reshape(bp, t_p, d), x_all[n_p:].reshape(bd, t_d, d),
            st["kp"], st["ks"], st["vp"], st["vs"], st["lfp"], st["lfs"],
            st["hgp"], st["hgs"], st["ssp"], st["sss"], st["cvp"], st["cvs"])
```

```python
import functools

import jax
import jax.numpy as jnp
import numpy as np
from jax import lax
from jax.experimental import pallas as pl
from jax.experimental.pallas import tpu as pltpu

F32 = jnp.float32
BF16 = jnp.bfloat16
I32 = jnp.int32
HIGHEST = lax.Precision.HIGHEST

D_MODEL = 1024
DEPTH = 2
PAGE_SIZE = 128
H_A, DK_A, DV_A, CHUNK_A = 4, 128, 128, 32
H_B, DH_B = 8, 64
D_A = H_A * DK_A
D_B = H_B * DH_B
EVEN_MAIN = 4 * D_A + 3 * D_B
D_INNER = 2 * D_MODEL
HEADDIM_C = 64
H_C = D_INNER // HEADDIM_C
N_GROUPS_C = 4
D_STATE_C = 128
D_BC = N_GROUPS_C * D_STATE_C
GROUP_W = D_INNER // N_GROUPS_C
CONV_W = 4
CONV_DIM = D_INNER + 2 * D_BC
ODD_MAIN = D_INNER + CONV_DIM
N_EXPERTS = 32
TOP_K = 4
D_FF = D_MODEL
SWIGLU_LIMIT = 7.0
SWIGLU_ALPHA = 1.702
DN_ALPHA = (2 * DEPTH) ** 0.25
LN_EPS = 1e-5
RMS_EPS = 1e-6

V7X_LANES = 128
V7X_SUBLANES = 8
V7X_MXU = 256
VMEM_LIMIT = 56 * 1024 * 1024
EXPERT_TILE = 512
ATTN_BLOCK = 256
SSD_BLOCK = 128
DECODE_PAGES = 8
DMA_UNROLL = 8

NT_DIMS = (((1,), (1,)), ((), ()))
TN_DIMS = (((0,), (0,)), ((), ()))


def _row_tile(m, cap):
    best = 0
    for t in range(16, cap + 1, 16):
        if m % t == 0:
            best = t
    assert best, (m, cap)
    return best


def _params(*sem):
    return pltpu.CompilerParams(dimension_semantics=sem, vmem_limit_bytes=VMEM_LIMIT)


def _silu(x):
    return x * jax.nn.sigmoid(x)


def _softplus(x):
    return jnp.maximum(x, 0.0) + jnp.log1p(jnp.exp(-jnp.abs(x)))


def _log_sigmoid(x):
    return -_softplus(-x)


def _iota2(shape):
    return lax.broadcasted_iota(I32, shape, 0), lax.broadcasted_iota(I32, shape, 1)


def _to_column(row_vec):
    n = row_vec.shape[1]
    r, c = _iota2((n, n))
    return jnp.sum(jnp.where(r == c, jnp.broadcast_to(row_vec, (n, n)), 0.0), axis=1, keepdims=True)


def _to_row(col_vec):
    n = col_vec.shape[0]
    r, c = _iota2((n, n))
    return jnp.sum(jnp.where(r == c, jnp.broadcast_to(col_vec, (n, n)), 0.0), axis=0, keepdims=True)


def _expand_heads(v, e_bf16):
    hi = v.astype(BF16)
    r1 = v - hi.astype(F32)
    mid = r1.astype(BF16)
    lo = (r1 - mid.astype(F32)).astype(BF16)
    dot = lambda a: jnp.dot(a, e_bf16, preferred_element_type=F32)
    return dot(hi) + dot(mid) + dot(lo)


def _mm_kernel(x_ref, w_ref, o_ref):
    o_ref[...] = jnp.dot(x_ref[...].astype(BF16), w_ref[...].astype(BF16), preferred_element_type=F32)


def _matmul(x, w, tm_cap=1024, tn=512):
    m, k = x.shape
    n = w.shape[1]
    n_pad = -(-n // V7X_LANES) * V7X_LANES
    if n_pad != n:
        w = jnp.pad(w, ((0, 0), (0, n_pad - n)))
    tm = _row_tile(m, tm_cap)
    tn = tn if n_pad % tn == 0 else (V7X_MXU if n_pad % V7X_MXU == 0 else V7X_LANES)
    return pl.pallas_call(
        _mm_kernel,
        grid=(m // tm, n_pad // tn),
        in_specs=[pl.BlockSpec((tm, k), lambda i, j: (i, 0)),
                  pl.BlockSpec((k, tn), lambda i, j: (0, j))],
        out_specs=pl.BlockSpec((tm, tn), lambda i, j: (i, j)),
        out_shape=jax.ShapeDtypeStruct((m, n_pad), F32),
        compiler_params=_params("parallel", "parallel"),
        name="dense_matmul",
    )(x, w)


def _post_mixer_kernel(x_ref, lhs_ref, w_ref, g_ref, b_ref, rw_ref, rb_ref, x1_ref, eid_ref, gate_ref):
    acc = DN_ALPHA * x_ref[...] + jnp.dot(lhs_ref[...].astype(BF16), w_ref[...], preferred_element_type=F32)
    xc = acc - jnp.mean(acc, axis=-1, keepdims=True)
    var = jnp.mean(xc * xc, axis=-1, keepdims=True)
    x1 = xc * lax.rsqrt(var + LN_EPS) * g_ref[...] + b_ref[...]
    x1_ref[...] = x1
    logits = jnp.dot(x1.astype(BF16), rw_ref[...].astype(BF16), preferred_element_type=F32) + rb_ref[...]
    lane = lax.broadcasted_iota(I32, logits.shape, 1)
    eids = jnp.zeros(logits.shape, I32)
    vals = []
    for k in range(TOP_K):
        top = jnp.max(logits, axis=-1, keepdims=True)
        idx = jnp.min(jnp.where(logits == top, lane, V7X_LANES), axis=-1, keepdims=True)
        vals.append(top)
        eids = jnp.where(lane == k, idx, eids)
        logits = jnp.where(lane == idx, -jnp.inf, logits)
    exps = [jnp.exp(v - vals[0]) for v in vals]
    denom = exps[0] + exps[1] + exps[2] + exps[3]
    gates = jnp.zeros(logits.shape, F32)
    for k in range(TOP_K):
        gates = jnp.where(lane == k, exps[k] / denom, gates)
    eid_ref[...] = eids
    gate_ref[...] = gates


def _post_mixer(x, lhs, w_bf16, g, b, rw, rb):
    m, d = x.shape
    k = lhs.shape[1]
    tm = _row_tile(m, 640)
    row = lambda i: (i, 0)
    fixed = lambda i: (0, 0)
    return pl.pallas_call(
        _post_mixer_kernel,
        grid=(m // tm,),
        in_specs=[pl.BlockSpec((tm, d), row), pl.BlockSpec((tm, k), row), pl.BlockSpec((k, d), fixed),
                  pl.BlockSpec((1, d), fixed), pl.BlockSpec((1, d), fixed),
                  pl.BlockSpec((d, V7X_LANES), fixed), pl.BlockSpec((1, V7X_LANES), fixed)],
        out_specs=[pl.BlockSpec((tm, d), row), pl.BlockSpec((tm, V7X_LANES), row),
                   pl.BlockSpec((tm, V7X_LANES), row)],
        out_shape=[jax.ShapeDtypeStruct((m, d), F32), jax.ShapeDtypeStruct((m, V7X_LANES), I32),
                   jax.ShapeDtypeStruct((m, V7X_LANES), F32)],
        compiler_params=_params("parallel"),
        name="post_mixer",
    )(x, lhs, w_bf16, g, b, rw, rb)


def _rank_kernel(eid_ref, rank_ref, cnt_ref, carry_ref):
    i = pl.program_id(0)

    @pl.when(i == 0)
    def _():
        carry_ref[...] = jnp.zeros_like(carry_ref)

    eids = eid_ref[...]
    tm = eids.shape[0]
    lane = lax.broadcasted_iota(I32, eids.shape, 1)
    sel = [jnp.sum(jnp.where(lane == k, eids, 0), axis=-1, keepdims=True) for k in range(TOP_K)]
    onehot = jnp.zeros(eids.shape, F32)
    for k in range(TOP_K):
        onehot = onehot + (lane == sel[k]).astype(F32)
    r, c = _iota2((tm, tm))
    before = (c < r).astype(BF16)
    prior = jnp.dot(before, onehot.astype(BF16), preferred_element_type=F32) + carry_ref[...]
    ranks = jnp.zeros(eids.shape, F32)
    for k in range(TOP_K):
        rk = jnp.sum(jnp.where(lane == sel[k], prior, 0.0), axis=-1, keepdims=True)
        ranks = jnp.where(lane == k, rk, ranks)
    rank_ref[...] = ranks.astype(I32)
    total = carry_ref[...] + jnp.sum(onehot, axis=0, keepdims=True)
    carry_ref[...] = total
    cnt_ref[...] = total.astype(I32)


def _route_ranks(eids):
    m = eids.shape[0]
    tm = _row_tile(m, 640)
    return pl.pallas_call(
        _rank_kernel,
        grid=(m // tm,),
        in_specs=[pl.BlockSpec((tm, V7X_LANES), lambda i: (i, 0))],
        out_specs=[pl.BlockSpec((tm, V7X_LANES), lambda i: (i, 0)), pl.BlockSpec((1, V7X_LANES), lambda i: (0, 0))],
        out_shape=[jax.ShapeDtypeStruct((m, V7X_LANES), I32), jax.ShapeDtypeStruct((1, V7X_LANES), I32)],
        scratch_shapes=[pltpu.VMEM((1, V7X_LANES), F32)],
        compiler_params=_params("arbitrary"),
        name="route_ranks",
    )(eids)


def _row_copy(src_ref, s, dst_ref, d, sem):
    return pltpu.make_async_copy(src_ref.at[pl.ds(s, 1)], dst_ref.at[pl.ds(d, 1)], sem)


def _dispatch_kernel(pos_ref, x_ref, out_ref, sem):
    i = pl.program_id(0)
    tm = x_ref.shape[0]
    base = i * tm * TOP_K

    def start(t, carry):
        for k in range(TOP_K):
            _row_copy(x_ref, t, out_ref, pos_ref[base + t * TOP_K + k], sem).start(priority=k % 2)
        return carry

    def wait(t, carry):
        for k in range(TOP_K):
            _row_copy(x_ref, t, out_ref, pos_ref[base + t * TOP_K + k], sem).wait()
        return carry

    lax.fori_loop(0, tm, start, 0, unroll=DMA_UNROLL)
    lax.fori_loop(0, tm, wait, 0, unroll=DMA_UNROLL)


def _dispatch_rows(x1, pos_flat, n_rows):
    m, d = x1.shape
    tm = _row_tile(m, 1024)
    return pl.pallas_call(
        _dispatch_kernel,
        grid_spec=pltpu.PrefetchScalarGridSpec(
            num_scalar_prefetch=1,
            grid=(m // tm,),
            in_specs=[pl.BlockSpec((tm, d), lambda i, pos: (i, 0))],
            out_specs=pl.BlockSpec(memory_space=pl.ANY),
            scratch_shapes=[pltpu.SemaphoreType.DMA(())],
        ),
        out_shape=jax.ShapeDtypeStruct((n_rows, d), x1.dtype),
        compiler_params=_params("arbitrary"),
        name="moe_dispatch",
    )(pos_flat, x1)


def _w1_prep_kernel(w_ref, g_ref, u_ref):
    r, c = _iota2((V7X_MXU, V7X_MXU))
    half = V7X_MXU // 2
    src = jnp.where(c < half, 2 * c, 2 * (c - half) + 1)
    perm = jnp.where(r == src, 1.0, 0.0).astype(BF16)
    for j in range(w_ref.shape[2] // V7X_MXU):
        blk = w_ref[0, :, j * V7X_MXU:(j + 1) * V7X_MXU].astype(BF16)
        res = jnp.dot(blk, perm, preferred_element_type=F32).astype(BF16)
        g_ref[0, :, j * half:(j + 1) * half] = res[:, :half]
        u_ref[0, :, j * half:(j + 1) * half] = res[:, half:]


def _w1_prep(w1):
    e, k, n2 = w1.shape
    tk = 512
    out = jax.ShapeDtypeStruct((e, k, n2 // 2), BF16)
    return pl.pallas_call(
        _w1_prep_kernel,
        grid=(e, k // tk),
        in_specs=[pl.BlockSpec((1, tk, n2), lambda i, j: (i, j, 0))],
        out_specs=[pl.BlockSpec((1, tk, n2 // 2), lambda i, j: (i, j, 0))] * 2,
        out_shape=[out, out],
        compiler_params=_params("parallel", "parallel"),
        name="w1_prep",
    )(w1)


def _ffn_kernel(te_ref, tr_ref, nv_ref, x_ref, w1g_ref, w1u_ref, b1g_ref, b1u_ref, w2_ref, b2_ref, o_ref):
    i = pl.program_id(0)

    @pl.when(i < nv_ref[0])
    def _():
        row = lax.broadcasted_iota(I32, x_ref.shape, 0)
        x = jnp.where(row < tr_ref[i], x_ref[...], 0.0).astype(BF16)
        hg = jnp.dot(x, w1g_ref[0], preferred_element_type=F32) + b1g_ref[0]
        hu = jnp.dot(x, w1u_ref[0], preferred_element_type=F32) + b1u_ref[0]
        gate = jnp.minimum(hg, SWIGLU_LIMIT)
        up = jnp.clip(hu, -SWIGLU_LIMIT, SWIGLU_LIMIT)
        act = (up + 1.0) * gate * jax.nn.sigmoid(SWIGLU_ALPHA * gate)
        o_ref[...] = jnp.dot(act.astype(BF16), w2_ref[0], preferred_element_type=F32) + b2_ref[0]


def _expert_ffn(x_rows, tile_expert, tile_rows, n_valid, w1g, w1u, b1g, b1u, w2, b2):
    rows, d = x_rows.shape
    tm = EXPERT_TILE
    rmap = lambda i, te, tr, nv: (jnp.minimum(i, nv[0] - 1), 0)
    wmap = lambda i, te, tr, nv: (te[i], 0, 0)
    return pl.pallas_call(
        _ffn_kernel,
        grid_spec=pltpu.PrefetchScalarGridSpec(
            num_scalar_prefetch=3,
            grid=(rows // tm,),
            in_specs=[pl.BlockSpec((tm, d), rmap),
                      pl.BlockSpec((1, d, D_FF), wmap), pl.BlockSpec((1, d, D_FF), wmap),
                      pl.BlockSpec((1, 1, D_FF), wmap), pl.BlockSpec((1, 1, D_FF), wmap),
                      pl.BlockSpec((1, D_FF, d), wmap), pl.BlockSpec((1, 1, d), wmap)],
            out_specs=pl.BlockSpec((tm, d), rmap),
        ),
        out_shape=jax.ShapeDtypeStruct((rows, d), F32),
        compiler_params=_params("arbitrary"),
        name="moe_expert_ffn",
    )(tile_expert, tile_rows, n_valid, x_rows, w1g, w1u, b1g, b1u, w2, b2)


def _combine_kernel(pos_ref, y_ref, gate_ref, x_ref, g_ref, b_ref, o_ref, buf_ref, sem):
    i = pl.program_id(0)
    tm = x_ref.shape[0]
    base = i * tm * TOP_K

    def start(t, carry):
        for k in range(TOP_K):
            _row_copy(y_ref, pos_ref[base + t * TOP_K + k], buf_ref.at[k], t, sem).start(priority=k % 2)
        return carry

    def wait(t, carry):
        for k in range(TOP_K):
            _row_copy(y_ref, pos_ref[base + t * TOP_K + k], buf_ref.at[k], t, sem).wait()
        return carry

    lax.fori_loop(0, tm, start, 0, unroll=DMA_UNROLL)
    lax.fori_loop(0, tm, wait, 0, unroll=DMA_UNROLL)
    gates = gate_ref[...]
    acc = DN_ALPHA * x_ref[...]
    for k in range(TOP_K):
        acc = acc + gates[:, k:k + 1] * buf_ref[k]
    xc = acc - jnp.mean(acc, axis=-1, keepdims=True)
    var = jnp.mean(xc * xc, axis=-1, keepdims=True)
    o_ref[...] = xc * lax.rsqrt(var + LN_EPS) * g_ref[...] + b_ref[...]


def _combine_ln(y_rows, pos_flat, gates, x1, g, b):
    m, d = x1.shape
    tm = _row_tile(m, 320)
    row = lambda i, pos: (i, 0)
    fixed = lambda i, pos: (0, 0)
    return pl.pallas_call(
        _combine_kernel,
        grid_spec=pltpu.PrefetchScalarGridSpec(
            num_scalar_prefetch=1,
            grid=(m // tm,),
            in_specs=[pl.BlockSpec(memory_space=pl.ANY), pl.BlockSpec((tm, V7X_LANES), row),
                      pl.BlockSpec((tm, d), row), pl.BlockSpec((1, d), fixed), pl.BlockSpec((1, d), fixed)],
            out_specs=pl.BlockSpec((tm, d), row),
            scratch_shapes=[pltpu.VMEM((TOP_K, tm, d), F32), pltpu.SemaphoreType.DMA(())],
        ),
        out_shape=jax.ShapeDtypeStruct((m, d), F32),
        compiler_params=_params("arbitrary"),
        name="moe_combine_ln",
    )(pos_flat, y_rows, gates, x1, g, b)


def _moe_ln(x1, eids, gates, w1g, w1u, b1g, b1u, w2, b2, g, b):
    m, d = x1.shape
    ranks, counts = _route_ranks(eids)
    counts = counts[0, :N_EXPERTS]
    padded = (counts + EXPERT_TILE - 1) // EXPERT_TILE * EXPERT_TILE
    ends = jnp.cumsum(padded)
    gstart = ends - padded
    n_tiles = -(-m * TOP_K // EXPERT_TILE) + N_EXPERTS
    tile_start = jnp.arange(n_tiles, dtype=I32) * EXPERT_TILE
    tile_expert = jnp.minimum(jnp.sum((tile_start[:, None] >= ends[None, :]).astype(I32), axis=1), N_EXPERTS - 1)
    onehot_t = tile_expert[:, None] == jnp.arange(N_EXPERTS, dtype=I32)[None, :]
    used = jnp.sum(jnp.where(onehot_t, (gstart + counts)[None, :], 0), axis=1)
    tile_rows = jnp.clip(used - tile_start, 0, EXPERT_TILE).astype(I32)
    n_valid = (ends[-1:] // EXPERT_TILE).astype(I32)
    sel = eids[:, :TOP_K, None] == jnp.arange(N_EXPERTS, dtype=I32)[None, None, :]
    pos_flat = (jnp.sum(jnp.where(sel, gstart[None, None, :], 0), axis=-1) + ranks[:, :TOP_K]).reshape(-1).astype(I32)
    x_rows = _dispatch_rows(x1, pos_flat, n_tiles * EXPERT_TILE)
    y_rows = _expert_ffn(x_rows, tile_expert, tile_rows, n_valid, w1g, w1u, b1g, b1u, w2, b2)
    return _combine_ln(y_rows, pos_flat, gates, x1, g, b)


def _fox_prep_kernel(t_ref, bias_ref, lf_ref, ccol_ref):
    t_len = t_ref.shape[0]
    r, c = _iota2((V7X_LANES, V7X_LANES))
    tril = jnp.where(c <= r, 1.0, 0.0)
    carry = jnp.zeros((1, V7X_LANES), F32)
    for blk in range(t_len // V7X_LANES):
        rows = slice(blk * V7X_LANES, (blk + 1) * V7X_LANES)
        lf = _log_sigmoid(t_ref[rows, :] + bias_ref[...])
        lf_ref[rows, :] = lf
        cs = jnp.dot(tril, lf, precision=HIGHEST, preferred_element_type=F32) + carry
        ccol_ref[rows, :] = cs
        carry = cs[V7X_LANES - 1:, :]


def _fox_prep(tail, bias_row, bsz, t_len):
    n_p = bsz * t_len
    blk = pl.BlockSpec((t_len, V7X_LANES), lambda b: (b, 0))
    out = jax.ShapeDtypeStruct((n_p, V7X_LANES), F32)
    return pl.pallas_call(
        _fox_prep_kernel,
        grid=(bsz,),
        in_specs=[blk, pl.BlockSpec((1, V7X_LANES), lambda b: (0, 0))],
        out_specs=[blk, blk],
        out_shape=[out, out],
        compiler_params=_params("parallel"),
        name="fox_prep",
    )(tail, bias_row)


def _fox_attn_kernel(q_ref, k_ref, v_ref, ccol_ref, o_ref, kb_ref, vt_ref, cb0_ref, cb1_ref, m0_ref, m1_ref, l0_ref, l1_ref,
                     acc0_ref, acc1_ref, s0_ref, s1_ref):
    pair = pl.program_id(1)
    qi = pl.program_id(2)
    tq = q_ref.shape[0]
    t_len = k_ref.shape[0]
    cb_refs, m_refs, l_refs, acc_refs = (cb0_ref, cb1_ref), (m0_ref, m1_ref), (l0_ref, l1_ref), (acc0_ref, acc1_ref)
    s_refs = (s0_ref, s1_ref)

    @pl.when(qi == 0)
    def _():
        kb_ref[...] = k_ref[...].astype(BF16)
        sr, sc = _iota2((V7X_LANES, V7X_LANES))
        for j in range(2):
            sel = jnp.where(sr == 2 * pair + j, 1.0, 0.0).astype(BF16)
            cb_refs[j][...] = _expand_heads(ccol_ref[...], sel)
        for blk in range(t_len // V7X_LANES):
            rows = slice(blk * V7X_LANES, (blk + 1) * V7X_LANES)
            vt_ref[:, rows] = v_ref[rows, :].T.astype(BF16)

    qt = (q_ref[...] * (DH_B ** -0.5)).T
    feat = lax.broadcasted_iota(I32, qt.shape, 0)
    qh = [jnp.where((feat // DH_B) == j, qt, 0.0).astype(BF16) for j in range(2)]
    key_id, qry_id = _iota2((tq, tq))
    for j in range(2):
        m_refs[j][...] = jnp.full(m_refs[j].shape, -jnp.inf, F32)
        l_refs[j][...] = jnp.zeros_like(l_refs[j])
        acc_refs[j][...] = jnp.zeros_like(acc_refs[j])

    def scores(kb, j):
        start = pl.multiple_of(kb * tq, tq)
        cb = cb_refs[j][pl.ds(start, tq), :]
        return (jnp.dot(kb_ref[pl.ds(start, tq), :], qh[j], preferred_element_type=F32)
                - jnp.concatenate([cb] * (tq // V7X_LANES), axis=1))

    def update(kb, j, s):
        start = pl.multiple_of(kb * tq, tq)
        m_old = m_refs[j][...]
        m_new = jnp.maximum(m_old, jnp.max(s, axis=0, keepdims=True))
        alpha = jnp.exp(m_old - m_new)
        pe = jnp.exp(s - m_new)
        l_refs[j][...] = alpha * l_refs[j][...] + jnp.sum(pe, axis=0, keepdims=True)
        acc_refs[j][...] = alpha * acc_refs[j][...] + jnp.dot(vt_ref[:, pl.ds(start, tq)], pe.astype(BF16),
                                                              preferred_element_type=F32)
        m_refs[j][...] = m_new

    for j in range(2):
        s_refs[j][...] = scores(0, j)

    def body(kb, carry):
        for j in range(2):
            s = s_refs[j][...]
            s_refs[j][...] = scores(kb + 1, j)
            update(kb, j, s)
        return carry

    lax.fori_loop(0, qi, body, 0)
    for j in range(2):
        update(qi, j, jnp.where(key_id <= qry_id, s_refs[j][...], -jnp.inf))
    out_t = jnp.where((feat // DH_B) == 0, acc0_ref[...] / l0_ref[...], acc1_ref[...] / l1_ref[...])
    o_ref[...] = out_t.T.astype(o_ref.dtype)


def _fox_attention(h_main, ccol, bsz, t_len):
    n_p = bsz * t_len
    tq = ATTN_BLOCK
    nq = t_len // tq
    qcol, kcol, vcol = (4 * D_A) // V7X_LANES, (4 * D_A + D_B) // V7X_LANES, (4 * D_A + 2 * D_B) // V7X_LANES
    cbs, row, acc = pltpu.VMEM((t_len, V7X_LANES), F32), pltpu.VMEM((1, tq), F32), pltpu.VMEM((V7X_LANES, tq), F32)
    return pl.pallas_call(
        _fox_attn_kernel,
        grid=(bsz, H_B // 2, nq),
        in_specs=[pl.BlockSpec((tq, V7X_LANES), lambda b, p, qi: (b * nq + qi, qcol + p)),
                  pl.BlockSpec((t_len, V7X_LANES), lambda b, p, qi: (b, kcol + p)),
                  pl.BlockSpec((t_len, V7X_LANES), lambda b, p, qi: (b, vcol + p)),
                  pl.BlockSpec((t_len, V7X_LANES), lambda b, p, qi: (b, 0))],
        out_specs=pl.BlockSpec((tq, V7X_LANES), lambda b, p, qi: (b * nq + qi, p)),
        out_shape=jax.ShapeDtypeStruct((n_p, D_B), BF16),
        scratch_shapes=[pltpu.VMEM((t_len, V7X_LANES), BF16), pltpu.VMEM((V7X_LANES, t_len), BF16),
                        cbs, cbs, row, row, row, row, acc, acc, pltpu.VMEM((tq, tq), F32), pltpu.VMEM((tq, tq), F32)],
        compiler_params=_params("parallel", "parallel", "arbitrary"),
        name="fox_attention",
    )(h_main, h_main, h_main, ccol)


def _fox_decode_kernel(pt_ref, q_ref, kn_ref, vn_ref, t_ref, bias_ref, *rest):
    np_ = DECODE_PAGES
    k_refs, v_refs, lf_refs = rest[:np_], rest[np_:2 * np_], rest[2 * np_:3 * np_]
    o_ref, lfo_ref, m_ref, l_ref, acc_ref, carry_ref = rest[3 * np_:]
    j = pl.program_id(1)
    flat = PAGE_SIZE * H_B
    q = q_ref[0] * (DH_B ** -0.5)
    lf_new = _log_sigmoid(t_ref[0] + bias_ref[...])

    @pl.when(j == 0)
    def _():
        m_ref[...] = jnp.sum(q * kn_ref[0], axis=-1, keepdims=True)
        l_ref[...] = jnp.ones_like(l_ref)
        acc_ref[...] = vn_ref[0]
        carry_ref[...] = jnp.zeros_like(carry_ref)
        lfo_ref[0] = lf_new

    r8, c8 = _iota2((H_B, H_B))
    cn = jnp.sum(jnp.where(r8 == c8, jnp.broadcast_to(lf_new[:, :H_B], (H_B, H_B)), 0.0), axis=-1, keepdims=True)
    hrow, col = _iota2((H_B, flat))
    own = (col % H_B) == hrow
    kr, kc = _iota2((PAGE_SIZE, flat))
    later = jnp.where(kr > kc // H_B, 1.0, 0.0).astype(BF16)
    qb = q.astype(BF16)
    lfts = [lf_refs[i][0] for i in range(np_)]
    carries = [None] * np_
    run = carry_ref[...]
    for i in reversed(range(np_)):
        carries[i] = run
        run = run + jnp.sum(lfts[i], axis=-1, keepdims=True)
    carry_ref[...] = run
    scores = []
    for i in range(np_):
        lf = lfts[i]
        hi = lf.astype(BF16)
        r1 = lf - hi.astype(F32)
        mid = r1.astype(BF16)
        lo = (r1 - mid.astype(F32)).astype(BF16)
        parts = jnp.dot(jnp.concatenate([hi, mid, lo], axis=0), later, preferred_element_type=F32)
        suffix = parts[:H_B] + parts[H_B:2 * H_B] + parts[2 * H_B:]
        k2 = k_refs[i][0].reshape(flat, DH_B).astype(BF16)
        s = lax.dot_general(qb, k2, NT_DIMS, preferred_element_type=F32) + (suffix + (cn + carries[i]))
        scores.append(jnp.where(own, s, -jnp.inf))
    m_old = m_ref[...]
    m_new = m_old
    for s in scores:
        m_new = jnp.maximum(m_new, jnp.max(s, axis=-1, keepdims=True))
    alpha = jnp.exp(m_old - m_new)
    l_new = alpha * l_ref[...]
    acc = alpha * acc_ref[...]
    for i, s in enumerate(scores):
        pe = jnp.exp(s - m_new)
        l_new = l_new + jnp.sum(pe, axis=-1, keepdims=True)
        v2 = v_refs[i][0].reshape(flat, DH_B).astype(BF16)
        acc = acc + jnp.dot(pe.astype(BF16), v2, preferred_element_type=F32)
    m_ref[...] = m_new
    l_ref[...] = l_new
    acc_ref[...] = acc

    @pl.when(j == pl.num_programs(1) - 1)
    def _():
        o_ref[0] = (acc / l_new).astype(o_ref.dtype)


def _fox_decode(q3, k3, v3, hs_tail, bias_row, cache_k, cache_v, logf_t, page_table):
    bd = q3.shape[0]
    n_pages = page_table.shape[1]
    steps = n_pages // DECODE_PAGES

    def page(i, nd):
        return lambda b, j, pt: (pt[b * n_pages + (steps - 1 - j) * DECODE_PAGES + i],) + (0,) * nd

    tok = pl.BlockSpec((1, H_B, DH_B), lambda b, j, pt: (b, 0, 0))
    in_specs = [tok, tok, tok,
                pl.BlockSpec((1, 1, V7X_LANES), lambda b, j, pt: (b, 0, 0)),
                pl.BlockSpec((1, V7X_LANES), lambda b, j, pt: (0, 0))]
    in_specs += [pl.BlockSpec((1, PAGE_SIZE, H_B, DH_B), page(i, 3)) for i in range(DECODE_PAGES)]
    in_specs += [pl.BlockSpec((1, PAGE_SIZE, H_B, DH_B), page(i, 3)) for i in range(DECODE_PAGES)]
    in_specs += [pl.BlockSpec((1, H_B, PAGE_SIZE), page(i, 2)) for i in range(DECODE_PAGES)]
    return pl.pallas_call(
        _fox_decode_kernel,
        grid_spec=pltpu.PrefetchScalarGridSpec(
            num_scalar_prefetch=1,
            grid=(bd, steps),
            in_specs=in_specs,
            out_specs=[pl.BlockSpec((1, H_B, DH_B), lambda b, j, pt: (b, 0, 0)),
                       pl.BlockSpec((1, 1, V7X_LANES), lambda b, j, pt: (b, 0, 0))],
            scratch_shapes=[pltpu.VMEM((H_B, 1), F32), pltpu.VMEM((H_B, 1), F32), pltpu.VMEM((H_B, DH_B), F32),
                            pltpu.VMEM((H_B, 1), F32)],
        ),
        out_shape=[jax.ShapeDtypeStruct((bd, H_B, DH_B), BF16), jax.ShapeDtypeStruct((bd, 1, V7X_LANES), F32)],
        compiler_params=_params("parallel", "arbitrary"),
        name="fox_decode",
    )(page_table.reshape(-1), q3, k3, v3, hs_tail, bias_row,
      *([cache_k] * DECODE_PAGES), *([cache_v] * DECODE_PAGES), *([logf_t] * DECODE_PAGES))


def _hgrn_gates(q, z, lb):
    qa = _silu(q)
    logf = jnp.log(lb + (1.0 - lb) * jax.nn.sigmoid(z))
    ka = (1.0 - lb) * jax.nn.sigmoid(-z)
    return qa, ka, logf


def _hgrn_out(o, g, nw):
    o = o * lax.rsqrt(jnp.mean(o * o, axis=-1, keepdims=True) + RMS_EPS)
    return o * nw * _silu(g)


def _hgrn_kernel(q_ref, f_ref, i_ref, g_ref, lb_ref, nw_ref, o_ref, s_ref, qd_ref, kd_ref, ke_ref, gf_ref, oc_ref,
                 st_ref):
    t_len = q_ref.shape[0]
    cs = CHUNK_A
    lb = lb_ref[0]
    r, c = _iota2((V7X_LANES, V7X_LANES))
    same = (r // cs) == (c // cs)
    incl = jnp.where(same & (c <= r), 1.0, 0.0)
    whole = jnp.where(same, 1.0, 0.0)
    for blk in range(t_len // V7X_LANES):
        rows = slice(blk * V7X_LANES, (blk + 1) * V7X_LANES)
        qa, ka, logf = _hgrn_gates(q_ref[rows, :], f_ref[rows, :], lb)
        b = jnp.dot(incl, logf, precision=HIGHEST, preferred_element_type=F32)
        gtot = jnp.dot(whole, logf, precision=HIGHEST, preferred_element_type=F32)
        qd_ref[rows, :] = qa * jnp.exp(b)
        kd_ref[rows, :] = ka * jnp.exp(-b)
        ke_ref[rows, :] = ka * jnp.exp(gtot - b)
        gf_ref[rows, :] = gtot
    st_ref[...] = jnp.zeros_like(st_ref)
    cr, cc = _iota2((cs, cs))
    causal = cc <= cr

    def chunk(ci, carry):
        start = pl.multiple_of(ci * cs, cs)
        rows = pl.ds(start, cs)
        qd = qd_ref[rows, :].astype(BF16)
        kd = kd_ref[rows, :].astype(BF16)
        ke = ke_ref[rows, :].astype(BF16)
        v = i_ref[rows, :].astype(BF16)
        sc = jnp.where(causal, lax.dot_general(qd, kd, NT_DIMS, preferred_element_type=F32), 0.0)
        st = st_ref[...]
        o = (jnp.dot(sc.astype(BF16), v, preferred_element_type=F32)
             + lax.dot_general(qd, st.astype(BF16), NT_DIMS, preferred_element_type=F32))
        oc_ref[rows, :] = o
        u_t = lax.dot_general(v, ke, TN_DIMS, preferred_element_type=F32)
        st_ref[...] = st * jnp.exp(gf_ref[pl.ds(start, 1), :]) + u_t
        return carry

    lax.fori_loop(0, t_len // cs, chunk, 0, unroll=2)
    for blk in range(t_len // V7X_LANES):
        rows = slice(blk * V7X_LANES, (blk + 1) * V7X_LANES)
        o_ref[rows, :] = _hgrn_out(oc_ref[rows, :], g_ref[rows, :], nw_ref[...]).astype(o_ref.dtype)
    s_ref[0, 0] = st_ref[...].T


def _hgrn_prompt(h_main, lb3, nw_row, bsz, t_len):
    n_p = bsz * t_len
    col = lambda grp: (lambda b, h: (b, grp * H_A + h))
    scr = pltpu.VMEM((t_len, V7X_LANES), F32)
    return pl.pallas_call(
        _hgrn_kernel,
        grid=(bsz, H_A),
        in_specs=[pl.BlockSpec((t_len, DK_A), col(0)), pl.BlockSpec((t_len, DK_A), col(1)),
                  pl.BlockSpec((t_len, DV_A), col(2)), pl.BlockSpec((t_len, DV_A), col(3)),
                  pl.BlockSpec((1, 1, DK_A), lambda b, h: (h, 0, 0)), pl.BlockSpec((1, DV_A), lambda b, h: (0, 0))],
        out_specs=[pl.BlockSpec((t_len, DV_A), lambda b, h: (b, h)),
                   pl.BlockSpec((1, 1, DK_A, DV_A), lambda b, h: (b, h, 0, 0))],
        out_shape=[jax.ShapeDtypeStruct((n_p, D_A), BF16), jax.ShapeDtypeStruct((bsz, H_A, DK_A, DV_A), F32)],
        scratch_shapes=[scr, scr, scr, scr, scr, pltpu.VMEM((DV_A, DK_A), F32)],
        compiler_params=_params("parallel", "parallel"),
        name="hgrn_prompt",
    )(h_main, h_main, h_main, h_main, lb3, nw_row)


def _hgrn_step_kernel(h_ref, lb_ref, nw_ref, s0_ref, o_ref, s_ref):
    for h in range(H_A):
        grp = lambda g: h_ref[0, :, g * D_A + h * DK_A: g * D_A + (h + 1) * DK_A]
        qa, ka, logf = _hgrn_gates(grp(0), grp(1), lb_ref[h])
        v = grp(2)
        s_new = _to_column(jnp.exp(logf)) * s0_ref[0, h] + _to_column(ka) * v
        s_ref[0, h] = s_new
        o = jnp.sum(_to_column(qa) * s_new, axis=0, keepdims=True)
        o_ref[0, :, h * DV_A:(h + 1) * DV_A] = _hgrn_out(o, grp(3), nw_ref[...]).astype(o_ref.dtype)


def _hgrn_step(hs_main, lb3, nw_row, s0):
    bd = hs_main.shape[0]
    return pl.pallas_call(
        _hgrn_step_kernel,
        grid=(bd,),
        in_specs=[pl.BlockSpec((1, 1, 4 * D_A), lambda b: (b, 0, 0)), pl.BlockSpec((H_A, 1, DK_A), lambda b: (0, 0, 0)),
                  pl.BlockSpec((1, DV_A), lambda b: (0, 0)), pl.BlockSpec((1, H_A, DK_A, DV_A), lambda b: (b, 0, 0, 0))],
        out_specs=[pl.BlockSpec((1, 1, D_A), lambda b: (b, 0, 0)),
                   pl.BlockSpec((1, H_A, DK_A, DV_A), lambda b: (b, 0, 0, 0))],
        out_shape=[jax.ShapeDtypeStruct((bd, 1, D_A), BF16), jax.ShapeDtypeStruct(s0.shape, F32)],
        compiler_params=_params("parallel"),
        name="hgrn_step",
    )(hs_main, lb3, nw_row, s0)


def _conv_silu(cur, prev, w, b):
    row8 = lax.broadcasted_iota(I32, prev.shape, 0)
    acc = b + cur * w[CONV_W - 1:CONV_W, :]
    for s in range(1, CONV_W):
        sh = pltpu.roll(cur, s, 0)
        head = jnp.where(row8 < s, pltpu.roll(prev, s, 0), sh[:V7X_SUBLANES, :])
        shifted = jnp.concatenate([head, sh[V7X_SUBLANES:, :]], axis=0)
        acc = acc + shifted * w[CONV_W - 1 - s:CONV_W - s, :]
    return _silu(acc)


def _gated_group_norm(y, z, nw):
    y = y * _silu(z)
    parts = []
    for g in range(N_GROUPS_C):
        seg = y[:, g * GROUP_W:(g + 1) * GROUP_W]
        parts.append(seg * lax.rsqrt(jnp.mean(seg * seg, axis=-1, keepdims=True) + RMS_EPS))
    return jnp.concatenate(parts, axis=-1) * nw


def _ssd_kernel(z_ref, x_ref, bc_ref, dt_ref, cwx_ref, cbx_ref, cwbc_ref, cbbc_ref, dtb_ref, alog_ref, dsk_ref,
                nw_ref, y_ref, hs_ref, tailx_ref, tailbc_ref, ht_ref, e_ref, yacc_ref, xw_ref):
    tb = pl.program_id(1)
    tt = x_ref.shape[0]
    pair_w = 2 * HEADDIM_C
    heads_per_group = H_C // N_GROUPS_C

    @pl.when(tb == 0)
    def _():
        tailx_ref[...] = jnp.zeros_like(tailx_ref)
        tailbc_ref[...] = jnp.zeros_like(tailbc_ref)
        ht_ref[...] = jnp.zeros_like(ht_ref)
        er, ec = _iota2(e_ref.shape)
        e_ref[...] = jnp.where(ec // HEADDIM_C == er, 1.0, 0.0).astype(BF16)

    x_raw = x_ref[...]
    bc_raw = bc_ref[...]
    xs = _conv_silu(x_raw, tailx_ref[...], cwx_ref[...], cbx_ref[...])
    bcv = _conv_silu(bc_raw, tailbc_ref[...], cwbc_ref[...], cbbc_ref[...])
    tailx_ref[...] = x_raw[tt - V7X_SUBLANES:, :]
    tailbc_ref[...] = bc_raw[tt - V7X_SUBLANES:, :]

    lane = lax.broadcasted_iota(I32, (tt, V7X_LANES), 1)
    dt = jnp.where(lane < H_C, _softplus(dt_ref[...] + dtb_ref[...]), 0.0)
    a = -jnp.exp(alog_ref[...])
    r, c = _iota2((tt, tt))
    causal = c <= r
    cum = jnp.dot(jnp.where(causal, 1.0, 0.0), dt * a, precision=HIGHEST, preferred_element_type=F32)
    cum_t = cum.T
    xdt = xs * _expand_heads(dt, e_ref[...])
    low = (lane % pair_w) < HEADDIM_C

    for g in range(N_GROUPS_C):
        b_g = bcv[:, g * D_STATE_C:(g + 1) * D_STATE_C]
        c_g = bcv[:, D_BC + g * D_STATE_C:D_BC + (g + 1) * D_STATE_C].astype(BF16)
        cb = lax.dot_general(c_g, b_g.astype(BF16), NT_DIMS, preferred_element_type=F32)
        y_inter = jnp.dot(c_g, ht_ref[g].astype(BF16), preferred_element_type=F32)
        decs = []
        for pr in range(heads_per_group // 2):
            slab = slice(g * GROUP_W + pr * pair_w, g * GROUP_W + (pr + 1) * pair_w)
            xdt_slab = xdt[:, slab]
            ys, es, tes = [], [], []
            for j in range(2):
                head = g * heads_per_group + pr * 2 + j
                colb = jnp.broadcast_to(cum[:, head:head + 1], (tt, tt))
                decay = jnp.exp(jnp.where(causal, colb - cum_t[head:head + 1, :], -jnp.inf))
                ys.append(jnp.dot((cb * decay).astype(BF16), xdt_slab.astype(BF16), preferred_element_type=F32))
                es.append(jnp.exp(colb))
                tes.append(jnp.exp(colb[tt - 1:, :] - colb))
            e_pair = jnp.where(low, es[0], es[1])
            yacc_ref[:, slab] = (jnp.where(low, ys[0], ys[1]) + e_pair * y_inter[:, pr * pair_w:(pr + 1) * pair_w]
                                 + dsk_ref[:, slab] * xs[:, slab])
            xw_ref[:, pr * pair_w:(pr + 1) * pair_w] = xdt_slab * jnp.where(low, tes[0], tes[1])
            decs.append(e_pair[tt - 1:, :])
        dec_row = jnp.concatenate(decs, axis=-1)
        ht_ref[g] = ht_ref[g] * dec_row + jnp.dot(b_g.T.astype(BF16), xw_ref[...].astype(BF16),
                                                  preferred_element_type=F32)

    y_ref[...] = _gated_group_norm(yacc_ref[...], z_ref[...], nw_ref[...]).astype(y_ref.dtype)

    @pl.when(tb == pl.num_programs(1) - 1)
    def _():
        for g in range(N_GROUPS_C):
            for q in range(GROUP_W // V7X_LANES):
                rows = slice(g * GROUP_W + q * V7X_LANES, g * GROUP_W + (q + 1) * V7X_LANES)
                hs_ref[0, rows, :] = ht_ref[g][:, q * V7X_LANES:(q + 1) * V7X_LANES].T


def _ssd_prompt(h_main, h_tail, conv_w, conv_b, dtb_row, alog_row, dsk_row, nw_row, bsz, t_len):
    n_p = bsz * t_len
    tt = SSD_BLOCK
    nt = t_len // tt
    rowmap = lambda col: (lambda b, t: (b * nt + t, col))
    fixed = lambda b, t: (0, 0)
    cwx, cwbc = conv_w[:, :D_INNER], conv_w[:, D_INNER:]
    cbx, cbbc = conv_b[None, :D_INNER], conv_b[None, D_INNER:]
    return pl.pallas_call(
        _ssd_kernel,
        grid=(bsz, nt),
        in_specs=[pl.BlockSpec((tt, D_INNER), rowmap(0)), pl.BlockSpec((tt, D_INNER), rowmap(1)),
                  pl.BlockSpec((tt, 2 * D_BC), rowmap(2 * D_INNER // (2 * D_BC))),
                  pl.BlockSpec((tt, V7X_LANES), rowmap(0)),
                  pl.BlockSpec((CONV_W, D_INNER), fixed), pl.BlockSpec((1, D_INNER), fixed),
                  pl.BlockSpec((CONV_W, 2 * D_BC), fixed), pl.BlockSpec((1, 2 * D_BC), fixed),
                  pl.BlockSpec((1, V7X_LANES), fixed), pl.BlockSpec((1, V7X_LANES), fixed),
                  pl.BlockSpec((1, D_INNER), fixed), pl.BlockSpec((1, D_INNER), fixed)],
        out_specs=[pl.BlockSpec((tt, D_INNER), rowmap(0)),
                   pl.BlockSpec((1, D_INNER, D_STATE_C), lambda b, t: (b, 0, 0))],
        out_shape=[jax.ShapeDtypeStruct((n_p, D_INNER), BF16), jax.ShapeDtypeStruct((bsz, D_INNER, D_STATE_C), F32)],
        scratch_shapes=[pltpu.VMEM((V7X_SUBLANES, D_INNER), F32), pltpu.VMEM((V7X_SUBLANES, 2 * D_BC), F32),
                        pltpu.VMEM((N_GROUPS_C, D_STATE_C, GROUP_W), F32), pltpu.VMEM((V7X_LANES, D_INNER), BF16),
                        pltpu.VMEM((tt, D_INNER), F32), pltpu.VMEM((tt, GROUP_W), F32)],
        compiler_params=_params("parallel", "arbitrary"),
        name="ssd_prompt",
    )(h_main, h_main, h_main, h_tail, cwx, cbx, cwbc, cbbc, dtb_row, alog_row, dsk_row, nw_row)


def _ssd_step_kernel(h_ref, t_ref, cs_ref, cw_ref, cb_ref, dtb_ref, alog_ref, dsk_ref, nw_ref, h0_ref,
                     y_ref, hn_ref):
    z = h_ref[0, :, :D_INNER]
    xbc_new = h_ref[0, :, D_INNER:]
    cw = cw_ref[...]
    conv = cb_ref[...] + xbc_new * cw[CONV_W - 1:CONV_W, :]
    for j in range(CONV_W - 1):
        conv = conv + cs_ref[0, j:j + 1, :] * cw[j:j + 1, :]
    xbc = _silu(conv)
    xs = xbc[:, :D_INNER]
    lane = lax.broadcasted_iota(I32, (1, V7X_LANES), 1)
    dt = jnp.where(lane < H_C, _softplus(t_ref[0] + dtb_ref[...]), 0.0)
    da = jnp.exp(dt * -jnp.exp(alog_ref[...]))
    er, ec = _iota2((V7X_LANES, D_INNER))
    expand = jnp.where(ec // HEADDIM_C == er, 1.0, 0.0).astype(BF16)
    rows8 = lambda v: jnp.broadcast_to(v, (V7X_SUBLANES, V7X_LANES))
    dt_x = _expand_heads(rows8(dt), expand)[:1, :]
    da_x = _expand_heads(rows8(da), expand)[:1, :]
    xdt = xs * dt_x
    y_parts = []
    for q in range(D_INNER // V7X_LANES):
        g = q // (GROUP_W // V7X_LANES)
        lanes = slice(q * V7X_LANES, (q + 1) * V7X_LANES)
        b_g = xbc[:, D_INNER + g * D_STATE_C:D_INNER + (g + 1) * D_STATE_C]
        c_g = xbc[:, D_INNER + D_BC + g * D_STATE_C:D_INNER + D_BC + (g + 1) * D_STATE_C]
        h_new = _to_column(da_x[:, lanes]) * h0_ref[0, lanes, :] + _to_column(xdt[:, lanes]) * b_g
        hn_ref[0, lanes, :] = h_new
        y_parts.append(_to_row(jnp.sum(h_new * c_g, axis=-1, keepdims=True)))
    y = jnp.concatenate(y_parts, axis=-1) + dsk_ref[...] * xs
    y_ref[0] = _gated_group_norm(y, z, nw_ref[...]).astype(y_ref.dtype)


def _ssd_step(hs_main, hs_tail, conv_state, conv_w, conv_b, dtb_row, alog_row, dsk_row, nw_row, h0):
    bd = hs_main.shape[0]
    fixed = lambda b: (0, 0)
    tok = lambda b: (b, 0, 0)
    return pl.pallas_call(
        _ssd_step_kernel,
        grid=(bd,),
        in_specs=[pl.BlockSpec((1, 1, ODD_MAIN), tok), pl.BlockSpec((1, 1, V7X_LANES), tok),
                  pl.BlockSpec((1, CONV_W - 1, CONV_DIM), tok),
                  pl.BlockSpec((CONV_W, CONV_DIM), fixed), pl.BlockSpec((1, CONV_DIM), fixed),
                  pl.BlockSpec((1, V7X_LANES), fixed), pl.BlockSpec((1, V7X_LANES), fixed),
                  pl.BlockSpec((1, D_INNER), fixed), pl.BlockSpec((1, D_INNER), fixed),
                  pl.BlockSpec((1, D_INNER, D_STATE_C), tok)],
        out_specs=[pl.BlockSpec((1, 1, D_INNER), tok), pl.BlockSpec((1, D_INNER, D_STATE_C), tok)],
        out_shape=[jax.ShapeDtypeStruct((bd, 1, D_INNER), BF16), jax.ShapeDtypeStruct((bd, D_INNER, D_STATE_C), F32)],
        compiler_params=_params("parallel"),
        name="ssd_step",
    )(hs_main, hs_tail, conv_state, conv_w, conv_b[None], dtb_row, alog_row, dsk_row, nw_row, h0)


def _pad_lanes(v):
    return jnp.pad(v, (0, V7X_LANES - v.shape[0]))[None]


def kernel(x_prompt, x_sample, cache_k, cache_v, cache_logf, page_table, state_hgrn, state_ssm, state_conv,
           w_in_even, hgrn_lower_bound, hgrn_norm_w, fox_f_bias, w_out_even,
           w_in_odd, conv_w, conv_b, dt_bias, a_log, d_skip, ssm_norm_w, w_out_odd,
           ln1_g, ln1_b, ln2_g, ln2_b, router_w, router_b, exp_w1, exp_b1, exp_w2, exp_b2):
    lb_all = jnp.cumsum(jax.nn.softmax(hgrn_lower_bound, axis=0), axis=0)
    bp, t_p, d = x_prompt.shape
    bd, t_d, _ = x_sample.shape
    assert t_d == 1
    n_p = bp * t_p
    x_all = jnp.concatenate([x_prompt.reshape(n_p, d), x_sample.reshape(bd, d)], axis=0)
    outs = {}
    for l in range(DEPTH):
        i = l // 2
        if l % 2 == 0:
            h_main = _matmul(x_all, w_in_even[i][:, :EVEN_MAIN])
            h_tail = _matmul(x_all, w_in_even[i][:, EVEN_MAIN:])
            hs_main, hs_tail = h_main[n_p:, None, :], h_tail[n_p:, None, :]
            lb3 = lb_all[i].reshape(H_A, 1, DK_A)
            nw_row = hgrn_norm_w[i][None]
            bias_row = _pad_lanes(fox_f_bias[i])
            lf_p, ccol = _fox_prep(h_tail, bias_row, bp, t_p)
            ob_p = _fox_attention(h_main, ccol, bp, t_p)
            oa_p, hg_p = _hgrn_prompt(h_main, lb3, nw_row, bp, t_p)
            oa_s, hg_s = _hgrn_step(hs_main, lb3, nw_row, state_hgrn[i])
            qcol, kcol, vcol = 4 * D_A, 4 * D_A + D_B, 4 * D_A + 2 * D_B
            q_s, k_s, v_s = (h_main[n_p:, c:c + D_B].reshape(bd, H_B, DH_B) for c in (qcol, kcol, vcol))
            ob_s, lf_s = _fox_decode(q_s, k_s, v_s, hs_tail, bias_row, cache_k[i], cache_v[i],
                                     jnp.swapaxes(cache_logf[i], 1, 2), page_table)
            mix = jnp.concatenate([jnp.concatenate([oa_p, ob_p], axis=1),
                                   jnp.concatenate([oa_s[:, 0], ob_s.reshape(bd, D_B)], axis=1)], axis=0)
            vals = (("kp", h_main[:n_p, kcol:kcol + D_B].reshape(bp, t_p, H_B, DH_B)),
                    ("ks", k_s.reshape(bd, t_d, H_B, DH_B)),
                    ("vp", h_main[:n_p, vcol:vcol + D_B].reshape(bp, t_p, H_B, DH_B)),
                    ("vs", v_s.reshape(bd, t_d, H_B, DH_B)),
                    ("lfp", lf_p[:, :H_B].reshape(bp, t_p, H_B)), ("lfs", lf_s[:, :, :H_B]),
                    ("hgp", hg_p), ("hgs", hg_s))
            w_out = w_out_even[i]
        else:
            h_main = _matmul(x_all, w_in_odd[i][:, :ODD_MAIN])
            h_tail = _matmul(x_all, w_in_odd[i][:, ODD_MAIN:])
            hs_main, hs_tail = h_main[n_p:, None, :], h_tail[n_p:, None, :]
            dtb_row, alog_row = _pad_lanes(dt_bias[i]), _pad_lanes(a_log[i])
            dsk_row = jnp.repeat(d_skip[i], HEADDIM_C)[None]
            nw_row = ssm_norm_w[i][None]
            y_p, ss_p = _ssd_prompt(h_main, h_tail, conv_w[i], conv_b[i], dtb_row, alog_row, dsk_row, nw_row, bp, t_p)
            y_s, ss_s = _ssd_step(hs_main, hs_tail, state_conv[i], conv_w[i], conv_b[i], dtb_row, alog_row, dsk_row,
                                  nw_row, state_ssm[i].reshape(bd, D_INNER, D_STATE_C))
            mix = jnp.concatenate([y_p, y_s[:, 0]], axis=0)
            xbc_p = h_main[:n_p, D_INNER:].reshape(bp, t_p, CONV_DIM)
            vals = (("ssp", ss_p.reshape(bp, H_C, HEADDIM_C, D_STATE_C)),
                    ("sss", ss_s.reshape(bd, H_C, HEADDIM_C, D_STATE_C)),
                    ("cvp", xbc_p[:, t_p - (CONV_W - 1):]),
                    ("cvs", jnp.concatenate([state_conv[i][:, 1:], hs_main[:, :, D_INNER:]], axis=1)))
            w_out = w_out_odd[i]
        for name, val in vals:
            outs.setdefault(name, []).append(val)
        rw = jnp.pad(router_w[l], ((0, 0), (0, V7X_LANES - N_EXPERTS)))
        rb = jnp.concatenate([router_b[l], jnp.full((V7X_LANES - N_EXPERTS,), -jnp.inf, F32)])[None]
        x1, eids, gates = _post_mixer(x_all, mix, w_out.astype(BF16), ln1_g[l][None], ln1_b[l][None], rw, rb)
        w1g, w1u = _w1_prep(exp_w1[l])
        b1g = exp_b1[l][:, None, 0::2]
        b1u = exp_b1[l][:, None, 1::2]
        x_all = _moe_ln(x1, eids, gates, w1g, w1u, b1g, b1u, exp_w2[l].astype(BF16), exp_b2[l][:, None, :],
                        ln2_g[l][None], ln2_b[l][None])
    st = {k: jnp.stack(v) for k, v in outs.items()}
    return (x_all[:n_p].reshape(bp, t_p, d), x_all[n_p:].reshape(bd, t_d, d),
            st["kp"], st["ks"], st["vp"], st["vs"], st["lfp"], st["lfs"],
            st["hgp"], st["hgs"], st["ssp"], st["sss"], st["cvp"], st["cvs"])
```

```python
import functools

import jax
import jax.numpy as jnp
import numpy as np
from jax import lax
from jax.experimental import pallas as pl
from jax.experimental.pallas import tpu as pltpu

F32 = jnp.float32
BF16 = jnp.bfloat16
I32 = jnp.int32
HIGHEST = lax.Precision.HIGHEST

D_MODEL = 1024
DEPTH = 2
PAGE_SIZE = 128
H_A, DK_A, DV_A, CHUNK_A = 4, 128, 128, 32
H_B, DH_B = 8, 64
D_A = H_A * DK_A
D_B = H_B * DH_B
EVEN_MAIN = 4 * D_A + 3 * D_B
D_INNER = 2 * D_MODEL
HEADDIM_C = 64
H_C = D_INNER // HEADDIM_C
N_GROUPS_C = 4
D_STATE_C = 128
D_BC = N_GROUPS_C * D_STATE_C
GROUP_W = D_INNER // N_GROUPS_C
CONV_W = 4
CONV_DIM = D_INNER + 2 * D_BC
ODD_MAIN = D_INNER + CONV_DIM
N_EXPERTS = 32
TOP_K = 4
D_FF = D_MODEL
SWIGLU_LIMIT = 7.0
SWIGLU_ALPHA = 1.702
DN_ALPHA = (2 * DEPTH) ** 0.25
LN_EPS = 1e-5
RMS_EPS = 1e-6

V7X_LANES = 128
V7X_SUBLANES = 8
V7X_MXU = 256
VMEM_LIMIT = 56 * 1024 * 1024
EXPERT_TILE = 512
ATTN_BLOCK = 256
SSD_BLOCK = 128
DECODE_PAGES = 8
DMA_UNROLL = 8

NT_DIMS = (((1,), (1,)), ((), ()))
TN_DIMS = (((0,), (0,)), ((), ()))


def _row_tile(m, cap):
    best = 0
    for t in range(16, cap + 1, 16):
        if m % t == 0:
            best = t
    assert best, (m, cap)
    return best


def _params(*sem):
    return pltpu.CompilerParams(dimension_semantics=sem, vmem_limit_bytes=VMEM_LIMIT)


def _silu(x):
    return x * jax.nn.sigmoid(x)


def _softplus(x):
    return jnp.maximum(x, 0.0) + jnp.log1p(jnp.exp(-jnp.abs(x)))


def _log_sigmoid(x):
    return -_softplus(-x)


def _iota2(shape):
    return lax.broadcasted_iota(I32, shape, 0), lax.broadcasted_iota(I32, shape, 1)


def _to_column(row_vec):
    n = row_vec.shape[1]
    r, c = _iota2((n, n))
    return jnp.sum(jnp.where(r == c, jnp.broadcast_to(row_vec, (n, n)), 0.0), axis=1, keepdims=True)


def _to_row(col_vec):
    n = col_vec.shape[0]
    r, c = _iota2((n, n))
    return jnp.sum(jnp.where(r == c, jnp.broadcast_to(col_vec, (n, n)), 0.0), axis=0, keepdims=True)


def _expand_heads(v, e_bf16):
    hi = v.astype(BF16)
    r1 = v - hi.astype(F32)
    mid = r1.astype(BF16)
    lo = (r1 - mid.astype(F32)).astype(BF16)
    dot = lambda a: jnp.dot(a, e_bf16, preferred_element_type=F32)
    return dot(hi) + dot(mid) + dot(lo)


def _mm_kernel(x_ref, w_ref, o_ref):
    o_ref[...] = jnp.dot(x_ref[...].astype(BF16), w_ref[...].astype(BF16), preferred_element_type=F32)


def _matmul(x, w, tm_cap=1024, tn=512):
    m, k = x.shape
    n = w.shape[1]
    n_pad = -(-n // V7X_LANES) * V7X_LANES
    if n_pad != n:
        w = jnp.pad(w, ((0, 0), (0, n_pad - n)))
    tm = _row_tile(m, tm_cap)
    tn = tn if n_pad % tn == 0 else (V7X_MXU if n_pad % V7X_MXU == 0 else V7X_LANES)
    return pl.pallas_call(
        _mm_kernel,
        grid=(m // tm, n_pad // tn),
        in_specs=[pl.BlockSpec((tm, k), lambda i, j: (i, 0)),
                  pl.BlockSpec((k, tn), lambda i, j: (0, j))],
        out_specs=pl.BlockSpec((tm, tn), lambda i, j: (i, j)),
        out_shape=jax.ShapeDtypeStruct((m, n_pad), F32),
        compiler_params=_params("parallel", "parallel"),
        name="dense_matmul",
    )(x, w)


def _proj_t_kernel(w_ref, x_ref, o_ref):
    o_ref[0] = lax.dot_general(w_ref[...], x_ref[...].astype(BF16), NT_DIMS, preferred_element_type=F32)


def _proj_t(x, w_t_bf16, bsz, t_len, tm=512):
    n, k = w_t_bf16.shape
    nt = t_len // tm
    return pl.pallas_call(
        _proj_t_kernel,
        grid=(bsz, nt),
        in_specs=[pl.BlockSpec((n, k), lambda b, t: (0, 0)), pl.BlockSpec((tm, k), lambda b, t: (b * nt + t, 0))],
        out_specs=pl.BlockSpec((1, n, tm), lambda b, t: (b, 0, t)),
        out_shape=jax.ShapeDtypeStruct((bsz, n, t_len), F32),
        compiler_params=_params("parallel", "parallel"),
        name="proj_transposed",
    )(w_t_bf16, x)


def _post_mixer_kernel(x_ref, lhs_ref, w_ref, g_ref, b_ref, rw_ref, rb_ref, x1_ref, eid_ref, gate_ref):
    acc = DN_ALPHA * x_ref[...] + jnp.dot(lhs_ref[...].astype(BF16), w_ref[...], preferred_element_type=F32)
    xc = acc - jnp.mean(acc, axis=-1, keepdims=True)
    var = jnp.mean(xc * xc, axis=-1, keepdims=True)
    x1 = xc * lax.rsqrt(var + LN_EPS) * g_ref[...] + b_ref[...]
    x1_ref[...] = x1
    logits = jnp.dot(x1.astype(BF16), rw_ref[...].astype(BF16), preferred_element_type=F32) + rb_ref[...]
    lane = lax.broadcasted_iota(I32, logits.shape, 1)
    eids = jnp.zeros(logits.shape, I32)
    vals = []
    for k in range(TOP_K):
        top = jnp.max(logits, axis=-1, keepdims=True)
        idx = jnp.min(jnp.where(logits == top, lane, V7X_LANES), axis=-1, keepdims=True)
        vals.append(top)
        eids = jnp.where(lane == k, idx, eids)
        logits = jnp.where(lane == idx, -jnp.inf, logits)
    exps = [jnp.exp(v - vals[0]) for v in vals]
    denom = exps[0] + exps[1] + exps[2] + exps[3]
    gates = jnp.zeros(logits.shape, F32)
    for k in range(TOP_K):
        gates = jnp.where(lane == k, exps[k] / denom, gates)
    eid_ref[...] = eids
    gate_ref[...] = gates


def _post_mixer(x, lhs, w_bf16, g, b, rw, rb):
    m, d = x.shape
    k = lhs.shape[1]
    tm = _row_tile(m, 640)
    row = lambda i: (i, 0)
    fixed = lambda i: (0, 0)
    return pl.pallas_call(
        _post_mixer_kernel,
        grid=(m // tm,),
        in_specs=[pl.BlockSpec((tm, d), row), pl.BlockSpec((tm, k), row), pl.BlockSpec((k, d), fixed),
                  pl.BlockSpec((1, d), fixed), pl.BlockSpec((1, d), fixed),
                  pl.BlockSpec((d, V7X_LANES), fixed), pl.BlockSpec((1, V7X_LANES), fixed)],
        out_specs=[pl.BlockSpec((tm, d), row), pl.BlockSpec((tm, V7X_LANES), row),
                   pl.BlockSpec((tm, V7X_LANES), row)],
        out_shape=[jax.ShapeDtypeStruct((m, d), F32), jax.ShapeDtypeStruct((m, V7X_LANES), I32),
                   jax.ShapeDtypeStruct((m, V7X_LANES), F32)],
        compiler_params=_params("parallel"),
        name="post_mixer",
    )(x, lhs, w_bf16, g, b, rw, rb)


def _rank_kernel(eid_ref, rank_ref, cnt_ref, carry_ref):
    i = pl.program_id(0)

    @pl.when(i == 0)
    def _():
        carry_ref[...] = jnp.zeros_like(carry_ref)

    eids = eid_ref[...]
    tm = eids.shape[0]
    lane = lax.broadcasted_iota(I32, eids.shape, 1)
    sel = [jnp.sum(jnp.where(lane == k, eids, 0), axis=-1, keepdims=True) for k in range(TOP_K)]
    onehot = jnp.zeros(eids.shape, F32)
    for k in range(TOP_K):
        onehot = onehot + (lane == sel[k]).astype(F32)
    r, c = _iota2((tm, tm))
    before = (c < r).astype(BF16)
    prior = jnp.dot(before, onehot.astype(BF16), preferred_element_type=F32) + carry_ref[...]
    ranks = jnp.zeros(eids.shape, F32)
    for k in range(TOP_K):
        rk = jnp.sum(jnp.where(lane == sel[k], prior, 0.0), axis=-1, keepdims=True)
        ranks = jnp.where(lane == k, rk, ranks)
    rank_ref[...] = ranks.astype(I32)
    total = carry_ref[...] + jnp.sum(onehot, axis=0, keepdims=True)
    carry_ref[...] = total
    cnt_ref[...] = total.astype(I32)


def _route_ranks(eids):
    m = eids.shape[0]
    tm = _row_tile(m, 640)
    return pl.pallas_call(
        _rank_kernel,
        grid=(m // tm,),
        in_specs=[pl.BlockSpec((tm, V7X_LANES), lambda i: (i, 0))],
        out_specs=[pl.BlockSpec((tm, V7X_LANES), lambda i: (i, 0)), pl.BlockSpec((1, V7X_LANES), lambda i: (0, 0))],
        out_shape=[jax.ShapeDtypeStruct((m, V7X_LANES), I32), jax.ShapeDtypeStruct((1, V7X_LANES), I32)],
        scratch_shapes=[pltpu.VMEM((1, V7X_LANES), F32)],
        compiler_params=_params("arbitrary"),
        name="route_ranks",
    )(eids)


def _row_copy(src_ref, s, dst_ref, d, sem):
    return pltpu.make_async_copy(src_ref.at[pl.ds(s, 1)], dst_ref.at[pl.ds(d, 1)], sem)


def _dispatch_kernel(pos_ref, x_ref, out_ref, sem):
    i = pl.program_id(0)
    tm = x_ref.shape[0]
    base = i * tm * TOP_K

    def start(t, carry):
        for k in range(TOP_K):
            _row_copy(x_ref, t, out_ref, pos_ref[base + t * TOP_K + k], sem).start(priority=k % 2)
        return carry

    def wait(t, carry):
        for k in range(TOP_K):
            _row_copy(x_ref, t, out_ref, pos_ref[base + t * TOP_K + k], sem).wait()
        return carry

    lax.fori_loop(0, tm, start, 0, unroll=DMA_UNROLL)
    lax.fori_loop(0, tm, wait, 0, unroll=DMA_UNROLL)


def _dispatch_rows(x1, pos_flat, n_rows):
    m, d = x1.shape
    tm = _row_tile(m, 1024)
    return pl.pallas_call(
        _dispatch_kernel,
        grid_spec=pltpu.PrefetchScalarGridSpec(
            num_scalar_prefetch=1,
            grid=(m // tm,),
            in_specs=[pl.BlockSpec((tm, d), lambda i, pos: (i, 0))],
            out_specs=pl.BlockSpec(memory_space=pl.ANY),
            scratch_shapes=[pltpu.SemaphoreType.DMA(())],
        ),
        out_shape=jax.ShapeDtypeStruct((n_rows, d), x1.dtype),
        compiler_params=_params("arbitrary"),
        name="moe_dispatch",
    )(pos_flat, x1)


def _w1_prep_kernel(w_ref, g_ref, u_ref):
    r, c = _iota2((V7X_MXU, V7X_MXU))
    half = V7X_MXU // 2
    src = jnp.where(c < half, 2 * c, 2 * (c - half) + 1)
    perm = jnp.where(r == src, 1.0, 0.0).astype(BF16)
    for j in range(w_ref.shape[2] // V7X_MXU):
        blk = w_ref[0, :, j * V7X_MXU:(j + 1) * V7X_MXU].astype(BF16)
        res = jnp.dot(blk, perm, preferred_element_type=F32).astype(BF16)
        g_ref[0, :, j * half:(j + 1) * half] = res[:, :half]
        u_ref[0, :, j * half:(j + 1) * half] = res[:, half:]


def _w1_prep(w1):
    e, k, n2 = w1.shape
    tk = 512
    out = jax.ShapeDtypeStruct((e, k, n2 // 2), BF16)
    return pl.pallas_call(
        _w1_prep_kernel,
        grid=(e, k // tk),
        in_specs=[pl.BlockSpec((1, tk, n2), lambda i, j: (i, j, 0))],
        out_specs=[pl.BlockSpec((1, tk, n2 // 2), lambda i, j: (i, j, 0))] * 2,
        out_shape=[out, out],
        compiler_params=_params("parallel", "parallel"),
        name="w1_prep",
    )(w1)


def _ffn_kernel(te_ref, tr_ref, nv_ref, x_ref, w1g_ref, w1u_ref, b1g_ref, b1u_ref, w2_ref, b2_ref, o_ref):
    i = pl.program_id(0)

    @pl.when(i < nv_ref[0])
    def _():
        row = lax.broadcasted_iota(I32, x_ref.shape, 0)
        x = jnp.where(row < tr_ref[i], x_ref[...], 0.0).astype(BF16)
        hg = jnp.dot(x, w1g_ref[0], preferred_element_type=F32) + b1g_ref[0]
        hu = jnp.dot(x, w1u_ref[0], preferred_element_type=F32) + b1u_ref[0]
        gate = jnp.minimum(hg, SWIGLU_LIMIT)
        up = jnp.clip(hu, -SWIGLU_LIMIT, SWIGLU_LIMIT)
        act = (up + 1.0) * gate * jax.nn.sigmoid(SWIGLU_ALPHA * gate)
        o_ref[...] = jnp.dot(act.astype(BF16), w2_ref[0], preferred_element_type=F32) + b2_ref[0]


def _expert_ffn(x_rows, tile_expert, tile_rows, n_valid, w1g, w1u, b1g, b1u, w2, b2):
    rows, d = x_rows.shape
    tm = EXPERT_TILE
    rmap = lambda i, te, tr, nv: (jnp.minimum(i, nv[0] - 1), 0)
    wmap = lambda i, te, tr, nv: (te[i], 0, 0)
    return pl.pallas_call(
        _ffn_kernel,
        grid_spec=pltpu.PrefetchScalarGridSpec(
            num_scalar_prefetch=3,
            grid=(rows // tm,),
            in_specs=[pl.BlockSpec((tm, d), rmap),
                      pl.BlockSpec((1, d, D_FF), wmap), pl.BlockSpec((1, d, D_FF), wmap),
                      pl.BlockSpec((1, 1, D_FF), wmap), pl.BlockSpec((1, 1, D_FF), wmap),
                      pl.BlockSpec((1, D_FF, d), wmap), pl.BlockSpec((1, 1, d), wmap)],
            out_specs=pl.BlockSpec((tm, d), rmap),
        ),
        out_shape=jax.ShapeDtypeStruct((rows, d), F32),
        compiler_params=_params("arbitrary"),
        name="moe_expert_ffn",
    )(tile_expert, tile_rows, n_valid, x_rows, w1g, w1u, b1g, b1u, w2, b2)


def _combine_kernel(pos_ref, y_ref, gate_ref, x_ref, g_ref, b_ref, o_ref, buf_ref, sem):
    i = pl.program_id(0)
    tm = x_ref.shape[0]
    base = i * tm * TOP_K

    def start(t, carry):
        for k in range(TOP_K):
            _row_copy(y_ref, pos_ref[base + t * TOP_K + k], buf_ref.at[k], t, sem).start(priority=k % 2)
        return carry

    def wait(t, carry):
        for k in range(TOP_K):
            _row_copy(y_ref, pos_ref[base + t * TOP_K + k], buf_ref.at[k], t, sem).wait()
        return carry

    lax.fori_loop(0, tm, start, 0, unroll=DMA_UNROLL)
    lax.fori_loop(0, tm, wait, 0, unroll=DMA_UNROLL)
    gates = gate_ref[...]
    acc = DN_ALPHA * x_ref[...]
    for k in range(TOP_K):
        acc = acc + gates[:, k:k + 1] * buf_ref[k]
    xc = acc - jnp.mean(acc, axis=-1, keepdims=True)
    var = jnp.mean(xc * xc, axis=-1, keepdims=True)
    o_ref[...] = xc * lax.rsqrt(var + LN_EPS) * g_ref[...] + b_ref[...]


def _combine_ln(y_rows, pos_flat, gates, x1, g, b):
    m, d = x1.shape
    tm = _row_tile(m, 320)
    row = lambda i, pos: (i, 0)
    fixed = lambda i, pos: (0, 0)
    return pl.pallas_call(
        _combine_kernel,
        grid_spec=pltpu.PrefetchScalarGridSpec(
            num_scalar_prefetch=1,
            grid=(m // tm,),
            in_specs=[pl.BlockSpec(memory_space=pl.ANY), pl.BlockSpec((tm, V7X_LANES), row),
                      pl.BlockSpec((tm, d), row), pl.BlockSpec((1, d), fixed), pl.BlockSpec((1, d), fixed)],
            out_specs=pl.BlockSpec((tm, d), row),
            scratch_shapes=[pltpu.VMEM((TOP_K, tm, d), F32), pltpu.SemaphoreType.DMA(())],
        ),
        out_shape=jax.ShapeDtypeStruct((m, d), F32),
        compiler_params=_params("arbitrary"),
        name="moe_combine_ln",
    )(pos_flat, y_rows, gates, x1, g, b)


def _moe_ln(x1, eids, gates, w1g, w1u, b1g, b1u, w2, b2, g, b):
    m, d = x1.shape
    ranks, counts = _route_ranks(eids)
    counts = counts[0, :N_EXPERTS]
    padded = (counts + EXPERT_TILE - 1) // EXPERT_TILE * EXPERT_TILE
    ends = jnp.cumsum(padded)
    gstart = ends - padded
    n_tiles = -(-m * TOP_K // EXPERT_TILE) + N_EXPERTS
    tile_start = jnp.arange(n_tiles, dtype=I32) * EXPERT_TILE
    tile_expert = jnp.minimum(jnp.sum((tile_start[:, None] >= ends[None, :]).astype(I32), axis=1), N_EXPERTS - 1)
    onehot_t = tile_expert[:, None] == jnp.arange(N_EXPERTS, dtype=I32)[None, :]
    used = jnp.sum(jnp.where(onehot_t, (gstart + counts)[None, :], 0), axis=1)
    tile_rows = jnp.clip(used - tile_start, 0, EXPERT_TILE).astype(I32)
    n_valid = (ends[-1:] // EXPERT_TILE).astype(I32)
    sel = eids[:, :TOP_K, None] == jnp.arange(N_EXPERTS, dtype=I32)[None, None, :]
    pos_flat = (jnp.sum(jnp.where(sel, gstart[None, None, :], 0), axis=-1) + ranks[:, :TOP_K]).reshape(-1).astype(I32)
    x_rows = _dispatch_rows(x1, pos_flat, n_tiles * EXPERT_TILE)
    y_rows = _expert_ffn(x_rows, tile_expert, tile_rows, n_valid, w1g, w1u, b1g, b1u, w2, b2)
    return _combine_ln(y_rows, pos_flat, gates, x1, g, b)


def _fox_prep_kernel(t_ref, bias_ref, lf_ref, ccol_ref):
    t_len = t_ref.shape[0]
    r, c = _iota2((V7X_LANES, V7X_LANES))
    tril = jnp.where(c <= r, 1.0, 0.0)
    carry = jnp.zeros((1, V7X_LANES), F32)
    for blk in range(t_len // V7X_LANES):
        rows = slice(blk * V7X_LANES, (blk + 1) * V7X_LANES)
        lf = _log_sigmoid(t_ref[rows, :] + bias_ref[...])
        lf_ref[rows, :] = lf
        cs = jnp.dot(tril, lf, precision=HIGHEST, preferred_element_type=F32) + carry
        ccol_ref[rows, :] = cs
        carry = cs[V7X_LANES - 1:, :]


def _fox_prep(tail, bias_row, bsz, t_len):
    n_p = bsz * t_len
    blk = pl.BlockSpec((t_len, V7X_LANES), lambda b: (b, 0))
    out = jax.ShapeDtypeStruct((n_p, V7X_LANES), F32)
    return pl.pallas_call(
        _fox_prep_kernel,
        grid=(bsz,),
        in_specs=[blk, pl.BlockSpec((1, V7X_LANES), lambda b: (0, 0))],
        out_specs=[blk, blk],
        out_shape=[out, out],
        compiler_params=_params("parallel"),
        name="fox_prep",
    )(tail, bias_row)


def _fox_attn_kernel(q_ref, k_ref, v_ref, ccol_ref, o_ref, kb_ref, vt_ref, cb0_ref, cb1_ref, m0_ref, m1_ref, l0_ref, l1_ref,
                     acc0_ref, acc1_ref, s0_ref, s1_ref):
    pair = pl.program_id(1)
    qi = pl.program_id(2)
    tq = q_ref.shape[0]
    t_len = k_ref.shape[0]
    cb_refs, m_refs, l_refs, acc_refs = (cb0_ref, cb1_ref), (m0_ref, m1_ref), (l0_ref, l1_ref), (acc0_ref, acc1_ref)
    s_refs = (s0_ref, s1_ref)

    @pl.when(qi == 0)
    def _():
        kb_ref[...] = k_ref[...].astype(BF16)
        sr, sc = _iota2((V7X_LANES, V7X_LANES))
        for j in range(2):
            sel = jnp.where(sr == 2 * pair + j, 1.0, 0.0).astype(BF16)
            cb_refs[j][...] = _expand_heads(ccol_ref[...], sel)
        vt_ref[...] = v_ref[0].astype(BF16)

    qt = (q_ref[...] * (DH_B ** -0.5)).T
    feat = lax.broadcasted_iota(I32, qt.shape, 0)
    qh = [jnp.where((feat // DH_B) == j, qt, 0.0).astype(BF16) for j in range(2)]
    key_id, qry_id = _iota2((tq, tq))
    for j in range(2):
        m_refs[j][...] = jnp.full(m_refs[j].shape, -jnp.inf, F32)
        l_refs[j][...] = jnp.zeros_like(l_refs[j])
        acc_refs[j][...] = jnp.zeros_like(acc_refs[j])

    def scores(kb, j):
        start = pl.multiple_of(kb * tq, tq)
        cb = cb_refs[j][pl.ds(start, tq), :]
        return (jnp.dot(kb_ref[pl.ds(start, tq), :], qh[j], preferred_element_type=F32)
                - jnp.concatenate([cb] * (tq // V7X_LANES), axis=1))

    def update(kb, j, s):
        start = pl.multiple_of(kb * tq, tq)
        m_old = m_refs[j][...]
        m_new = jnp.maximum(m_old, jnp.max(s, axis=0, keepdims=True))
        alpha = jnp.exp(m_old - m_new)
        pe = jnp.exp(s - m_new)
        l_refs[j][...] = alpha * l_refs[j][...] + jnp.sum(pe, axis=0, keepdims=True)
        acc_refs[j][...] = alpha * acc_refs[j][...] + jnp.dot(vt_ref[:, pl.ds(start, tq)], pe.astype(BF16),
                                                              preferred_element_type=F32)
        m_refs[j][...] = m_new

    for j in range(2):
        s_refs[j][...] = scores(0, j)

    def body(kb, carry):
        for j in range(2):
            s = s_refs[j][...]
            s_refs[j][...] = scores(kb + 1, j)
            update(kb, j, s)
        return carry

    lax.fori_loop(0, qi, body, 0)
    for j in range(2):
        update(qi, j, jnp.where(key_id <= qry_id, s_refs[j][...], -jnp.inf))
    out_t = jnp.where((feat // DH_B) == 0, acc0_ref[...] / l0_ref[...], acc1_ref[...] / l1_ref[...])
    o_ref[...] = out_t.T.astype(o_ref.dtype)


def _fox_attention(h_main, kv_t, ccol, bsz, t_len):
    n_p = bsz * t_len
    tq = ATTN_BLOCK
    nq = t_len // tq
    qcol, kcol = (4 * D_A) // V7X_LANES, (4 * D_A + D_B) // V7X_LANES
    cbs, row, acc = pltpu.VMEM((t_len, V7X_LANES), F32), pltpu.VMEM((1, tq), F32), pltpu.VMEM((V7X_LANES, tq), F32)
    return pl.pallas_call(
        _fox_attn_kernel,
        grid=(bsz, H_B // 2, nq),
        in_specs=[pl.BlockSpec((tq, V7X_LANES), lambda b, p, qi: (b * nq + qi, qcol + p)),
                  pl.BlockSpec((t_len, V7X_LANES), lambda b, p, qi: (b, kcol + p)),
                  pl.BlockSpec((1, V7X_LANES, t_len), lambda b, p, qi: (b, D_B // V7X_LANES + p, 0)),
                  pl.BlockSpec((t_len, V7X_LANES), lambda b, p, qi: (b, 0))],
        out_specs=pl.BlockSpec((tq, V7X_LANES), lambda b, p, qi: (b * nq + qi, p)),
        out_shape=jax.ShapeDtypeStruct((n_p, D_B), BF16),
        scratch_shapes=[pltpu.VMEM((t_len, V7X_LANES), BF16), pltpu.VMEM((V7X_LANES, t_len), BF16),
                        cbs, cbs, row, row, row, row, acc, acc, pltpu.VMEM((tq, tq), F32), pltpu.VMEM((tq, tq), F32)],
        compiler_params=_params("parallel", "parallel", "arbitrary"),
        name="fox_attention",
    )(h_main, h_main, kv_t, ccol)


def _fox_decode_kernel(pt_ref, q_ref, kn_ref, vn_ref, t_ref, bias_ref, *rest):
    np_ = DECODE_PAGES
    k_refs, v_refs, lf_refs = rest[:np_], rest[np_:2 * np_], rest[2 * np_:3 * np_]
    o_ref, lfo_ref, m_ref, l_ref, acc_ref, carry_ref = rest[3 * np_:]
    j = pl.program_id(1)
    q = q_ref[0] * (DH_B ** -0.5)
    hrow, hlane = _iota2((H_B, D_B))
    own = hlane // DH_B == hrow
    qmat = jnp.where(own, jnp.broadcast_to(q, (H_B, D_B)), 0.0)
    lf_new = _log_sigmoid(t_ref[0] + bias_ref[...])

    @pl.when(j == 0)
    def _():
        m_ref[...] = jnp.sum(qmat * kn_ref[0], axis=-1, keepdims=True)
        l_ref[...] = jnp.ones_like(l_ref)
        acc_ref[...] = jnp.broadcast_to(vn_ref[0], acc_ref.shape)
        carry_ref[...] = jnp.zeros_like(carry_ref)
        lfo_ref[0] = lf_new

    r8, c8 = _iota2((H_B, H_B))
    cn = jnp.sum(jnp.where(r8 == c8, jnp.broadcast_to(lf_new[:, :H_B], (H_B, H_B)), 0.0), axis=-1, keepdims=True)
    kr, kc = _iota2((PAGE_SIZE, PAGE_SIZE))
    later = jnp.where(kr > kc, 1.0, 0.0).astype(BF16)
    qb = qmat.astype(BF16)
    lfts = [lf_refs[i][0] for i in range(np_)]
    carries = [None] * np_
    run = carry_ref[...]
    for i in reversed(range(np_)):
        carries[i] = run
        run = run + jnp.sum(lfts[i], axis=-1, keepdims=True)
    carry_ref[...] = run
    scores = []
    for i in range(np_):
        lf = lfts[i]
        hi = lf.astype(BF16)
        r1 = lf - hi.astype(F32)
        mid = r1.astype(BF16)
        lo = (r1 - mid.astype(F32)).astype(BF16)
        parts = jnp.dot(jnp.concatenate([hi, mid, lo], axis=0), later, preferred_element_type=F32)
        suffix = parts[:H_B] + parts[H_B:2 * H_B] + parts[2 * H_B:]
        k2 = k_refs[i][0].reshape(D_B, PAGE_SIZE).astype(BF16)
        scores.append(jnp.dot(qb, k2, preferred_element_type=F32) + (suffix + (cn + carries[i])))
    m_old = m_ref[...]
    m_new = m_old
    for s in scores:
        m_new = jnp.maximum(m_new, jnp.max(s, axis=-1, keepdims=True))
    alpha = jnp.exp(m_old - m_new)
    l_new = alpha * l_ref[...]
    acc = alpha * acc_ref[...]
    for i, s in enumerate(scores):
        pe = jnp.exp(s - m_new)
        l_new = l_new + jnp.sum(pe, axis=-1, keepdims=True)
        v2 = v_refs[i][0].reshape(D_B, PAGE_SIZE).astype(BF16)
        acc = acc + lax.dot_general(pe.astype(BF16), v2, NT_DIMS, preferred_element_type=F32)
    m_ref[...] = m_new
    l_ref[...] = l_new
    acc_ref[...] = acc

    @pl.when(j == pl.num_programs(1) - 1)
    def _():
        o = jnp.where(own, acc / l_new, 0.0)
        o_ref[0] = jnp.sum(o, axis=0, keepdims=True).astype(o_ref.dtype)


def _fox_decode(hs_main, hs_tail, bias_row, cache_kt, cache_vt, logf_t, page_table):
    bd = hs_main.shape[0]
    n_pages = page_table.shape[1]
    steps = n_pages // DECODE_PAGES
    qcol, kcol, vcol = (4 * D_A) // D_B, (4 * D_A + D_B) // D_B, (4 * D_A + 2 * D_B) // D_B

    def page(i, nd):
        return lambda b, j, pt: (pt[b * n_pages + (steps - 1 - j) * DECODE_PAGES + i],) + (0,) * nd

    tok = lambda col: pl.BlockSpec((1, 1, D_B), lambda b, j, pt: (b, 0, col))
    in_specs = [tok(qcol), tok(kcol), tok(vcol),
                pl.BlockSpec((1, 1, V7X_LANES), lambda b, j, pt: (b, 0, 0)),
                pl.BlockSpec((1, V7X_LANES), lambda b, j, pt: (0, 0))]
    in_specs += [pl.BlockSpec((1, H_B, DH_B, PAGE_SIZE), page(i, 3)) for i in range(DECODE_PAGES)]
    in_specs += [pl.BlockSpec((1, H_B, DH_B, PAGE_SIZE), page(i, 3)) for i in range(DECODE_PAGES)]
    in_specs += [pl.BlockSpec((1, H_B, PAGE_SIZE), page(i, 2)) for i in range(DECODE_PAGES)]
    return pl.pallas_call(
        _fox_decode_kernel,
        grid_spec=pltpu.PrefetchScalarGridSpec(
            num_scalar_prefetch=1,
            grid=(bd, steps),
            in_specs=in_specs,
            out_specs=[pl.BlockSpec((1, 1, D_B), lambda b, j, pt: (b, 0, 0)),
                       pl.BlockSpec((1, 1, V7X_LANES), lambda b, j, pt: (b, 0, 0))],
            scratch_shapes=[pltpu.VMEM((H_B, 1), F32), pltpu.VMEM((H_B, 1), F32), pltpu.VMEM((H_B, D_B), F32),
                            pltpu.VMEM((H_B, 1), F32)],
        ),
        out_shape=[jax.ShapeDtypeStruct((bd, 1, D_B), BF16), jax.ShapeDtypeStruct((bd, 1, V7X_LANES), F32)],
        compiler_params=_params("parallel", "arbitrary"),
        name="fox_decode",
    )(page_table.reshape(-1), hs_main, hs_main, hs_main, hs_tail, bias_row,
      *([cache_kt] * DECODE_PAGES), *([cache_vt] * DECODE_PAGES), *([logf_t] * DECODE_PAGES))


def _hgrn_gates(q, z, lb):
    qa = _silu(q)
    logf = jnp.log(lb + (1.0 - lb) * jax.nn.sigmoid(z))
    ka = (1.0 - lb) * jax.nn.sigmoid(-z)
    return qa, ka, logf


def _hgrn_out(o, g, nw):
    o = o * lax.rsqrt(jnp.mean(o * o, axis=-1, keepdims=True) + RMS_EPS)
    return o * nw * _silu(g)


def _hgrn_kernel(q_ref, f_ref, i_ref, g_ref, lb_ref, nw_ref, o_ref, s_ref, qd_ref, kd_ref, ke_ref, gf_ref, oc_ref,
                 st_ref):
    t_len = q_ref.shape[0]
    cs = CHUNK_A
    lb = lb_ref[0]
    r, c = _iota2((V7X_LANES, V7X_LANES))
    same = (r // cs) == (c // cs)
    incl = jnp.where(same & (c <= r), 1.0, 0.0)
    whole = jnp.where(same, 1.0, 0.0)
    for blk in range(t_len // V7X_LANES):
        rows = slice(blk * V7X_LANES, (blk + 1) * V7X_LANES)
        qa, ka, logf = _hgrn_gates(q_ref[rows, :], f_ref[rows, :], lb)
        b = jnp.dot(incl, logf, precision=HIGHEST, preferred_element_type=F32)
        gtot = jnp.dot(whole, logf, precision=HIGHEST, preferred_element_type=F32)
        qd_ref[rows, :] = qa * jnp.exp(b)
        kd_ref[rows, :] = ka * jnp.exp(-b)
        ke_ref[rows, :] = ka * jnp.exp(gtot - b)
        gf_ref[rows, :] = gtot
    st_ref[...] = jnp.zeros_like(st_ref)
    cr, cc = _iota2((cs, cs))
    causal = cc <= cr

    def chunk(ci, carry):
        start = pl.multiple_of(ci * cs, cs)
        rows = pl.ds(start, cs)
        qd = qd_ref[rows, :].astype(BF16)
        kd = kd_ref[rows, :].astype(BF16)
        ke = ke_ref[rows, :].astype(BF16)
        v = i_ref[rows, :].astype(BF16)
        sc = jnp.where(causal, lax.dot_general(qd, kd, NT_DIMS, preferred_element_type=F32), 0.0)
        st = st_ref[...]
        o = (jnp.dot(sc.astype(BF16), v, preferred_element_type=F32)
             + lax.dot_general(qd, st.astype(BF16), NT_DIMS, preferred_element_type=F32))
        oc_ref[rows, :] = o
        u_t = lax.dot_general(v, ke, TN_DIMS, preferred_element_type=F32)
        st_ref[...] = st * jnp.exp(gf_ref[pl.ds(start, 1), :]) + u_t
        return carry

    lax.fori_loop(0, t_len // cs, chunk, 0, unroll=2)
    for blk in range(t_len // V7X_LANES):
        rows = slice(blk * V7X_LANES, (blk + 1) * V7X_LANES)
        o_ref[rows, :] = _hgrn_out(oc_ref[rows, :], g_ref[rows, :], nw_ref[...]).astype(o_ref.dtype)
    s_ref[0, 0] = st_ref[...].T


def _hgrn_prompt(h_main, lb3, nw_row, bsz, t_len):
    n_p = bsz * t_len
    col = lambda grp: (lambda b, h: (b, grp * H_A + h))
    scr = pltpu.VMEM((t_len, V7X_LANES), F32)
    return pl.pallas_call(
        _hgrn_kernel,
        grid=(bsz, H_A),
        in_specs=[pl.BlockSpec((t_len, DK_A), col(0)), pl.BlockSpec((t_len, DK_A), col(1)),
                  pl.BlockSpec((t_len, DV_A), col(2)), pl.BlockSpec((t_len, DV_A), col(3)),
                  pl.BlockSpec((1, 1, DK_A), lambda b, h: (h, 0, 0)), pl.BlockSpec((1, DV_A), lambda b, h: (0, 0))],
        out_specs=[pl.BlockSpec((t_len, DV_A), lambda b, h: (b, h)),
                   pl.BlockSpec((1, 1, DK_A, DV_A), lambda b, h: (b, h, 0, 0))],
        out_shape=[jax.ShapeDtypeStruct((n_p, D_A), BF16), jax.ShapeDtypeStruct((bsz, H_A, DK_A, DV_A), F32)],
        scratch_shapes=[scr, scr, scr, scr, scr, pltpu.VMEM((DV_A, DK_A), F32)],
        compiler_params=_params("parallel", "parallel"),
        name="hgrn_prompt",
    )(h_main, h_main, h_main, h_main, lb3, nw_row)


def _hgrn_step_kernel(h_ref, lb_ref, nw_ref, s0_ref, o_ref, s_ref):
    for h in range(H_A):
        grp = lambda g: h_ref[0, :, g * D_A + h * DK_A: g * D_A + (h + 1) * DK_A]
        qa, ka, logf = _hgrn_gates(grp(0), grp(1), lb_ref[h])
        v = grp(2)
        s_new = _to_column(jnp.exp(logf)) * s0_ref[0, h] + _to_column(ka) * v
        s_ref[0, h] = s_new
        o = jnp.sum(_to_column(qa) * s_new, axis=0, keepdims=True)
        o_ref[0, :, h * DV_A:(h + 1) * DV_A] = _hgrn_out(o, grp(3), nw_ref[...]).astype(o_ref.dtype)


def _hgrn_step(hs_main, lb3, nw_row, s0):
    bd = hs_main.shape[0]
    return pl.pallas_call(
        _hgrn_step_kernel,
        grid=(bd,),
        in_specs=[pl.BlockSpec((1, 1, 4 * D_A), lambda b: (b, 0, 0)), pl.BlockSpec((H_A, 1, DK_A), lambda b: (0, 0, 0)),
                  pl.BlockSpec((1, DV_A), lambda b: (0, 0)), pl.BlockSpec((1, H_A, DK_A, DV_A), lambda b: (b, 0, 0, 0))],
        out_specs=[pl.BlockSpec((1, 1, D_A), lambda b: (b, 0, 0)),
                   pl.BlockSpec((1, H_A, DK_A, DV_A), lambda b: (b, 0, 0, 0))],
        out_shape=[jax.ShapeDtypeStruct((bd, 1, D_A), BF16), jax.ShapeDtypeStruct(s0.shape, F32)],
        compiler_params=_params("parallel"),
        name="hgrn_step",
    )(hs_main, lb3, nw_row, s0)


def _conv_silu(cur, prev, w, b):
    row8 = lax.broadcasted_iota(I32, prev.shape, 0)
    acc = b + cur * w[CONV_W - 1:CONV_W, :]
    for s in range(1, CONV_W):
        sh = pltpu.roll(cur, s, 0)
        head = jnp.where(row8 < s, pltpu.roll(prev, s, 0), sh[:V7X_SUBLANES, :])
        shifted = jnp.concatenate([head, sh[V7X_SUBLANES:, :]], axis=0)
        acc = acc + shifted * w[CONV_W - 1 - s:CONV_W - s, :]
    return _silu(acc)


def _gated_group_norm(y, z, nw):
    y = y * _silu(z)
    parts = []
    for g in range(N_GROUPS_C):
        seg = y[:, g * GROUP_W:(g + 1) * GROUP_W]
        parts.append(seg * lax.rsqrt(jnp.mean(seg * seg, axis=-1, keepdims=True) + RMS_EPS))
    return jnp.concatenate(parts, axis=-1) * nw


def _ssd_kernel(z_ref, x_ref, bc_ref, dt_ref, cwx_ref, cbx_ref, cwbc_ref, cbbc_ref, dtb_ref, alog_ref, dsk_ref,
                nw_ref, y_ref, hs_ref, tailx_ref, tailbc_ref, ht_ref, e_ref, yacc_ref, xw_ref):
    tb = pl.program_id(1)
    tt = x_ref.shape[0]
    pair_w = 2 * HEADDIM_C
    heads_per_group = H_C // N_GROUPS_C

    @pl.when(tb == 0)
    def _():
        tailx_ref[...] = jnp.zeros_like(tailx_ref)
        tailbc_ref[...] = jnp.zeros_like(tailbc_ref)
        ht_ref[...] = jnp.zeros_like(ht_ref)
        er, ec = _iota2(e_ref.shape)
        e_ref[...] = jnp.where(ec // HEADDIM_C == er, 1.0, 0.0).astype(BF16)

    x_raw = x_ref[...]
    bc_raw = bc_ref[...]
    xs = _conv_silu(x_raw, tailx_ref[...], cwx_ref[...], cbx_ref[...])
    bcv = _conv_silu(bc_raw, tailbc_ref[...], cwbc_ref[...], cbbc_ref[...])
    tailx_ref[...] = x_raw[tt - V7X_SUBLANES:, :]
    tailbc_ref[...] = bc_raw[tt - V7X_SUBLANES:, :]

    lane = lax.broadcasted_iota(I32, (tt, V7X_LANES), 1)
    dt = jnp.where(lane < H_C, _softplus(dt_ref[...] + dtb_ref[...]), 0.0)
    a = -jnp.exp(alog_ref[...])
    r, c = _iota2((tt, tt))
    causal = c <= r
    cum = jnp.dot(jnp.where(causal, 1.0, 0.0), dt * a, precision=HIGHEST, preferred_element_type=F32)
    cum_t = cum.T
    xdt = xs * _expand_heads(dt, e_ref[...])
    low = (lane % pair_w) < HEADDIM_C

    for g in range(N_GROUPS_C):
        b_g = bcv[:, g * D_STATE_C:(g + 1) * D_STATE_C]
        c_g = bcv[:, D_BC + g * D_STATE_C:D_BC + (g + 1) * D_STATE_C].astype(BF16)
        cb = lax.dot_general(c_g, b_g.astype(BF16), NT_DIMS, preferred_element_type=F32)
        y_inter = jnp.dot(c_g, ht_ref[g].astype(BF16), preferred_element_type=F32)
        decs = []
        for pr in range(heads_per_group // 2):
            slab = slice(g * GROUP_W + pr * pair_w, g * GROUP_W + (pr + 1) * pair_w)
            xdt_slab = xdt[:, slab]
            ys, es, tes = [], [], []
            for j in range(2):
                head = g * heads_per_group + pr * 2 + j
                colb = jnp.broadcast_to(cum[:, head:head + 1], (tt, tt))
                decay = jnp.exp(jnp.where(causal, colb - cum_t[head:head + 1, :], -jnp.inf))
                ys.append(jnp.dot((cb * decay).astype(BF16), xdt_slab.astype(BF16), preferred_element_type=F32))
                es.append(jnp.exp(colb))
                tes.append(jnp.exp(colb[tt - 1:, :] - colb))
            e_pair = jnp.where(low, es[0], es[1])
            yacc_ref[:, slab] = (jnp.where(low, ys[0], ys[1]) + e_pair * y_inter[:, pr * pair_w:(pr + 1) * pair_w]
                                 + dsk_ref[:, slab] * xs[:, slab])
            xw_ref[:, pr * pair_w:(pr + 1) * pair_w] = xdt_slab * jnp.where(low, tes[0], tes[1])
            decs.append(e_pair[tt - 1:, :])
        dec_row = jnp.concatenate(decs, axis=-1)
        ht_ref[g] = ht_ref[g] * dec_row + jnp.dot(b_g.T.astype(BF16), xw_ref[...].astype(BF16),
                                                  preferred_element_type=F32)

    y_ref[...] = _gated_group_norm(yacc_ref[...], z_ref[...], nw_ref[...]).astype(y_ref.dtype)

    @pl.when(tb == pl.num_programs(1) - 1)
    def _():
        for g in range(N_GROUPS_C):
            for q in range(GROUP_W // V7X_LANES):
                rows = slice(g * GROUP_W + q * V7X_LANES, g * GROUP_W + (q + 1) * V7X_LANES)
                hs_ref[0, rows, :] = ht_ref[g][:, q * V7X_LANES:(q + 1) * V7X_LANES].T


def _ssd_prompt(h_main, h_tail, conv_w, conv_b, dtb_row, alog_row, dsk_row, nw_row, bsz, t_len):
    n_p = bsz * t_len
    tt = SSD_BLOCK
    nt = t_len // tt
    rowmap = lambda col: (lambda b, t: (b * nt + t, col))
    fixed = lambda b, t: (0, 0)
    cwx, cwbc = conv_w[:, :D_INNER], conv_w[:, D_INNER:]
    cbx, cbbc = conv_b[None, :D_INNER], conv_b[None, D_INNER:]
    return pl.pallas_call(
        _ssd_kernel,
        grid=(bsz, nt),
        in_specs=[pl.BlockSpec((tt, D_INNER), rowmap(0)), pl.BlockSpec((tt, D_INNER), rowmap(1)),
                  pl.BlockSpec((tt, 2 * D_BC), rowmap(2 * D_INNER // (2 * D_BC))),
                  pl.BlockSpec((tt, V7X_LANES), rowmap(0)),
                  pl.BlockSpec((CONV_W, D_INNER), fixed), pl.BlockSpec((1, D_INNER), fixed),
                  pl.BlockSpec((CONV_W, 2 * D_BC), fixed), pl.BlockSpec((1, 2 * D_BC), fixed),
                  pl.BlockSpec((1, V7X_LANES), fixed), pl.BlockSpec((1, V7X_LANES), fixed),
                  pl.BlockSpec((1, D_INNER), fixed), pl.BlockSpec((1, D_INNER), fixed)],
        out_specs=[pl.BlockSpec((tt, D_INNER), rowmap(0)),
                   pl.BlockSpec((1, D_INNER, D_STATE_C), lambda b, t: (b, 0, 0))],
        out_shape=[jax.ShapeDtypeStruct((n_p, D_INNER), BF16), jax.ShapeDtypeStruct((bsz, D_INNER, D_STATE_C), F32)],
        scratch_shapes=[pltpu.VMEM((V7X_SUBLANES, D_INNER), F32), pltpu.VMEM((V7X_SUBLANES, 2 * D_BC), F32),
                        pltpu.VMEM((N_GROUPS_C, D_STATE_C, GROUP_W), F32), pltpu.VMEM((V7X_LANES, D_INNER), BF16),
                        pltpu.VMEM((tt, D_INNER), F32), pltpu.VMEM((tt, GROUP_W), F32)],
        compiler_params=_params("parallel", "arbitrary"),
        name="ssd_prompt",
    )(h_main, h_main, h_main, h_tail, cwx, cbx, cwbc, cbbc, dtb_row, alog_row, dsk_row, nw_row)


def _ssd_step_kernel(h_ref, t_ref, cs_ref, cw_ref, cb_ref, dtb_ref, alog_ref, dsk_ref, nw_ref, h0_ref,
                     y_ref, hn_ref):
    z = h_ref[0, :, :D_INNER]
    xbc_new = h_ref[0, :, D_INNER:]
    cw = cw_ref[...]
    conv = cb_ref[...] + xbc_new * cw[CONV_W - 1:CONV_W, :]
    for j in range(CONV_W - 1):
        conv = conv + cs_ref[0, j:j + 1, :] * cw[j:j + 1, :]
    xbc = _silu(conv)
    xs = xbc[:, :D_INNER]
    lane = lax.broadcasted_iota(I32, (1, V7X_LANES), 1)
    dt = jnp.where(lane < H_C, _softplus(t_ref[0] + dtb_ref[...]), 0.0)
    da = jnp.exp(dt * -jnp.exp(alog_ref[...]))
    er, ec = _iota2((V7X_LANES, D_INNER))
    expand = jnp.where(ec // HEADDIM_C == er, 1.0, 0.0).astype(BF16)
    rows8 = lambda v: jnp.broadcast_to(v, (V7X_SUBLANES, V7X_LANES))
    dt_x = _expand_heads(rows8(dt), expand)[:1, :]
    da_x = _expand_heads(rows8(da), expand)[:1, :]
    xdt = xs * dt_x
    y_parts = []
    for q in range(D_INNER // V7X_LANES):
        g = q // (GROUP_W // V7X_LANES)
        lanes = slice(q * V7X_LANES, (q + 1) * V7X_LANES)
        b_g = xbc[:, D_INNER + g * D_STATE_C:D_INNER + (g + 1) * D_STATE_C]
        c_g = xbc[:, D_INNER + D_BC + g * D_STATE_C:D_INNER + D_BC + (g + 1) * D_STATE_C]
        h_new = _to_column(da_x[:, lanes]) * h0_ref[0, lanes, :] + _to_column(xdt[:, lanes]) * b_g
        hn_ref[0, lanes, :] = h_new
        y_parts.append(_to_row(jnp.sum(h_new * c_g, axis=-1, keepdims=True)))
    y = jnp.concatenate(y_parts, axis=-1) + dsk_ref[...] * xs
    y_ref[0] = _gated_group_norm(y, z, nw_ref[...]).astype(y_ref.dtype)


def _ssd_step(hs_main, hs_tail, conv_state, conv_w, conv_b, dtb_row, alog_row, dsk_row, nw_row, h0):
    bd = hs_main.shape[0]
    fixed = lambda b: (0, 0)
    tok = lambda b: (b, 0, 0)
    return pl.pallas_call(
        _ssd_step_kernel,
        grid=(bd,),
        in_specs=[pl.BlockSpec((1, 1, ODD_MAIN), tok), pl.BlockSpec((1, 1, V7X_LANES), tok),
                  pl.BlockSpec((1, CONV_W - 1, CONV_DIM), tok),
                  pl.BlockSpec((CONV_W, CONV_DIM), fixed), pl.BlockSpec((1, CONV_DIM), fixed),
                  pl.BlockSpec((1, V7X_LANES), fixed), pl.BlockSpec((1, V7X_LANES), fixed),
                  pl.BlockSpec((1, D_INNER), fixed), pl.BlockSpec((1, D_INNER), fixed),
                  pl.BlockSpec((1, D_INNER, D_STATE_C), tok)],
        out_specs=[pl.BlockSpec((1, 1, D_INNER), tok), pl.BlockSpec((1, D_INNER, D_STATE_C), tok)],
        out_shape=[jax.ShapeDtypeStruct((bd, 1, D_INNER), BF16), jax.ShapeDtypeStruct((bd, D_INNER, D_STATE_C), F32)],
        compiler_params=_params("parallel"),
        name="ssd_step",
    )(hs_main, hs_tail, conv_state, conv_w, conv_b[None], dtb_row, alog_row, dsk_row, nw_row, h0)


def _pad_lanes(v):
    return jnp.pad(v, (0, V7X_LANES - v.shape[0]))[None]


def kernel(x_prompt, x_sample, cache_k, cache_v, cache_logf, page_table, state_hgrn, state_ssm, state_conv,
           w_in_even, hgrn_lower_bound, hgrn_norm_w, fox_f_bias, w_out_even,
           w_in_odd, conv_w, conv_b, dt_bias, a_log, d_skip, ssm_norm_w, w_out_odd,
           ln1_g, ln1_b, ln2_g, ln2_b, router_w, router_b, exp_w1, exp_b1, exp_w2, exp_b2):
    lb_all = jnp.cumsum(jax.nn.softmax(hgrn_lower_bound, axis=0), axis=0)
    bp, t_p, d = x_prompt.shape
    bd, t_d, _ = x_sample.shape
    assert t_d == 1
    n_p = bp * t_p
    x_all = jnp.concatenate([x_prompt.reshape(n_p, d), x_sample.reshape(bd, d)], axis=0)
    outs = {}
    for l in range(DEPTH):
        i = l // 2
        if l % 2 == 0:
            h_main = _matmul(x_all, w_in_even[i][:, :EVEN_MAIN])
            h_tail = _matmul(x_all, w_in_even[i][:, EVEN_MAIN:])
            hs_main, hs_tail = h_main[n_p:, None, :], h_tail[n_p:, None, :]
            lb3 = lb_all[i].reshape(H_A, 1, DK_A)
            nw_row = hgrn_norm_w[i][None]
            bias_row = _pad_lanes(fox_f_bias[i])
            lf_p, ccol = _fox_prep(h_tail, bias_row, bp, t_p)
            kcol, vcol = 4 * D_A + D_B, 4 * D_A + 2 * D_B
            kv_t = _proj_t(x_all, w_in_even[i][:, kcol:kcol + 2 * D_B].T.astype(BF16), bp, t_p)
            ob_p = _fox_attention(h_main, kv_t, ccol, bp, t_p)
            oa_p, hg_p = _hgrn_prompt(h_main, lb3, nw_row, bp, t_p)
            oa_s, hg_s = _hgrn_step(hs_main, lb3, nw_row, state_hgrn[i])
            ob_s, lf_s = _fox_decode(hs_main, hs_tail, bias_row, jnp.transpose(cache_k[i], (0, 2, 3, 1)),
                                     jnp.transpose(cache_v[i], (0, 2, 3, 1)), jnp.swapaxes(cache_logf[i], 1, 2),
                                     page_table)
            mix = jnp.concatenate([jnp.concatenate([oa_p, ob_p], axis=1),
                                   jnp.concatenate([oa_s[:, 0], ob_s[:, 0]], axis=1)], axis=0)
            heads_last = lambda a: jnp.transpose(a.reshape(bp, H_B, DH_B, t_p), (0, 3, 1, 2))
            vals = (("kp", heads_last(kv_t[:, :D_B])),
                    ("ks", h_main[n_p:, kcol:kcol + D_B].reshape(bd, t_d, H_B, DH_B)),
                    ("vp", heads_last(kv_t[:, D_B:])),
                    ("vs", h_main[n_p:, vcol:vcol + D_B].reshape(bd, t_d, H_B, DH_B)),
                    ("lfp", lf_p[:, :H_B].reshape(bp, t_p, H_B)), ("lfs", lf_s[:, :, :H_B]),
                    ("hgp", hg_p), ("hgs", hg_s))
            w_out = w_out_even[i]
        else:
            h_main = _matmul(x_all, w_in_odd[i][:, :ODD_MAIN])
            h_tail = _matmul(x_all, w_in_odd[i][:, ODD_MAIN:])
            hs_main, hs_tail = h_main[n_p:, None, :], h_tail[n_p:, None, :]
            dtb_row, alog_row = _pad_lanes(dt_bias[i]), _pad_lanes(a_log[i])
            dsk_row = jnp.repeat(d_skip[i], HEADDIM_C)[None]
            nw_row = ssm_norm_w[i][None]
            y_p, ss_p = _ssd_prompt(h_main, h_tail, conv_w[i], conv_b[i], dtb_row, alog_row, dsk_row, nw_row, bp, t_p)
            y_s, ss_s = _ssd_step(hs_main, hs_tail, state_conv[i], conv_w[i], conv_b[i], dtb_row, alog_row, dsk_row,
                                  nw_row, state_ssm[i].reshape(bd, D_INNER, D_STATE_C))
            mix = jnp.concatenate([y_p, y_s[:, 0]], axis=0)
            xbc_p = h_main[:n_p, D_INNER:].reshape(bp, t_p, CONV_DIM)
            vals = (("ssp", ss_p.reshape(bp, H_C, HEADDIM_C, D_STATE_C)),
                    ("sss", ss_s.reshape(bd, H_C, HEADDIM_C, D_STATE_C)),
                    ("cvp", xbc_p[:, t_p - (CONV_W - 1):]),
                    ("cvs", jnp.concatenate([state_conv[i][:, 1:], hs_main[:, :, D_INNER:]], axis=1)))
            w_out = w_out_odd[i]
        for name, val in vals:
            outs.setdefault(name, []).append(val)
        rw = jnp.pad(router_w[l], ((0, 0), (0, V7X_LANES - N_EXPERTS)))
        rb = jnp.concatenate([router_b[l], jnp.full((V7X_LANES - N_EXPERTS,), -jnp.inf, F32)])[None]
        x1, eids, gates = _post_mixer(x_all, mix, w_out.astype(BF16), ln1_g[l][None], ln1_b[l][None], rw, rb)
        w1g, w1u = _w1_prep(exp_w1[l])
        b1g = exp_b1[l][:, None, 0::2]
        b1u = exp_b1[l][:, None, 1::2]
        x_all = _moe_ln(x1, eids, gates, w1g, w1u, b1g, b1u, exp_w2[l].astype(BF16), exp_b2[l][:, None, :],
                        ln2_g[l][None], ln2_b[l][None])
    st = {k: jnp.stack(v) for k, v in outs.items()}
    return (x_all[:n_p].reshape(bp, t_p, d), x_all[n_p:].reshape(bd, t_d, d),
            st["kp"], st["ks"], st["vp"], st["vs"], st["lfp"], st["lfs"],
            st["hgp"], st["hgs"], st["ssp"], st["sss"], st["cvp"], st["cvs"])
```

```python
import functools

import jax
import jax.numpy as jnp
import numpy as np
from jax import lax
from jax.experimental import pallas as pl
from jax.experimental.pallas import tpu as pltpu

F32 = jnp.float32
BF16 = jnp.bfloat16
I32 = jnp.int32
HIGHEST = lax.Precision.HIGHEST

D_MODEL = 1024
DEPTH = 2
PAGE_SIZE = 128
H_A, DK_A, DV_A, CHUNK_A = 4, 128, 128, 32
H_B, DH_B = 8, 64
D_A = H_A * DK_A
D_B = H_B * DH_B
EVEN_MAIN = 4 * D_A + 3 * D_B
D_INNER = 2 * D_MODEL
HEADDIM_C = 64
H_C = D_INNER // HEADDIM_C
N_GROUPS_C = 4
D_STATE_C = 128
D_BC = N_GROUPS_C * D_STATE_C
GROUP_W = D_INNER // N_GROUPS_C
CONV_W = 4
CONV_DIM = D_INNER + 2 * D_BC
ODD_MAIN = D_INNER + CONV_DIM
N_EXPERTS = 32
TOP_K = 4
D_FF = D_MODEL
SWIGLU_LIMIT = 7.0
SWIGLU_ALPHA = 1.702
DN_ALPHA = (2 * DEPTH) ** 0.25
LN_EPS = 1e-5
RMS_EPS = 1e-6

V7X_LANES = 128
V7X_SUBLANES = 8
V7X_MXU = 256
VMEM_LIMIT = 56 * 1024 * 1024
EXPERT_TILE = 512
ATTN_BLOCK = 256
SSD_BLOCK = 128
DECODE_PAGES = 8
DMA_UNROLL = 8

NT_DIMS = (((1,), (1,)), ((), ()))
TN_DIMS = (((0,), (0,)), ((), ()))


def _row_tile(m, cap):
    best = 0
    for t in range(16, cap + 1, 16):
        if m % t == 0:
            best = t
    assert best, (m, cap)
    return best


def _params(*sem):
    return pltpu.CompilerParams(dimension_semantics=sem, vmem_limit_bytes=VMEM_LIMIT)


def _silu(x):
    return x * jax.nn.sigmoid(x)


def _softplus(x):
    return jnp.maximum(x, 0.0) + jnp.log1p(jnp.exp(-jnp.abs(x)))


def _log_sigmoid(x):
    return -_softplus(-x)


def _iota2(shape):
    return lax.broadcasted_iota(I32, shape, 0), lax.broadcasted_iota(I32, shape, 1)


def _to_column(row_vec):
    n = row_vec.shape[1]
    r, c = _iota2((n, n))
    return jnp.sum(jnp.where(r == c, jnp.broadcast_to(row_vec, (n, n)), 0.0), axis=1, keepdims=True)


def _to_row(col_vec):
    n = col_vec.shape[0]
    r, c = _iota2((n, n))
    return jnp.sum(jnp.where(r == c, jnp.broadcast_to(col_vec, (n, n)), 0.0), axis=0, keepdims=True)


def _expand_heads(v, e_bf16):
    hi = v.astype(BF16)
    r1 = v - hi.astype(F32)
    mid = r1.astype(BF16)
    lo = (r1 - mid.astype(F32)).astype(BF16)
    dot = lambda a: jnp.dot(a, e_bf16, preferred_element_type=F32)
    return dot(hi) + dot(mid) + dot(lo)


def _select_sum(m01_bf16, x):
    hi = x.astype(BF16)
    r1 = x - hi.astype(F32)
    mid = r1.astype(BF16)
    lo = (r1 - mid.astype(F32)).astype(BF16)
    dot = lambda a: jnp.dot(m01_bf16, a, preferred_element_type=F32)
    return dot(hi) + dot(mid) + dot(lo)


def _mm_kernel(x_ref, w_ref, o_ref):
    o_ref[...] = jnp.dot(x_ref[...].astype(BF16), w_ref[...].astype(BF16), preferred_element_type=F32)


def _matmul(x, w, tm_cap=1024):
    m, k = x.shape
    n = w.shape[1]
    n_pad = -(-n // V7X_LANES) * V7X_LANES
    if n_pad != n:
        w = jnp.pad(w, ((0, 0), (0, n_pad - n)))
    tm = _row_tile(m, tm_cap)
    tn = next(t for t in (1024, 512, V7X_MXU, V7X_LANES) if n_pad % t == 0)
    return pl.pallas_call(
        _mm_kernel,
        grid=(m // tm, n_pad // tn),
        in_specs=[pl.BlockSpec((tm, k), lambda i, j: (i, 0)),
                  pl.BlockSpec((k, tn), lambda i, j: (0, j))],
        out_specs=pl.BlockSpec((tm, tn), lambda i, j: (i, j)),
        out_shape=jax.ShapeDtypeStruct((m, n_pad), F32),
        compiler_params=_params("parallel", "parallel"),
        name="dense_matmul",
    )(x, w)


def _proj_t_kernel(w_ref, x_ref, o_ref):
    o_ref[0] = lax.dot_general(w_ref[...], x_ref[...].astype(BF16), NT_DIMS, preferred_element_type=F32)


def _proj_t(x, w_t_bf16, bsz, t_len, tm=512):
    n, k = w_t_bf16.shape
    nt = t_len // tm
    return pl.pallas_call(
        _proj_t_kernel,
        grid=(bsz, nt),
        in_specs=[pl.BlockSpec((n, k), lambda b, t: (0, 0)), pl.BlockSpec((tm, k), lambda b, t: (b * nt + t, 0))],
        out_specs=pl.BlockSpec((1, n, tm), lambda b, t: (b, 0, t)),
        out_shape=jax.ShapeDtypeStruct((bsz, n, t_len), F32),
        compiler_params=_params("parallel", "parallel"),
        name="proj_transposed",
    )(w_t_bf16, x)


def _post_mixer_kernel(x_ref, lhs_ref, w_ref, g_ref, b_ref, rw_ref, rb_ref, x1_ref, eid_ref, gate_ref):
    acc = DN_ALPHA * x_ref[...] + jnp.dot(lhs_ref[...].astype(BF16), w_ref[...], preferred_element_type=F32)
    xc = acc - jnp.mean(acc, axis=-1, keepdims=True)
    var = jnp.mean(xc * xc, axis=-1, keepdims=True)
    x1 = xc * lax.rsqrt(var + LN_EPS) * g_ref[...] + b_ref[...]
    x1_ref[...] = x1
    logits = jnp.dot(x1, rw_ref[...], precision=HIGHEST, preferred_element_type=F32) + rb_ref[...]
    lane = lax.broadcasted_iota(I32, logits.shape, 1)
    eids = jnp.zeros(logits.shape, I32)
    vals = []
    for k in range(TOP_K):
        top = jnp.max(logits, axis=-1, keepdims=True)
        idx = jnp.min(jnp.where(logits == top, lane, V7X_LANES), axis=-1, keepdims=True)
        vals.append(top)
        eids = jnp.where(lane == k, idx, eids)
        logits = jnp.where(lane == idx, -jnp.inf, logits)
    exps = [jnp.exp(v - vals[0]) for v in vals]
    denom = exps[0] + exps[1] + exps[2] + exps[3]
    gates = jnp.zeros(logits.shape, F32)
    for k in range(TOP_K):
        gates = jnp.where(lane == k, exps[k] / denom, gates)
    eid_ref[...] = eids
    gate_ref[...] = gates


def _post_mixer(x, lhs, w_bf16, g, b, rw, rb):
    m, d = x.shape
    k = lhs.shape[1]
    tm = _row_tile(m, 640)
    row = lambda i: (i, 0)
    fixed = lambda i: (0, 0)
    return pl.pallas_call(
        _post_mixer_kernel,
        grid=(m // tm,),
        in_specs=[pl.BlockSpec((tm, d), row), pl.BlockSpec((tm, k), row), pl.BlockSpec((k, d), fixed),
                  pl.BlockSpec((1, d), fixed), pl.BlockSpec((1, d), fixed),
                  pl.BlockSpec((d, V7X_LANES), fixed), pl.BlockSpec((1, V7X_LANES), fixed)],
        out_specs=[pl.BlockSpec((tm, d), row), pl.BlockSpec((tm, V7X_LANES), row),
                   pl.BlockSpec((tm, V7X_LANES), row)],
        out_shape=[jax.ShapeDtypeStruct((m, d), F32), jax.ShapeDtypeStruct((m, V7X_LANES), I32),
                   jax.ShapeDtypeStruct((m, V7X_LANES), F32)],
        compiler_params=_params("parallel"),
        name="post_mixer",
    )(x, lhs, w_bf16, g, b, rw, rb)


def _rank_kernel(eid_ref, rank_ref, cnt_ref, carry_ref):
    i = pl.program_id(0)

    @pl.when(i == 0)
    def _():
        carry_ref[...] = jnp.zeros_like(carry_ref)

    eids = eid_ref[...]
    tm = eids.shape[0]
    lane = lax.broadcasted_iota(I32, eids.shape, 1)
    sel = [jnp.sum(jnp.where(lane == k, eids, 0), axis=-1, keepdims=True) for k in range(TOP_K)]
    onehot = jnp.zeros(eids.shape, F32)
    for k in range(TOP_K):
        onehot = onehot + (lane == sel[k]).astype(F32)
    r, c = _iota2((tm, tm))
    before = (c < r).astype(BF16)
    prior = jnp.dot(before, onehot.astype(BF16), preferred_element_type=F32) + carry_ref[...]
    ranks = jnp.zeros(eids.shape, F32)
    for k in range(TOP_K):
        rk = jnp.sum(jnp.where(lane == sel[k], prior, 0.0), axis=-1, keepdims=True)
        ranks = jnp.where(lane == k, rk, ranks)
    rank_ref[...] = ranks.astype(I32)
    total = carry_ref[...] + jnp.sum(onehot, axis=0, keepdims=True)
    carry_ref[...] = total
    cnt_ref[...] = total.astype(I32)


def _route_ranks(eids):
    m = eids.shape[0]
    tm = _row_tile(m, 640)
    return pl.pallas_call(
        _rank_kernel,
        grid=(m // tm,),
        in_specs=[pl.BlockSpec((tm, V7X_LANES), lambda i: (i, 0))],
        out_specs=[pl.BlockSpec((tm, V7X_LANES), lambda i: (i, 0)), pl.BlockSpec((1, V7X_LANES), lambda i: (0, 0))],
        out_shape=[jax.ShapeDtypeStruct((m, V7X_LANES), I32), jax.ShapeDtypeStruct((1, V7X_LANES), I32)],
        scratch_shapes=[pltpu.VMEM((1, V7X_LANES), F32)],
        compiler_params=_params("arbitrary"),
        name="route_ranks",
    )(eids)


def _row_copy(src_ref, s, dst_ref, d, sem):
    return pltpu.make_async_copy(src_ref.at[pl.ds(s, 1)], dst_ref.at[pl.ds(d, 1)], sem)


def _dispatch_kernel(pos_ref, x_ref, out_ref, sem):
    i = pl.program_id(0)
    tm = x_ref.shape[0]
    base = i * tm * TOP_K

    def start(t, carry):
        for k in range(TOP_K):
            _row_copy(x_ref, t, out_ref, pos_ref[base + t * TOP_K + k], sem).start(priority=k % 2)
        return carry

    def wait(t, carry):
        for k in range(TOP_K):
            _row_copy(x_ref, t, out_ref, pos_ref[base + t * TOP_K + k], sem).wait()
        return carry

    lax.fori_loop(0, tm, start, 0, unroll=DMA_UNROLL)
    lax.fori_loop(0, tm, wait, 0, unroll=DMA_UNROLL)


def _dispatch_rows(x1, pos_flat, n_rows):
    m, d = x1.shape
    tm = _row_tile(m, 1024)
    return pl.pallas_call(
        _dispatch_kernel,
        grid_spec=pltpu.PrefetchScalarGridSpec(
            num_scalar_prefetch=1,
            grid=(m // tm,),
            in_specs=[pl.BlockSpec((tm, d), lambda i, pos: (i, 0))],
            out_specs=pl.BlockSpec(memory_space=pl.ANY),
            scratch_shapes=[pltpu.SemaphoreType.DMA(())],
        ),
        out_shape=jax.ShapeDtypeStruct((n_rows, d), x1.dtype),
        compiler_params=_params("arbitrary"),
        name="moe_dispatch",
    )(pos_flat, x1)


def _w1_prep_kernel(w_ref, g_ref, u_ref):
    r, c = _iota2((V7X_MXU, V7X_MXU))
    half = V7X_MXU // 2
    src = jnp.where(c < half, 2 * c, 2 * (c - half) + 1)
    perm = jnp.where(r == src, 1.0, 0.0).astype(BF16)
    for j in range(w_ref.shape[3] // V7X_MXU):
        blk = w_ref[0, 0, :, j * V7X_MXU:(j + 1) * V7X_MXU].astype(BF16)
        res = jnp.dot(blk, perm, preferred_element_type=F32).astype(BF16)
        g_ref[0, :, j * half:(j + 1) * half] = res[:, :half]
        u_ref[0, :, j * half:(j + 1) * half] = res[:, half:]


def _w1_prep(w1_all, layer):
    _, e, k, n2 = w1_all.shape
    tk = 512
    out = jax.ShapeDtypeStruct((e, k, n2 // 2), BF16)
    return pl.pallas_call(
        _w1_prep_kernel,
        grid=(e, k // tk),
        in_specs=[pl.BlockSpec((1, 1, tk, n2), lambda i, j: (layer, i, j, 0))],
        out_specs=[pl.BlockSpec((1, tk, n2 // 2), lambda i, j: (i, j, 0))] * 2,
        out_shape=[out, out],
        compiler_params=_params("parallel", "parallel"),
        name="w1_prep",
    )(w1_all)


def _ffn_kernel(te_ref, tr_ref, nv_ref, x_ref, w1g_ref, w1u_ref, b1g_ref, b1u_ref, w2_ref, b2_ref, o_ref):
    i = pl.program_id(0)

    @pl.when(i < nv_ref[0])
    def _():
        row = lax.broadcasted_iota(I32, x_ref.shape, 0)
        x = jnp.where(row < tr_ref[i], x_ref[...], 0.0).astype(BF16)
        hg = jnp.dot(x, w1g_ref[0], preferred_element_type=F32) + b1g_ref[0]
        hu = jnp.dot(x, w1u_ref[0], preferred_element_type=F32) + b1u_ref[0]
        gate = jnp.minimum(hg, SWIGLU_LIMIT)
        up = jnp.clip(hu, -SWIGLU_LIMIT, SWIGLU_LIMIT)
        act = (up + 1.0) * gate * jax.nn.sigmoid(SWIGLU_ALPHA * gate)
        o_ref[...] = jnp.dot(act.astype(BF16), w2_ref[0, 0], preferred_element_type=F32) + b2_ref[0]


def _expert_ffn(x_rows, tile_expert, tile_rows, n_valid, w1g, w1u, b1g, b1u, w2_all, layer, b2):
    rows, d = x_rows.shape
    tm = EXPERT_TILE
    rmap = lambda i, te, tr, nv: (jnp.minimum(i, nv[0] - 1), 0)
    wmap = lambda i, te, tr, nv: (te[i], 0, 0)
    return pl.pallas_call(
        _ffn_kernel,
        grid_spec=pltpu.PrefetchScalarGridSpec(
            num_scalar_prefetch=3,
            grid=(rows // tm,),
            in_specs=[pl.BlockSpec((tm, d), rmap),
                      pl.BlockSpec((1, d, D_FF), wmap), pl.BlockSpec((1, d, D_FF), wmap),
                      pl.BlockSpec((1, 1, D_FF), wmap), pl.BlockSpec((1, 1, D_FF), wmap),
                      pl.BlockSpec((1, 1, D_FF, d), lambda i, te, tr, nv: (layer, te[i], 0, 0)),
                      pl.BlockSpec((1, 1, d), wmap)],
            out_specs=pl.BlockSpec((tm, d), rmap),
        ),
        out_shape=jax.ShapeDtypeStruct((rows, d), F32),
        compiler_params=_params("arbitrary"),
        name="moe_expert_ffn",
    )(tile_expert, tile_rows, n_valid, x_rows, w1g, w1u, b1g, b1u, w2_all, b2)


def _combine_kernel(pos_ref, y_ref, gate_ref, x_ref, g_ref, b_ref, o_ref, *rest):
    ob_ref = rest[0] if len(rest) == 3 else None
    buf_ref, sem = rest[-2:]
    i = pl.program_id(0)
    tm = x_ref.shape[0]
    base = i * tm * TOP_K

    def start(t, carry):
        for k in range(TOP_K):
            _row_copy(y_ref, pos_ref[base + t * TOP_K + k], buf_ref.at[k], t, sem).start(priority=k % 2)
        return carry

    def wait(t, carry):
        for k in range(TOP_K):
            _row_copy(y_ref, pos_ref[base + t * TOP_K + k], buf_ref.at[k], t, sem).wait()
        return carry

    lax.fori_loop(0, tm, start, 0, unroll=DMA_UNROLL)
    lax.fori_loop(0, tm, wait, 0, unroll=DMA_UNROLL)
    gates = gate_ref[...]
    acc = DN_ALPHA * x_ref[...]
    for k in range(TOP_K):
        acc = acc + gates[:, k:k + 1] * buf_ref[k]
    xc = acc - jnp.mean(acc, axis=-1, keepdims=True)
    var = jnp.mean(xc * xc, axis=-1, keepdims=True)
    out = xc * lax.rsqrt(var + LN_EPS) * g_ref[...] + b_ref[...]
    o_ref[...] = out
    if ob_ref is not None:
        ob_ref[...] = out.astype(BF16)


def _combine_ln(y_rows, pos_flat, gates, x1, g, b, emit_bf16):
    m, d = x1.shape
    tm = _row_tile(m, 320)
    row = lambda i, pos: (i, 0)
    fixed = lambda i, pos: (0, 0)
    return pl.pallas_call(
        _combine_kernel,
        grid_spec=pltpu.PrefetchScalarGridSpec(
            num_scalar_prefetch=1,
            grid=(m // tm,),
            in_specs=[pl.BlockSpec(memory_space=pl.ANY), pl.BlockSpec((tm, V7X_LANES), row),
                      pl.BlockSpec((tm, d), row), pl.BlockSpec((1, d), fixed), pl.BlockSpec((1, d), fixed)],
            out_specs=[pl.BlockSpec((tm, d), row)] * (2 if emit_bf16 else 1),
            scratch_shapes=[pltpu.VMEM((TOP_K, tm, d), F32), pltpu.SemaphoreType.DMA(())],
        ),
        out_shape=[jax.ShapeDtypeStruct((m, d), F32)] + ([jax.ShapeDtypeStruct((m, d), BF16)] if emit_bf16 else []),
        compiler_params=_params("arbitrary"),
        name="moe_combine_ln",
    )(pos_flat, y_rows, gates, x1, g, b)


def _moe_ln(x1, eids, gates, w1g, w1u, b1g, b1u, w2_all, layer, b2, g, b, emit_bf16):
    m, d = x1.shape
    ranks, counts = _route_ranks(eids)
    counts = counts[0, :N_EXPERTS]
    padded = (counts + EXPERT_TILE - 1) // EXPERT_TILE * EXPERT_TILE
    ends = jnp.cumsum(padded)
    gstart = ends - padded
    n_tiles = -(-m * TOP_K // EXPERT_TILE) + N_EXPERTS
    tile_start = jnp.arange(n_tiles, dtype=I32) * EXPERT_TILE
    tile_expert = jnp.minimum(jnp.sum((tile_start[:, None] >= ends[None, :]).astype(I32), axis=1), N_EXPERTS - 1)
    onehot_t = tile_expert[:, None] == jnp.arange(N_EXPERTS, dtype=I32)[None, :]
    used = jnp.sum(jnp.where(onehot_t, (gstart + counts)[None, :], 0), axis=1)
    tile_rows = jnp.clip(used - tile_start, 0, EXPERT_TILE).astype(I32)
    n_valid = (ends[-1:] // EXPERT_TILE).astype(I32)
    sel = eids[:, :TOP_K, None] == jnp.arange(N_EXPERTS, dtype=I32)[None, None, :]
    pos_flat = (jnp.sum(jnp.where(sel, gstart[None, None, :], 0), axis=-1) + ranks[:, :TOP_K]).reshape(-1).astype(I32)
    x_rows = _dispatch_rows(x1, pos_flat, n_tiles * EXPERT_TILE)
    y_rows = _expert_ffn(x_rows, tile_expert, tile_rows, n_valid, w1g, w1u, b1g, b1u, w2_all, layer, b2)
    return _combine_ln(y_rows, pos_flat, gates, x1, g, b, emit_bf16)


def _fox_prep_kernel(t_ref, bias_ref, lf_ref, ccol_ref):
    t_len = t_ref.shape[0]
    r, c = _iota2((V7X_LANES, V7X_LANES))
    tril = jnp.where(c <= r, 1.0, 0.0)
    carry = jnp.zeros((1, V7X_LANES), F32)
    for blk in range(t_len // V7X_LANES):
        rows = slice(blk * V7X_LANES, (blk + 1) * V7X_LANES)
        lf = _log_sigmoid(t_ref[rows, :] + bias_ref[...])
        lf_ref[rows, :] = lf
        cs = jnp.dot(tril, lf, precision=HIGHEST, preferred_element_type=F32) + carry
        ccol_ref[rows, :] = cs
        carry = cs[V7X_LANES - 1:, :]


def _fox_prep(tail, bias_row, bsz, t_len):
    n_p = bsz * t_len
    blk = pl.BlockSpec((t_len, V7X_LANES), lambda b: (b, 0))
    out = jax.ShapeDtypeStruct((n_p, V7X_LANES), F32)
    return pl.pallas_call(
        _fox_prep_kernel,
        grid=(bsz,),
        in_specs=[blk, pl.BlockSpec((1, V7X_LANES), lambda b: (0, 0))],
        out_specs=[blk, blk],
        out_shape=[out, out],
        compiler_params=_params("parallel"),
        name="fox_prep",
    )(tail, bias_row)


def _fox_attn_kernel(q_ref, k_ref, v_ref, ccol_ref, o_ref, kb_ref, vt_ref, cb0_ref, cb1_ref, m0_ref, m1_ref, l0_ref, l1_ref,
                     acc0_ref, acc1_ref, s0_ref, s1_ref):
    pair = pl.program_id(1)
    qi = pl.program_id(2)
    tq = q_ref.shape[0]
    t_len = k_ref.shape[0]
    cb_refs, m_refs, l_refs, acc_refs = (cb0_ref, cb1_ref), (m0_ref, m1_ref), (l0_ref, l1_ref), (acc0_ref, acc1_ref)
    s_refs = (s0_ref, s1_ref)

    @pl.when(qi == 0)
    def _():
        kb_ref[...] = k_ref[...].astype(BF16)
        sr, sc = _iota2((V7X_LANES, V7X_LANES))
        for j in range(2):
            sel = jnp.where(sr == 2 * pair + j, 1.0, 0.0).astype(BF16)
            cb_refs[j][...] = _expand_heads(ccol_ref[...], sel)
        vt_ref[...] = v_ref[0].astype(BF16)

    qt = (q_ref[...] * (DH_B ** -0.5)).T
    feat = lax.broadcasted_iota(I32, qt.shape, 0)
    qh = [jnp.where((feat // DH_B) == j, qt, 0.0).astype(BF16) for j in range(2)]
    key_id, qry_id = _iota2((tq, tq))
    for j in range(2):
        m_refs[j][...] = jnp.full(m_refs[j].shape, -jnp.inf, F32)
        l_refs[j][...] = jnp.zeros_like(l_refs[j])
        acc_refs[j][...] = jnp.zeros_like(acc_refs[j])

    def scores(kb, j):
        start = pl.multiple_of(kb * tq, tq)
        cb = cb_refs[j][pl.ds(start, tq), :]
        return (jnp.dot(kb_ref[pl.ds(start, tq), :], qh[j], preferred_element_type=F32)
                - jnp.concatenate([cb] * (tq // V7X_LANES), axis=1))

    def update(kb, j, s):
        start = pl.multiple_of(kb * tq, tq)
        m_old = m_refs[j][...]
        m_new = jnp.maximum(m_old, jnp.max(s, axis=0, keepdims=True))
        alpha = jnp.exp(m_old - m_new)
        pe = jnp.exp(s - m_new)
        l_refs[j][...] = alpha * l_refs[j][...] + jnp.sum(pe, axis=0, keepdims=True)
        acc_refs[j][...] = alpha * acc_refs[j][...] + jnp.dot(vt_ref[:, pl.ds(start, tq)], pe.astype(BF16),
                                                              preferred_element_type=F32)
        m_refs[j][...] = m_new

    for j in range(2):
        s_refs[j][...] = scores(0, j)

    def body(kb, carry):
        for j in range(2):
            s = s_refs[j][...]
            s_refs[j][...] = scores(kb + 1, j)
            update(kb, j, s)
        return carry

    lax.fori_loop(0, qi, body, 0)
    for j in range(2):
        update(qi, j, jnp.where(key_id <= qry_id, s_refs[j][...], -jnp.inf))
    out_t = jnp.where((feat // DH_B) == 0, acc0_ref[...] / l0_ref[...], acc1_ref[...] / l1_ref[...])
    o_ref[...] = out_t.T.astype(o_ref.dtype)


def _fox_attention(h_main, kv_t, ccol, bsz, t_len):
    n_p = bsz * t_len
    tq = ATTN_BLOCK
    nq = t_len // tq
    qcol, kcol = (4 * D_A) // V7X_LANES, (4 * D_A + D_B) // V7X_LANES
    cbs, row, acc = pltpu.VMEM((t_len, V7X_LANES), F32), pltpu.VMEM((1, tq), F32), pltpu.VMEM((V7X_LANES, tq), F32)
    return pl.pallas_call(
        _fox_attn_kernel,
        grid=(bsz, H_B // 2, nq),
        in_specs=[pl.BlockSpec((tq, V7X_LANES), lambda b, p, qi: (b * nq + qi, qcol + p)),
                  pl.BlockSpec((t_len, V7X_LANES), lambda b, p, qi: (b, kcol + p)),
                  pl.BlockSpec((1, V7X_LANES, t_len), lambda b, p, qi: (b, D_B // V7X_LANES + p, 0)),
                  pl.BlockSpec((t_len, V7X_LANES), lambda b, p, qi: (b, 0))],
        out_specs=pl.BlockSpec((tq, V7X_LANES), lambda b, p, qi: (b * nq + qi, p)),
        out_shape=jax.ShapeDtypeStruct((n_p, D_B), BF16),
        scratch_shapes=[pltpu.VMEM((t_len, V7X_LANES), BF16), pltpu.VMEM((V7X_LANES, t_len), BF16),
                        cbs, cbs, row, row, row, row, acc, acc, pltpu.VMEM((tq, tq), F32), pltpu.VMEM((tq, tq), F32)],
        compiler_params=_params("parallel", "parallel", "arbitrary"),
        name="fox_attention",
    )(h_main, h_main, kv_t, ccol)


def _fox_decode_kernel(pt_ref, q_ref, kn_ref, vn_ref, t_ref, bias_ref, *rest):
    np_ = DECODE_PAGES
    k_refs, v_refs, lf_refs = rest[:np_], rest[np_:2 * np_], rest[2 * np_:3 * np_]
    o_ref, lfo_ref, m_ref, l_ref, acc_ref, carry_ref = rest[3 * np_:]
    j = pl.program_id(1)
    q = q_ref[0] * (DH_B ** -0.5)
    hrow, hlane = _iota2((H_B, D_B))
    own = hlane // DH_B == hrow
    qmat = jnp.where(own, jnp.broadcast_to(q, (H_B, D_B)), 0.0)
    lf_new = _log_sigmoid(t_ref[0] + bias_ref[...])

    @pl.when(j == 0)
    def _():
        m_ref[...] = jnp.sum(qmat * kn_ref[0], axis=-1, keepdims=True)
        l_ref[...] = jnp.ones_like(l_ref)
        acc_ref[...] = jnp.broadcast_to(vn_ref[0], acc_ref.shape)
        carry_ref[...] = jnp.zeros_like(carry_ref)
        lfo_ref[0] = lf_new

    r8, c8 = _iota2((H_B, H_B))
    cn = jnp.sum(jnp.where(r8 == c8, jnp.broadcast_to(lf_new[:, :H_B], (H_B, H_B)), 0.0), axis=-1, keepdims=True)
    kr, kc = _iota2((PAGE_SIZE, PAGE_SIZE))
    later = jnp.where(kr > kc, 1.0, 0.0).astype(BF16)
    qb = qmat.astype(BF16)
    lfts = [lf_refs[i][0] for i in range(np_)]
    carries = [None] * np_
    run = carry_ref[...]
    for i in reversed(range(np_)):
        carries[i] = run
        run = run + jnp.sum(lfts[i], axis=-1, keepdims=True)
    carry_ref[...] = run
    scores = []
    for i in range(np_):
        lf = lfts[i]
        hi = lf.astype(BF16)
        r1 = lf - hi.astype(F32)
        mid = r1.astype(BF16)
        lo = (r1 - mid.astype(F32)).astype(BF16)
        parts = jnp.dot(jnp.concatenate([hi, mid, lo], axis=0), later, preferred_element_type=F32)
        suffix = parts[:H_B] + parts[H_B:2 * H_B] + parts[2 * H_B:]
        k2 = k_refs[i][0].reshape(D_B, PAGE_SIZE).astype(BF16)
        scores.append(jnp.dot(qb, k2, preferred_element_type=F32) + (suffix + (cn + carries[i])))
    m_old = m_ref[...]
    m_new = m_old
    for s in scores:
        m_new = jnp.maximum(m_new, jnp.max(s, axis=-1, keepdims=True))
    alpha = jnp.exp(m_old - m_new)
    l_new = alpha * l_ref[...]
    acc = alpha * acc_ref[...]
    for i, s in enumerate(scores):
        pe = jnp.exp(s - m_new)
        l_new = l_new + jnp.sum(pe, axis=-1, keepdims=True)
        v2 = v_refs[i][0].reshape(D_B, PAGE_SIZE).astype(BF16)
        acc = acc + lax.dot_general(pe.astype(BF16), v2, NT_DIMS, preferred_element_type=F32)
    m_ref[...] = m_new
    l_ref[...] = l_new
    acc_ref[...] = acc

    @pl.when(j == pl.num_programs(1) - 1)
    def _():
        o = jnp.where(own, acc / l_new, 0.0)
        o_ref[0] = jnp.sum(o, axis=0, keepdims=True).astype(o_ref.dtype)


def _fox_decode(hs_main, hs_tail, bias_row, cache_kt, cache_vt, logf_t, page_table):
    bd = hs_main.shape[0]
    n_pages = page_table.shape[1]
    steps = n_pages // DECODE_PAGES
    qcol, kcol, vcol = (4 * D_A) // D_B, (4 * D_A + D_B) // D_B, (4 * D_A + 2 * D_B) // D_B

    def page(i, nd):
        return lambda b, j, pt: (pt[b * n_pages + (steps - 1 - j) * DECODE_PAGES + i],) + (0,) * nd

    tok = lambda col: pl.BlockSpec((1, 1, D_B), lambda b, j, pt: (b, 0, col))
    in_specs = [tok(qcol), tok(kcol), tok(vcol),
                pl.BlockSpec((1, 1, V7X_LANES), lambda b, j, pt: (b, 0, 0)),
                pl.BlockSpec((1, V7X_LANES), lambda b, j, pt: (0, 0))]
    in_specs += [pl.BlockSpec((1, H_B, DH_B, PAGE_SIZE), page(i, 3)) for i in range(DECODE_PAGES)]
    in_specs += [pl.BlockSpec((1, H_B, DH_B, PAGE_SIZE), page(i, 3)) for i in range(DECODE_PAGES)]
    in_specs += [pl.BlockSpec((1, H_B, PAGE_SIZE), page(i, 2)) for i in range(DECODE_PAGES)]
    return pl.pallas_call(
        _fox_decode_kernel,
        grid_spec=pltpu.PrefetchScalarGridSpec(
            num_scalar_prefetch=1,
            grid=(bd, steps),
            in_specs=in_specs,
            out_specs=[pl.BlockSpec((1, 1, D_B), lambda b, j, pt: (b, 0, 0)),
                       pl.BlockSpec((1, 1, V7X_LANES), lambda b, j, pt: (b, 0, 0))],
            scratch_shapes=[pltpu.VMEM((H_B, 1), F32), pltpu.VMEM((H_B, 1), F32), pltpu.VMEM((H_B, D_B), F32),
                            pltpu.VMEM((H_B, 1), F32)],
        ),
        out_shape=[jax.ShapeDtypeStruct((bd, 1, D_B), BF16), jax.ShapeDtypeStruct((bd, 1, V7X_LANES), F32)],
        compiler_params=_params("parallel", "arbitrary"),
        name="fox_decode",
    )(page_table.reshape(-1), hs_main, hs_main, hs_main, hs_tail, bias_row,
      *([cache_kt] * DECODE_PAGES), *([cache_vt] * DECODE_PAGES), *([logf_t] * DECODE_PAGES))


def _hgrn_gates(q, z, lb):
    qa = _silu(q)
    logf = jnp.log(lb + (1.0 - lb) * jax.nn.sigmoid(z))
    ka = (1.0 - lb) * jax.nn.sigmoid(-z)
    return qa, ka, logf


def _hgrn_out(o, g, nw):
    o = o * lax.rsqrt(jnp.mean(o * o, axis=-1, keepdims=True) + RMS_EPS)
    return o * nw * _silu(g)


def _hgrn_kernel(q_ref, f_ref, i_ref, g_ref, lb_ref, nw_ref, o_ref, s_ref, qd_ref, dec_ref, oc_ref, u_ref):
    t_len = q_ref.shape[0]
    cs = CHUNK_A
    per_blk = V7X_LANES // cs
    n_blk = t_len // V7X_LANES
    lb = lb_ref[0]
    r, c = _iota2((V7X_LANES, V7X_LANES))
    same = (r // cs) == (c // cs)
    causal = same & (c <= r)
    sums = jnp.concatenate([jnp.where(causal, 1.0, 0.0), jnp.where(same, 1.0, 0.0)], axis=0).astype(BF16)

    def intra(blk, carry):
        start = pl.multiple_of(blk * V7X_LANES, V7X_LANES)
        rows = pl.ds(start, V7X_LANES)
        qa, ka, logf = _hgrn_gates(q_ref[rows, :], f_ref[rows, :], lb)
        both = _select_sum(sums, logf)
        b, gtot = both[:V7X_LANES], both[V7X_LANES:]
        qd = (qa * jnp.exp(b)).astype(BF16)
        kd = (ka * jnp.exp(-b)).astype(BF16)
        ke = (ka * jnp.exp(gtot - b)).astype(BF16)
        v = i_ref[rows, :].astype(BF16)
        qd_ref[rows, :] = qd
        sc = jnp.where(causal, lax.dot_general(qd, kd, NT_DIMS, preferred_element_type=F32), 0.0)
        oc_ref[rows, :] = jnp.dot(sc.astype(BF16), v, preferred_element_type=F32)
        dec = jnp.exp(gtot)
        for k in range(per_blk):
            sub = slice(k * cs, (k + 1) * cs)
            ci = blk * per_blk + k
            u_ref[ci] = lax.dot_general(v[sub], ke[sub], TN_DIMS, preferred_element_type=F32)
            dec_ref[pl.ds(ci, 1), :] = dec[k * cs:k * cs + 1, :]
        return carry

    lax.fori_loop(0, n_blk, intra, 0, unroll=4)

    def scan(ci, st):
        u = u_ref[ci]
        u_ref[ci] = st
        return st * dec_ref[pl.ds(ci, 1), :] + u

    st_last = lax.fori_loop(0, t_len // cs, scan, jnp.zeros((DV_A, DK_A), F32), unroll=4)
    s_ref[0, 0] = st_last.T

    def inter(blk, carry):
        start = pl.multiple_of(blk * V7X_LANES, V7X_LANES)
        rows = pl.ds(start, V7X_LANES)
        qd = qd_ref[rows, :]
        parts = [lax.dot_general(qd[k * cs:(k + 1) * cs], u_ref[blk * per_blk + k].astype(BF16), NT_DIMS,
                                 preferred_element_type=F32) for k in range(per_blk)]
        o = oc_ref[rows, :] + jnp.concatenate(parts, axis=0)
        o_ref[rows, :] = _hgrn_out(o, g_ref[rows, :], nw_ref[...]).astype(o_ref.dtype)
        return carry

    lax.fori_loop(0, n_blk, inter, 0, unroll=4)


def _hgrn_prompt(h_main, lb3, nw_row, bsz, t_len):
    n_p = bsz * t_len
    n_chunks = t_len // CHUNK_A
    col = lambda grp: (lambda b, h: (b, grp * H_A + h))
    return pl.pallas_call(
        _hgrn_kernel,
        grid=(bsz, H_A),
        in_specs=[pl.BlockSpec((t_len, DK_A), col(0)), pl.BlockSpec((t_len, DK_A), col(1)),
                  pl.BlockSpec((t_len, DV_A), col(2)), pl.BlockSpec((t_len, DV_A), col(3)),
                  pl.BlockSpec((1, 1, DK_A), lambda b, h: (h, 0, 0)), pl.BlockSpec((1, DV_A), lambda b, h: (0, 0))],
        out_specs=[pl.BlockSpec((t_len, DV_A), lambda b, h: (b, h)),
                   pl.BlockSpec((1, 1, DK_A, DV_A), lambda b, h: (b, h, 0, 0))],
        out_shape=[jax.ShapeDtypeStruct((n_p, D_A), BF16), jax.ShapeDtypeStruct((bsz, H_A, DK_A, DV_A), F32)],
        scratch_shapes=[pltpu.VMEM((t_len, DK_A), BF16), pltpu.VMEM((n_chunks, DK_A), F32),
                        pltpu.VMEM((t_len, DV_A), F32), pltpu.VMEM((n_chunks, DV_A, DK_A), F32)],
        compiler_params=_params("parallel", "parallel"),
        name="hgrn_prompt",
    )(h_main, h_main, h_main, h_main, lb3, nw_row)


def _hgrn_step_kernel(h_ref, lb_ref, nw_ref, s0_ref, o_ref, s_ref):
    for h in range(H_A):
        grp = lambda g: h_ref[0, :, g * D_A + h * DK_A: g * D_A + (h + 1) * DK_A]
        qa, ka, logf = _hgrn_gates(grp(0), grp(1), lb_ref[h])
        v = grp(2)
        s_new = _to_column(jnp.exp(logf)) * s0_ref[0, h] + _to_column(ka) * v
        s_ref[0, h] = s_new
        o = jnp.sum(_to_column(qa) * s_new, axis=0, keepdims=True)
        o_ref[0, :, h * DV_A:(h + 1) * DV_A] = _hgrn_out(o, grp(3), nw_ref[...]).astype(o_ref.dtype)


def _hgrn_step(hs_main, lb3, nw_row, s0):
    bd = hs_main.shape[0]
    return pl.pallas_call(
        _hgrn_step_kernel,
        grid=(bd,),
        in_specs=[pl.BlockSpec((1, 1, 4 * D_A), lambda b: (b, 0, 0)), pl.BlockSpec((H_A, 1, DK_A), lambda b: (0, 0, 0)),
                  pl.BlockSpec((1, DV_A), lambda b: (0, 0)), pl.BlockSpec((1, H_A, DK_A, DV_A), lambda b: (b, 0, 0, 0))],
        out_specs=[pl.BlockSpec((1, 1, D_A), lambda b: (b, 0, 0)),
                   pl.BlockSpec((1, H_A, DK_A, DV_A), lambda b: (b, 0, 0, 0))],
        out_shape=[jax.ShapeDtypeStruct((bd, 1, D_A), BF16), jax.ShapeDtypeStruct(s0.shape, F32)],
        compiler_params=_params("parallel"),
        name="hgrn_step",
    )(hs_main, lb3, nw_row, s0)


def _conv_silu(cur, prev, w, b):
    row8 = lax.broadcasted_iota(I32, prev.shape, 0)
    acc = b + cur * w[CONV_W - 1:CONV_W, :]
    for s in range(1, CONV_W):
        sh = pltpu.roll(cur, s, 0)
        head = jnp.where(row8 < s, pltpu.roll(prev, s, 0), sh[:V7X_SUBLANES, :])
        shifted = jnp.concatenate([head, sh[V7X_SUBLANES:, :]], axis=0)
        acc = acc + shifted * w[CONV_W - 1 - s:CONV_W - s, :]
    return _silu(acc)


def _gated_group_norm(y, z, nw):
    y = y * _silu(z)
    parts = []
    for g in range(N_GROUPS_C):
        seg = y[:, g * GROUP_W:(g + 1) * GROUP_W]
        parts.append(seg * lax.rsqrt(jnp.mean(seg * seg, axis=-1, keepdims=True) + RMS_EPS))
    return jnp.concatenate(parts, axis=-1) * nw


def _ssd_kernel(z_ref, x_ref, bc_ref, dt_ref, cwx_ref, cbx_ref, cwbc_ref, cbbc_ref, dtb_ref, alog_ref, dsk_ref,
                nw_ref, y_ref, hs_ref, tailx_ref, tailbc_ref, ht_ref, e_ref, yacc_ref, xw_ref):
    tb = pl.program_id(1)
    tt = x_ref.shape[0]
    pair_w = 2 * HEADDIM_C
    heads_per_group = H_C // N_GROUPS_C

    @pl.when(tb == 0)
    def _():
        tailx_ref[...] = jnp.zeros_like(tailx_ref)
        tailbc_ref[...] = jnp.zeros_like(tailbc_ref)
        ht_ref[...] = jnp.zeros_like(ht_ref)
        er, ec = _iota2(e_ref.shape)
        e_ref[...] = jnp.where(ec // HEADDIM_C == er, 1.0, 0.0).astype(BF16)

    x_raw = x_ref[...]
    bc_raw = bc_ref[...]
    xs = _conv_silu(x_raw, tailx_ref[...], cwx_ref[...], cbx_ref[...])
    bcv = _conv_silu(bc_raw, tailbc_ref[...], cwbc_ref[...], cbbc_ref[...])
    tailx_ref[...] = x_raw[tt - V7X_SUBLANES:, :]
    tailbc_ref[...] = bc_raw[tt - V7X_SUBLANES:, :]

    lane = lax.broadcasted_iota(I32, (tt, V7X_LANES), 1)
    dt = jnp.where(lane < H_C, _softplus(dt_ref[...] + dtb_ref[...]), 0.0)
    a = -jnp.exp(alog_ref[...])
    r, c = _iota2((tt, tt))
    causal = c <= r
    cum = jnp.dot(jnp.where(causal, 1.0, 0.0), dt * a, precision=HIGHEST, preferred_element_type=F32)
    cum_t = cum.T
    xdt = xs * _expand_heads(dt, e_ref[...])
    low = (lane % pair_w) < HEADDIM_C

    for g in range(N_GROUPS_C):
        b_g = bcv[:, g * D_STATE_C:(g + 1) * D_STATE_C]
        c_g = bcv[:, D_BC + g * D_STATE_C:D_BC + (g + 1) * D_STATE_C].astype(BF16)
        cb = lax.dot_general(c_g, b_g.astype(BF16), NT_DIMS, preferred_element_type=F32)
        y_inter = jnp.dot(c_g, ht_ref[g].astype(BF16), preferred_element_type=F32)
        decs = []
        for pr in range(heads_per_group // 2):
            slab = slice(g * GROUP_W + pr * pair_w, g * GROUP_W + (pr + 1) * pair_w)
            xdt_slab = xdt[:, slab]
            ys, es, tes = [], [], []
            for j in range(2):
                head = g * heads_per_group + pr * 2 + j
                colb = jnp.broadcast_to(cum[:, head:head + 1], (tt, tt))
                decay = jnp.exp(jnp.where(causal, colb - cum_t[head:head + 1, :], -jnp.inf))
                ys.append(jnp.dot((cb * decay).astype(BF16), xdt_slab.astype(BF16), preferred_element_type=F32))
                es.append(jnp.exp(colb))
                tes.append(jnp.exp(colb[tt - 1:, :] - colb))
            e_pair = jnp.where(low, es[0], es[1])
            yacc_ref[:, slab] = (jnp.where(low, ys[0], ys[1]) + e_pair * y_inter[:, pr * pair_w:(pr + 1) * pair_w]
                                 + dsk_ref[:, slab] * xs[:, slab])
            xw_ref[:, pr * pair_w:(pr + 1) * pair_w] = xdt_slab * jnp.where(low, tes[0], tes[1])
            decs.append(e_pair[tt - 1:, :])
        dec_row = jnp.concatenate(decs, axis=-1)
        ht_ref[g] = ht_ref[g] * dec_row + jnp.dot(b_g.T.astype(BF16), xw_ref[...].astype(BF16),
                                                  preferred_element_type=F32)

    y_ref[...] = _gated_group_norm(yacc_ref[...], z_ref[...], nw_ref[...]).astype(y_ref.dtype)

    @pl.when(tb == pl.num_programs(1) - 1)
    def _():
        for g in range(N_GROUPS_C):
            for q in range(GROUP_W // V7X_LANES):
                rows = slice(g * GROUP_W + q * V7X_LANES, g * GROUP_W + (q + 1) * V7X_LANES)
                hs_ref[0, rows, :] = ht_ref[g][:, q * V7X_LANES:(q + 1) * V7X_LANES].T


def _ssd_prompt(h_main, h_tail, conv_w, conv_b, dtb_row, alog_row, dsk_row, nw_row, bsz, t_len):
    n_p = bsz * t_len
    tt = SSD_BLOCK
    nt = t_len // tt
    rowmap = lambda col: (lambda b, t: (b * nt + t, col))
    fixed = lambda b, t: (0, 0)
    cwx, cwbc = conv_w[:, :D_INNER], conv_w[:, D_INNER:]
    cbx, cbbc = conv_b[None, :D_INNER], conv_b[None, D_INNER:]
    return pl.pallas_call(
        _ssd_kernel,
        grid=(bsz, nt),
        in_specs=[pl.BlockSpec((tt, D_INNER), rowmap(0)), pl.BlockSpec((tt, D_INNER), rowmap(1)),
                  pl.BlockSpec((tt, 2 * D_BC), rowmap(2 * D_INNER // (2 * D_BC))),
                  pl.BlockSpec((tt, V7X_LANES), rowmap(0)),
                  pl.BlockSpec((CONV_W, D_INNER), fixed), pl.BlockSpec((1, D_INNER), fixed),
                  pl.BlockSpec((CONV_W, 2 * D_BC), fixed), pl.BlockSpec((1, 2 * D_BC), fixed),
                  pl.BlockSpec((1, V7X_LANES), fixed), pl.BlockSpec((1, V7X_LANES), fixed),
                  pl.BlockSpec((1, D_INNER), fixed), pl.BlockSpec((1, D_INNER), fixed)],
        out_specs=[pl.BlockSpec((tt, D_INNER), rowmap(0)),
                   pl.BlockSpec((1, D_INNER, D_STATE_C), lambda b, t: (b, 0, 0))],
        out_shape=[jax.ShapeDtypeStruct((n_p, D_INNER), BF16), jax.ShapeDtypeStruct((bsz, D_INNER, D_STATE_C), F32)],
        scratch_shapes=[pltpu.VMEM((V7X_SUBLANES, D_INNER), F32), pltpu.VMEM((V7X_SUBLANES, 2 * D_BC), F32),
                        pltpu.VMEM((N_GROUPS_C, D_STATE_C, GROUP_W), F32), pltpu.VMEM((V7X_LANES, D_INNER), BF16),
                        pltpu.VMEM((tt, D_INNER), F32), pltpu.VMEM((tt, GROUP_W), F32)],
        compiler_params=_params("parallel", "arbitrary"),
        name="ssd_prompt",
    )(h_main, h_main, h_main, h_tail, cwx, cbx, cwbc, cbbc, dtb_row, alog_row, dsk_row, nw_row)


def _ssd_step_kernel(h_ref, t_ref, cs_ref, cw_ref, cb_ref, dtb_ref, alog_ref, dsk_ref, nw_ref, h0_ref,
                     y_ref, hn_ref):
    z = h_ref[0, :, :D_INNER]
    xbc_new = h_ref[0, :, D_INNER:]
    cw = cw_ref[...]
    conv = cb_ref[...] + xbc_new * cw[CONV_W - 1:CONV_W, :]
    for j in range(CONV_W - 1):
        conv = conv + cs_ref[0, j:j + 1, :] * cw[j:j + 1, :]
    xbc = _silu(conv)
    xs = xbc[:, :D_INNER]
    lane = lax.broadcasted_iota(I32, (1, V7X_LANES), 1)
    dt = jnp.where(lane < H_C, _softplus(t_ref[0] + dtb_ref[...]), 0.0)
    da = jnp.exp(dt * -jnp.exp(alog_ref[...]))
    er, ec = _iota2((V7X_LANES, D_INNER))
    expand = jnp.where(ec // HEADDIM_C == er, 1.0, 0.0).astype(BF16)
    rows8 = lambda v: jnp.broadcast_to(v, (V7X_SUBLANES, V7X_LANES))
    dt_x = _expand_heads(rows8(dt), expand)[:1, :]
    da_x = _expand_heads(rows8(da), expand)[:1, :]
    xdt = xs * dt_x
    y_parts = []
    for q in range(D_INNER // V7X_LANES):
        g = q // (GROUP_W // V7X_LANES)
        lanes = slice(q * V7X_LANES, (q + 1) * V7X_LANES)
        b_g = xbc[:, D_INNER + g * D_STATE_C:D_INNER + (g + 1) * D_STATE_C]
        c_g = xbc[:, D_INNER + D_BC + g * D_STATE_C:D_INNER + D_BC + (g + 1) * D_STATE_C]
        h_new = _to_column(da_x[:, lanes]) * h0_ref[0, lanes, :] + _to_column(xdt[:, lanes]) * b_g
        hn_ref[0, lanes, :] = h_new
        y_parts.append(_to_row(jnp.sum(h_new * c_g, axis=-1, keepdims=True)))
    y = jnp.concatenate(y_parts, axis=-1) + dsk_ref[...] * xs
    y_ref[0] = _gated_group_norm(y, z, nw_ref[...]).astype(y_ref.dtype)


def _ssd_step(hs_main, hs_tail, conv_state, conv_w, conv_b, dtb_row, alog_row, dsk_row, nw_row, h0):
    bd = hs_main.shape[0]
    fixed = lambda b: (0, 0)
    tok = lambda b: (b, 0, 0)
    return pl.pallas_call(
        _ssd_step_kernel,
        grid=(bd,),
        in_specs=[pl.BlockSpec((1, 1, ODD_MAIN), tok), pl.BlockSpec((1, 1, V7X_LANES), tok),
                  pl.BlockSpec((1, CONV_W - 1, CONV_DIM), tok),
                  pl.BlockSpec((CONV_W, CONV_DIM), fixed), pl.BlockSpec((1, CONV_DIM), fixed),
                  pl.BlockSpec((1, V7X_LANES), fixed), pl.BlockSpec((1, V7X_LANES), fixed),
                  pl.BlockSpec((1, D_INNER), fixed), pl.BlockSpec((1, D_INNER), fixed),
                  pl.BlockSpec((1, D_INNER, D_STATE_C), tok)],
        out_specs=[pl.BlockSpec((1, 1, D_INNER), tok), pl.BlockSpec((1, D_INNER, D_STATE_C), tok)],
        out_shape=[jax.ShapeDtypeStruct((bd, 1, D_INNER), BF16), jax.ShapeDtypeStruct((bd, D_INNER, D_STATE_C), F32)],
        compiler_params=_params("parallel"),
        name="ssd_step",
    )(hs_main, hs_tail, conv_state, conv_w, conv_b[None], dtb_row, alog_row, dsk_row, nw_row, h0)


def _pad_lanes(v):
    return jnp.pad(v, (0, V7X_LANES - v.shape[0]))[None]


def kernel(x_prompt, x_sample, cache_k, cache_v, cache_logf, page_table, state_hgrn, state_ssm, state_conv,
           w_in_even, hgrn_lower_bound, hgrn_norm_w, fox_f_bias, w_out_even,
           w_in_odd, conv_w, conv_b, dt_bias, a_log, d_skip, ssm_norm_w, w_out_odd,
           ln1_g, ln1_b, ln2_g, ln2_b, router_w, router_b, exp_w1, exp_b1, exp_w2, exp_b2):
    lb_all = jnp.cumsum(jax.nn.softmax(hgrn_lower_bound, axis=0), axis=0)
    bp, t_p, d = x_prompt.shape
    bd, t_d, _ = x_sample.shape
    assert t_d == 1
    n_p = bp * t_p
    x_all = jnp.concatenate([x_prompt.reshape(n_p, d), x_sample.reshape(bd, d)], axis=0)
    x_bf = x_all.astype(BF16)
    w2_all = exp_w2.astype(BF16)
    outs = {}
    for l in range(DEPTH):
        i = l // 2
        if l % 2 == 0:
            h_main = _matmul(x_bf, w_in_even[i][:, :EVEN_MAIN].astype(BF16))
            h_tail = _matmul(x_bf, w_in_even[i][:, EVEN_MAIN:].astype(BF16))
            hs_main, hs_tail = h_main[n_p:, None, :], h_tail[n_p:, None, :]
            lb3 = lb_all[i].reshape(H_A, 1, DK_A)
            nw_row = hgrn_norm_w[i][None]
            bias_row = _pad_lanes(fox_f_bias[i])
            lf_p, ccol = _fox_prep(h_tail, bias_row, bp, t_p)
            kcol, vcol = 4 * D_A + D_B, 4 * D_A + 2 * D_B
            kv_t = _proj_t(x_bf, w_in_even[i][:, kcol:kcol + 2 * D_B].T.astype(BF16), bp, t_p)
            ob_p = _fox_attention(h_main, kv_t, ccol, bp, t_p)
            oa_p, hg_p = _hgrn_prompt(h_main, lb3, nw_row, bp, t_p)
            oa_s, hg_s = _hgrn_step(hs_main, lb3, nw_row, state_hgrn[i])
            ob_s, lf_s = _fox_decode(hs_main, hs_tail, bias_row, jnp.transpose(cache_k[i], (0, 2, 3, 1)),
                                     jnp.transpose(cache_v[i], (0, 2, 3, 1)), jnp.swapaxes(cache_logf[i], 1, 2),
                                     page_table)
            mix = jnp.concatenate([jnp.concatenate([oa_p, ob_p], axis=1),
                                   jnp.concatenate([oa_s[:, 0], ob_s[:, 0]], axis=1)], axis=0)
            heads_last = lambda a: jnp.transpose(a.reshape(bp, H_B, DH_B, t_p), (0, 3, 1, 2))
            vals = (("kp", heads_last(kv_t[:, :D_B])),
                    ("ks", h_main[n_p:, kcol:kcol + D_B].reshape(bd, t_d, H_B, DH_B)),
                    ("vp", heads_last(kv_t[:, D_B:])),
                    ("vs", h_main[n_p:, vcol:vcol + D_B].reshape(bd, t_d, H_B, DH_B)),
                    ("lfp", lf_p[:, :H_B].reshape(bp, t_p, H_B)), ("lfs", lf_s[:, :, :H_B]),
                    ("hgp", hg_p), ("hgs", hg_s))
            w_out = w_out_even[i]
        else:
            h_main = _matmul(x_bf, w_in_odd[i][:, :ODD_MAIN].astype(BF16))
            h_tail = _matmul(x_bf, w_in_odd[i][:, ODD_MAIN:].astype(BF16))
            hs_main, hs_tail = h_main[n_p:, None, :], h_tail[n_p:, None, :]
            dtb_row, alog_row = _pad_lanes(dt_bias[i]), _pad_lanes(a_log[i])
            dsk_row = jnp.repeat(d_skip[i], HEADDIM_C)[None]
            nw_row = ssm_norm_w[i][None]
            y_p, ss_p = _ssd_prompt(h_main, h_tail, conv_w[i], conv_b[i], dtb_row, alog_row, dsk_row, nw_row, bp, t_p)
            y_s, ss_s = _ssd_step(hs_main, hs_tail, state_conv[i], conv_w[i], conv_b[i], dtb_row, alog_row, dsk_row,
                                  nw_row, state_ssm[i].reshape(bd, D_INNER, D_STATE_C))
            mix = jnp.concatenate([y_p, y_s[:, 0]], axis=0)
            tail_rows = jnp.stack([h_main[(b + 1) * t_p - (CONV_W - 1):(b + 1) * t_p, D_INNER:] for b in range(bp)])
            vals = (("ssp", ss_p.reshape(bp, H_C, HEADDIM_C, D_STATE_C)),
                    ("sss", ss_s.reshape(bd, H_C, HEADDIM_C, D_STATE_C)),
                    ("cvp", tail_rows),
                    ("cvs", jnp.concatenate([state_conv[i][:, 1:], hs_main[:, :, D_INNER:]], axis=1)))
            w_out = w_out_odd[i]
        for name, val in vals:
            outs.setdefault(name, []).append(val)
        rw = jnp.pad(router_w[l], ((0, 0), (0, V7X_LANES - N_EXPERTS)))
        rb = jnp.concatenate([router_b[l], jnp.full((V7X_LANES - N_EXPERTS,), -jnp.inf, F32)])[None]
        x1, eids, gates = _post_mixer(x_all, mix, w_out.astype(BF16), ln1_g[l][None], ln1_b[l][None], rw, rb)
        w1g, w1u = _w1_prep(exp_w1, l)
        b1g = exp_b1[l][:, None, 0::2]
        b1u = exp_b1[l][:, None, 1::2]
        more = l + 1 < DEPTH
        res = _moe_ln(x1, eids, gates, w1g, w1u, b1g, b1u, w2_all, l, exp_b2[l][:, None, :],
                      ln2_g[l][None], ln2_b[l][None], more)
        x_all = res[0]
        x_bf = res[1] if more else None
    st = {k: jnp.stack(v) for k, v in outs.items()}
    return (x_all[:n_p].reshape(bp, t_p, d), x_all[n_p:].reshape(bd, t_d, d),
            st["kp"], st["ks"], st["vp"], st["vs"], st["lfp"], st["lfs"],
            st["hgp"], st["hgs"], st["ssp"], st["sss"], st["cvp"], st["cvs"])
```

```python
import functools

import jax
import jax.numpy as jnp
import numpy as np
from jax import lax
from jax.experimental import pallas as pl
from jax.experimental.pallas import tpu as pltpu

F32 = jnp.float32
BF16 = jnp.bfloat16
I32 = jnp.int32
HIGHEST = lax.Precision.HIGHEST

D_MODEL = 1024
DEPTH = 2
PAGE_SIZE = 128
H_A, DK_A, DV_A, CHUNK_A = 4, 128, 128, 32
H_B, DH_B = 8, 64
D_A = H_A * DK_A
D_B = H_B * DH_B
EVEN_MAIN = 4 * D_A + 3 * D_B
D_INNER = 2 * D_MODEL
HEADDIM_C = 64
H_C = D_INNER // HEADDIM_C
N_GROUPS_C = 4
D_STATE_C = 128
D_BC = N_GROUPS_C * D_STATE_C
GROUP_W = D_INNER // N_GROUPS_C
CONV_W = 4
CONV_DIM = D_INNER + 2 * D_BC
ODD_MAIN = D_INNER + CONV_DIM
N_EXPERTS = 32
TOP_K = 4
D_FF = D_MODEL
SWIGLU_LIMIT = 7.0
SWIGLU_ALPHA = 1.702
DN_ALPHA = (2 * DEPTH) ** 0.25
LN_EPS = 1e-5
RMS_EPS = 1e-6

V7X_LANES = 128
V7X_SUBLANES = 8
V7X_MXU = 256
VMEM_LIMIT = 56 * 1024 * 1024
EXPERT_TILE = 512
ATTN_BLOCK = 256
SSD_BLOCK = 128
DECODE_PAGES = 16
DMA_UNROLL = 8

NT_DIMS = (((1,), (1,)), ((), ()))
TN_DIMS = (((0,), (0,)), ((), ()))


def _row_tile(m, cap):
    best = 0
    for t in range(16, cap + 1, 16):
        if m % t == 0:
            best = t
    assert best, (m, cap)
    return best


def _token_rows_spec(n_rows, bd, width):
    assert n_rows % bd == 0
    return pl.BlockSpec((bd, width), lambda *_: (n_rows // bd, 0))


def _params(*sem):
    return pltpu.CompilerParams(dimension_semantics=sem, vmem_limit_bytes=VMEM_LIMIT)


def _silu(x):
    return x * jax.nn.sigmoid(x)


def _softplus(x):
    return jnp.maximum(x, 0.0) + jnp.log1p(jnp.exp(-jnp.abs(x)))


def _log_sigmoid(x):
    return -_softplus(-x)


def _iota2(shape):
    return lax.broadcasted_iota(I32, shape, 0), lax.broadcasted_iota(I32, shape, 1)


def _to_column(row_vec):
    n = row_vec.shape[1]
    r, c = _iota2((n, n))
    return jnp.sum(jnp.where(r == c, jnp.broadcast_to(row_vec, (n, n)), 0.0), axis=1, keepdims=True)


def _to_row(col_vec):
    n = col_vec.shape[0]
    r, c = _iota2((n, n))
    return jnp.sum(jnp.where(r == c, jnp.broadcast_to(col_vec, (n, n)), 0.0), axis=0, keepdims=True)


def _expand_heads(v, e_bf16):
    hi = v.astype(BF16)
    r1 = v - hi.astype(F32)
    mid = r1.astype(BF16)
    lo = (r1 - mid.astype(F32)).astype(BF16)
    dot = lambda a: jnp.dot(a, e_bf16, preferred_element_type=F32)
    return dot(hi) + dot(mid) + dot(lo)


def _select_sum(m01_bf16, x):
    hi = x.astype(BF16)
    r1 = x - hi.astype(F32)
    mid = r1.astype(BF16)
    lo = (r1 - mid.astype(F32)).astype(BF16)
    dot = lambda a: jnp.dot(m01_bf16, a, preferred_element_type=F32)
    return dot(hi) + dot(mid) + dot(lo)


def _mm_kernel(x_ref, w_ref, o_ref):
    o_ref[...] = jnp.dot(x_ref[...].astype(BF16), w_ref[...].astype(BF16), preferred_element_type=F32)


def _matmul(x, w, tm_cap=1024):
    m, k = x.shape
    n = w.shape[1]
    n_pad = -(-n // V7X_LANES) * V7X_LANES
    if n_pad != n:
        w = jnp.pad(w, ((0, 0), (0, n_pad - n)))
    tm = _row_tile(m, tm_cap)
    tn = next(t for t in (1024, 512, V7X_MXU, V7X_LANES) if n_pad % t == 0)
    return pl.pallas_call(
        _mm_kernel,
        grid=(m // tm, n_pad // tn),
        in_specs=[pl.BlockSpec((tm, k), lambda i, j: (i, 0)),
                  pl.BlockSpec((k, tn), lambda i, j: (0, j))],
        out_specs=pl.BlockSpec((tm, tn), lambda i, j: (i, j)),
        out_shape=jax.ShapeDtypeStruct((m, n_pad), F32),
        compiler_params=_params("parallel", "parallel"),
        name="dense_matmul",
    )(x, w)


def _proj_t_kernel(w_ref, x_ref, o_ref):
    o_ref[0] = lax.dot_general(w_ref[...], x_ref[...].astype(BF16), NT_DIMS, preferred_element_type=F32)


def _proj_t(x, w_t_bf16, bsz, t_len, tm=512):
    n, k = w_t_bf16.shape
    nt = t_len // tm
    return pl.pallas_call(
        _proj_t_kernel,
        grid=(bsz, nt),
        in_specs=[pl.BlockSpec((n, k), lambda b, t: (0, 0)), pl.BlockSpec((tm, k), lambda b, t: (b * nt + t, 0))],
        out_specs=pl.BlockSpec((1, n, tm), lambda b, t: (b, 0, t)),
        out_shape=jax.ShapeDtypeStruct((bsz, n, t_len), F32),
        compiler_params=_params("parallel", "parallel"),
        name="proj_transposed",
    )(w_t_bf16, x)


def _post_mixer_kernel(n_parts, x_ref, *refs):
    lhs_refs, w_refs = refs[:n_parts], refs[n_parts:2 * n_parts]
    g_ref, b_ref, rwh_ref, rwl_ref, rb_ref, x1_ref, eid_ref, gate_ref = refs[2 * n_parts:]
    acc = DN_ALPHA * x_ref[...]
    for lhs_ref, w_ref in zip(lhs_refs, w_refs):
        acc = acc + jnp.dot(lhs_ref[...].astype(BF16), w_ref[...], preferred_element_type=F32)
    xc = acc - jnp.mean(acc, axis=-1, keepdims=True)
    var = jnp.mean(xc * xc, axis=-1, keepdims=True)
    x1 = xc * lax.rsqrt(var + LN_EPS) * g_ref[...] + b_ref[...]
    x1_ref[...] = x1
    x_hi = x1.astype(BF16)
    x_lo = (x1 - x_hi.astype(F32)).astype(BF16)
    dot = lambda a, w_ref: jnp.dot(a, w_ref[...], preferred_element_type=F32)
    logits = dot(x_hi, rwh_ref) + (dot(x_lo, rwh_ref) + dot(x_hi, rwl_ref)) + rb_ref[...]
    lane = lax.broadcasted_iota(I32, logits.shape, 1)
    eids = jnp.zeros(logits.shape, I32)
    vals = []
    for k in range(TOP_K):
        top = jnp.max(logits, axis=-1, keepdims=True)
        idx = jnp.min(jnp.where(logits == top, lane, V7X_LANES), axis=-1, keepdims=True)
        vals.append(top)
        eids = jnp.where(lane == k, idx, eids)
        logits = jnp.where(lane == idx, -jnp.inf, logits)
    exps = [jnp.exp(v - vals[0]) for v in vals]
    denom = exps[0] + exps[1] + exps[2] + exps[3]
    gates = jnp.zeros(logits.shape, F32)
    for k in range(TOP_K):
        gates = jnp.where(lane == k, exps[k] / denom, gates)
    eid_ref[...] = eids
    gate_ref[...] = gates


def _post_mixer(x, parts, g, b, rw, rb):
    m, d = x.shape
    tm = _row_tile(m, 640)
    row = lambda i: (i, 0)
    fixed = lambda i: (0, 0)
    rw_hi = rw.astype(BF16)
    rw_lo = (rw - rw_hi.astype(F32)).astype(BF16)
    lhs_specs = [pl.BlockSpec((tm, lhs.shape[1]), row) for lhs, _ in parts]
    w_specs = [pl.BlockSpec(w.shape, fixed) for _, w in parts]
    return pl.pallas_call(
        functools.partial(_post_mixer_kernel, len(parts)),
        grid=(m // tm,),
        in_specs=[pl.BlockSpec((tm, d), row)] + lhs_specs + w_specs + [
                  pl.BlockSpec((1, d), fixed), pl.BlockSpec((1, d), fixed),
                  pl.BlockSpec((d, V7X_LANES), fixed), pl.BlockSpec((d, V7X_LANES), fixed),
                  pl.BlockSpec((1, V7X_LANES), fixed)],
        out_specs=[pl.BlockSpec((tm, d), row), pl.BlockSpec((tm, V7X_LANES), row),
                   pl.BlockSpec((tm, V7X_LANES), row)],
        out_shape=[jax.ShapeDtypeStruct((m, d), F32), jax.ShapeDtypeStruct((m, V7X_LANES), I32),
                   jax.ShapeDtypeStruct((m, V7X_LANES), F32)],
        compiler_params=_params("parallel"),
        name="post_mixer",
    )(x, *[lhs for lhs, _ in parts], *[w for _, w in parts], g, b, rw_hi, rw_lo, rb)


def _rank_kernel(eid_ref, rank_ref, cnt_ref, carry_ref):
    i = pl.program_id(0)

    @pl.when(i == 0)
    def _():
        carry_ref[...] = jnp.zeros_like(carry_ref)

    eids = eid_ref[...]
    tm = eids.shape[0]
    lane = lax.broadcasted_iota(I32, eids.shape, 1)
    sel = [jnp.sum(jnp.where(lane == k, eids, 0), axis=-1, keepdims=True) for k in range(TOP_K)]
    onehot = jnp.zeros(eids.shape, F32)
    for k in range(TOP_K):
        onehot = onehot + (lane == sel[k]).astype(F32)
    r, c = _iota2((tm, tm))
    before = (c < r).astype(BF16)
    prior = jnp.dot(before, onehot.astype(BF16), preferred_element_type=F32) + carry_ref[...]
    ranks = jnp.zeros(eids.shape, F32)
    for k in range(TOP_K):
        rk = jnp.sum(jnp.where(lane == sel[k], prior, 0.0), axis=-1, keepdims=True)
        ranks = jnp.where(lane == k, rk, ranks)
    rank_ref[...] = ranks.astype(I32)
    total = carry_ref[...] + jnp.sum(onehot, axis=0, keepdims=True)
    carry_ref[...] = total
    cnt_ref[...] = total.astype(I32)


def _route_ranks(eids):
    m = eids.shape[0]
    tm = _row_tile(m, 640)
    return pl.pallas_call(
        _rank_kernel,
        grid=(m // tm,),
        in_specs=[pl.BlockSpec((tm, V7X_LANES), lambda i: (i, 0))],
        out_specs=[pl.BlockSpec((tm, V7X_LANES), lambda i: (i, 0)), pl.BlockSpec((1, V7X_LANES), lambda i: (0, 0))],
        out_shape=[jax.ShapeDtypeStruct((m, V7X_LANES), I32), jax.ShapeDtypeStruct((1, V7X_LANES), I32)],
        scratch_shapes=[pltpu.VMEM((1, V7X_LANES), F32)],
        compiler_params=_params("arbitrary"),
        name="route_ranks",
    )(eids)


def _row_copy(src_ref, s, dst_ref, d, sem):
    return pltpu.make_async_copy(src_ref.at[pl.ds(s, 1)], dst_ref.at[pl.ds(d, 1)], sem)


def _dispatch_kernel(pos_ref, x_ref, out_ref, sem):
    i = pl.program_id(0)
    tm = x_ref.shape[0]
    base = i * tm * TOP_K

    def start(t, carry):
        for k in range(TOP_K):
            _row_copy(x_ref, t, out_ref, pos_ref[base + t * TOP_K + k], sem).start(priority=k % 2)
        return carry

    def wait(t, carry):
        for k in range(TOP_K):
            _row_copy(x_ref, t, out_ref, pos_ref[base + t * TOP_K + k], sem).wait()
        return carry

    lax.fori_loop(0, tm, start, 0, unroll=DMA_UNROLL)
    lax.fori_loop(0, tm, wait, 0, unroll=DMA_UNROLL)


def _dispatch_rows(x1, pos_flat, n_rows):
    m, d = x1.shape
    tm = _row_tile(m, 1024)
    return pl.pallas_call(
        _dispatch_kernel,
        grid_spec=pltpu.PrefetchScalarGridSpec(
            num_scalar_prefetch=1,
            grid=(m // tm,),
            in_specs=[pl.BlockSpec((tm, d), lambda i, pos: (i, 0))],
            out_specs=pl.BlockSpec(memory_space=pl.ANY),
            scratch_shapes=[pltpu.SemaphoreType.DMA(())],
        ),
        out_shape=jax.ShapeDtypeStruct((n_rows, d), x1.dtype),
        compiler_params=_params("arbitrary"),
        name="moe_dispatch",
    )(pos_flat, x1)


def _w1_prep_kernel(w_ref, g_ref, u_ref):
    r, c = _iota2((V7X_MXU, V7X_MXU))
    half = V7X_MXU // 2
    src = jnp.where(c < half, 2 * c, 2 * (c - half) + 1)
    perm = jnp.where(r == src, 1.0, 0.0).astype(BF16)
    for j in range(w_ref.shape[3] // V7X_MXU):
        blk = w_ref[0, 0, :, j * V7X_MXU:(j + 1) * V7X_MXU].astype(BF16)
        res = jnp.dot(blk, perm, preferred_element_type=F32).astype(BF16)
        g_ref[0, :, j * half:(j + 1) * half] = res[:, :half]
        u_ref[0, :, j * half:(j + 1) * half] = res[:, half:]


def _w1_prep(w1_all, layer):
    _, e, k, n2 = w1_all.shape
    tk = 512
    out = jax.ShapeDtypeStruct((e, k, n2 // 2), BF16)
    return pl.pallas_call(
        _w1_prep_kernel,
        grid=(e, k // tk),
        in_specs=[pl.BlockSpec((1, 1, tk, n2), lambda i, j: (layer, i, j, 0))],
        out_specs=[pl.BlockSpec((1, tk, n2 // 2), lambda i, j: (i, j, 0))] * 2,
        out_shape=[out, out],
        compiler_params=_params("parallel", "parallel"),
        name="w1_prep",
    )(w1_all)


def _ffn_kernel(te_ref, tr_ref, nv_ref, x_ref, w1g_ref, w1u_ref, b1g_ref, b1u_ref, w2_ref, b2_ref, o_ref):
    i = pl.program_id(0)

    @pl.when(i < nv_ref[0])
    def _():
        row = lax.broadcasted_iota(I32, x_ref.shape, 0)
        x = jnp.where(row < tr_ref[i], x_ref[...], 0.0).astype(BF16)
        hg = jnp.dot(x, w1g_ref[0], preferred_element_type=F32) + b1g_ref[0]
        hu = jnp.dot(x, w1u_ref[0], preferred_element_type=F32) + b1u_ref[0]
        gate = jnp.minimum(hg, SWIGLU_LIMIT)
        up = jnp.clip(hu, -SWIGLU_LIMIT, SWIGLU_LIMIT)
        act = (up + 1.0) * gate * jax.nn.sigmoid(SWIGLU_ALPHA * gate)
        o_ref[...] = jnp.dot(act.astype(BF16), w2_ref[0, 0], preferred_element_type=F32) + b2_ref[0]


def _expert_ffn(x_rows, tile_expert, tile_rows, n_valid, w1g, w1u, b1g, b1u, w2_all, layer, b2):
    rows, d = x_rows.shape
    tm = EXPERT_TILE
    rmap = lambda i, te, tr, nv: (jnp.minimum(i, nv[0] - 1), 0)
    wmap = lambda i, te, tr, nv: (te[i], 0, 0)
    return pl.pallas_call(
        _ffn_kernel,
        grid_spec=pltpu.PrefetchScalarGridSpec(
            num_scalar_prefetch=3,
            grid=(rows // tm,),
            in_specs=[pl.BlockSpec((tm, d), rmap),
                      pl.BlockSpec((1, d, D_FF), wmap), pl.BlockSpec((1, d, D_FF), wmap),
                      pl.BlockSpec((1, 1, D_FF), wmap), pl.BlockSpec((1, 1, D_FF), wmap),
                      pl.BlockSpec((1, 1, D_FF, d), lambda i, te, tr, nv: (layer, te[i], 0, 0)),
                      pl.BlockSpec((1, 1, d), wmap)],
            out_specs=pl.BlockSpec((tm, d), rmap),
        ),
        out_shape=jax.ShapeDtypeStruct((rows, d), F32),
        compiler_params=_params("arbitrary"),
        name="moe_expert_ffn",
    )(tile_expert, tile_rows, n_valid, x_rows, w1g, w1u, b1g, b1u, w2_all, b2)


def _combine_kernel(pos_ref, y_ref, gate_ref, x_ref, g_ref, b_ref, o_ref, *rest):
    ob_ref = rest[0] if len(rest) == 3 else None
    buf_ref, sem = rest[-2:]
    i = pl.program_id(0)
    tm = x_ref.shape[0]
    base = i * tm * TOP_K

    def start(t, carry):
        for k in range(TOP_K):
            _row_copy(y_ref, pos_ref[base + t * TOP_K + k], buf_ref.at[k], t, sem).start(priority=k % 2)
        return carry

    def wait(t, carry):
        for k in range(TOP_K):
            _row_copy(y_ref, pos_ref[base + t * TOP_K + k], buf_ref.at[k], t, sem).wait()
        return carry

    lax.fori_loop(0, tm, start, 0, unroll=DMA_UNROLL)
    lax.fori_loop(0, tm, wait, 0, unroll=DMA_UNROLL)
    gates = gate_ref[...]
    acc = DN_ALPHA * x_ref[...]
    for k in range(TOP_K):
        acc = acc + gates[:, k:k + 1] * buf_ref[k]
    xc = acc - jnp.mean(acc, axis=-1, keepdims=True)
    var = jnp.mean(xc * xc, axis=-1, keepdims=True)
    out = xc * lax.rsqrt(var + LN_EPS) * g_ref[...] + b_ref[...]
    o_ref[...] = out
    if ob_ref is not None:
        ob_ref[...] = out.astype(BF16)


def _combine_ln(y_rows, pos_flat, gates, x1, g, b, emit_bf16):
    m, d = x1.shape
    tm = _row_tile(m, 320)
    row = lambda i, pos: (i, 0)
    fixed = lambda i, pos: (0, 0)
    return pl.pallas_call(
        _combine_kernel,
        grid_spec=pltpu.PrefetchScalarGridSpec(
            num_scalar_prefetch=1,
            grid=(m // tm,),
            in_specs=[pl.BlockSpec(memory_space=pl.ANY), pl.BlockSpec((tm, V7X_LANES), row),
                      pl.BlockSpec((tm, d), row), pl.BlockSpec((1, d), fixed), pl.BlockSpec((1, d), fixed)],
            out_specs=[pl.BlockSpec((tm, d), row)] * (2 if emit_bf16 else 1),
            scratch_shapes=[pltpu.VMEM((TOP_K, tm, d), F32), pltpu.SemaphoreType.DMA(())],
        ),
        out_shape=[jax.ShapeDtypeStruct((m, d), F32)] + ([jax.ShapeDtypeStruct((m, d), BF16)] if emit_bf16 else []),
        compiler_params=_params("arbitrary"),
        name="moe_combine_ln",
    )(pos_flat, y_rows, gates, x1, g, b)


def _moe_ln(x1, eids, gates, w1g, w1u, b1g, b1u, w2_all, layer, b2, g, b, emit_bf16):
    m, d = x1.shape
    ranks, counts = _route_ranks(eids)
    counts = counts[0, :N_EXPERTS]
    padded = (counts + EXPERT_TILE - 1) // EXPERT_TILE * EXPERT_TILE
    ends = jnp.cumsum(padded)
    gstart = ends - padded
    n_tiles = -(-m * TOP_K // EXPERT_TILE) + N_EXPERTS
    tile_start = jnp.arange(n_tiles, dtype=I32) * EXPERT_TILE
    tile_expert = jnp.minimum(jnp.sum((tile_start[:, None] >= ends[None, :]).astype(I32), axis=1), N_EXPERTS - 1)
    onehot_t = tile_expert[:, None] == jnp.arange(N_EXPERTS, dtype=I32)[None, :]
    used = jnp.sum(jnp.where(onehot_t, (gstart + counts)[None, :], 0), axis=1)
    tile_rows = jnp.clip(used - tile_start, 0, EXPERT_TILE).astype(I32)
    n_valid = (ends[-1:] // EXPERT_TILE).astype(I32)
    sel = eids[:, :TOP_K, None] == jnp.arange(N_EXPERTS, dtype=I32)[None, None, :]
    pos_flat = (jnp.sum(jnp.where(sel, gstart[None, None, :], 0), axis=-1) + ranks[:, :TOP_K]).reshape(-1).astype(I32)
    x_rows = _dispatch_rows(x1, pos_flat, n_tiles * EXPERT_TILE)
    y_rows = _expert_ffn(x_rows, tile_expert, tile_rows, n_valid, w1g, w1u, b1g, b1u, w2_all, layer, b2)
    return _combine_ln(y_rows, pos_flat, gates, x1, g, b, emit_bf16)


def _fox_prep_kernel(t_ref, bias_ref, lf_ref, ccol_ref):
    t_len = t_ref.shape[0]
    r, c = _iota2((V7X_LANES, V7X_LANES))
    tril = jnp.where(c <= r, 1.0, 0.0)
    carry = jnp.zeros((1, V7X_LANES), F32)
    for blk in range(t_len // V7X_LANES):
        rows = slice(blk * V7X_LANES, (blk + 1) * V7X_LANES)
        lf = _log_sigmoid(t_ref[rows, :] + bias_ref[...])
        lf_ref[rows, :] = lf
        cs = jnp.dot(tril, lf, precision=HIGHEST, preferred_element_type=F32) + carry
        ccol_ref[rows, :] = cs
        carry = cs[V7X_LANES - 1:, :]


def _fox_prep(tail, bias_row, bsz, t_len):
    n_p = bsz * t_len
    blk = pl.BlockSpec((t_len, V7X_LANES), lambda b: (b, 0))
    out = jax.ShapeDtypeStruct((n_p, V7X_LANES), F32)
    return pl.pallas_call(
        _fox_prep_kernel,
        grid=(bsz,),
        in_specs=[blk, pl.BlockSpec((1, V7X_LANES), lambda b: (0, 0))],
        out_specs=[blk, blk],
        out_shape=[out, out],
        compiler_params=_params("parallel"),
        name="fox_prep",
    )(tail, bias_row)


def _fox_attn_kernel(q_ref, k_ref, v_ref, ccol_ref, o_ref, kb_ref, vt_ref, cb0_ref, cb1_ref, m0_ref, m1_ref, l0_ref, l1_ref,
                     acc0_ref, acc1_ref, s0_ref, s1_ref):
    pair = pl.program_id(1)
    qi = pl.program_id(2)
    tq = q_ref.shape[0]
    t_len = k_ref.shape[0]
    cb_refs, m_refs, l_refs, acc_refs = (cb0_ref, cb1_ref), (m0_ref, m1_ref), (l0_ref, l1_ref), (acc0_ref, acc1_ref)
    s_refs = (s0_ref, s1_ref)

    @pl.when(qi == 0)
    def _():
        kb_ref[...] = k_ref[...].astype(BF16)
        sr, sc = _iota2((V7X_LANES, V7X_LANES))
        for j in range(2):
            sel = jnp.where(sr == 2 * pair + j, 1.0, 0.0).astype(BF16)
            cb_refs[j][...] = _expand_heads(ccol_ref[...], sel)
        vt_ref[...] = v_ref[0].astype(BF16)

    qt = (q_ref[...] * (DH_B ** -0.5)).T
    feat = lax.broadcasted_iota(I32, qt.shape, 0)
    qh = [jnp.where((feat // DH_B) == j, qt, 0.0).astype(BF16) for j in range(2)]
    key_id, qry_id = _iota2((tq, tq))
    for j in range(2):
        m_refs[j][...] = jnp.full(m_refs[j].shape, -jnp.inf, F32)
        l_refs[j][...] = jnp.zeros_like(l_refs[j])
        acc_refs[j][...] = jnp.zeros_like(acc_refs[j])

    def scores(kb, j):
        start = pl.multiple_of(kb * tq, tq)
        cb = cb_refs[j][pl.ds(start, tq), :]
        return (jnp.dot(kb_ref[pl.ds(start, tq), :], qh[j], preferred_element_type=F32)
                - jnp.concatenate([cb] * (tq // V7X_LANES), axis=1))

    def update(kb, j, s):
        start = pl.multiple_of(kb * tq, tq)
        m_old = m_refs[j][...]
        m_new = jnp.maximum(m_old, jnp.max(s, axis=0, keepdims=True))
        alpha = jnp.exp(m_old - m_new)
        pe = jnp.exp(s - m_new)
        l_refs[j][...] = alpha * l_refs[j][...] + jnp.sum(pe, axis=0, keepdims=True)
        acc_refs[j][...] = alpha * acc_refs[j][...] + jnp.dot(vt_ref[:, pl.ds(start, tq)], pe.astype(BF16),
                                                              preferred_element_type=F32)
        m_refs[j][...] = m_new

    for j in range(2):
        s_refs[j][...] = scores(0, j)

    def body(kb, carry):
        for j in range(2):
            s = s_refs[j][...]
            s_refs[j][...] = scores(kb + 1, j)
            update(kb, j, s)
        return carry

    lax.fori_loop(0, qi, body, 0)
    for j in range(2):
        update(qi, j, jnp.where(key_id <= qry_id, s_refs[j][...], -jnp.inf))
    out_t = jnp.where((feat // DH_B) == 0, acc0_ref[...] / l0_ref[...], acc1_ref[...] / l1_ref[...])
    o_ref[...] = out_t.T.astype(o_ref.dtype)


def _fox_attention(h_main, kv_t, ccol, bsz, t_len):
    n_p = bsz * t_len
    tq = ATTN_BLOCK
    nq = t_len // tq
    qcol, kcol = (4 * D_A) // V7X_LANES, (4 * D_A + D_B) // V7X_LANES
    cbs, row, acc = pltpu.VMEM((t_len, V7X_LANES), F32), pltpu.VMEM((1, tq), F32), pltpu.VMEM((V7X_LANES, tq), F32)
    return pl.pallas_call(
        _fox_attn_kernel,
        grid=(bsz, H_B // 2, nq),
        in_specs=[pl.BlockSpec((tq, V7X_LANES), lambda b, p, qi: (b * nq + qi, qcol + p)),
                  pl.BlockSpec((t_len, V7X_LANES), lambda b, p, qi: (b, kcol + p)),
                  pl.BlockSpec((1, V7X_LANES, t_len), lambda b, p, qi: (b, D_B // V7X_LANES + p, 0)),
                  pl.BlockSpec((t_len, V7X_LANES), lambda b, p, qi: (b, 0))],
        out_specs=pl.BlockSpec((tq, V7X_LANES), lambda b, p, qi: (b * nq + qi, p)),
        out_shape=jax.ShapeDtypeStruct((h_main.shape[0], D_B), BF16),
        scratch_shapes=[pltpu.VMEM((t_len, V7X_LANES), BF16), pltpu.VMEM((V7X_LANES, t_len), BF16),
                        cbs, cbs, row, row, row, row, acc, acc, pltpu.VMEM((tq, tq), F32), pltpu.VMEM((tq, tq), F32)],
        compiler_params=_params("parallel", "parallel", "arbitrary"),
        name="fox_attention",
    )(h_main, h_main, kv_t, ccol)


def _fox_decode_kernel(pt_ref, q_ref, kn_ref, vn_ref, t_ref, bias_ref, *rest):
    np_ = DECODE_PAGES
    k_refs, v_refs, lf_refs = rest[:np_], rest[np_:2 * np_], rest[2 * np_:3 * np_]
    all_ref, o_ref, lfo_ref, m_ref, l_ref, acc_ref, carry_ref, rows_ref = rest[3 * np_:]
    del all_ref
    b = pl.program_id(0)
    j = pl.program_id(1)
    q = q_ref[0] * (DH_B ** -0.5)
    hrow, hlane = _iota2((H_B, D_B))
    own = hlane // DH_B == hrow
    qmat = jnp.where(own, jnp.broadcast_to(q, (H_B, D_B)), 0.0)
    lf_new = _log_sigmoid(t_ref[0] + bias_ref[...])

    @pl.when(j == 0)
    def _():
        m_ref[...] = jnp.sum(qmat * kn_ref[0], axis=-1, keepdims=True)
        l_ref[...] = jnp.ones_like(l_ref)
        acc_ref[...] = jnp.broadcast_to(vn_ref[0], acc_ref.shape)
        carry_ref[...] = jnp.zeros_like(carry_ref)
        lfo_ref[0] = lf_new

    r8, c8 = _iota2((H_B, H_B))
    cn = jnp.sum(jnp.where(r8 == c8, jnp.broadcast_to(lf_new[:, :H_B], (H_B, H_B)), 0.0), axis=-1, keepdims=True)
    kr, kc = _iota2((PAGE_SIZE, PAGE_SIZE))
    later = jnp.where(kr > kc, 1.0, 0.0).astype(BF16)
    qb = qmat.astype(BF16)
    lfts = [lf_refs[i][0] for i in range(np_)]
    carries = [None] * np_
    run = carry_ref[...]
    for i in reversed(range(np_)):
        carries[i] = run
        run = run + jnp.sum(lfts[i], axis=-1, keepdims=True)
    carry_ref[...] = run
    scores = []
    for i in range(np_):
        lf = lfts[i]
        hi = lf.astype(BF16)
        r1 = lf - hi.astype(F32)
        mid = r1.astype(BF16)
        lo = (r1 - mid.astype(F32)).astype(BF16)
        parts = jnp.dot(jnp.concatenate([hi, mid, lo], axis=0), later, preferred_element_type=F32)
        suffix = parts[:H_B] + parts[H_B:2 * H_B] + parts[2 * H_B:]
        k2 = k_refs[i][0].reshape(D_B, PAGE_SIZE).astype(BF16)
        scores.append(jnp.dot(qb, k2, preferred_element_type=F32) + (suffix + (cn + carries[i])))
    m_old = m_ref[...]
    m_new = m_old
    for s in scores:
        m_new = jnp.maximum(m_new, jnp.max(s, axis=-1, keepdims=True))
    alpha = jnp.exp(m_old - m_new)
    l_new = alpha * l_ref[...]
    acc = alpha * acc_ref[...]
    for i, s in enumerate(scores):
        pe = jnp.exp(s - m_new)
        l_new = l_new + jnp.sum(pe, axis=-1, keepdims=True)
        v2 = v_refs[i][0].reshape(D_B, PAGE_SIZE).astype(BF16)
        acc = acc + lax.dot_general(pe.astype(BF16), v2, NT_DIMS, preferred_element_type=F32)
    m_ref[...] = m_new
    l_ref[...] = l_new
    acc_ref[...] = acc

    @pl.when(j == pl.num_programs(1) - 1)
    def _():
        o = jnp.where(own, acc / l_new, 0.0)
        rows_ref[pl.ds(b, 1), :] = jnp.sum(o, axis=0, keepdims=True)

    @pl.when((j == pl.num_programs(1) - 1) & (b == pl.num_programs(0) - 1))
    def _():
        o_ref[...] = rows_ref[...].astype(o_ref.dtype)


def _fox_decode(hs_main, hs_tail, bias_row, cache_kt, cache_vt, logf_t, page_table, ob_all):
    bd = hs_main.shape[0]
    n_p = ob_all.shape[0] - bd
    n_pages = page_table.shape[1]
    steps = n_pages // DECODE_PAGES
    qcol, kcol, vcol = (4 * D_A) // D_B, (4 * D_A + D_B) // D_B, (4 * D_A + 2 * D_B) // D_B

    def page(i, nd):
        return lambda b, j, pt: (pt[b * n_pages + (steps - 1 - j) * DECODE_PAGES + i],) + (0,) * nd

    tok = lambda col: pl.BlockSpec((1, 1, D_B), lambda b, j, pt: (b, 0, col))
    in_specs = [tok(qcol), tok(kcol), tok(vcol),
                pl.BlockSpec((1, 1, V7X_LANES), lambda b, j, pt: (b, 0, 0)),
                pl.BlockSpec((1, V7X_LANES), lambda b, j, pt: (0, 0))]
    in_specs += [pl.BlockSpec((1, H_B, DH_B, PAGE_SIZE), page(i, 3)) for i in range(DECODE_PAGES)]
    in_specs += [pl.BlockSpec((1, H_B, DH_B, PAGE_SIZE), page(i, 3)) for i in range(DECODE_PAGES)]
    in_specs += [pl.BlockSpec((1, H_B, PAGE_SIZE), page(i, 2)) for i in range(DECODE_PAGES)]
    in_specs += [pl.BlockSpec(memory_space=pl.ANY)]
    n_in = len(in_specs) + 1
    return pl.pallas_call(
        _fox_decode_kernel,
        grid_spec=pltpu.PrefetchScalarGridSpec(
            num_scalar_prefetch=1,
            grid=(bd, steps),
            in_specs=in_specs,
            out_specs=[_token_rows_spec(n_p, bd, D_B),
                       pl.BlockSpec((1, 1, V7X_LANES), lambda b, j, pt: (b, 0, 0))],
            scratch_shapes=[pltpu.VMEM((H_B, 1), F32), pltpu.VMEM((H_B, 1), F32), pltpu.VMEM((H_B, D_B), F32),
                            pltpu.VMEM((H_B, 1), F32), pltpu.VMEM((bd, D_B), F32)],
        ),
        out_shape=[jax.ShapeDtypeStruct(ob_all.shape, BF16), jax.ShapeDtypeStruct((bd, 1, V7X_LANES), F32)],
        input_output_aliases={n_in - 1: 0},
        compiler_params=_params("arbitrary", "arbitrary"),
        name="fox_decode",
    )(page_table.reshape(-1), hs_main, hs_main, hs_main, hs_tail, bias_row,
      *([cache_kt] * DECODE_PAGES), *([cache_vt] * DECODE_PAGES), *([logf_t] * DECODE_PAGES), ob_all)


def _hgrn_gates(q, z, lb):
    qa = _silu(q)
    logf = jnp.log(lb + (1.0 - lb) * jax.nn.sigmoid(z))
    ka = (1.0 - lb) * jax.nn.sigmoid(-z)
    return qa, ka, logf


def _hgrn_out(o, g, nw):
    o = o * lax.rsqrt(jnp.mean(o * o, axis=-1, keepdims=True) + RMS_EPS)
    return o * nw * _silu(g)


def _hgrn_kernel(q_ref, f_ref, i_ref, g_ref, lb_ref, nw_ref, o_ref, s_ref, qd_ref, dec_ref, oc_ref, u_ref):
    t_len = q_ref.shape[0]
    cs = CHUNK_A
    per_blk = V7X_LANES // cs
    n_blk = t_len // V7X_LANES
    lb = lb_ref[0]
    r, c = _iota2((V7X_LANES, V7X_LANES))
    same = (r // cs) == (c // cs)
    causal = same & (c <= r)
    sums = jnp.concatenate([jnp.where(causal, 1.0, 0.0), jnp.where(same, 1.0, 0.0)], axis=0).astype(BF16)

    def intra(blk, carry):
        start = pl.multiple_of(blk * V7X_LANES, V7X_LANES)
        rows = pl.ds(start, V7X_LANES)
        qa, ka, logf = _hgrn_gates(q_ref[rows, :], f_ref[rows, :], lb)
        both = _select_sum(sums, logf)
        b, gtot = both[:V7X_LANES], both[V7X_LANES:]
        qd = (qa * jnp.exp(b)).astype(BF16)
        kd = (ka * jnp.exp(-b)).astype(BF16)
        ke = (ka * jnp.exp(gtot - b)).astype(BF16)
        v = i_ref[rows, :].astype(BF16)
        qd_ref[rows, :] = qd
        sc = jnp.where(causal, lax.dot_general(qd, kd, NT_DIMS, preferred_element_type=F32), 0.0)
        oc_ref[rows, :] = jnp.dot(sc.astype(BF16), v, preferred_element_type=F32)
        dec = jnp.exp(gtot)
        for k in range(per_blk):
            sub = slice(k * cs, (k + 1) * cs)
            ci = blk * per_blk + k
            u_ref[ci] = lax.dot_general(v[sub], ke[sub], TN_DIMS, preferred_element_type=F32)
            dec_ref[pl.ds(ci, 1), :] = dec[k * cs:k * cs + 1, :]
        return carry

    lax.fori_loop(0, n_blk, intra, 0, unroll=4)

    def scan(ci, st):
        u = u_ref[ci]
        u_ref[ci] = st
        return st * dec_ref[pl.ds(ci, 1), :] + u

    st_last = lax.fori_loop(0, t_len // cs, scan, jnp.zeros((DV_A, DK_A), F32), unroll=4)
    s_ref[0, 0] = st_last.T

    def inter(blk, carry):
        start = pl.multiple_of(blk * V7X_LANES, V7X_LANES)
        rows = pl.ds(start, V7X_LANES)
        qd = qd_ref[rows, :]
        parts = [lax.dot_general(qd[k * cs:(k + 1) * cs], u_ref[blk * per_blk + k].astype(BF16), NT_DIMS,
                                 preferred_element_type=F32) for k in range(per_blk)]
        o = oc_ref[rows, :] + jnp.concatenate(parts, axis=0)
        o_ref[rows, :] = _hgrn_out(o, g_ref[rows, :], nw_ref[...]).astype(o_ref.dtype)
        return carry

    lax.fori_loop(0, n_blk, inter, 0, unroll=4)


def _hgrn_prompt(h_main, lb3, nw_row, bsz, t_len):
    n_p = bsz * t_len
    n_chunks = t_len // CHUNK_A
    col = lambda grp: (lambda b, h: (b, grp * H_A + h))
    return pl.pallas_call(
        _hgrn_kernel,
        grid=(bsz, H_A),
        in_specs=[pl.BlockSpec((t_len, DK_A), col(0)), pl.BlockSpec((t_len, DK_A), col(1)),
                  pl.BlockSpec((t_len, DV_A), col(2)), pl.BlockSpec((t_len, DV_A), col(3)),
                  pl.BlockSpec((1, 1, DK_A), lambda b, h: (h, 0, 0)), pl.BlockSpec((1, DV_A), lambda b, h: (0, 0))],
        out_specs=[pl.BlockSpec((t_len, DV_A), lambda b, h: (b, h)),
                   pl.BlockSpec((1, 1, DK_A, DV_A), lambda b, h: (b, h, 0, 0))],
        out_shape=[jax.ShapeDtypeStruct((h_main.shape[0], D_A), BF16),
                   jax.ShapeDtypeStruct((bsz, H_A, DK_A, DV_A), F32)],
        scratch_shapes=[pltpu.VMEM((t_len, DK_A), BF16), pltpu.VMEM((n_chunks, DK_A), F32),
                        pltpu.VMEM((t_len, DV_A), F32), pltpu.VMEM((n_chunks, DV_A, DK_A), F32)],
        compiler_params=_params("parallel", "parallel"),
        name="hgrn_prompt",
    )(h_main, h_main, h_main, h_main, lb3, nw_row)


def _hgrn_step_kernel(h_ref, lb_ref, nw_ref, s0_ref, all_ref, o_ref, s_ref, rows_ref):
    del all_ref
    b = pl.program_id(0)
    outs = []
    for h in range(H_A):
        grp = lambda g: h_ref[0, :, g * D_A + h * DK_A: g * D_A + (h + 1) * DK_A]
        qa, ka, logf = _hgrn_gates(grp(0), grp(1), lb_ref[h])
        v = grp(2)
        s_new = _to_column(jnp.exp(logf)) * s0_ref[0, h] + _to_column(ka) * v
        s_ref[0, h] = s_new
        o = jnp.sum(_to_column(qa) * s_new, axis=0, keepdims=True)
        outs.append(_hgrn_out(o, grp(3), nw_ref[...]))
    rows_ref[pl.ds(b, 1), :] = jnp.concatenate(outs, axis=-1)

    @pl.when(b == pl.num_programs(0) - 1)
    def _():
        o_ref[...] = rows_ref[...].astype(o_ref.dtype)


def _hgrn_step(hs_main, lb3, nw_row, s0, oa_all):
    bd = hs_main.shape[0]
    n_p = oa_all.shape[0] - bd
    return pl.pallas_call(
        _hgrn_step_kernel,
        grid=(bd,),
        in_specs=[pl.BlockSpec((1, 1, 4 * D_A), lambda b: (b, 0, 0)), pl.BlockSpec((H_A, 1, DK_A), lambda b: (0, 0, 0)),
                  pl.BlockSpec((1, DV_A), lambda b: (0, 0)), pl.BlockSpec((1, H_A, DK_A, DV_A), lambda b: (b, 0, 0, 0)),
                  pl.BlockSpec(memory_space=pl.ANY)],
        out_specs=[_token_rows_spec(n_p, bd, D_A),
                   pl.BlockSpec((1, H_A, DK_A, DV_A), lambda b: (b, 0, 0, 0))],
        out_shape=[jax.ShapeDtypeStruct(oa_all.shape, BF16), jax.ShapeDtypeStruct(s0.shape, F32)],
        scratch_shapes=[pltpu.VMEM((bd, D_A), F32)],
        input_output_aliases={4: 0},
        compiler_params=_params("arbitrary"),
        name="hgrn_step",
    )(hs_main, lb3, nw_row, s0, oa_all)


def _conv_silu(cur, prev, w, b):
    row8 = lax.broadcasted_iota(I32, prev.shape, 0)
    acc = b + cur * w[CONV_W - 1:CONV_W, :]
    for s in range(1, CONV_W):
        sh = pltpu.roll(cur, s, 0)
        head = jnp.where(row8 < s, pltpu.roll(prev, s, 0), sh[:V7X_SUBLANES, :])
        shifted = jnp.concatenate([head, sh[V7X_SUBLANES:, :]], axis=0)
        acc = acc + shifted * w[CONV_W - 1 - s:CONV_W - s, :]
    return _silu(acc)


def _gated_group_norm(y, z, nw):
    y = y * _silu(z)
    parts = []
    for g in range(N_GROUPS_C):
        seg = y[:, g * GROUP_W:(g + 1) * GROUP_W]
        parts.append(seg * lax.rsqrt(jnp.mean(seg * seg, axis=-1, keepdims=True) + RMS_EPS))
    return jnp.concatenate(parts, axis=-1) * nw


def _ssd_kernel(z_ref, x_ref, bc_ref, dt_ref, cwx_ref, cbx_ref, cwbc_ref, cbbc_ref, dtb_ref, alog_ref, dsk_ref,
                nw_ref, y_ref, hs_ref, tailx_ref, tailbc_ref, ht_ref, e_ref, yacc_ref, xw_ref):
    tb = pl.program_id(1)
    tt = x_ref.shape[0]
    pair_w = 2 * HEADDIM_C
    heads_per_group = H_C // N_GROUPS_C

    @pl.when(tb == 0)
    def _():
        tailx_ref[...] = jnp.zeros_like(tailx_ref)
        tailbc_ref[...] = jnp.zeros_like(tailbc_ref)
        ht_ref[...] = jnp.zeros_like(ht_ref)
        er, ec = _iota2(e_ref.shape)
        e_ref[...] = jnp.where(ec // HEADDIM_C == er, 1.0, 0.0).astype(BF16)

    x_raw = x_ref[...]
    bc_raw = bc_ref[...]
    xs = _conv_silu(x_raw, tailx_ref[...], cwx_ref[...], cbx_ref[...])
    bcv = _conv_silu(bc_raw, tailbc_ref[...], cwbc_ref[...], cbbc_ref[...])
    tailx_ref[...] = x_raw[tt - V7X_SUBLANES:, :]
    tailbc_ref[...] = bc_raw[tt - V7X_SUBLANES:, :]

    lane = lax.broadcasted_iota(I32, (tt, V7X_LANES), 1)
    dt = jnp.where(lane < H_C, _softplus(dt_ref[...] + dtb_ref[...]), 0.0)
    a = -jnp.exp(alog_ref[...])
    r, c = _iota2((tt, tt))
    causal = c <= r
    cum = jnp.dot(jnp.where(causal, 1.0, 0.0), dt * a, precision=HIGHEST, preferred_element_type=F32)
    cum_t = cum.T
    xdt = xs * _expand_heads(dt, e_ref[...])
    low = (lane % pair_w) < HEADDIM_C

    for g in range(N_GROUPS_C):
        b_g = bcv[:, g * D_STATE_C:(g + 1) * D_STATE_C]
        c_g = bcv[:, D_BC + g * D_STATE_C:D_BC + (g + 1) * D_STATE_C].astype(BF16)
        cb = lax.dot_general(c_g, b_g.astype(BF16), NT_DIMS, preferred_element_type=F32)
        y_inter = jnp.dot(c_g, ht_ref[g].astype(BF16), preferred_element_type=F32)
        decs = []
        for pr in range(heads_per_group // 2):
            slab = slice(g * GROUP_W + pr * pair_w, g * GROUP_W + (pr + 1) * pair_w)
            xdt_slab = xdt[:, slab]
            ys, es, tes = [], [], []
            for j in range(2):
                head = g * heads_per_group + pr * 2 + j
                colb = jnp.broadcast_to(cum[:, head:head + 1], (tt, tt))
                decay = jnp.exp(jnp.where(causal, colb - cum_t[head:head + 1, :], -jnp.inf))
                ys.append(jnp.dot((cb * decay).astype(BF16), xdt_slab.astype(BF16), preferred_element_type=F32))
                es.append(jnp.exp(colb))
                tes.append(jnp.exp(colb[tt - 1:, :] - colb))
            e_pair = jnp.where(low, es[0], es[1])
            yacc_ref[:, slab] = (jnp.where(low, ys[0], ys[1]) + e_pair * y_inter[:, pr * pair_w:(pr + 1) * pair_w]
                                 + dsk_ref[:, slab] * xs[:, slab])
            xw_ref[:, pr * pair_w:(pr + 1) * pair_w] = xdt_slab * jnp.where(low, tes[0], tes[1])
            decs.append(e_pair[tt - 1:, :])
        dec_row = jnp.concatenate(decs, axis=-1)
        ht_ref[g] = ht_ref[g] * dec_row + jnp.dot(b_g.T.astype(BF16), xw_ref[...].astype(BF16),
                                                  preferred_element_type=F32)

    y_ref[...] = _gated_group_norm(yacc_ref[...], z_ref[...], nw_ref[...]).astype(y_ref.dtype)

    @pl.when(tb == pl.num_programs(1) - 1)
    def _():
        for g in range(N_GROUPS_C):
            for q in range(GROUP_W // V7X_LANES):
                rows = slice(g * GROUP_W + q * V7X_LANES, g * GROUP_W + (q + 1) * V7X_LANES)
                hs_ref[0, rows, :] = ht_ref[g][:, q * V7X_LANES:(q + 1) * V7X_LANES].T


def _ssd_prompt(h_main, h_tail, conv_w, conv_b, dtb_row, alog_row, dsk_row, nw_row, bsz, t_len):
    n_p = bsz * t_len
    tt = SSD_BLOCK
    nt = t_len // tt
    rowmap = lambda col: (lambda b, t: (b * nt + t, col))
    fixed = lambda b, t: (0, 0)
    cwx, cwbc = conv_w[:, :D_INNER], conv_w[:, D_INNER:]
    cbx, cbbc = conv_b[None, :D_INNER], conv_b[None, D_INNER:]
    return pl.pallas_call(
        _ssd_kernel,
        grid=(bsz, nt),
        in_specs=[pl.BlockSpec((tt, D_INNER), rowmap(0)), pl.BlockSpec((tt, D_INNER), rowmap(1)),
                  pl.BlockSpec((tt, 2 * D_BC), rowmap(2 * D_INNER // (2 * D_BC))),
                  pl.BlockSpec((tt, V7X_LANES), rowmap(0)),
                  pl.BlockSpec((CONV_W, D_INNER), fixed), pl.BlockSpec((1, D_INNER), fixed),
                  pl.BlockSpec((CONV_W, 2 * D_BC), fixed), pl.BlockSpec((1, 2 * D_BC), fixed),
                  pl.BlockSpec((1, V7X_LANES), fixed), pl.BlockSpec((1, V7X_LANES), fixed),
                  pl.BlockSpec((1, D_INNER), fixed), pl.BlockSpec((1, D_INNER), fixed)],
        out_specs=[pl.BlockSpec((tt, D_INNER), rowmap(0)),
                   pl.BlockSpec((1, D_INNER, D_STATE_C), lambda b, t: (b, 0, 0))],
        out_shape=[jax.ShapeDtypeStruct((h_main.shape[0], D_INNER), BF16),
                   jax.ShapeDtypeStruct((bsz, D_INNER, D_STATE_C), F32)],
        scratch_shapes=[pltpu.VMEM((V7X_SUBLANES, D_INNER), F32), pltpu.VMEM((V7X_SUBLANES, 2 * D_BC), F32),
                        pltpu.VMEM((N_GROUPS_C, D_STATE_C, GROUP_W), F32), pltpu.VMEM((V7X_LANES, D_INNER), BF16),
                        pltpu.VMEM((tt, D_INNER), F32), pltpu.VMEM((tt, GROUP_W), F32)],
        compiler_params=_params("parallel", "arbitrary"),
        name="ssd_prompt",
    )(h_main, h_main, h_main, h_tail, cwx, cbx, cwbc, cbbc, dtb_row, alog_row, dsk_row, nw_row)


def _ssd_step_kernel(h_ref, t_ref, cs_ref, cw_ref, cb_ref, dtb_ref, alog_ref, dsk_ref, nw_ref, h0_ref, all_ref,
                     y_ref, hn_ref, rows_ref):
    del all_ref
    b = pl.program_id(0)
    z = h_ref[0, :, :D_INNER]
    xbc_new = h_ref[0, :, D_INNER:]
    cw = cw_ref[...]
    conv = cb_ref[...] + xbc_new * cw[CONV_W - 1:CONV_W, :]
    for j in range(CONV_W - 1):
        conv = conv + cs_ref[0, j:j + 1, :] * cw[j:j + 1, :]
    xbc = _silu(conv)
    xs = xbc[:, :D_INNER]
    lane = lax.broadcasted_iota(I32, (1, V7X_LANES), 1)
    dt = jnp.where(lane < H_C, _softplus(t_ref[0] + dtb_ref[...]), 0.0)
    da = jnp.exp(dt * -jnp.exp(alog_ref[...]))
    er, ec = _iota2((V7X_LANES, D_INNER))
    expand = jnp.where(ec // HEADDIM_C == er, 1.0, 0.0).astype(BF16)
    rows8 = lambda v: jnp.broadcast_to(v, (V7X_SUBLANES, V7X_LANES))
    dt_x = _expand_heads(rows8(dt), expand)[:1, :]
    da_x = _expand_heads(rows8(da), expand)[:1, :]
    xdt = xs * dt_x
    y_parts = []
    for q in range(D_INNER // V7X_LANES):
        g = q // (GROUP_W // V7X_LANES)
        lanes = slice(q * V7X_LANES, (q + 1) * V7X_LANES)
        b_g = xbc[:, D_INNER + g * D_STATE_C:D_INNER + (g + 1) * D_STATE_C]
        c_g = xbc[:, D_INNER + D_BC + g * D_STATE_C:D_INNER + D_BC + (g + 1) * D_STATE_C]
        h_new = _to_column(da_x[:, lanes]) * h0_ref[0, lanes, :] + _to_column(xdt[:, lanes]) * b_g
        hn_ref[0, lanes, :] = h_new
        y_parts.append(_to_row(jnp.sum(h_new * c_g, axis=-1, keepdims=True)))
    y = jnp.concatenate(y_parts, axis=-1) + dsk_ref[...] * xs
    rows_ref[pl.ds(b, 1), :] = _gated_group_norm(y, z, nw_ref[...])

    @pl.when(b == pl.num_programs(0) - 1)
    def _():
        y_ref[...] = rows_ref[...].astype(y_ref.dtype)


def _ssd_step(hs_main, hs_tail, conv_state, conv_w, conv_b, dtb_row, alog_row, dsk_row, nw_row, h0, y_all):
    bd = hs_main.shape[0]
    n_p = y_all.shape[0] - bd
    fixed = lambda b: (0, 0)
    tok = lambda b: (b, 0, 0)
    return pl.pallas_call(
        _ssd_step_kernel,
        grid=(bd,),
        in_specs=[pl.BlockSpec((1, 1, ODD_MAIN), tok), pl.BlockSpec((1, 1, V7X_LANES), tok),
                  pl.BlockSpec((1, CONV_W - 1, CONV_DIM), tok),
                  pl.BlockSpec((CONV_W, CONV_DIM), fixed), pl.BlockSpec((1, CONV_DIM), fixed),
                  pl.BlockSpec((1, V7X_LANES), fixed), pl.BlockSpec((1, V7X_LANES), fixed),
                  pl.BlockSpec((1, D_INNER), fixed), pl.BlockSpec((1, D_INNER), fixed),
                  pl.BlockSpec((1, D_INNER, D_STATE_C), tok), pl.BlockSpec(memory_space=pl.ANY)],
        out_specs=[_token_rows_spec(n_p, bd, D_INNER), pl.BlockSpec((1, D_INNER, D_STATE_C), tok)],
        out_shape=[jax.ShapeDtypeStruct(y_all.shape, BF16), jax.ShapeDtypeStruct((bd, D_INNER, D_STATE_C), F32)],
        scratch_shapes=[pltpu.VMEM((bd, D_INNER), F32)],
        input_output_aliases={10: 0},
        compiler_params=_params("arbitrary"),
        name="ssd_step",
    )(hs_main, hs_tail, conv_state, conv_w, conv_b[None], dtb_row, alog_row, dsk_row, nw_row, h0, y_all)


def _pad_lanes(v):
    return jnp.pad(v, (0, V7X_LANES - v.shape[0]))[None]


def kernel(x_prompt, x_sample, cache_k, cache_v, cache_logf, page_table, state_hgrn, state_ssm, state_conv,
           w_in_even, hgrn_lower_bound, hgrn_norm_w, fox_f_bias, w_out_even,
           w_in_odd, conv_w, conv_b, dt_bias, a_log, d_skip, ssm_norm_w, w_out_odd,
           ln1_g, ln1_b, ln2_g, ln2_b, router_w, router_b, exp_w1, exp_b1, exp_w2, exp_b2):
    lb_all = jnp.cumsum(jax.nn.softmax(hgrn_lower_bound, axis=0), axis=0)
    bp, t_p, d = x_prompt.shape
    bd, t_d, _ = x_sample.shape
    assert t_d == 1
    n_p = bp * t_p
    x_all = jnp.concatenate([x_prompt.reshape(n_p, d), x_sample.reshape(bd, d)], axis=0)
    x_bf = x_all.astype(BF16)
    w2_all = exp_w2.astype(BF16)
    outs = {}
    for l in range(DEPTH):
        i = l // 2
        if l % 2 == 0:
            h_main = _matmul(x_bf, w_in_even[i][:, :EVEN_MAIN].astype(BF16))
            h_tail = _matmul(x_bf, w_in_even[i][:, EVEN_MAIN:].astype(BF16))
            hs_main, hs_tail = h_main[n_p:, None, :], h_tail[n_p:, None, :]
            lb3 = lb_all[i].reshape(H_A, 1, DK_A)
            nw_row = hgrn_norm_w[i][None]
            bias_row = _pad_lanes(fox_f_bias[i])
            lf_p, ccol = _fox_prep(h_tail, bias_row, bp, t_p)
            kcol, vcol = 4 * D_A + D_B, 4 * D_A + 2 * D_B
            kv_t = _proj_t(x_bf, w_in_even[i][:, kcol:kcol + 2 * D_B].T.astype(BF16), bp, t_p)
            ob_p = _fox_attention(h_main, kv_t, ccol, bp, t_p)
            oa_p, hg_p = _hgrn_prompt(h_main, lb3, nw_row, bp, t_p)
            oa_all, hg_s = _hgrn_step(hs_main, lb3, nw_row, state_hgrn[i], oa_p)
            ob_all, lf_s = _fox_decode(hs_main, hs_tail, bias_row, jnp.transpose(cache_k[i], (0, 2, 3, 1)),
                                       jnp.transpose(cache_v[i], (0, 2, 3, 1)), jnp.swapaxes(cache_logf[i], 1, 2),
                                       page_table, ob_p)
            w_out = w_out_even[i].astype(BF16)
            parts = [(oa_all, w_out[:D_A]), (ob_all, w_out[D_A:])]
            heads_last = lambda a: jnp.transpose(a.reshape(bp, H_B, DH_B, t_p), (0, 3, 1, 2))
            vals = (("kp", heads_last(kv_t[:, :D_B])),
                    ("ks", h_main[n_p:, kcol:kcol + D_B].reshape(bd, t_d, H_B, DH_B)),
                    ("vp", heads_last(kv_t[:, D_B:])),
                    ("vs", h_main[n_p:, vcol:vcol + D_B].reshape(bd, t_d, H_B, DH_B)),
                    ("lfp", lf_p[:, :H_B].reshape(bp, t_p, H_B)), ("lfs", lf_s[:, :, :H_B]),
                    ("hgp", hg_p), ("hgs", hg_s))
        else:
            h_main = _matmul(x_bf, w_in_odd[i][:, :ODD_MAIN].astype(BF16))
            h_tail = _matmul(x_bf, w_in_odd[i][:, ODD_MAIN:].astype(BF16))
            hs_main, hs_tail = h_main[n_p:, None, :], h_tail[n_p:, None, :]
            dtb_row, alog_row = _pad_lanes(dt_bias[i]), _pad_lanes(a_log[i])
            dsk_row = jnp.repeat(d_skip[i], HEADDIM_C)[None]
            nw_row = ssm_norm_w[i][None]
            y_p, ss_p = _ssd_prompt(h_main, h_tail, conv_w[i], conv_b[i], dtb_row, alog_row, dsk_row, nw_row, bp, t_p)
            y_all, ss_s = _ssd_step(hs_main, hs_tail, state_conv[i], conv_w[i], conv_b[i], dtb_row, alog_row, dsk_row,
                                    nw_row, state_ssm[i].reshape(bd, D_INNER, D_STATE_C), y_p)
            parts = [(y_all, w_out_odd[i].astype(BF16))]
            tail_rows = jnp.stack([h_main[(b + 1) * t_p - (CONV_W - 1):(b + 1) * t_p, D_INNER:] for b in range(bp)])
            vals = (("ssp", ss_p.reshape(bp, H_C, HEADDIM_C, D_STATE_C)),
                    ("sss", ss_s.reshape(bd, H_C, HEADDIM_C, D_STATE_C)),
                    ("cvp", tail_rows),
                    ("cvs", jnp.concatenate([state_conv[i][:, 1:], hs_main[:, :, D_INNER:]], axis=1)))
        for name, val in vals:
            outs.setdefault(name, []).append(val)
        rw = jnp.pad(router_w[l], ((0, 0), (0, V7X_LANES - N_EXPERTS)))
        rb = jnp.concatenate([router_b[l], jnp.full((V7X_LANES - N_EXPERTS,), -jnp.inf, F32)])[None]
        x1, eids, gates = _post_mixer(x_all, parts, ln1_g[l][None], ln1_b[l][None], rw, rb)
        w1g, w1u = _w1_prep(exp_w1, l)
        b1g = exp_b1[l][:, None, 0::2]
        b1u = exp_b1[l][:, None, 1::2]
        more = l + 1 < DEPTH
        res = _moe_ln(x1, eids, gates, w1g, w1u, b1g, b1u, w2_all, l, exp_b2[l][:, None, :],
                      ln2_g[l][None], ln2_b[l][None], more)
        x_all = res[0]
        x_bf = res[1] if more else None
    st = {k: jnp.stack(v) for k, v in outs.items()}
    return (x_all[:n_p].reshape(bp, t_p, d), x_all[n_p:].reshape(bd, t_d, d),
            st["kp"], st["ks"], st["vp"], st["vs"], st["lfp"], st["lfs"],
            st["hgp"], st["hgs"], st["ssp"], st["sss"], st["cvp"], st["cvs"])
```

```python
import functools

import jax
import jax.numpy as jnp
import numpy as np
from jax import lax
from jax.experimental import pallas as pl
from jax.experimental.pallas import tpu as pltpu

F32 = jnp.float32
BF16 = jnp.bfloat16
I32 = jnp.int32
HIGHEST = lax.Precision.HIGHEST

D_MODEL = 1024
DEPTH = 2
PAGE_SIZE = 128
H_A, DK_A, DV_A, CHUNK_A = 4, 128, 128, 32
H_B, DH_B = 8, 64
D_A = H_A * DK_A
D_B = H_B * DH_B
EVEN_MAIN = 4 * D_A + 3 * D_B
D_INNER = 2 * D_MODEL
HEADDIM_C = 64
H_C = D_INNER // HEADDIM_C
N_GROUPS_C = 4
D_STATE_C = 128
D_BC = N_GROUPS_C * D_STATE_C
GROUP_W = D_INNER // N_GROUPS_C
CONV_W = 4
CONV_DIM = D_INNER + 2 * D_BC
ODD_MAIN = D_INNER + CONV_DIM
N_EXPERTS = 32
TOP_K = 4
D_FF = D_MODEL
SWIGLU_LIMIT = 7.0
SWIGLU_ALPHA = 1.702
DN_ALPHA = (2 * DEPTH) ** 0.25
LN_EPS = 1e-5
RMS_EPS = 1e-6

V7X_LANES = 128
V7X_SUBLANES = 8
V7X_MXU = 256
VMEM_LIMIT = 56 * 1024 * 1024
EXPERT_TILE = 512
ATTN_BLOCK = 256
SSD_BLOCK = 128
DECODE_PAGES = 16
DMA_UNROLL = 8

NT_DIMS = (((1,), (1,)), ((), ()))
TN_DIMS = (((0,), (0,)), ((), ()))


def _row_tile(m, cap):
    best = 0
    for t in range(16, cap + 1, 16):
        if m % t == 0:
            best = t
    assert best, (m, cap)
    return best


def _token_rows_spec(n_rows, bd, width):
    assert n_rows % bd == 0
    return pl.BlockSpec((bd, width), lambda *_: (n_rows // bd, 0))


def _params(*sem):
    return pltpu.CompilerParams(dimension_semantics=sem, vmem_limit_bytes=VMEM_LIMIT)


def _silu(x):
    return x * jax.nn.sigmoid(x)


def _softplus(x):
    return jnp.maximum(x, 0.0) + jnp.log1p(jnp.exp(-jnp.abs(x)))


def _log_sigmoid(x):
    return -_softplus(-x)


def _iota2(shape):
    return lax.broadcasted_iota(I32, shape, 0), lax.broadcasted_iota(I32, shape, 1)


def _to_column(row_vec):
    n = row_vec.shape[1]
    r, c = _iota2((n, n))
    return jnp.sum(jnp.where(r == c, jnp.broadcast_to(row_vec, (n, n)), 0.0), axis=1, keepdims=True)


def _to_row(col_vec):
    n = col_vec.shape[0]
    r, c = _iota2((n, n))
    return jnp.sum(jnp.where(r == c, jnp.broadcast_to(col_vec, (n, n)), 0.0), axis=0, keepdims=True)


def _expand_heads(v, e_bf16):
    hi = v.astype(BF16)
    r1 = v - hi.astype(F32)
    mid = r1.astype(BF16)
    lo = (r1 - mid.astype(F32)).astype(BF16)
    dot = lambda a: jnp.dot(a, e_bf16, preferred_element_type=F32)
    return dot(hi) + dot(mid) + dot(lo)


def _select_sum(m01_bf16, x):
    hi = x.astype(BF16)
    r1 = x - hi.astype(F32)
    mid = r1.astype(BF16)
    lo = (r1 - mid.astype(F32)).astype(BF16)
    dot = lambda a: jnp.dot(m01_bf16, a, preferred_element_type=F32)
    return dot(hi) + dot(mid) + dot(lo)


def _mm_kernel(x_ref, w_ref, o_ref):
    o_ref[...] = jnp.dot(x_ref[...].astype(BF16), w_ref[...].astype(BF16), preferred_element_type=F32)


def _matmul(x, w, tm_cap=1024):
    m, k = x.shape
    n = w.shape[1]
    n_pad = -(-n // V7X_LANES) * V7X_LANES
    if n_pad != n:
        w = jnp.pad(w, ((0, 0), (0, n_pad - n)))
    tm = _row_tile(m, tm_cap)
    tn = next(t for t in (1024, 512, V7X_MXU, V7X_LANES) if n_pad % t == 0)
    return pl.pallas_call(
        _mm_kernel,
        grid=(m // tm, n_pad // tn),
        in_specs=[pl.BlockSpec((tm, k), lambda i, j: (i, 0)),
                  pl.BlockSpec((k, tn), lambda i, j: (0, j))],
        out_specs=pl.BlockSpec((tm, tn), lambda i, j: (i, j)),
        out_shape=jax.ShapeDtypeStruct((m, n_pad), F32),
        compiler_params=_params("parallel", "parallel"),
        name="dense_matmul",
    )(x, w)


def _proj_t_kernel(w_ref, x_ref, o_ref):
    o_ref[0] = lax.dot_general(w_ref[...], x_ref[...].astype(BF16), NT_DIMS, preferred_element_type=F32)


def _proj_t(x, w_t_bf16, bsz, t_len, tm=512):
    n, k = w_t_bf16.shape
    nt = t_len // tm
    return pl.pallas_call(
        _proj_t_kernel,
        grid=(bsz, nt),
        in_specs=[pl.BlockSpec((n, k), lambda b, t: (0, 0)), pl.BlockSpec((tm, k), lambda b, t: (b * nt + t, 0))],
        out_specs=pl.BlockSpec((1, n, tm), lambda b, t: (b, 0, t)),
        out_shape=jax.ShapeDtypeStruct((bsz, n, t_len), F32),
        compiler_params=_params("parallel", "parallel"),
        name="proj_transposed",
    )(w_t_bf16, x)


def _post_mixer_kernel(n_parts, x_ref, *refs):
    lhs_refs, w_refs = refs[:n_parts], refs[n_parts:2 * n_parts]
    g_ref, b_ref, rwh_ref, rwl_ref, rb_ref, x1_ref, eid_ref, gate_ref = refs[2 * n_parts:]
    acc = DN_ALPHA * x_ref[...]
    for lhs_ref, w_ref in zip(lhs_refs, w_refs):
        acc = acc + jnp.dot(lhs_ref[...].astype(BF16), w_ref[...], preferred_element_type=F32)
    xc = acc - jnp.mean(acc, axis=-1, keepdims=True)
    var = jnp.mean(xc * xc, axis=-1, keepdims=True)
    x1 = xc * lax.rsqrt(var + LN_EPS) * g_ref[...] + b_ref[...]
    x1_ref[...] = x1
    x_hi = x1.astype(BF16)
    x_lo = (x1 - x_hi.astype(F32)).astype(BF16)
    dot = lambda a, w_ref: jnp.dot(a, w_ref[...], preferred_element_type=F32)
    logits = dot(x_hi, rwh_ref) + (dot(x_lo, rwh_ref) + dot(x_hi, rwl_ref)) + rb_ref[...]
    lane = lax.broadcasted_iota(I32, logits.shape, 1)
    eids = jnp.zeros(logits.shape, I32)
    vals = []
    for k in range(TOP_K):
        top = jnp.max(logits, axis=-1, keepdims=True)
        idx = jnp.min(jnp.where(logits == top, lane, V7X_LANES), axis=-1, keepdims=True)
        vals.append(top)
        eids = jnp.where(lane == k, idx, eids)
        logits = jnp.where(lane == idx, -jnp.inf, logits)
    exps = [jnp.exp(v - vals[0]) for v in vals]
    denom = exps[0] + exps[1] + exps[2] + exps[3]
    gates = jnp.zeros(logits.shape, F32)
    for k in range(TOP_K):
        gates = jnp.where(lane == k, exps[k] / denom, gates)
    eid_ref[...] = eids
    gate_ref[...] = gates


def _post_mixer(x, parts, g, b, rw, rb):
    m, d = x.shape
    tm = _row_tile(m, 640)
    row = lambda i: (i, 0)
    fixed = lambda i: (0, 0)
    rw_hi = rw.astype(BF16)
    rw_lo = (rw - rw_hi.astype(F32)).astype(BF16)
    lhs_specs = [pl.BlockSpec((tm, lhs.shape[1]), row) for lhs, _ in parts]
    w_specs = [pl.BlockSpec(w.shape, fixed) for _, w in parts]
    return pl.pallas_call(
        functools.partial(_post_mixer_kernel, len(parts)),
        grid=(m // tm,),
        in_specs=[pl.BlockSpec((tm, d), row)] + lhs_specs + w_specs + [
                  pl.BlockSpec((1, d), fixed), pl.BlockSpec((1, d), fixed),
                  pl.BlockSpec((d, V7X_LANES), fixed), pl.BlockSpec((d, V7X_LANES), fixed),
                  pl.BlockSpec((1, V7X_LANES), fixed)],
        out_specs=[pl.BlockSpec((tm, d), row), pl.BlockSpec((tm, V7X_LANES), row),
                   pl.BlockSpec((tm, V7X_LANES), row)],
        out_shape=[jax.ShapeDtypeStruct((m, d), F32), jax.ShapeDtypeStruct((m, V7X_LANES), I32),
                   jax.ShapeDtypeStruct((m, V7X_LANES), F32)],
        compiler_params=_params("parallel"),
        name="post_mixer",
    )(x, *[lhs for lhs, _ in parts], *[w for _, w in parts], g, b, rw_hi, rw_lo, rb)


def _rank_kernel(eid_ref, rank_ref, cnt_ref, carry_ref):
    i = pl.program_id(0)

    @pl.when(i == 0)
    def _():
        carry_ref[...] = jnp.zeros_like(carry_ref)

    eids = eid_ref[...]
    tm = eids.shape[0]
    lane = lax.broadcasted_iota(I32, eids.shape, 1)
    sel = [jnp.sum(jnp.where(lane == k, eids, 0), axis=-1, keepdims=True) for k in range(TOP_K)]
    onehot = jnp.zeros(eids.shape, F32)
    for k in range(TOP_K):
        onehot = onehot + (lane == sel[k]).astype(F32)
    r, c = _iota2((tm, tm))
    before = (c < r).astype(BF16)
    prior = jnp.dot(before, onehot.astype(BF16), preferred_element_type=F32) + carry_ref[...]
    ranks = jnp.zeros(eids.shape, F32)
    for k in range(TOP_K):
        rk = jnp.sum(jnp.where(lane == sel[k], prior, 0.0), axis=-1, keepdims=True)
        ranks = jnp.where(lane == k, rk, ranks)
    rank_ref[...] = ranks.astype(I32)
    total = carry_ref[...] + jnp.sum(onehot, axis=0, keepdims=True)
    carry_ref[...] = total
    cnt_ref[...] = total.astype(I32)


def _route_ranks(eids):
    m = eids.shape[0]
    tm = _row_tile(m, 640)
    return pl.pallas_call(
        _rank_kernel,
        grid=(m // tm,),
        in_specs=[pl.BlockSpec((tm, V7X_LANES), lambda i: (i, 0))],
        out_specs=[pl.BlockSpec((tm, V7X_LANES), lambda i: (i, 0)), pl.BlockSpec((1, V7X_LANES), lambda i: (0, 0))],
        out_shape=[jax.ShapeDtypeStruct((m, V7X_LANES), I32), jax.ShapeDtypeStruct((1, V7X_LANES), I32)],
        scratch_shapes=[pltpu.VMEM((1, V7X_LANES), F32)],
        compiler_params=_params("arbitrary"),
        name="route_ranks",
    )(eids)


def _row_copy(src_ref, s, dst_ref, d, sem):
    return pltpu.make_async_copy(src_ref.at[pl.ds(s, 1)], dst_ref.at[pl.ds(d, 1)], sem)


def _dispatch_kernel(pos_ref, x_ref, out_ref, sem):
    i = pl.program_id(0)
    tm = x_ref.shape[0]
    base = i * tm * TOP_K

    def start(t, carry):
        for k in range(TOP_K):
            _row_copy(x_ref, t, out_ref, pos_ref[base + t * TOP_K + k], sem).start(priority=k % 2)
        return carry

    def wait(t, carry):
        for k in range(TOP_K):
            _row_copy(x_ref, t, out_ref, pos_ref[base + t * TOP_K + k], sem).wait()
        return carry

    lax.fori_loop(0, tm, start, 0, unroll=DMA_UNROLL)
    lax.fori_loop(0, tm, wait, 0, unroll=DMA_UNROLL)


def _dispatch_rows(x1, pos_flat, n_rows):
    m, d = x1.shape
    tm = _row_tile(m, 1024)
    return pl.pallas_call(
        _dispatch_kernel,
        grid_spec=pltpu.PrefetchScalarGridSpec(
            num_scalar_prefetch=1,
            grid=(m // tm,),
            in_specs=[pl.BlockSpec((tm, d), lambda i, pos: (i, 0))],
            out_specs=pl.BlockSpec(memory_space=pl.ANY),
            scratch_shapes=[pltpu.SemaphoreType.DMA(())],
        ),
        out_shape=jax.ShapeDtypeStruct((n_rows, d), x1.dtype),
        compiler_params=_params("arbitrary"),
        name="moe_dispatch",
    )(pos_flat, x1)


def _expert_weight_prep(w1_ref, w2_ref, w1g_ref, w1u_ref, w2b_ref):
    r, c = _iota2((V7X_MXU, V7X_MXU))
    half = V7X_MXU // 2
    src = jnp.where(c < half, 2 * c, 2 * (c - half) + 1)
    perm = jnp.where(r == src, 1.0, 0.0).astype(BF16)
    for j in range(w1_ref.shape[3] // V7X_MXU):
        blk = w1_ref[0, 0, :, j * V7X_MXU:(j + 1) * V7X_MXU].astype(BF16)
        res = jnp.dot(blk, perm, preferred_element_type=F32).astype(BF16)
        w1g_ref[:, j * half:(j + 1) * half] = res[:, :half]
        w1u_ref[:, j * half:(j + 1) * half] = res[:, half:]
    w2b_ref[...] = w2_ref[0, 0].astype(BF16)


def _ffn_kernel(te_ref, tr_ref, nv_ref, x_ref, w1_ref, b1g_ref, b1u_ref, w2_ref, b2_ref, o_ref,
                w1g_ref, w1u_ref, w2b_ref):
    i = pl.program_id(0)
    live = i < nv_ref[0]

    @pl.when(live & ((i == 0) | (te_ref[i] != te_ref[jnp.maximum(i - 1, 0)])))
    def _():
        _expert_weight_prep(w1_ref, w2_ref, w1g_ref, w1u_ref, w2b_ref)

    @pl.when(live)
    def _():
        row = lax.broadcasted_iota(I32, x_ref.shape, 0)
        x = jnp.where(row < tr_ref[i], x_ref[...], 0.0).astype(BF16)
        hg = jnp.dot(x, w1g_ref[...], preferred_element_type=F32) + b1g_ref[0]
        hu = jnp.dot(x, w1u_ref[...], preferred_element_type=F32) + b1u_ref[0]
        gate = jnp.minimum(hg, SWIGLU_LIMIT)
        up = jnp.clip(hu, -SWIGLU_LIMIT, SWIGLU_LIMIT)
        act = (up + 1.0) * gate * jax.nn.sigmoid(SWIGLU_ALPHA * gate)
        o_ref[...] = jnp.dot(act.astype(BF16), w2b_ref[...], preferred_element_type=F32) + b2_ref[0]


def _expert_ffn(x_rows, tile_expert, tile_rows, n_valid, w1_all, b1g, b1u, w2_all, layer, b2):
    rows, d = x_rows.shape
    tm = EXPERT_TILE
    rmap = lambda i, te, tr, nv: (jnp.minimum(i, nv[0] - 1), 0)
    wmap = lambda i, te, tr, nv: (te[i], 0, 0)
    lmap = lambda i, te, tr, nv: (layer, te[i], 0, 0)
    return pl.pallas_call(
        _ffn_kernel,
        grid_spec=pltpu.PrefetchScalarGridSpec(
            num_scalar_prefetch=3,
            grid=(rows // tm,),
            in_specs=[pl.BlockSpec((tm, d), rmap),
                      pl.BlockSpec((1, 1, d, 2 * D_FF), lmap),
                      pl.BlockSpec((1, 1, D_FF), wmap), pl.BlockSpec((1, 1, D_FF), wmap),
                      pl.BlockSpec((1, 1, D_FF, d), lmap),
                      pl.BlockSpec((1, 1, d), wmap)],
            out_specs=pl.BlockSpec((tm, d), rmap),
            scratch_shapes=[pltpu.VMEM((d, D_FF), BF16), pltpu.VMEM((d, D_FF), BF16), pltpu.VMEM((D_FF, d), BF16)],
        ),
        out_shape=jax.ShapeDtypeStruct((rows, d), F32),
        compiler_params=_params("arbitrary"),
        name="moe_expert_ffn",
    )(tile_expert, tile_rows, n_valid, x_rows, w1_all, b1g, b1u, w2_all, b2)


def _combine_kernel(pos_ref, y_ref, gate_ref, x_ref, g_ref, b_ref, o_ref, *rest):
    ob_ref = rest[0] if len(rest) == 3 else None
    buf_ref, sem = rest[-2:]
    i = pl.program_id(0)
    tm = x_ref.shape[0]
    base = i * tm * TOP_K

    def start(t, carry):
        for k in range(TOP_K):
            _row_copy(y_ref, pos_ref[base + t * TOP_K + k], buf_ref.at[k], t, sem).start(priority=k % 2)
        return carry

    def wait(t, carry):
        for k in range(TOP_K):
            _row_copy(y_ref, pos_ref[base + t * TOP_K + k], buf_ref.at[k], t, sem).wait()
        return carry

    lax.fori_loop(0, tm, start, 0, unroll=DMA_UNROLL)
    lax.fori_loop(0, tm, wait, 0, unroll=DMA_UNROLL)
    gates = gate_ref[...]
    acc = DN_ALPHA * x_ref[...]
    for k in range(TOP_K):
        acc = acc + gates[:, k:k + 1] * buf_ref[k]
    xc = acc - jnp.mean(acc, axis=-1, keepdims=True)
    var = jnp.mean(xc * xc, axis=-1, keepdims=True)
    out = xc * lax.rsqrt(var + LN_EPS) * g_ref[...] + b_ref[...]
    o_ref[...] = out
    if ob_ref is not None:
        ob_ref[...] = out.astype(BF16)


def _combine_ln(y_rows, pos_flat, gates, x1, g, b, emit_bf16):
    m, d = x1.shape
    tm = _row_tile(m, 320)
    row = lambda i, pos: (i, 0)
    fixed = lambda i, pos: (0, 0)
    return pl.pallas_call(
        _combine_kernel,
        grid_spec=pltpu.PrefetchScalarGridSpec(
            num_scalar_prefetch=1,
            grid=(m // tm,),
            in_specs=[pl.BlockSpec(memory_space=pl.ANY), pl.BlockSpec((tm, V7X_LANES), row),
                      pl.BlockSpec((tm, d), row), pl.BlockSpec((1, d), fixed), pl.BlockSpec((1, d), fixed)],
            out_specs=[pl.BlockSpec((tm, d), row)] * (2 if emit_bf16 else 1),
            scratch_shapes=[pltpu.VMEM((TOP_K, tm, d), F32), pltpu.SemaphoreType.DMA(())],
        ),
        out_shape=[jax.ShapeDtypeStruct((m, d), F32)] + ([jax.ShapeDtypeStruct((m, d), BF16)] if emit_bf16 else []),
        compiler_params=_params("arbitrary"),
        name="moe_combine_ln",
    )(pos_flat, y_rows, gates, x1, g, b)


def _moe_ln(x1, eids, gates, w1_all, b1g, b1u, w2_all, layer, b2, g, b, emit_bf16):
    m, d = x1.shape
    ranks, counts = _route_ranks(eids)
    counts = counts[0, :N_EXPERTS]
    padded = (counts + EXPERT_TILE - 1) // EXPERT_TILE * EXPERT_TILE
    ends = jnp.cumsum(padded)
    gstart = ends - padded
    n_tiles = -(-m * TOP_K // EXPERT_TILE) + N_EXPERTS
    tile_start = jnp.arange(n_tiles, dtype=I32) * EXPERT_TILE
    tile_expert = jnp.minimum(jnp.sum((tile_start[:, None] >= ends[None, :]).astype(I32), axis=1), N_EXPERTS - 1)
    onehot_t = tile_expert[:, None] == jnp.arange(N_EXPERTS, dtype=I32)[None, :]
    used = jnp.sum(jnp.where(onehot_t, (gstart + counts)[None, :], 0), axis=1)
    tile_rows = jnp.clip(used - tile_start, 0, EXPERT_TILE).astype(I32)
    n_valid = (ends[-1:] // EXPERT_TILE).astype(I32)
    sel = eids[:, :TOP_K, None] == jnp.arange(N_EXPERTS, dtype=I32)[None, None, :]
    pos_flat = (jnp.sum(jnp.where(sel, gstart[None, None, :], 0), axis=-1) + ranks[:, :TOP_K]).reshape(-1).astype(I32)
    x_rows = _dispatch_rows(x1, pos_flat, n_tiles * EXPERT_TILE)
    y_rows = _expert_ffn(x_rows, tile_expert, tile_rows, n_valid, w1_all, b1g, b1u, w2_all, layer, b2)
    return _combine_ln(y_rows, pos_flat, gates, x1, g, b, emit_bf16)


def _fox_prep_kernel(t_ref, bias_ref, lf_ref, ccol_ref):
    t_len = t_ref.shape[0]
    r, c = _iota2((V7X_LANES, V7X_LANES))
    tril = jnp.where(c <= r, 1.0, 0.0)
    carry = jnp.zeros((1, V7X_LANES), F32)
    for blk in range(t_len // V7X_LANES):
        rows = slice(blk * V7X_LANES, (blk + 1) * V7X_LANES)
        lf = _log_sigmoid(t_ref[rows, :] + bias_ref[...])
        lf_ref[rows, :] = lf
        cs = jnp.dot(tril, lf, precision=HIGHEST, preferred_element_type=F32) + carry
        ccol_ref[rows, :] = cs
        carry = cs[V7X_LANES - 1:, :]


def _fox_prep(tail, bias_row, bsz, t_len):
    n_p = bsz * t_len
    blk = pl.BlockSpec((t_len, V7X_LANES), lambda b: (b, 0))
    out = jax.ShapeDtypeStruct((n_p, V7X_LANES), F32)
    return pl.pallas_call(
        _fox_prep_kernel,
        grid=(bsz,),
        in_specs=[blk, pl.BlockSpec((1, V7X_LANES), lambda b: (0, 0))],
        out_specs=[blk, blk],
        out_shape=[out, out],
        compiler_params=_params("parallel"),
        name="fox_prep",
    )(tail, bias_row)


def _fox_attn_kernel(q_ref, k_ref, v_ref, ccol_ref, o_ref, kb_ref, vt_ref, cb0_ref, cb1_ref, m0_ref, m1_ref, l0_ref, l1_ref,
                     acc0_ref, acc1_ref, s0_ref, s1_ref):
    pair = pl.program_id(1)
    qi = pl.program_id(2)
    tq = q_ref.shape[0]
    t_len = k_ref.shape[0]
    cb_refs, m_refs, l_refs, acc_refs = (cb0_ref, cb1_ref), (m0_ref, m1_ref), (l0_ref, l1_ref), (acc0_ref, acc1_ref)
    s_refs = (s0_ref, s1_ref)

    @pl.when(qi == 0)
    def _():
        kb_ref[...] = k_ref[...].astype(BF16)
        sr, sc = _iota2((V7X_LANES, V7X_LANES))
        for j in range(2):
            sel = jnp.where(sr == 2 * pair + j, 1.0, 0.0).astype(BF16)
            cb_refs[j][...] = _expand_heads(ccol_ref[...], sel)
        vt_ref[...] = v_ref[0].astype(BF16)

    qt = (q_ref[...] * (DH_B ** -0.5)).T
    feat = lax.broadcasted_iota(I32, qt.shape, 0)
    qh = [jnp.where((feat // DH_B) == j, qt, 0.0).astype(BF16) for j in range(2)]
    key_id, qry_id = _iota2((tq, tq))
    for j in range(2):
        m_refs[j][...] = jnp.full(m_refs[j].shape, -jnp.inf, F32)
        l_refs[j][...] = jnp.zeros_like(l_refs[j])
        acc_refs[j][...] = jnp.zeros_like(acc_refs[j])

    def scores(kb, j):
        start = pl.multiple_of(kb * tq, tq)
        cb = cb_refs[j][pl.ds(start, tq), :]
        return (jnp.dot(kb_ref[pl.ds(start, tq), :], qh[j], preferred_element_type=F32)
                - jnp.concatenate([cb] * (tq // V7X_LANES), axis=1))

    def update(kb, j, s):
        start = pl.multiple_of(kb * tq, tq)
        m_old = m_refs[j][...]
        m_new = jnp.maximum(m_old, jnp.max(s, axis=0, keepdims=True))
        alpha = jnp.exp(m_old - m_new)
        pe = jnp.exp(s - m_new)
        l_refs[j][...] = alpha * l_refs[j][...] + jnp.sum(pe, axis=0, keepdims=True)
        acc_refs[j][...] = alpha * acc_refs[j][...] + jnp.dot(vt_ref[:, pl.ds(start, tq)], pe.astype(BF16),
                                                              preferred_element_type=F32)
        m_refs[j][...] = m_new

    for j in range(2):
        s_refs[j][...] = scores(0, j)

    def body(kb, carry):
        for j in range(2):
            s = s_refs[j][...]
            s_refs[j][...] = scores(kb + 1, j)
            update(kb, j, s)
        return carry

    lax.fori_loop(0, qi, body, 0)
    for j in range(2):
        update(qi, j, jnp.where(key_id <= qry_id, s_refs[j][...], -jnp.inf))
    out_t = jnp.where((feat // DH_B) == 0, acc0_ref[...] / l0_ref[...], acc1_ref[...] / l1_ref[...])
    o_ref[...] = out_t.T.astype(o_ref.dtype)


def _fox_attention(h_main, kv_t, ccol, bsz, t_len):
    n_p = bsz * t_len
    tq = ATTN_BLOCK
    nq = t_len // tq
    qcol, kcol = (4 * D_A) // V7X_LANES, (4 * D_A + D_B) // V7X_LANES
    cbs, row, acc = pltpu.VMEM((t_len, V7X_LANES), F32), pltpu.VMEM((1, tq), F32), pltpu.VMEM((V7X_LANES, tq), F32)
    return pl.pallas_call(
        _fox_attn_kernel,
        grid=(bsz, H_B // 2, nq),
        in_specs=[pl.BlockSpec((tq, V7X_LANES), lambda b, p, qi: (b * nq + qi, qcol + p)),
                  pl.BlockSpec((t_len, V7X_LANES), lambda b, p, qi: (b, kcol + p)),
                  pl.BlockSpec((1, V7X_LANES, t_len), lambda b, p, qi: (b, D_B // V7X_LANES + p, 0)),
                  pl.BlockSpec((t_len, V7X_LANES), lambda b, p, qi: (b, 0))],
        out_specs=pl.BlockSpec((tq, V7X_LANES), lambda b, p, qi: (b * nq + qi, p)),
        out_shape=jax.ShapeDtypeStruct((h_main.shape[0], D_B), BF16),
        scratch_shapes=[pltpu.VMEM((t_len, V7X_LANES), BF16), pltpu.VMEM((V7X_LANES, t_len), BF16),
                        cbs, cbs, row, row, row, row, acc, acc, pltpu.VMEM((tq, tq), F32), pltpu.VMEM((tq, tq), F32)],
        compiler_params=_params("parallel", "parallel", "arbitrary"),
        name="fox_attention",
    )(h_main, h_main, kv_t, ccol)


def _fox_decode_kernel(pt_ref, q_ref, kn_ref, vn_ref, t_ref, bias_ref, *rest):
    np_ = DECODE_PAGES
    k_refs, v_refs, lf_refs = rest[:np_], rest[np_:2 * np_], rest[2 * np_:3 * np_]
    all_ref, o_ref, lfo_ref, m_ref, l_ref, acc_ref, carry_ref, rows_ref = rest[3 * np_:]
    del all_ref
    b = pl.program_id(0)
    j = pl.program_id(1)
    q = q_ref[0] * (DH_B ** -0.5)
    hrow, hlane = _iota2((H_B, D_B))
    own = hlane // DH_B == hrow
    qmat = jnp.where(own, jnp.broadcast_to(q, (H_B, D_B)), 0.0)
    lf_new = _log_sigmoid(t_ref[0] + bias_ref[...])

    @pl.when(j == 0)
    def _():
        m_ref[...] = jnp.sum(qmat * kn_ref[0], axis=-1, keepdims=True)
        l_ref[...] = jnp.ones_like(l_ref)
        acc_ref[...] = jnp.broadcast_to(vn_ref[0], acc_ref.shape)
        carry_ref[...] = jnp.zeros_like(carry_ref)
        lfo_ref[0] = lf_new

    r8, c8 = _iota2((H_B, H_B))
    cn = jnp.sum(jnp.where(r8 == c8, jnp.broadcast_to(lf_new[:, :H_B], (H_B, H_B)), 0.0), axis=-1, keepdims=True)
    kr, kc = _iota2((PAGE_SIZE, PAGE_SIZE))
    later = jnp.where(kr > kc, 1.0, 0.0).astype(BF16)
    qb = qmat.astype(BF16)
    lfts = [lf_refs[i][0] for i in range(np_)]
    carries = [None] * np_
    run = carry_ref[...]
    for i in reversed(range(np_)):
        carries[i] = run
        run = run + jnp.sum(lfts[i], axis=-1, keepdims=True)
    carry_ref[...] = run
    scores = []
    for i in range(np_):
        lf = lfts[i]
        hi = lf.astype(BF16)
        r1 = lf - hi.astype(F32)
        mid = r1.astype(BF16)
        lo = (r1 - mid.astype(F32)).astype(BF16)
        parts = jnp.dot(jnp.concatenate([hi, mid, lo], axis=0), later, preferred_element_type=F32)
        suffix = parts[:H_B] + parts[H_B:2 * H_B] + parts[2 * H_B:]
        k2 = k_refs[i][0].reshape(D_B, PAGE_SIZE).astype(BF16)
        scores.append(jnp.dot(qb, k2, preferred_element_type=F32) + (suffix + (cn + carries[i])))
    m_old = m_ref[...]
    m_new = m_old
    for s in scores:
        m_new = jnp.maximum(m_new, jnp.max(s, axis=-1, keepdims=True))
    alpha = jnp.exp(m_old - m_new)
    l_new = alpha * l_ref[...]
    acc = alpha * acc_ref[...]
    for i, s in enumerate(scores):
        pe = jnp.exp(s - m_new)
        l_new = l_new + jnp.sum(pe, axis=-1, keepdims=True)
        v2 = v_refs[i][0].reshape(D_B, PAGE_SIZE).astype(BF16)
        acc = acc + lax.dot_general(pe.astype(BF16), v2, NT_DIMS, preferred_element_type=F32)
    m_ref[...] = m_new
    l_ref[...] = l_new
    acc_ref[...] = acc

    @pl.when(j == pl.num_programs(1) - 1)
    def _():
        o = jnp.where(own, acc / l_new, 0.0)
        rows_ref[pl.ds(b, 1), :] = jnp.sum(o, axis=0, keepdims=True)

    @pl.when((j == pl.num_programs(1) - 1) & (b == pl.num_programs(0) - 1))
    def _():
        o_ref[...] = rows_ref[...].astype(o_ref.dtype)


def _fox_decode(hs_main, hs_tail, bias_row, cache_kt, cache_vt, logf_t, page_table, ob_all):
    bd = hs_main.shape[0]
    n_p = ob_all.shape[0] - bd
    n_pages = page_table.shape[1]
    steps = n_pages // DECODE_PAGES
    qcol, kcol, vcol = (4 * D_A) // D_B, (4 * D_A + D_B) // D_B, (4 * D_A + 2 * D_B) // D_B

    def page(i, nd):
        return lambda b, j, pt: (pt[b * n_pages + (steps - 1 - j) * DECODE_PAGES + i],) + (0,) * nd

    tok = lambda col: pl.BlockSpec((1, 1, D_B), lambda b, j, pt: (b, 0, col))
    in_specs = [tok(qcol), tok(kcol), tok(vcol),
                pl.BlockSpec((1, 1, V7X_LANES), lambda b, j, pt: (b, 0, 0)),
                pl.BlockSpec((1, V7X_LANES), lambda b, j, pt: (0, 0))]
    in_specs += [pl.BlockSpec((1, H_B, DH_B, PAGE_SIZE), page(i, 3)) for i in range(DECODE_PAGES)]
    in_specs += [pl.BlockSpec((1, H_B, DH_B, PAGE_SIZE), page(i, 3)) for i in range(DECODE_PAGES)]
    in_specs += [pl.BlockSpec((1, H_B, PAGE_SIZE), page(i, 2)) for i in range(DECODE_PAGES)]
    in_specs += [pl.BlockSpec(memory_space=pl.ANY)]
    n_in = len(in_specs) + 1
    return pl.pallas_call(
        _fox_decode_kernel,
        grid_spec=pltpu.PrefetchScalarGridSpec(
            num_scalar_prefetch=1,
            grid=(bd, steps),
            in_specs=in_specs,
            out_specs=[_token_rows_spec(n_p, bd, D_B),
                       pl.BlockSpec((1, 1, V7X_LANES), lambda b, j, pt: (b, 0, 0))],
            scratch_shapes=[pltpu.VMEM((H_B, 1), F32), pltpu.VMEM((H_B, 1), F32), pltpu.VMEM((H_B, D_B), F32),
                            pltpu.VMEM((H_B, 1), F32), pltpu.VMEM((bd, D_B), F32)],
        ),
        out_shape=[jax.ShapeDtypeStruct(ob_all.shape, BF16), jax.ShapeDtypeStruct((bd, 1, V7X_LANES), F32)],
        input_output_aliases={n_in - 1: 0},
        compiler_params=_params("arbitrary", "arbitrary"),
        name="fox_decode",
    )(page_table.reshape(-1), hs_main, hs_main, hs_main, hs_tail, bias_row,
      *([cache_kt] * DECODE_PAGES), *([cache_vt] * DECODE_PAGES), *([logf_t] * DECODE_PAGES), ob_all)


def _hgrn_gates(q, z, lb):
    qa = _silu(q)
    logf = jnp.log(lb + (1.0 - lb) * jax.nn.sigmoid(z))
    ka = (1.0 - lb) * jax.nn.sigmoid(-z)
    return qa, ka, logf


def _hgrn_out(o, g, nw):
    o = o * lax.rsqrt(jnp.mean(o * o, axis=-1, keepdims=True) + RMS_EPS)
    return o * nw * _silu(g)


def _hgrn_kernel(q_ref, f_ref, i_ref, g_ref, lb_ref, nw_ref, o_ref, s_ref, qd_ref, dec_ref, oc_ref, u_ref):
    t_len = q_ref.shape[0]
    cs = CHUNK_A
    per_blk = V7X_LANES // cs
    n_blk = t_len // V7X_LANES
    lb = lb_ref[0]
    r, c = _iota2((V7X_LANES, V7X_LANES))
    same = (r // cs) == (c // cs)
    causal = same & (c <= r)
    sums = jnp.concatenate([jnp.where(causal, 1.0, 0.0), jnp.where(same, 1.0, 0.0)], axis=0).astype(BF16)

    def intra(blk, carry):
        start = pl.multiple_of(blk * V7X_LANES, V7X_LANES)
        rows = pl.ds(start, V7X_LANES)
        qa, ka, logf = _hgrn_gates(q_ref[rows, :], f_ref[rows, :], lb)
        both = _select_sum(sums, logf)
        b, gtot = both[:V7X_LANES], both[V7X_LANES:]
        qd = (qa * jnp.exp(b)).astype(BF16)
        kd = (ka * jnp.exp(-b)).astype(BF16)
        ke = (ka * jnp.exp(gtot - b)).astype(BF16)
        v = i_ref[rows, :].astype(BF16)
        qd_ref[rows, :] = qd
        sc = jnp.where(causal, lax.dot_general(qd, kd, NT_DIMS, preferred_element_type=F32), 0.0)
        oc_ref[rows, :] = jnp.dot(sc.astype(BF16), v, preferred_element_type=F32)
        dec = jnp.exp(gtot)
        for k in range(per_blk):
            sub = slice(k * cs, (k + 1) * cs)
            ci = blk * per_blk + k
            u_ref[ci] = lax.dot_general(v[sub], ke[sub], TN_DIMS, preferred_element_type=F32)
            dec_ref[pl.ds(ci, 1), :] = dec[k * cs:k * cs + 1, :]
        return carry

    lax.fori_loop(0, n_blk, intra, 0, unroll=4)

    def scan(ci, st):
        u = u_ref[ci]
        u_ref[ci] = st
        return st * dec_ref[pl.ds(ci, 1), :] + u

    st_last = lax.fori_loop(0, t_len // cs, scan, jnp.zeros((DV_A, DK_A), F32), unroll=4)
    s_ref[0, 0] = st_last.T

    def inter(blk, carry):
        start = pl.multiple_of(blk * V7X_LANES, V7X_LANES)
        rows = pl.ds(start, V7X_LANES)
        qd = qd_ref[rows, :]
        parts = [lax.dot_general(qd[k * cs:(k + 1) * cs], u_ref[blk * per_blk + k].astype(BF16), NT_DIMS,
                                 preferred_element_type=F32) for k in range(per_blk)]
        o = oc_ref[rows, :] + jnp.concatenate(parts, axis=0)
        o_ref[rows, :] = _hgrn_out(o, g_ref[rows, :], nw_ref[...]).astype(o_ref.dtype)
        return carry

    lax.fori_loop(0, n_blk, inter, 0, unroll=4)


def _hgrn_prompt(h_main, lb3, nw_row, bsz, t_len):
    n_p = bsz * t_len
    n_chunks = t_len // CHUNK_A
    col = lambda grp: (lambda b, h: (b, grp * H_A + h))
    return pl.pallas_call(
        _hgrn_kernel,
        grid=(bsz, H_A),
        in_specs=[pl.BlockSpec((t_len, DK_A), col(0)), pl.BlockSpec((t_len, DK_A), col(1)),
                  pl.BlockSpec((t_len, DV_A), col(2)), pl.BlockSpec((t_len, DV_A), col(3)),
                  pl.BlockSpec((1, 1, DK_A), lambda b, h: (h, 0, 0)), pl.BlockSpec((1, DV_A), lambda b, h: (0, 0))],
        out_specs=[pl.BlockSpec((t_len, DV_A), lambda b, h: (b, h)),
                   pl.BlockSpec((1, 1, DK_A, DV_A), lambda b, h: (b, h, 0, 0))],
        out_shape=[jax.ShapeDtypeStruct((h_main.shape[0], D_A), BF16),
                   jax.ShapeDtypeStruct((bsz, H_A, DK_A, DV_A), F32)],
        scratch_shapes=[pltpu.VMEM((t_len, DK_A), BF16), pltpu.VMEM((n_chunks, DK_A), F32),
                        pltpu.VMEM((t_len, DV_A), F32), pltpu.VMEM((n_chunks, DV_A, DK_A), F32)],
        compiler_params=_params("parallel", "parallel"),
        name="hgrn_prompt",
    )(h_main, h_main, h_main, h_main, lb3, nw_row)


def _hgrn_step_kernel(h_ref, lb_ref, nw_ref, s0_ref, all_ref, o_ref, s_ref, rows_ref):
    del all_ref
    b = pl.program_id(0)
    outs = []
    for h in range(H_A):
        grp = lambda g: h_ref[0, :, g * D_A + h * DK_A: g * D_A + (h + 1) * DK_A]
        qa, ka, logf = _hgrn_gates(grp(0), grp(1), lb_ref[h])
        v = grp(2)
        s_new = _to_column(jnp.exp(logf)) * s0_ref[0, h] + _to_column(ka) * v
        s_ref[0, h] = s_new
        o = jnp.sum(_to_column(qa) * s_new, axis=0, keepdims=True)
        outs.append(_hgrn_out(o, grp(3), nw_ref[...]))
    rows_ref[pl.ds(b, 1), :] = jnp.concatenate(outs, axis=-1)

    @pl.when(b == pl.num_programs(0) - 1)
    def _():
        o_ref[...] = rows_ref[...].astype(o_ref.dtype)


def _hgrn_step(hs_main, lb3, nw_row, s0, oa_all):
    bd = hs_main.shape[0]
    n_p = oa_all.shape[0] - bd
    return pl.pallas_call(
        _hgrn_step_kernel,
        grid=(bd,),
        in_specs=[pl.BlockSpec((1, 1, 4 * D_A), lambda b: (b, 0, 0)), pl.BlockSpec((H_A, 1, DK_A), lambda b: (0, 0, 0)),
                  pl.BlockSpec((1, DV_A), lambda b: (0, 0)), pl.BlockSpec((1, H_A, DK_A, DV_A), lambda b: (b, 0, 0, 0)),
                  pl.BlockSpec(memory_space=pl.ANY)],
        out_specs=[_token_rows_spec(n_p, bd, D_A),
                   pl.BlockSpec((1, H_A, DK_A, DV_A), lambda b: (b, 0, 0, 0))],
        out_shape=[jax.ShapeDtypeStruct(oa_all.shape, BF16), jax.ShapeDtypeStruct(s0.shape, F32)],
        scratch_shapes=[pltpu.VMEM((bd, D_A), F32)],
        input_output_aliases={4: 0},
        compiler_params=_params("arbitrary"),
        name="hgrn_step",
    )(hs_main, lb3, nw_row, s0, oa_all)


def _conv_silu(cur, prev, w, b):
    row8 = lax.broadcasted_iota(I32, prev.shape, 0)
    acc = b + cur * w[CONV_W - 1:CONV_W, :]
    for s in range(1, CONV_W):
        sh = pltpu.roll(cur, s, 0)
        head = jnp.where(row8 < s, pltpu.roll(prev, s, 0), sh[:V7X_SUBLANES, :])
        shifted = jnp.concatenate([head, sh[V7X_SUBLANES:, :]], axis=0)
        acc = acc + shifted * w[CONV_W - 1 - s:CONV_W - s, :]
    return _silu(acc)


def _gated_group_norm(y, z, nw):
    y = y * _silu(z)
    parts = []
    for g in range(N_GROUPS_C):
        seg = y[:, g * GROUP_W:(g + 1) * GROUP_W]
        parts.append(seg * lax.rsqrt(jnp.mean(seg * seg, axis=-1, keepdims=True) + RMS_EPS))
    return jnp.concatenate(parts, axis=-1) * nw


def _ssd_kernel(z_ref, x_ref, bc_ref, dt_ref, cwx_ref, cbx_ref, cwbc_ref, cbbc_ref, dtb_ref, alog_ref, dsk_ref,
                nw_ref, y_ref, hs_ref, tailx_ref, tailbc_ref, ht_ref, e_ref, yacc_ref, xw_ref):
    tb = pl.program_id(1)
    tt = x_ref.shape[0]
    pair_w = 2 * HEADDIM_C
    heads_per_group = H_C // N_GROUPS_C

    @pl.when(tb == 0)
    def _():
        tailx_ref[...] = jnp.zeros_like(tailx_ref)
        tailbc_ref[...] = jnp.zeros_like(tailbc_ref)
        ht_ref[...] = jnp.zeros_like(ht_ref)
        er, ec = _iota2(e_ref.shape)
        e_ref[...] = jnp.where(ec // HEADDIM_C == er, 1.0, 0.0).astype(BF16)

    x_raw = x_ref[...]
    bc_raw = bc_ref[...]
    xs = _conv_silu(x_raw, tailx_ref[...], cwx_ref[...], cbx_ref[...])
    bcv = _conv_silu(bc_raw, tailbc_ref[...], cwbc_ref[...], cbbc_ref[...])
    tailx_ref[...] = x_raw[tt - V7X_SUBLANES:, :]
    tailbc_ref[...] = bc_raw[tt - V7X_SUBLANES:, :]

    lane = lax.broadcasted_iota(I32, (tt, V7X_LANES), 1)
    dt = jnp.where(lane < H_C, _softplus(dt_ref[...] + dtb_ref[...]), 0.0)
    a = -jnp.exp(alog_ref[...])
    r, c = _iota2((tt, tt))
    causal = c <= r
    cum = jnp.dot(jnp.where(causal, 1.0, 0.0), dt * a, precision=HIGHEST, preferred_element_type=F32)
    cum_t = cum.T
    xdt = xs * _expand_heads(dt, e_ref[...])
    low = (lane % pair_w) < HEADDIM_C

    for g in range(N_GROUPS_C):
        b_g = bcv[:, g * D_STATE_C:(g + 1) * D_STATE_C]
        c_g = bcv[:, D_BC + g * D_STATE_C:D_BC + (g + 1) * D_STATE_C].astype(BF16)
        cb = lax.dot_general(c_g, b_g.astype(BF16), NT_DIMS, preferred_element_type=F32)
        y_inter = jnp.dot(c_g, ht_ref[g].astype(BF16), preferred_element_type=F32)
        decs = []
        for pr in range(heads_per_group // 2):
            slab = slice(g * GROUP_W + pr * pair_w, g * GROUP_W + (pr + 1) * pair_w)
            xdt_slab = xdt[:, slab]
            ys, es, tes = [], [], []
            for j in range(2):
                head = g * heads_per_group + pr * 2 + j
                colb = jnp.broadcast_to(cum[:, head:head + 1], (tt, tt))
                decay = jnp.exp(jnp.where(causal, colb - cum_t[head:head + 1, :], -jnp.inf))
                ys.append(jnp.dot((cb * decay).astype(BF16), xdt_slab.astype(BF16), preferred_element_type=F32))
                es.append(jnp.exp(colb))
                tes.append(jnp.exp(colb[tt - 1:, :] - colb))
            e_pair = jnp.where(low, es[0], es[1])
            yacc_ref[:, slab] = (jnp.where(low, ys[0], ys[1]) + e_pair * y_inter[:, pr * pair_w:(pr + 1) * pair_w]
                                 + dsk_ref[:, slab] * xs[:, slab])
            xw_ref[:, pr * pair_w:(pr + 1) * pair_w] = xdt_slab * jnp.where(low, tes[0], tes[1])
            decs.append(e_pair[tt - 1:, :])
        dec_row = jnp.concatenate(decs, axis=-1)
        ht_ref[g] = ht_ref[g] * dec_row + jnp.dot(b_g.T.astype(BF16), xw_ref[...].astype(BF16),
                                                  preferred_element_type=F32)

    y_ref[...] = _gated_group_norm(yacc_ref[...], z_ref[...], nw_ref[...]).astype(y_ref.dtype)

    @pl.when(tb == pl.num_programs(1) - 1)
    def _():
        for g in range(N_GROUPS_C):
            for q in range(GROUP_W // V7X_LANES):
                rows = slice(g * GROUP_W + q * V7X_LANES, g * GROUP_W + (q + 1) * V7X_LANES)
                hs_ref[0, rows, :] = ht_ref[g][:, q * V7X_LANES:(q + 1) * V7X_LANES].T


def _ssd_prompt(h_main, h_tail, conv_w, conv_b, dtb_row, alog_row, dsk_row, nw_row, bsz, t_len):
    n_p = bsz * t_len
    tt = SSD_BLOCK
    nt = t_len // tt
    rowmap = lambda col: (lambda b, t: (b * nt + t, col))
    fixed = lambda b, t: (0, 0)
    cwx, cwbc = conv_w[:, :D_INNER], conv_w[:, D_INNER:]
    cbx, cbbc = conv_b[None, :D_INNER], conv_b[None, D_INNER:]
    return pl.pallas_call(
        _ssd_kernel,
        grid=(bsz, nt),
        in_specs=[pl.BlockSpec((tt, D_INNER), rowmap(0)), pl.BlockSpec((tt, D_INNER), rowmap(1)),
                  pl.BlockSpec((tt, 2 * D_BC), rowmap(2 * D_INNER // (2 * D_BC))),
                  pl.BlockSpec((tt, V7X_LANES), rowmap(0)),
                  pl.BlockSpec((CONV_W, D_INNER), fixed), pl.BlockSpec((1, D_INNER), fixed),
                  pl.BlockSpec((CONV_W, 2 * D_BC), fixed), pl.BlockSpec((1, 2 * D_BC), fixed),
                  pl.BlockSpec((1, V7X_LANES), fixed), pl.BlockSpec((1, V7X_LANES), fixed),
                  pl.BlockSpec((1, D_INNER), fixed), pl.BlockSpec((1, D_INNER), fixed)],
        out_specs=[pl.BlockSpec((tt, D_INNER), rowmap(0)),
                   pl.BlockSpec((1, D_INNER, D_STATE_C), lambda b, t: (b, 0, 0))],
        out_shape=[jax.ShapeDtypeStruct((h_main.shape[0], D_INNER), BF16),
                   jax.ShapeDtypeStruct((bsz, D_INNER, D_STATE_C), F32)],
        scratch_shapes=[pltpu.VMEM((V7X_SUBLANES, D_INNER), F32), pltpu.VMEM((V7X_SUBLANES, 2 * D_BC), F32),
                        pltpu.VMEM((N_GROUPS_C, D_STATE_C, GROUP_W), F32), pltpu.VMEM((V7X_LANES, D_INNER), BF16),
                        pltpu.VMEM((tt, D_INNER), F32), pltpu.VMEM((tt, GROUP_W), F32)],
        compiler_params=_params("parallel", "arbitrary"),
        name="ssd_prompt",
    )(h_main, h_main, h_main, h_tail, cwx, cbx, cwbc, cbbc, dtb_row, alog_row, dsk_row, nw_row)


def _ssd_step_kernel(h_ref, t_ref, cs_ref, cw_ref, cb_ref, dtb_ref, alog_ref, dsk_ref, nw_ref, h0_ref, all_ref,
                     y_ref, hn_ref, rows_ref):
    del all_ref
    b = pl.program_id(0)
    z = h_ref[0, :, :D_INNER]
    xbc_new = h_ref[0, :, D_INNER:]
    cw = cw_ref[...]
    conv = cb_ref[...] + xbc_new * cw[CONV_W - 1:CONV_W, :]
    for j in range(CONV_W - 1):
        conv = conv + cs_ref[0, j:j + 1, :] * cw[j:j + 1, :]
    xbc = _silu(conv)
    xs = xbc[:, :D_INNER]
    lane = lax.broadcasted_iota(I32, (1, V7X_LANES), 1)
    dt = jnp.where(lane < H_C, _softplus(t_ref[0] + dtb_ref[...]), 0.0)
    da = jnp.exp(dt * -jnp.exp(alog_ref[...]))
    er, ec = _iota2((V7X_LANES, D_INNER))
    expand = jnp.where(ec // HEADDIM_C == er, 1.0, 0.0).astype(BF16)
    rows8 = lambda v: jnp.broadcast_to(v, (V7X_SUBLANES, V7X_LANES))
    dt_x = _expand_heads(rows8(dt), expand)[:1, :]
    da_x = _expand_heads(rows8(da), expand)[:1, :]
    xdt = xs * dt_x
    y_parts = []
    for q in range(D_INNER // V7X_LANES):
        g = q // (GROUP_W // V7X_LANES)
        lanes = slice(q * V7X_LANES, (q + 1) * V7X_LANES)
        b_g = xbc[:, D_INNER + g * D_STATE_C:D_INNER + (g + 1) * D_STATE_C]
        c_g = xbc[:, D_INNER + D_BC + g * D_STATE_C:D_INNER + D_BC + (g + 1) * D_STATE_C]
        h_new = _to_column(da_x[:, lanes]) * h0_ref[0, lanes, :] + _to_column(xdt[:, lanes]) * b_g
        hn_ref[0, lanes, :] = h_new
        y_parts.append(_to_row(jnp.sum(h_new * c_g, axis=-1, keepdims=True)))
    y = jnp.concatenate(y_parts, axis=-1) + dsk_ref[...] * xs
    rows_ref[pl.ds(b, 1), :] = _gated_group_norm(y, z, nw_ref[...])

    @pl.when(b == pl.num_programs(0) - 1)
    def _():
        y_ref[...] = rows_ref[...].astype(y_ref.dtype)


def _ssd_step(hs_main, hs_tail, conv_state, conv_w, conv_b, dtb_row, alog_row, dsk_row, nw_row, h0, y_all):
    bd = hs_main.shape[0]
    n_p = y_all.shape[0] - bd
    fixed = lambda b: (0, 0)
    tok = lambda b: (b, 0, 0)
    return pl.pallas_call(
        _ssd_step_kernel,
        grid=(bd,),
        in_specs=[pl.BlockSpec((1, 1, ODD_MAIN), tok), pl.BlockSpec((1, 1, V7X_LANES), tok),
                  pl.BlockSpec((1, CONV_W - 1, CONV_DIM), tok),
                  pl.BlockSpec((CONV_W, CONV_DIM), fixed), pl.BlockSpec((1, CONV_DIM), fixed),
                  pl.BlockSpec((1, V7X_LANES), fixed), pl.BlockSpec((1, V7X_LANES), fixed),
                  pl.BlockSpec((1, D_INNER), fixed), pl.BlockSpec((1, D_INNER), fixed),
                  pl.BlockSpec((1, D_INNER, D_STATE_C), tok), pl.BlockSpec(memory_space=pl.ANY)],
        out_specs=[_token_rows_spec(n_p, bd, D_INNER), pl.BlockSpec((1, D_INNER, D_STATE_C), tok)],
        out_shape=[jax.ShapeDtypeStruct(y_all.shape, BF16), jax.ShapeDtypeStruct((bd, D_INNER, D_STATE_C), F32)],
        scratch_shapes=[pltpu.VMEM((bd, D_INNER), F32)],
        input_output_aliases={10: 0},
        compiler_params=_params("arbitrary"),
        name="ssd_step",
    )(hs_main, hs_tail, conv_state, conv_w, conv_b[None], dtb_row, alog_row, dsk_row, nw_row, h0, y_all)


def _pad_lanes(v):
    return jnp.pad(v, (0, V7X_LANES - v.shape[0]))[None]


def kernel(x_prompt, x_sample, cache_k, cache_v, cache_logf, page_table, state_hgrn, state_ssm, state_conv,
           w_in_even, hgrn_lower_bound, hgrn_norm_w, fox_f_bias, w_out_even,
           w_in_odd, conv_w, conv_b, dt_bias, a_log, d_skip, ssm_norm_w, w_out_odd,
           ln1_g, ln1_b, ln2_g, ln2_b, router_w, router_b, exp_w1, exp_b1, exp_w2, exp_b2):
    lb_all = jnp.cumsum(jax.nn.softmax(hgrn_lower_bound, axis=0), axis=0)
    bp, t_p, d = x_prompt.shape
    bd, t_d, _ = x_sample.shape
    assert t_d == 1
    n_p = bp * t_p
    x_all = jnp.concatenate([x_prompt.reshape(n_p, d), x_sample.reshape(bd, d)], axis=0)
    x_bf = x_all.astype(BF16)
    outs = {}
    for l in range(DEPTH):
        i = l // 2
        if l % 2 == 0:
            h_main = _matmul(x_bf, w_in_even[i][:, :EVEN_MAIN].astype(BF16))
            h_tail = _matmul(x_bf, w_in_even[i][:, EVEN_MAIN:].astype(BF16))
            hs_main, hs_tail = h_main[n_p:, None, :], h_tail[n_p:, None, :]
            lb3 = lb_all[i].reshape(H_A, 1, DK_A)
            nw_row = hgrn_norm_w[i][None]
            bias_row = _pad_lanes(fox_f_bias[i])
            lf_p, ccol = _fox_prep(h_tail, bias_row, bp, t_p)
            kcol, vcol = 4 * D_A + D_B, 4 * D_A + 2 * D_B
            kv_t = _proj_t(x_bf, w_in_even[i][:, kcol:kcol + 2 * D_B].T.astype(BF16), bp, t_p)
            ob_p = _fox_attention(h_main, kv_t, ccol, bp, t_p)
            oa_p, hg_p = _hgrn_prompt(h_main, lb3, nw_row, bp, t_p)
            oa_all, hg_s = _hgrn_step(hs_main, lb3, nw_row, state_hgrn[i], oa_p)
            ob_all, lf_s = _fox_decode(hs_main, hs_tail, bias_row, jnp.transpose(cache_k[i], (0, 2, 3, 1)),
                                       jnp.transpose(cache_v[i], (0, 2, 3, 1)), jnp.swapaxes(cache_logf[i], 1, 2),
                                       page_table, ob_p)
            w_out = w_out_even[i].astype(BF16)
            parts = [(oa_all, w_out[:D_A]), (ob_all, w_out[D_A:])]
            heads_last = lambda a: jnp.transpose(a.reshape(bp, H_B, DH_B, t_p), (0, 3, 1, 2))
            vals = (("kp", heads_last(kv_t[:, :D_B])),
                    ("ks", h_main[n_p:, kcol:kcol + D_B].reshape(bd, t_d, H_B, DH_B)),
                    ("vp", heads_last(kv_t[:, D_B:])),
                    ("vs", h_main[n_p:, vcol:vcol + D_B].reshape(bd, t_d, H_B, DH_B)),
                    ("lfp", lf_p[:, :H_B].reshape(bp, t_p, H_B)), ("lfs", lf_s[:, :, :H_B]),
                    ("hgp", hg_p), ("hgs", hg_s))
        else:
            h_main = _matmul(x_bf, w_in_odd[i][:, :ODD_MAIN].astype(BF16))
            h_tail = _matmul(x_bf, w_in_odd[i][:, ODD_MAIN:].astype(BF16))
            hs_main, hs_tail = h_main[n_p:, None, :], h_tail[n_p:, None, :]
            dtb_row, alog_row = _pad_lanes(dt_bias[i]), _pad_lanes(a_log[i])
            dsk_row = jnp.repeat(d_skip[i], HEADDIM_C)[None]
            nw_row = ssm_norm_w[i][None]
            y_p, ss_p = _ssd_prompt(h_main, h_tail, conv_w[i], conv_b[i], dtb_row, alog_row, dsk_row, nw_row, bp, t_p)
            y_all, ss_s = _ssd_step(hs_main, hs_tail, state_conv[i], conv_w[i], conv_b[i], dtb_row, alog_row, dsk_row,
                                    nw_row, state_ssm[i].reshape(bd, D_INNER, D_STATE_C), y_p)
            parts = [(y_all, w_out_odd[i].astype(BF16))]
            tail_rows = jnp.stack([h_main[(b + 1) * t_p - (CONV_W - 1):(b + 1) * t_p, D_INNER:] for b in range(bp)])
            vals = (("ssp", ss_p.reshape(bp, H_C, HEADDIM_C, D_STATE_C)),
                    ("sss", ss_s.reshape(bd, H_C, HEADDIM_C, D_STATE_C)),
                    ("cvp", tail_rows),
                    ("cvs", jnp.concatenate([state_conv[i][:, 1:], hs_main[:, :, D_INNER:]], axis=1)))
        for name, val in vals:
            outs.setdefault(name, []).append(val)
        rw = jnp.pad(router_w[l], ((0, 0), (0, V7X_LANES - N_EXPERTS)))
        rb = jnp.concatenate([router_b[l], jnp.full((V7X_LANES - N_EXPERTS,), -jnp.inf, F32)])[None]
        x1, eids, gates = _post_mixer(x_all, parts, ln1_g[l][None], ln1_b[l][None], rw, rb)
        b1g = exp_b1[l][:, None, 0::2]
        b1u = exp_b1[l][:, None, 1::2]
        more = l + 1 < DEPTH
        res = _moe_ln(x1, eids, gates, exp_w1, b1g, b1u, exp_w2, l, exp_b2[l][:, None, :],
                      ln2_g[l][None], ln2_b[l][None], more)
        x_all = res[0]
        x_bf = res[1] if more else None
    st = {k: jnp.stack(v) for k, v in outs.items()}
    return (x_all[:n_p].reshape(bp, t_p, d), x_all[n_p:].reshape(bd, t_d, d),
            st["kp"], st["ks"], st["vp"], st["vs"], st["lfp"], st["lfs"],
            st["hgp"], st["hgs"], st["ssp"], st["sss"], st["cvp"], st["cvs"])
```

```python
import functools

import jax
import jax.numpy as jnp
import numpy as np
from jax import lax
from jax.experimental import pallas as pl
from jax.experimental.pallas import tpu as pltpu

F32 = jnp.float32
BF16 = jnp.bfloat16
I32 = jnp.int32
HIGHEST = lax.Precision.HIGHEST

D_MODEL = 1024
DEPTH = 2
PAGE_SIZE = 128
H_A, DK_A, DV_A, CHUNK_A = 4, 128, 128, 32
H_B, DH_B = 8, 64
D_A = H_A * DK_A
D_B = H_B * DH_B
EVEN_MAIN = 4 * D_A + 3 * D_B
D_INNER = 2 * D_MODEL
HEADDIM_C = 64
H_C = D_INNER // HEADDIM_C
N_GROUPS_C = 4
D_STATE_C = 128
D_BC = N_GROUPS_C * D_STATE_C
GROUP_W = D_INNER // N_GROUPS_C
CONV_W = 4
CONV_DIM = D_INNER + 2 * D_BC
ODD_MAIN = D_INNER + CONV_DIM
N_EXPERTS = 32
TOP_K = 4
D_FF = D_MODEL
SWIGLU_LIMIT = 7.0
SWIGLU_ALPHA = 1.702
DN_ALPHA = (2 * DEPTH) ** 0.25
LN_EPS = 1e-5
RMS_EPS = 1e-6

V7X_LANES = 128
V7X_SUBLANES = 8
V7X_MXU = 256
VMEM_LIMIT = 56 * 1024 * 1024
EXPERT_TILE = 512
ATTN_BLOCK = 256
SSD_BLOCK = 128
DECODE_PAGES = 16
DMA_UNROLL = 8

NT_DIMS = (((1,), (1,)), ((), ()))
TN_DIMS = (((0,), (0,)), ((), ()))


def _row_tile(m, cap):
    best = 0
    for t in range(16, cap + 1, 16):
        if m % t == 0:
            best = t
    assert best, (m, cap)
    return best


def _token_rows_spec(n_rows, bd, width):
    assert n_rows % bd == 0
    return pl.BlockSpec((bd, width), lambda *_: (n_rows // bd, 0))


def _params(*sem):
    return pltpu.CompilerParams(dimension_semantics=sem, vmem_limit_bytes=VMEM_LIMIT)


def _silu(x):
    return x * jax.nn.sigmoid(x)


def _softplus(x):
    return jnp.maximum(x, 0.0) + jnp.log1p(jnp.exp(-jnp.abs(x)))


def _log_sigmoid(x):
    return -_softplus(-x)


def _iota2(shape):
    return lax.broadcasted_iota(I32, shape, 0), lax.broadcasted_iota(I32, shape, 1)


def _to_column(row_vec):
    n = row_vec.shape[1]
    r, c = _iota2((n, n))
    return jnp.sum(jnp.where(r == c, jnp.broadcast_to(row_vec, (n, n)), 0.0), axis=1, keepdims=True)


def _to_row(col_vec):
    n = col_vec.shape[0]
    r, c = _iota2((n, n))
    return jnp.sum(jnp.where(r == c, jnp.broadcast_to(col_vec, (n, n)), 0.0), axis=0, keepdims=True)


def _expand_heads(v, e_bf16):
    hi = v.astype(BF16)
    r1 = v - hi.astype(F32)
    mid = r1.astype(BF16)
    lo = (r1 - mid.astype(F32)).astype(BF16)
    dot = lambda a: jnp.dot(a, e_bf16, preferred_element_type=F32)
    return dot(hi) + dot(mid) + dot(lo)


def _select_sum(m01_bf16, x):
    hi = x.astype(BF16)
    r1 = x - hi.astype(F32)
    mid = r1.astype(BF16)
    lo = (r1 - mid.astype(F32)).astype(BF16)
    dot = lambda a: jnp.dot(m01_bf16, a, preferred_element_type=F32)
    return dot(hi) + dot(mid) + dot(lo)


def _mm_kernel(x_ref, w_ref, o_ref):
    o_ref[...] = jnp.dot(x_ref[...].astype(BF16), w_ref[...].astype(BF16), preferred_element_type=F32)


def _matmul(x, w, tm_cap=1024):
    m, k = x.shape
    n = w.shape[1]
    n_pad = -(-n // V7X_LANES) * V7X_LANES
    if n_pad != n:
        w = jnp.pad(w, ((0, 0), (0, n_pad - n)))
    tm = _row_tile(m, tm_cap)
    tn = next(t for t in (1024, 512, V7X_MXU, V7X_LANES) if n_pad % t == 0)
    return pl.pallas_call(
        _mm_kernel,
        grid=(m // tm, n_pad // tn),
        in_specs=[pl.BlockSpec((tm, k), lambda i, j: (i, 0)),
                  pl.BlockSpec((k, tn), lambda i, j: (0, j))],
        out_specs=pl.BlockSpec((tm, tn), lambda i, j: (i, j)),
        out_shape=jax.ShapeDtypeStruct((m, n_pad), F32),
        compiler_params=_params("parallel", "parallel"),
        name="dense_matmul",
    )(x, w)


def _proj_t_kernel(w_ref, x_ref, o_ref):
    o_ref[0] = lax.dot_general(w_ref[...], x_ref[...].astype(BF16), NT_DIMS, preferred_element_type=F32)


def _proj_t(x, w_t_bf16, bsz, t_len, tm=512):
    n, k = w_t_bf16.shape
    nt = t_len // tm
    return pl.pallas_call(
        _proj_t_kernel,
        grid=(bsz, nt),
        in_specs=[pl.BlockSpec((n, k), lambda b, t: (0, 0)), pl.BlockSpec((tm, k), lambda b, t: (b * nt + t, 0))],
        out_specs=pl.BlockSpec((1, n, tm), lambda b, t: (b, 0, t)),
        out_shape=jax.ShapeDtypeStruct((bsz, n, t_len), F32),
        compiler_params=_params("parallel", "parallel"),
        name="proj_transposed",
    )(w_t_bf16, x)


def _post_mixer_kernel(n_parts, x_ref, *refs):
    lhs_refs, w_refs = refs[:n_parts], refs[n_parts:2 * n_parts]
    g_ref, b_ref, rwh_ref, rwl_ref, rb_ref, x1_ref, eid_ref, gate_ref = refs[2 * n_parts:]
    acc = DN_ALPHA * x_ref[...]
    for lhs_ref, w_ref in zip(lhs_refs, w_refs):
        acc = acc + jnp.dot(lhs_ref[...].astype(BF16), w_ref[...], preferred_element_type=F32)
    xc = acc - jnp.mean(acc, axis=-1, keepdims=True)
    var = jnp.mean(xc * xc, axis=-1, keepdims=True)
    x1 = xc * lax.rsqrt(var + LN_EPS) * g_ref[...] + b_ref[...]
    x1_ref[...] = x1
    x_hi = x1.astype(BF16)
    x_lo = (x1 - x_hi.astype(F32)).astype(BF16)
    dot = lambda a, w_ref: jnp.dot(a, w_ref[...], preferred_element_type=F32)
    logits = dot(x_hi, rwh_ref) + (dot(x_lo, rwh_ref) + dot(x_hi, rwl_ref)) + rb_ref[...]
    lane = lax.broadcasted_iota(I32, logits.shape, 1)
    eids = jnp.zeros(logits.shape, I32)
    vals = []
    for k in range(TOP_K):
        top = jnp.max(logits, axis=-1, keepdims=True)
        idx = jnp.min(jnp.where(logits == top, lane, V7X_LANES), axis=-1, keepdims=True)
        vals.append(top)
        eids = jnp.where(lane == k, idx, eids)
        logits = jnp.where(lane == idx, -jnp.inf, logits)
    exps = [jnp.exp(v - vals[0]) for v in vals]
    denom = exps[0] + exps[1] + exps[2] + exps[3]
    gates = jnp.zeros(logits.shape, F32)
    for k in range(TOP_K):
        gates = jnp.where(lane == k, exps[k] / denom, gates)
    eid_ref[...] = eids
    gate_ref[...] = gates


def _post_mixer(x, parts, g, b, rw, rb):
    m, d = x.shape
    tm = _row_tile(m, 640)
    row = lambda i: (i, 0)
    fixed = lambda i: (0, 0)
    rw_hi = rw.astype(BF16)
    rw_lo = (rw - rw_hi.astype(F32)).astype(BF16)
    lhs_specs = [pl.BlockSpec((tm, lhs.shape[1]), row) for lhs, _ in parts]
    w_specs = [pl.BlockSpec(w.shape, fixed) for _, w in parts]
    return pl.pallas_call(
        functools.partial(_post_mixer_kernel, len(parts)),
        grid=(m // tm,),
        in_specs=[pl.BlockSpec((tm, d), row)] + lhs_specs + w_specs + [
                  pl.BlockSpec((1, d), fixed), pl.BlockSpec((1, d), fixed),
                  pl.BlockSpec((d, V7X_LANES), fixed), pl.BlockSpec((d, V7X_LANES), fixed),
                  pl.BlockSpec((1, V7X_LANES), fixed)],
        out_specs=[pl.BlockSpec((tm, d), row), pl.BlockSpec((tm, V7X_LANES), row),
                   pl.BlockSpec((tm, V7X_LANES), row)],
        out_shape=[jax.ShapeDtypeStruct((m, d), F32), jax.ShapeDtypeStruct((m, V7X_LANES), I32),
                   jax.ShapeDtypeStruct((m, V7X_LANES), F32)],
        compiler_params=_params("parallel"),
        name="post_mixer",
    )(x, *[lhs for lhs, _ in parts], *[w for _, w in parts], g, b, rw_hi, rw_lo, rb)


def _rank_kernel(eid_ref, rank_ref, cnt_ref, carry_ref):
    i = pl.program_id(0)

    @pl.when(i == 0)
    def _():
        carry_ref[...] = jnp.zeros_like(carry_ref)

    eids = eid_ref[...]
    tm = eids.shape[0]
    lane = lax.broadcasted_iota(I32, eids.shape, 1)
    sel = [jnp.sum(jnp.where(lane == k, eids, 0), axis=-1, keepdims=True) for k in range(TOP_K)]
    onehot = jnp.zeros(eids.shape, F32)
    for k in range(TOP_K):
        onehot = onehot + (lane == sel[k]).astype(F32)
    r, c = _iota2((tm, tm))
    before = (c < r).astype(BF16)
    prior = jnp.dot(before, onehot.astype(BF16), preferred_element_type=F32) + carry_ref[...]
    ranks = jnp.zeros(eids.shape, F32)
    for k in range(TOP_K):
        rk = jnp.sum(jnp.where(lane == sel[k], prior, 0.0), axis=-1, keepdims=True)
        ranks = jnp.where(lane == k, rk, ranks)
    rank_ref[...] = ranks.astype(I32)
    total = carry_ref[...] + jnp.sum(onehot, axis=0, keepdims=True)
    carry_ref[...] = total
    cnt_ref[...] = total.astype(I32)


def _route_ranks(eids):
    m = eids.shape[0]
    tm = _row_tile(m, 640)
    return pl.pallas_call(
        _rank_kernel,
        grid=(m // tm,),
        in_specs=[pl.BlockSpec((tm, V7X_LANES), lambda i: (i, 0))],
        out_specs=[pl.BlockSpec((tm, V7X_LANES), lambda i: (i, 0)), pl.BlockSpec((1, V7X_LANES), lambda i: (0, 0))],
        out_shape=[jax.ShapeDtypeStruct((m, V7X_LANES), I32), jax.ShapeDtypeStruct((1, V7X_LANES), I32)],
        scratch_shapes=[pltpu.VMEM((1, V7X_LANES), F32)],
        compiler_params=_params("arbitrary"),
        name="route_ranks",
    )(eids)


def _row_copy(src_ref, s, dst_ref, d, sem):
    return pltpu.make_async_copy(src_ref.at[pl.ds(s, 1)], dst_ref.at[pl.ds(d, 1)], sem)


def _dispatch_kernel(pos_ref, x_ref, out_ref, sem):
    i = pl.program_id(0)
    tm = x_ref.shape[0]
    base = i * tm * TOP_K

    def start(t, carry):
        for k in range(TOP_K):
            _row_copy(x_ref, t, out_ref, pos_ref[base + t * TOP_K + k], sem).start(priority=k % 2)
        return carry

    def wait(t, carry):
        for k in range(TOP_K):
            _row_copy(x_ref, t, out_ref, pos_ref[base + t * TOP_K + k], sem).wait()
        return carry

    lax.fori_loop(0, tm, start, 0, unroll=DMA_UNROLL)
    lax.fori_loop(0, tm, wait, 0, unroll=DMA_UNROLL)


def _dispatch_rows(x1, pos_flat, n_rows):
    m, d = x1.shape
    tm = _row_tile(m, 1024)
    return pl.pallas_call(
        _dispatch_kernel,
        grid_spec=pltpu.PrefetchScalarGridSpec(
            num_scalar_prefetch=1,
            grid=(m // tm,),
            in_specs=[pl.BlockSpec((tm, d), lambda i, pos: (i, 0))],
            out_specs=pl.BlockSpec(memory_space=pl.ANY),
            scratch_shapes=[pltpu.SemaphoreType.DMA(())],
        ),
        out_shape=jax.ShapeDtypeStruct((n_rows, d), x1.dtype),
        compiler_params=_params("arbitrary"),
        name="moe_dispatch",
    )(pos_flat, x1)


def _expert_weight_prep(w1_ref, w2_ref, w1g_ref, w1u_ref, w2b_ref):
    r, c = _iota2((V7X_MXU, V7X_MXU))
    half = V7X_MXU // 2
    src = jnp.where(c < half, 2 * c, 2 * (c - half) + 1)
    perm = jnp.where(r == src, 1.0, 0.0).astype(BF16)
    for j in range(w1_ref.shape[3] // V7X_MXU):
        blk = w1_ref[0, 0, :, j * V7X_MXU:(j + 1) * V7X_MXU].astype(BF16)
        res = jnp.dot(blk, perm, preferred_element_type=F32).astype(BF16)
        w1g_ref[:, j * half:(j + 1) * half] = res[:, :half]
        w1u_ref[:, j * half:(j + 1) * half] = res[:, half:]
    w2b_ref[...] = w2_ref[0, 0].astype(BF16)


def _ffn_kernel(te_ref, tr_ref, nv_ref, x_ref, w1_ref, b1g_ref, b1u_ref, w2_ref, b2_ref, o_ref,
                w1g_ref, w1u_ref, w2b_ref):
    i = pl.program_id(0)
    live = i < nv_ref[0]

    @pl.when(live & ((i == 0) | (te_ref[i] != te_ref[jnp.maximum(i - 1, 0)])))
    def _():
        _expert_weight_prep(w1_ref, w2_ref, w1g_ref, w1u_ref, w2b_ref)

    @pl.when(live)
    def _():
        row = lax.broadcasted_iota(I32, x_ref.shape, 0)
        x = jnp.where(row < tr_ref[i], x_ref[...], 0.0).astype(BF16)
        hg = jnp.dot(x, w1g_ref[...], preferred_element_type=F32) + b1g_ref[0]
        hu = jnp.dot(x, w1u_ref[...], preferred_element_type=F32) + b1u_ref[0]
        gate = jnp.minimum(hg, SWIGLU_LIMIT)
        up = jnp.clip(hu, -SWIGLU_LIMIT, SWIGLU_LIMIT)
        act = (up + 1.0) * gate * jax.nn.sigmoid(SWIGLU_ALPHA * gate)
        o_ref[...] = jnp.dot(act.astype(BF16), w2b_ref[...], preferred_element_type=F32) + b2_ref[0]


def _expert_ffn(x_rows, tile_expert, tile_rows, n_valid, w1_all, b1g, b1u, w2_all, layer, b2):
    rows, d = x_rows.shape
    tm = EXPERT_TILE
    rmap = lambda i, te, tr, nv: (jnp.minimum(i, nv[0] - 1), 0)
    wmap = lambda i, te, tr, nv: (te[i], 0, 0)
    lmap = lambda i, te, tr, nv: (layer, te[i], 0, 0)
    return pl.pallas_call(
        _ffn_kernel,
        grid_spec=pltpu.PrefetchScalarGridSpec(
            num_scalar_prefetch=3,
            grid=(rows // tm,),
            in_specs=[pl.BlockSpec((tm, d), rmap),
                      pl.BlockSpec((1, 1, d, 2 * D_FF), lmap),
                      pl.BlockSpec((1, 1, D_FF), wmap), pl.BlockSpec((1, 1, D_FF), wmap),
                      pl.BlockSpec((1, 1, D_FF, d), lmap),
                      pl.BlockSpec((1, 1, d), wmap)],
            out_specs=pl.BlockSpec((tm, d), rmap),
            scratch_shapes=[pltpu.VMEM((d, D_FF), BF16), pltpu.VMEM((d, D_FF), BF16), pltpu.VMEM((D_FF, d), BF16)],
        ),
        out_shape=jax.ShapeDtypeStruct((rows, d), F32),
        compiler_params=_params("arbitrary"),
        name="moe_expert_ffn",
    )(tile_expert, tile_rows, n_valid, x_rows, w1_all, b1g, b1u, w2_all, b2)


def _combine_kernel(pos_ref, y_ref, gate_ref, x_ref, g_ref, b_ref, o_ref, *rest):
    ob_ref = rest[0] if len(rest) == 3 else None
    buf_ref, sem = rest[-2:]
    i = pl.program_id(0)
    tm = x_ref.shape[0]
    base = i * tm * TOP_K

    def start(t, carry):
        for k in range(TOP_K):
            _row_copy(y_ref, pos_ref[base + t * TOP_K + k], buf_ref.at[k], t, sem).start(priority=k % 2)
        return carry

    def wait(t, carry):
        for k in range(TOP_K):
            _row_copy(y_ref, pos_ref[base + t * TOP_K + k], buf_ref.at[k], t, sem).wait()
        return carry

    lax.fori_loop(0, tm, start, 0, unroll=DMA_UNROLL)
    lax.fori_loop(0, tm, wait, 0, unroll=DMA_UNROLL)
    gates = gate_ref[...]
    acc = DN_ALPHA * x_ref[...]
    for k in range(TOP_K):
        acc = acc + gates[:, k:k + 1] * buf_ref[k]
    xc = acc - jnp.mean(acc, axis=-1, keepdims=True)
    var = jnp.mean(xc * xc, axis=-1, keepdims=True)
    out = xc * lax.rsqrt(var + LN_EPS) * g_ref[...] + b_ref[...]
    o_ref[...] = out
    if ob_ref is not None:
        ob_ref[...] = out.astype(BF16)


def _combine_ln(y_rows, pos_flat, gates, x1, g, b, emit_bf16):
    m, d = x1.shape
    tm = _row_tile(m, 320)
    row = lambda i, pos: (i, 0)
    fixed = lambda i, pos: (0, 0)
    return pl.pallas_call(
        _combine_kernel,
        grid_spec=pltpu.PrefetchScalarGridSpec(
            num_scalar_prefetch=1,
            grid=(m // tm,),
            in_specs=[pl.BlockSpec(memory_space=pl.ANY), pl.BlockSpec((tm, V7X_LANES), row),
                      pl.BlockSpec((tm, d), row), pl.BlockSpec((1, d), fixed), pl.BlockSpec((1, d), fixed)],
            out_specs=[pl.BlockSpec((tm, d), row)] * (2 if emit_bf16 else 1),
            scratch_shapes=[pltpu.VMEM((TOP_K, tm, d), F32), pltpu.SemaphoreType.DMA(())],
        ),
        out_shape=[jax.ShapeDtypeStruct((m, d), F32)] + ([jax.ShapeDtypeStruct((m, d), BF16)] if emit_bf16 else []),
        compiler_params=_params("arbitrary"),
        name="moe_combine_ln",
    )(pos_flat, y_rows, gates, x1, g, b)


def _moe_ln(x1, eids, gates, w1_all, b1g, b1u, w2_all, layer, b2, g, b, emit_bf16):
    m, d = x1.shape
    ranks, counts = _route_ranks(eids)
    counts = counts[0, :N_EXPERTS]
    padded = (counts + EXPERT_TILE - 1) // EXPERT_TILE * EXPERT_TILE
    ends = jnp.cumsum(padded)
    gstart = ends - padded
    n_tiles = -(-m * TOP_K // EXPERT_TILE) + N_EXPERTS
    tile_start = jnp.arange(n_tiles, dtype=I32) * EXPERT_TILE
    tile_expert = jnp.minimum(jnp.sum((tile_start[:, None] >= ends[None, :]).astype(I32), axis=1), N_EXPERTS - 1)
    onehot_t = tile_expert[:, None] == jnp.arange(N_EXPERTS, dtype=I32)[None, :]
    used = jnp.sum(jnp.where(onehot_t, (gstart + counts)[None, :], 0), axis=1)
    tile_rows = jnp.clip(used - tile_start, 0, EXPERT_TILE).astype(I32)
    n_valid = (ends[-1:] // EXPERT_TILE).astype(I32)
    sel = eids[:, :TOP_K, None] == jnp.arange(N_EXPERTS, dtype=I32)[None, None, :]
    pos_flat = (jnp.sum(jnp.where(sel, gstart[None, None, :], 0), axis=-1) + ranks[:, :TOP_K]).reshape(-1).astype(I32)
    x_rows = _dispatch_rows(x1, pos_flat, n_tiles * EXPERT_TILE)
    y_rows = _expert_ffn(x_rows, tile_expert, tile_rows, n_valid, w1_all, b1g, b1u, w2_all, layer, b2)
    return _combine_ln(y_rows, pos_flat, gates, x1, g, b, emit_bf16)


def _fox_prep_kernel(t_ref, bias_ref, lf_ref, ccol_ref):
    t_len = t_ref.shape[0]
    r, c = _iota2((V7X_LANES, V7X_LANES))
    tril = jnp.where(c <= r, 1.0, 0.0)
    carry = jnp.zeros((1, V7X_LANES), F32)
    for blk in range(t_len // V7X_LANES):
        rows = slice(blk * V7X_LANES, (blk + 1) * V7X_LANES)
        lf = _log_sigmoid(t_ref[rows, :] + bias_ref[...])
        lf_ref[rows, :] = lf
        cs = jnp.dot(tril, lf, precision=HIGHEST, preferred_element_type=F32) + carry
        ccol_ref[rows, :] = cs
        carry = cs[V7X_LANES - 1:, :]


def _fox_prep(tail, bias_row, bsz, t_len):
    n_p = bsz * t_len
    blk = pl.BlockSpec((t_len, V7X_LANES), lambda b: (b, 0))
    out = jax.ShapeDtypeStruct((n_p, V7X_LANES), F32)
    return pl.pallas_call(
        _fox_prep_kernel,
        grid=(bsz,),
        in_specs=[blk, pl.BlockSpec((1, V7X_LANES), lambda b: (0, 0))],
        out_specs=[blk, blk],
        out_shape=[out, out],
        compiler_params=_params("parallel"),
        name="fox_prep",
    )(tail, bias_row)


def _fox_attn_kernel(q_ref, k_ref, v_ref, ccol_ref, o_ref, kb_ref, vt_ref, cb0_ref, cb1_ref, m0_ref, m1_ref, l0_ref, l1_ref,
                     acc0_ref, acc1_ref, s0_ref, s1_ref):
    pair = pl.program_id(1)
    tq = ATTN_BLOCK
    t_len = k_ref.shape[0]
    cb_refs, m_refs, l_refs, acc_refs = (cb0_ref, cb1_ref), (m0_ref, m1_ref), (l0_ref, l1_ref), (acc0_ref, acc1_ref)
    s_refs = (s0_ref, s1_ref)

    kb_ref[...] = k_ref[...].astype(BF16)
    sr, sc = _iota2((V7X_LANES, V7X_LANES))
    for j in range(2):
        sel = jnp.where(sr == 2 * pair + j, 1.0, 0.0).astype(BF16)
        cb_refs[j][...] = _expand_heads(ccol_ref[...], sel)
    vt_ref[...] = v_ref[0].astype(BF16)
    feat = lax.broadcasted_iota(I32, (V7X_LANES, tq), 0)
    key_id, qry_id = _iota2((tq, tq))

    def scores(kb, qh_j, j):
        rows = slice(kb * tq, (kb + 1) * tq)
        return (jnp.dot(kb_ref[rows, :], qh_j, preferred_element_type=F32)
                - jnp.concatenate([cb_refs[j][rows, :]] * (tq // V7X_LANES), axis=1))

    def update(kb, j, s):
        m_old = m_refs[j][...]
        m_new = jnp.maximum(m_old, jnp.max(s, axis=0, keepdims=True))
        alpha = jnp.exp(m_old - m_new)
        pe = jnp.exp(s - m_new)
        l_refs[j][...] = alpha * l_refs[j][...] + jnp.sum(pe, axis=0, keepdims=True)
        acc_refs[j][...] = alpha * acc_refs[j][...] + jnp.dot(vt_ref[:, kb * tq:(kb + 1) * tq], pe.astype(BF16),
                                                              preferred_element_type=F32)
        m_refs[j][...] = m_new

    for qi in range(t_len // tq):
        q_rows = slice(qi * tq, (qi + 1) * tq)
        qt = (q_ref[q_rows, :] * (DH_B ** -0.5)).T
        qh = [jnp.where((feat // DH_B) == j, qt, 0.0).astype(BF16) for j in range(2)]
        for j in range(2):
            m_refs[j][...] = jnp.full(m_refs[j].shape, -jnp.inf, F32)
            l_refs[j][...] = jnp.zeros_like(l_refs[j])
            acc_refs[j][...] = jnp.zeros_like(acc_refs[j])
            s_refs[j][...] = scores(0, qh[j], j)
        for kb in range(qi):
            for j in range(2):
                s = s_refs[j][...]
                s_refs[j][...] = scores(kb + 1, qh[j], j)
                update(kb, j, s)
        for j in range(2):
            update(qi, j, jnp.where(key_id <= qry_id, s_refs[j][...], -jnp.inf))
        out_t = jnp.where((feat // DH_B) == 0, acc0_ref[...] / l0_ref[...], acc1_ref[...] / l1_ref[...])
        o_ref[q_rows, :] = out_t.T.astype(o_ref.dtype)


def _fox_attention(h_main, kv_t, ccol, bsz, t_len):
    tq = ATTN_BLOCK
    qcol, kcol = (4 * D_A) // V7X_LANES, (4 * D_A + D_B) // V7X_LANES
    cbs, row, acc = pltpu.VMEM((t_len, V7X_LANES), F32), pltpu.VMEM((1, tq), F32), pltpu.VMEM((V7X_LANES, tq), F32)
    seq = lambda col: pl.BlockSpec((t_len, V7X_LANES), lambda b, p: (b, col + p))
    return pl.pallas_call(
        _fox_attn_kernel,
        grid=(bsz, H_B // 2),
        in_specs=[seq(qcol), seq(kcol),
                  pl.BlockSpec((1, V7X_LANES, t_len), lambda b, p: (b, D_B // V7X_LANES + p, 0)),
                  pl.BlockSpec((t_len, V7X_LANES), lambda b, p: (b, 0))],
        out_specs=pl.BlockSpec((t_len, V7X_LANES), lambda b, p: (b, p)),
        out_shape=jax.ShapeDtypeStruct((h_main.shape[0], D_B), BF16),
        scratch_shapes=[pltpu.VMEM((t_len, V7X_LANES), BF16), pltpu.VMEM((V7X_LANES, t_len), BF16),
                        cbs, cbs, row, row, row, row, acc, acc, pltpu.VMEM((tq, tq), F32), pltpu.VMEM((tq, tq), F32)],
        compiler_params=_params("parallel", "parallel"),
        name="fox_attention",
    )(h_main, h_main, kv_t, ccol)


def _fox_decode_kernel(pt_ref, q_ref, kn_ref, vn_ref, t_ref, bias_ref, *rest):
    np_ = DECODE_PAGES
    k_refs, v_refs, lf_refs = rest[:np_], rest[np_:2 * np_], rest[2 * np_:3 * np_]
    all_ref, o_ref, lfo_ref, m_ref, l_ref, acc_ref, carry_ref, rows_ref = rest[3 * np_:]
    del all_ref
    b = pl.program_id(0)
    j = pl.program_id(1)
    q = q_ref[0] * (DH_B ** -0.5)
    hrow, hlane = _iota2((H_B, D_B))
    own = hlane // DH_B == hrow
    qmat = jnp.where(own, jnp.broadcast_to(q, (H_B, D_B)), 0.0)
    lf_new = _log_sigmoid(t_ref[0] + bias_ref[...])

    @pl.when(j == 0)
    def _():
        m_ref[...] = jnp.sum(qmat * kn_ref[0], axis=-1, keepdims=True)
        l_ref[...] = jnp.ones_like(l_ref)
        acc_ref[...] = jnp.broadcast_to(vn_ref[0], acc_ref.shape)
        carry_ref[...] = jnp.zeros_like(carry_ref)
        lfo_ref[0] = lf_new

    r8, c8 = _iota2((H_B, H_B))
    cn = jnp.sum(jnp.where(r8 == c8, jnp.broadcast_to(lf_new[:, :H_B], (H_B, H_B)), 0.0), axis=-1, keepdims=True)
    kr, kc = _iota2((PAGE_SIZE, PAGE_SIZE))
    later = jnp.where(kr > kc, 1.0, 0.0).astype(BF16)
    qb = qmat.astype(BF16)
    lfts = [lf_refs[i][0] for i in range(np_)]
    carries = [None] * np_
    run = carry_ref[...]
    for i in reversed(range(np_)):
        carries[i] = run
        run = run + jnp.sum(lfts[i], axis=-1, keepdims=True)
    carry_ref[...] = run
    scores = []
    for i in range(np_):
        lf = lfts[i]
        hi = lf.astype(BF16)
        r1 = lf - hi.astype(F32)
        mid = r1.astype(BF16)
        lo = (r1 - mid.astype(F32)).astype(BF16)
        parts = jnp.dot(jnp.concatenate([hi, mid, lo], axis=0), later, preferred_element_type=F32)
        suffix = parts[:H_B] + parts[H_B:2 * H_B] + parts[2 * H_B:]
        k2 = k_refs[i][0].reshape(D_B, PAGE_SIZE).astype(BF16)
        scores.append(jnp.dot(qb, k2, preferred_element_type=F32) + (suffix + (cn + carries[i])))
    m_old = m_ref[...]
    m_new = m_old
    for s in scores:
        m_new = jnp.maximum(m_new, jnp.max(s, axis=-1, keepdims=True))
    alpha = jnp.exp(m_old - m_new)
    l_new = alpha * l_ref[...]
    acc = alpha * acc_ref[...]
    for i, s in enumerate(scores):
        pe = jnp.exp(s - m_new)
        l_new = l_new + jnp.sum(pe, axis=-1, keepdims=True)
        v2 = v_refs[i][0].reshape(D_B, PAGE_SIZE).astype(BF16)
        acc = acc + lax.dot_general(pe.astype(BF16), v2, NT_DIMS, preferred_element_type=F32)
    m_ref[...] = m_new
    l_ref[...] = l_new
    acc_ref[...] = acc

    @pl.when(j == pl.num_programs(1) - 1)
    def _():
        o = jnp.where(own, acc / l_new, 0.0)
        rows_ref[pl.ds(b, 1), :] = jnp.sum(o, axis=0, keepdims=True)

    @pl.when((j == pl.num_programs(1) - 1) & (b == pl.num_programs(0) - 1))
    def _():
        o_ref[...] = rows_ref[...].astype(o_ref.dtype)


def _fox_decode(hs_main, hs_tail, bias_row, cache_kt, cache_vt, logf_t, page_table, ob_all):
    bd = hs_main.shape[0]
    n_p = ob_all.shape[0] - bd
    n_pages = page_table.shape[1]
    steps = n_pages // DECODE_PAGES
    qcol, kcol, vcol = (4 * D_A) // D_B, (4 * D_A + D_B) // D_B, (4 * D_A + 2 * D_B) // D_B

    def page(i, nd):
        return lambda b, j, pt: (pt[b * n_pages + (steps - 1 - j) * DECODE_PAGES + i],) + (0,) * nd

    tok = lambda col: pl.BlockSpec((1, 1, D_B), lambda b, j, pt: (b, 0, col))
    in_specs = [tok(qcol), tok(kcol), tok(vcol),
                pl.BlockSpec((1, 1, V7X_LANES), lambda b, j, pt: (b, 0, 0)),
                pl.BlockSpec((1, V7X_LANES), lambda b, j, pt: (0, 0))]
    in_specs += [pl.BlockSpec((1, H_B, DH_B, PAGE_SIZE), page(i, 3)) for i in range(DECODE_PAGES)]
    in_specs += [pl.BlockSpec((1, H_B, DH_B, PAGE_SIZE), page(i, 3)) for i in range(DECODE_PAGES)]
    in_specs += [pl.BlockSpec((1, H_B, PAGE_SIZE), page(i, 2)) for i in range(DECODE_PAGES)]
    in_specs += [pl.BlockSpec(memory_space=pl.ANY)]
    n_in = len(in_specs) + 1
    return pl.pallas_call(
        _fox_decode_kernel,
        grid_spec=pltpu.PrefetchScalarGridSpec(
            num_scalar_prefetch=1,
            grid=(bd, steps),
            in_specs=in_specs,
            out_specs=[_token_rows_spec(n_p, bd, D_B),
                       pl.BlockSpec((1, 1, V7X_LANES), lambda b, j, pt: (b, 0, 0))],
            scratch_shapes=[pltpu.VMEM((H_B, 1), F32), pltpu.VMEM((H_B, 1), F32), pltpu.VMEM((H_B, D_B), F32),
                            pltpu.VMEM((H_B, 1), F32), pltpu.VMEM((bd, D_B), F32)],
        ),
        out_shape=[jax.ShapeDtypeStruct(ob_all.shape, BF16), jax.ShapeDtypeStruct((bd, 1, V7X_LANES), F32)],
        input_output_aliases={n_in - 1: 0},
        compiler_params=_params("arbitrary", "arbitrary"),
        name="fox_decode",
    )(page_table.reshape(-1), hs_main, hs_main, hs_main, hs_tail, bias_row,
      *([cache_kt] * DECODE_PAGES), *([cache_vt] * DECODE_PAGES), *([logf_t] * DECODE_PAGES), ob_all)


def _hgrn_gates(q, z, lb):
    qa = _silu(q)
    logf = jnp.log(lb + (1.0 - lb) * jax.nn.sigmoid(z))
    ka = (1.0 - lb) * jax.nn.sigmoid(-z)
    return qa, ka, logf


def _hgrn_out(o, g, nw):
    o = o * lax.rsqrt(jnp.mean(o * o, axis=-1, keepdims=True) + RMS_EPS)
    return o * nw * _silu(g)


def _hgrn_kernel(q_ref, f_ref, i_ref, g_ref, lb_ref, nw_ref, o_ref, s_ref, qd_ref, dec_ref, oc_ref, u_ref):
    t_len = q_ref.shape[0]
    cs = CHUNK_A
    per_blk = V7X_LANES // cs
    n_blk = t_len // V7X_LANES
    lb = lb_ref[0]
    r, c = _iota2((V7X_LANES, V7X_LANES))
    same = (r // cs) == (c // cs)
    causal = same & (c <= r)
    sums = jnp.concatenate([jnp.where(causal, 1.0, 0.0), jnp.where(same, 1.0, 0.0)], axis=0).astype(BF16)

    def intra(blk, carry):
        start = pl.multiple_of(blk * V7X_LANES, V7X_LANES)
        rows = pl.ds(start, V7X_LANES)
        qa, ka, logf = _hgrn_gates(q_ref[rows, :], f_ref[rows, :], lb)
        both = _select_sum(sums, logf)
        b, gtot = both[:V7X_LANES], both[V7X_LANES:]
        qd = (qa * jnp.exp(b)).astype(BF16)
        kd = (ka * jnp.exp(-b)).astype(BF16)
        ke = (ka * jnp.exp(gtot - b)).astype(BF16)
        v = i_ref[rows, :].astype(BF16)
        qd_ref[rows, :] = qd
        sc = jnp.where(causal, lax.dot_general(qd, kd, NT_DIMS, preferred_element_type=F32), 0.0)
        oc_ref[rows, :] = jnp.dot(sc.astype(BF16), v, preferred_element_type=F32)
        dec = jnp.exp(gtot)
        for k in range(per_blk):
            sub = slice(k * cs, (k + 1) * cs)
            ci = blk * per_blk + k
            u_ref[ci] = lax.dot_general(v[sub], ke[sub], TN_DIMS, preferred_element_type=F32)
            dec_ref[pl.ds(ci, 1), :] = dec[k * cs:k * cs + 1, :]
        return carry

    lax.fori_loop(0, n_blk, intra, 0, unroll=4)

    def scan(ci, st):
        u = u_ref[ci]
        u_ref[ci] = st
        return st * dec_ref[pl.ds(ci, 1), :] + u

    st_last = lax.fori_loop(0, t_len // cs, scan, jnp.zeros((DV_A, DK_A), F32), unroll=4)
    s_ref[0, 0] = st_last.T

    def inter(blk, carry):
        start = pl.multiple_of(blk * V7X_LANES, V7X_LANES)
        rows = pl.ds(start, V7X_LANES)
        qd = qd_ref[rows, :]
        parts = [lax.dot_general(qd[k * cs:(k + 1) * cs], u_ref[blk * per_blk + k].astype(BF16), NT_DIMS,
                                 preferred_element_type=F32) for k in range(per_blk)]
        o = oc_ref[rows, :] + jnp.concatenate(parts, axis=0)
        o_ref[rows, :] = _hgrn_out(o, g_ref[rows, :], nw_ref[...]).astype(o_ref.dtype)
        return carry

    lax.fori_loop(0, n_blk, inter, 0, unroll=4)


def _hgrn_prompt(h_main, lb3, nw_row, bsz, t_len):
    n_p = bsz * t_len
    n_chunks = t_len // CHUNK_A
    col = lambda grp: (lambda b, h: (b, grp * H_A + h))
    return pl.pallas_call(
        _hgrn_kernel,
        grid=(bsz, H_A),
        in_specs=[pl.BlockSpec((t_len, DK_A), col(0)), pl.BlockSpec((t_len, DK_A), col(1)),
                  pl.BlockSpec((t_len, DV_A), col(2)), pl.BlockSpec((t_len, DV_A), col(3)),
                  pl.BlockSpec((1, 1, DK_A), lambda b, h: (h, 0, 0)), pl.BlockSpec((1, DV_A), lambda b, h: (0, 0))],
        out_specs=[pl.BlockSpec((t_len, DV_A), lambda b, h: (b, h)),
                   pl.BlockSpec((1, 1, DK_A, DV_A), lambda b, h: (b, h, 0, 0))],
        out_shape=[jax.ShapeDtypeStruct((h_main.shape[0], D_A), BF16),
                   jax.ShapeDtypeStruct((bsz, H_A, DK_A, DV_A), F32)],
        scratch_shapes=[pltpu.VMEM((t_len, DK_A), BF16), pltpu.VMEM((n_chunks, DK_A), F32),
                        pltpu.VMEM((t_len, DV_A), F32), pltpu.VMEM((n_chunks, DV_A, DK_A), F32)],
        compiler_params=_params("parallel", "parallel"),
        name="hgrn_prompt",
    )(h_main, h_main, h_main, h_main, lb3, nw_row)


def _hgrn_step_kernel(h_ref, lb_ref, nw_ref, s0_ref, all_ref, o_ref, s_ref, rows_ref):
    del all_ref
    b = pl.program_id(0)
    outs = []
    for h in range(H_A):
        grp = lambda g: h_ref[0, :, g * D_A + h * DK_A: g * D_A + (h + 1) * DK_A]
        qa, ka, logf = _hgrn_gates(grp(0), grp(1), lb_ref[h])
        v = grp(2)
        s_new = _to_column(jnp.exp(logf)) * s0_ref[0, h] + _to_column(ka) * v
        s_ref[0, h] = s_new
        o = jnp.sum(_to_column(qa) * s_new, axis=0, keepdims=True)
        outs.append(_hgrn_out(o, grp(3), nw_ref[...]))
    rows_ref[pl.ds(b, 1), :] = jnp.concatenate(outs, axis=-1)

    @pl.when(b == pl.num_programs(0) - 1)
    def _():
        o_ref[...] = rows_ref[...].astype(o_ref.dtype)


def _hgrn_step(hs_main, lb3, nw_row, s0, oa_all):
    bd = hs_main.shape[0]
    n_p = oa_all.shape[0] - bd
    return pl.pallas_call(
        _hgrn_step_kernel,
        grid=(bd,),
        in_specs=[pl.BlockSpec((1, 1, 4 * D_A), lambda b: (b, 0, 0)), pl.BlockSpec((H_A, 1, DK_A), lambda b: (0, 0, 0)),
                  pl.BlockSpec((1, DV_A), lambda b: (0, 0)), pl.BlockSpec((1, H_A, DK_A, DV_A), lambda b: (b, 0, 0, 0)),
                  pl.BlockSpec(memory_space=pl.ANY)],
        out_specs=[_token_rows_spec(n_p, bd, D_A),
                   pl.BlockSpec((1, H_A, DK_A, DV_A), lambda b: (b, 0, 0, 0))],
        out_shape=[jax.ShapeDtypeStruct(oa_all.shape, BF16), jax.ShapeDtypeStruct(s0.shape, F32)],
        scratch_shapes=[pltpu.VMEM((bd, D_A), F32)],
        input_output_aliases={4: 0},
        compiler_params=_params("arbitrary"),
        name="hgrn_step",
    )(hs_main, lb3, nw_row, s0, oa_all)


def _conv_silu(cur, prev, w, b):
    row8 = lax.broadcasted_iota(I32, prev.shape, 0)
    acc = b + cur * w[CONV_W - 1:CONV_W, :]
    for s in range(1, CONV_W):
        sh = pltpu.roll(cur, s, 0)
        head = jnp.where(row8 < s, pltpu.roll(prev, s, 0), sh[:V7X_SUBLANES, :])
        shifted = jnp.concatenate([head, sh[V7X_SUBLANES:, :]], axis=0)
        acc = acc + shifted * w[CONV_W - 1 - s:CONV_W - s, :]
    return _silu(acc)


def _gated_group_norm(y, z, nw):
    y = y * _silu(z)
    parts = []
    for g in range(N_GROUPS_C):
        seg = y[:, g * GROUP_W:(g + 1) * GROUP_W]
        parts.append(seg * lax.rsqrt(jnp.mean(seg * seg, axis=-1, keepdims=True) + RMS_EPS))
    return jnp.concatenate(parts, axis=-1) * nw


def _ssd_kernel(z_ref, x_ref, bc_ref, dt_ref, cwx_ref, cbx_ref, cwbc_ref, cbbc_ref, dtb_ref, alog_ref, dsk_ref,
                nw_ref, y_ref, hs_ref, tailx_ref, tailbc_ref, ht_ref, e_ref, yacc_ref, xw_ref):
    tb = pl.program_id(1)
    tt = x_ref.shape[0]
    pair_w = 2 * HEADDIM_C
    heads_per_group = H_C // N_GROUPS_C

    @pl.when(tb == 0)
    def _():
        tailx_ref[...] = jnp.zeros_like(tailx_ref)
        tailbc_ref[...] = jnp.zeros_like(tailbc_ref)
        ht_ref[...] = jnp.zeros_like(ht_ref)
        er, ec = _iota2(e_ref.shape)
        e_ref[...] = jnp.where(ec // HEADDIM_C == er, 1.0, 0.0).astype(BF16)

    x_raw = x_ref[...]
    bc_raw = bc_ref[...]
    xs = _conv_silu(x_raw, tailx_ref[...], cwx_ref[...], cbx_ref[...])
    bcv = _conv_silu(bc_raw, tailbc_ref[...], cwbc_ref[...], cbbc_ref[...])
    tailx_ref[...] = x_raw[tt - V7X_SUBLANES:, :]
    tailbc_ref[...] = bc_raw[tt - V7X_SUBLANES:, :]

    lane = lax.broadcasted_iota(I32, (tt, V7X_LANES), 1)
    dt = jnp.where(lane < H_C, _softplus(dt_ref[...] + dtb_ref[...]), 0.0)
    a = -jnp.exp(alog_ref[...])
    r, c = _iota2((tt, tt))
    causal = c <= r
    cum = jnp.dot(jnp.where(causal, 1.0, 0.0), dt * a, precision=HIGHEST, preferred_element_type=F32)
    cum_t = cum.T
    xdt = xs * _expand_heads(dt, e_ref[...])
    low = (lane % pair_w) < HEADDIM_C

    for g in range(N_GROUPS_C):
        b_g = bcv[:, g * D_STATE_C:(g + 1) * D_STATE_C]
        c_g = bcv[:, D_BC + g * D_STATE_C:D_BC + (g + 1) * D_STATE_C].astype(BF16)
        cb = lax.dot_general(c_g, b_g.astype(BF16), NT_DIMS, preferred_element_type=F32)
        y_inter = jnp.dot(c_g, ht_ref[g].astype(BF16), preferred_element_type=F32)
        decs = []
        for pr in range(heads_per_group // 2):
            slab = slice(g * GROUP_W + pr * pair_w, g * GROUP_W + (pr + 1) * pair_w)
            xdt_slab = xdt[:, slab]
            ys, es, tes = [], [], []
            for j in range(2):
                head = g * heads_per_group + pr * 2 + j
                colb = jnp.broadcast_to(cum[:, head:head + 1], (tt, tt))
                decay = jnp.exp(jnp.where(causal, colb - cum_t[head:head + 1, :], -jnp.inf))
                ys.append(jnp.dot((cb * decay).astype(BF16), xdt_slab.astype(BF16), preferred_element_type=F32))
                es.append(jnp.exp(colb))
                tes.append(jnp.exp(colb[tt - 1:, :] - colb))
            e_pair = jnp.where(low, es[0], es[1])
            yacc_ref[:, slab] = (jnp.where(low, ys[0], ys[1]) + e_pair * y_inter[:, pr * pair_w:(pr + 1) * pair_w]
                                 + dsk_ref[:, slab] * xs[:, slab])
            xw_ref[:, pr * pair_w:(pr + 1) * pair_w] = xdt_slab * jnp.where(low, tes[0], tes[1])
            decs.append(e_pair[tt - 1:, :])
        dec_row = jnp.concatenate(decs, axis=-1)
        ht_ref[g] = ht_ref[g] * dec_row + jnp.dot(b_g.T.astype(BF16), xw_ref[...].astype(BF16),
                                                  preferred_element_type=F32)

    y_ref[...] = _gated_group_norm(yacc_ref[...], z_ref[...], nw_ref[...]).astype(y_ref.dtype)

    @pl.when(tb == pl.num_programs(1) - 1)
    def _():
        for g in range(N_GROUPS_C):
            for q in range(GROUP_W // V7X_LANES):
                rows = slice(g * GROUP_W + q * V7X_LANES, g * GROUP_W + (q + 1) * V7X_LANES)
                hs_ref[0, rows, :] = ht_ref[g][:, q * V7X_LANES:(q + 1) * V7X_LANES].T


def _ssd_prompt(h_main, h_tail, conv_w, conv_b, dtb_row, alog_row, dsk_row, nw_row, bsz, t_len):
    n_p = bsz * t_len
    tt = SSD_BLOCK
    nt = t_len // tt
    rowmap = lambda col: (lambda b, t: (b * nt + t, col))
    fixed = lambda b, t: (0, 0)
    cwx, cwbc = conv_w[:, :D_INNER], conv_w[:, D_INNER:]
    cbx, cbbc = conv_b[None, :D_INNER], conv_b[None, D_INNER:]
    return pl.pallas_call(
        _ssd_kernel,
        grid=(bsz, nt),
        in_specs=[pl.BlockSpec((tt, D_INNER), rowmap(0)), pl.BlockSpec((tt, D_INNER), rowmap(1)),
                  pl.BlockSpec((tt, 2 * D_BC), rowmap(2 * D_INNER // (2 * D_BC))),
                  pl.BlockSpec((tt, V7X_LANES), rowmap(0)),
                  pl.BlockSpec((CONV_W, D_INNER), fixed), pl.BlockSpec((1, D_INNER), fixed),
                  pl.BlockSpec((CONV_W, 2 * D_BC), fixed), pl.BlockSpec((1, 2 * D_BC), fixed),
                  pl.BlockSpec((1, V7X_LANES), fixed), pl.BlockSpec((1, V7X_LANES), fixed),
                  pl.BlockSpec((1, D_INNER), fixed), pl.BlockSpec((1, D_INNER), fixed)],
        out_specs=[pl.BlockSpec((tt, D_INNER), rowmap(0)),
                   pl.BlockSpec((1, D_INNER, D_STATE_C), lambda b, t: (b, 0, 0))],
        out_shape=[jax.ShapeDtypeStruct((h_main.shape[0], D_INNER), BF16),
                   jax.ShapeDtypeStruct((bsz, D_INNER, D_STATE_C), F32)],
        scratch_shapes=[pltpu.VMEM((V7X_SUBLANES, D_INNER), F32), pltpu.VMEM((V7X_SUBLANES, 2 * D_BC), F32),
                        pltpu.VMEM((N_GROUPS_C, D_STATE_C, GROUP_W), F32), pltpu.VMEM((V7X_LANES, D_INNER), BF16),
                        pltpu.VMEM((tt, D_INNER), F32), pltpu.VMEM((tt, GROUP_W), F32)],
        compiler_params=_params("parallel", "arbitrary"),
        name="ssd_prompt",
    )(h_main, h_main, h_main, h_tail, cwx, cbx, cwbc, cbbc, dtb_row, alog_row, dsk_row, nw_row)


def _ssd_step_kernel(h_ref, t_ref, cs_ref, cw_ref, cb_ref, dtb_ref, alog_ref, dsk_ref, nw_ref, h0_ref, all_ref,
                     y_ref, hn_ref, rows_ref):
    del all_ref
    b = pl.program_id(0)
    z = h_ref[0, :, :D_INNER]
    xbc_new = h_ref[0, :, D_INNER:]
    cw = cw_ref[...]
    conv = cb_ref[...] + xbc_new * cw[CONV_W - 1:CONV_W, :]
    for j in range(CONV_W - 1):
        conv = conv + cs_ref[0, j:j + 1, :] * cw[j:j + 1, :]
    xbc = _silu(conv)
    xs = xbc[:, :D_INNER]
    lane = lax.broadcasted_iota(I32, (1, V7X_LANES), 1)
    dt = jnp.where(lane < H_C, _softplus(t_ref[0] + dtb_ref[...]), 0.0)
    da = jnp.exp(dt * -jnp.exp(alog_ref[...]))
    er, ec = _iota2((V7X_LANES, D_INNER))
    expand = jnp.where(ec // HEADDIM_C == er, 1.0, 0.0).astype(BF16)
    rows8 = lambda v: jnp.broadcast_to(v, (V7X_SUBLANES, V7X_LANES))
    dt_x = _expand_heads(rows8(dt), expand)[:1, :]
    da_x = _expand_heads(rows8(da), expand)[:1, :]
    xdt = xs * dt_x
    y_parts = []
    for q in range(D_INNER // V7X_LANES):
        g = q // (GROUP_W // V7X_LANES)
        lanes = slice(q * V7X_LANES, (q + 1) * V7X_LANES)
        b_g = xbc[:, D_INNER + g * D_STATE_C:D_INNER + (g + 1) * D_STATE_C]
        c_g = xbc[:, D_INNER + D_BC + g * D_STATE_C:D_INNER + D_BC + (g + 1) * D_STATE_C]
        h_new = _to_column(da_x[:, lanes]) * h0_ref[0, lanes, :] + _to_column(xdt[:, lanes]) * b_g
        hn_ref[0, lanes, :] = h_new
        y_parts.append(_to_row(jnp.sum(h_new * c_g, axis=-1, keepdims=True)))
    y = jnp.concatenate(y_parts, axis=-1) + dsk_ref[...] * xs
    rows_ref[pl.ds(b, 1), :] = _gated_group_norm(y, z, nw_ref[...])

    @pl.when(b == pl.num_programs(0) - 1)
    def _():
        y_ref[...] = rows_ref[...].astype(y_ref.dtype)


def _ssd_step(hs_main, hs_tail, conv_state, conv_w, conv_b, dtb_row, alog_row, dsk_row, nw_row, h0, y_all):
    bd = hs_main.shape[0]
    n_p = y_all.shape[0] - bd
    fixed = lambda b: (0, 0)
    tok = lambda b: (b, 0, 0)
    return pl.pallas_call(
        _ssd_step_kernel,
        grid=(bd,),
        in_specs=[pl.BlockSpec((1, 1, ODD_MAIN), tok), pl.BlockSpec((1, 1, V7X_LANES), tok),
                  pl.BlockSpec((1, CONV_W - 1, CONV_DIM), tok),
                  pl.BlockSpec((CONV_W, CONV_DIM), fixed), pl.BlockSpec((1, CONV_DIM), fixed),
                  pl.BlockSpec((1, V7X_LANES), fixed), pl.BlockSpec((1, V7X_LANES), fixed),
                  pl.BlockSpec((1, D_INNER), fixed), pl.BlockSpec((1, D_INNER), fixed),
                  pl.BlockSpec((1, D_INNER, D_STATE_C), tok), pl.BlockSpec(memory_space=pl.ANY)],
        out_specs=[_token_rows_spec(n_p, bd, D_INNER), pl.BlockSpec((1, D_INNER, D_STATE_C), tok)],
        out_shape=[jax.ShapeDtypeStruct(y_all.shape, BF16), jax.ShapeDtypeStruct((bd, D_INNER, D_STATE_C), F32)],
        scratch_shapes=[pltpu.VMEM((bd, D_INNER), F32)],
        input_output_aliases={10: 0},
        compiler_params=_params("arbitrary"),
        name="ssd_step",
    )(hs_main, hs_tail, conv_state, conv_w, conv_b[None], dtb_row, alog_row, dsk_row, nw_row, h0, y_all)


def _pad_lanes(v):
    return jnp.pad(v, (0, V7X_LANES - v.shape[0]))[None]


def kernel(x_prompt, x_sample, cache_k, cache_v, cache_logf, page_table, state_hgrn, state_ssm, state_conv,
           w_in_even, hgrn_lower_bound, hgrn_norm_w, fox_f_bias, w_out_even,
           w_in_odd, conv_w, conv_b, dt_bias, a_log, d_skip, ssm_norm_w, w_out_odd,
           ln1_g, ln1_b, ln2_g, ln2_b, router_w, router_b, exp_w1, exp_b1, exp_w2, exp_b2):
    lb_all = jnp.cumsum(jax.nn.softmax(hgrn_lower_bound, axis=0), axis=0)
    bp, t_p, d = x_prompt.shape
    bd, t_d, _ = x_sample.shape
    assert t_d == 1
    n_p = bp * t_p
    x_all = jnp.concatenate([x_prompt.reshape(n_p, d), x_sample.reshape(bd, d)], axis=0)
    x_bf = x_all.astype(BF16)
    outs = {}
    for l in range(DEPTH):
        i = l // 2
        if l % 2 == 0:
            h_main = _matmul(x_bf, w_in_even[i][:, :EVEN_MAIN].astype(BF16))
            h_tail = _matmul(x_bf, w_in_even[i][:, EVEN_MAIN:].astype(BF16))
            hs_main, hs_tail = h_main[n_p:, None, :], h_tail[n_p:, None, :]
            lb3 = lb_all[i].reshape(H_A, 1, DK_A)
            nw_row = hgrn_norm_w[i][None]
            bias_row = _pad_lanes(fox_f_bias[i])
            lf_p, ccol = _fox_prep(h_tail, bias_row, bp, t_p)
            kcol, vcol = 4 * D_A + D_B, 4 * D_A + 2 * D_B
            kv_t = _proj_t(x_bf, w_in_even[i][:, kcol:kcol + 2 * D_B].T.astype(BF16), bp, t_p)
            ob_p = _fox_attention(h_main, kv_t, ccol, bp, t_p)
            oa_p, hg_p = _hgrn_prompt(h_main, lb3, nw_row, bp, t_p)
            oa_all, hg_s = _hgrn_step(hs_main, lb3, nw_row, state_hgrn[i], oa_p)
            ob_all, lf_s = _fox_decode(hs_main, hs_tail, bias_row, jnp.transpose(cache_k[i], (0, 2, 3, 1)),
                                       jnp.transpose(cache_v[i], (0, 2, 3, 1)), jnp.swapaxes(cache_logf[i], 1, 2),
                                       page_table, ob_p)
            w_out = w_out_even[i].astype(BF16)
            parts = [(oa_all, w_out[:D_A]), (ob_all, w_out[D_A:])]
            heads_last = lambda a: jnp.transpose(a.reshape(bp, H_B, DH_B, t_p), (0, 3, 1, 2))
            vals = (("kp", heads_last(kv_t[:, :D_B])),
                    ("ks", h_main[n_p:, kcol:kcol + D_B].reshape(bd, t_d, H_B, DH_B)),
                    ("vp", heads_last(kv_t[:, D_B:])),
                    ("vs", h_main[n_p:, vcol:vcol + D_B].reshape(bd, t_d, H_B, DH_B)),
                    ("lfp", lf_p[:, :H_B].reshape(bp, t_p, H_B)), ("lfs", lf_s[:, :, :H_B]),
                    ("hgp", hg_p), ("hgs", hg_s))
        else:
            h_main = _matmul(x_bf, w_in_odd[i][:, :ODD_MAIN].astype(BF16))
            h_tail = _matmul(x_bf, w_in_odd[i][:, ODD_MAIN:].astype(BF16))
            hs_main, hs_tail = h_main[n_p:, None, :], h_tail[n_p:, None, :]
            dtb_row, alog_row = _pad_lanes(dt_bias[i]), _pad_lanes(a_log[i])
            dsk_row = jnp.repeat(d_skip[i], HEADDIM_C)[None]
            nw_row = ssm_norm_w[i][None]
            y_p, ss_p = _ssd_prompt(h_main, h_tail, conv_w[i], conv_b[i], dtb_row, alog_row, dsk_row, nw_row, bp, t_p)
            y_all, ss_s = _ssd_step(hs_main, hs_tail, state_conv[i], conv_w[i], conv_b[i], dtb_row, alog_row, dsk_row,
                                    nw_row, state_ssm[i].reshape(bd, D_INNER, D_STATE_C), y_p)
            parts = [(y_all, w_out_odd[i].astype(BF16))]
            tail_rows = jnp.stack([h_main[(b + 1) * t_p - (CONV_W - 1):(b + 1) * t_p, D_INNER:] for b in range(bp)])
            vals = (("ssp", ss_p.reshape(bp, H_C, HEADDIM_C, D_STATE_C)),
                    ("sss", ss_s.reshape(bd, H_C, HEADDIM_C, D_STATE_C)),
                    ("cvp", tail_rows),
                    ("cvs", jnp.concatenate([state_conv[i][:, 1:], hs_main[:, :, D_INNER:]], axis=1)))
        for name, val in vals:
            outs.setdefault(name, []).append(val)
        rw = jnp.pad(router_w[l], ((0, 0), (0, V7X_LANES - N_EXPERTS)))
        rb = jnp.concatenate([router_b[l], jnp.full((V7X_LANES - N_EXPERTS,), -jnp.inf, F32)])[None]
        x1, eids, gates = _post_mixer(x_all, parts, ln1_g[l][None], ln1_b[l][None], rw, rb)
        b1g = exp_b1[l][:, None, 0::2]
        b1u = exp_b1[l][:, None, 1::2]
        more = l + 1 < DEPTH
        res = _moe_ln(x1, eids, gates, exp_w1, b1g, b1u, exp_w2, l, exp_b2[l][:, None, :],
                      ln2_g[l][None], ln2_b[l][None], more)
        x_all = res[0]
        x_bf = res[1] if more else None
    st = {k: jnp.stack(v) for k, v in outs.items()}
    return (x_all[:n_p].reshape(bp, t_p, d), x_all[n_p:].reshape(bd, t_d, d),
            st["kp"], st["ks"], st["vp"], st["vs"], st["lfp"], st["lfs"],
            st["hgp"], st["hgs"], st["ssp"], st["sss"], st["cvp"], st["cvs"])
```

```python
import functools

import jax
import jax.numpy as jnp
import numpy as np
from jax import lax
from jax.experimental import pallas as pl
from jax.experimental.pallas import tpu as pltpu

F32 = jnp.float32
BF16 = jnp.bfloat16
I32 = jnp.int32
HIGHEST = lax.Precision.HIGHEST

D_MODEL = 1024
DEPTH = 2
PAGE_SIZE = 128
H_A, DK_A, DV_A, CHUNK_A = 4, 128, 128, 32
H_B, DH_B = 8, 64
D_A = H_A * DK_A
D_B = H_B * DH_B
EVEN_MAIN = 4 * D_A + 3 * D_B
D_INNER = 2 * D_MODEL
HEADDIM_C = 64
H_C = D_INNER // HEADDIM_C
N_GROUPS_C = 4
D_STATE_C = 128
D_BC = N_GROUPS_C * D_STATE_C
GROUP_W = D_INNER // N_GROUPS_C
CONV_W = 4
CONV_DIM = D_INNER + 2 * D_BC
ODD_MAIN = D_INNER + CONV_DIM
N_EXPERTS = 32
TOP_K = 4
D_FF = D_MODEL
SWIGLU_LIMIT = 7.0
SWIGLU_ALPHA = 1.702
DN_ALPHA = (2 * DEPTH) ** 0.25
LN_EPS = 1e-5
RMS_EPS = 1e-6

V7X_LANES = 128
V7X_SUBLANES = 8
V7X_MXU = 256
VMEM_LIMIT = 56 * 1024 * 1024
EXPERT_TILE = 512
ATTN_BLOCK = 256
SSD_BLOCK = 128
DECODE_PAGES = 16
DMA_UNROLL = 8

NT_DIMS = (((1,), (1,)), ((), ()))
TN_DIMS = (((0,), (0,)), ((), ()))


def _row_tile(m, cap):
    best = 0
    for t in range(16, cap + 1, 16):
        if m % t == 0:
            best = t
    assert best, (m, cap)
    return best


def _token_rows_spec(n_rows, bd, width):
    assert n_rows % bd == 0
    return pl.BlockSpec((bd, width), lambda *_: (n_rows // bd, 0))


def _params(*sem):
    return pltpu.CompilerParams(dimension_semantics=sem, vmem_limit_bytes=VMEM_LIMIT)


def _silu(x):
    return x * jax.nn.sigmoid(x)


def _softplus(x):
    return jnp.maximum(x, 0.0) + jnp.log1p(jnp.exp(-jnp.abs(x)))


def _log_sigmoid(x):
    return -_softplus(-x)


def _iota2(shape):
    return lax.broadcasted_iota(I32, shape, 0), lax.broadcasted_iota(I32, shape, 1)


def _to_column(row_vec):
    n = row_vec.shape[1]
    r, c = _iota2((n, n))
    return jnp.sum(jnp.where(r == c, jnp.broadcast_to(row_vec, (n, n)), 0.0), axis=1, keepdims=True)


def _to_row(col_vec):
    n = col_vec.shape[0]
    r, c = _iota2((n, n))
    return jnp.sum(jnp.where(r == c, jnp.broadcast_to(col_vec, (n, n)), 0.0), axis=0, keepdims=True)


def _expand_heads(v, e_bf16):
    hi = v.astype(BF16)
    r1 = v - hi.astype(F32)
    mid = r1.astype(BF16)
    lo = (r1 - mid.astype(F32)).astype(BF16)
    dot = lambda a: jnp.dot(a, e_bf16, preferred_element_type=F32)
    return dot(hi) + dot(mid) + dot(lo)


def _select_sum(m01_bf16, x):
    hi = x.astype(BF16)
    r1 = x - hi.astype(F32)
    mid = r1.astype(BF16)
    lo = (r1 - mid.astype(F32)).astype(BF16)
    dot = lambda a: jnp.dot(m01_bf16, a, preferred_element_type=F32)
    return dot(hi) + dot(mid) + dot(lo)


def _mm_kernel(x_ref, w_ref, o_ref):
    o_ref[...] = jnp.dot(x_ref[...].astype(BF16), w_ref[...].astype(BF16), preferred_element_type=F32)


def _matmul(x, w, tm_cap=1024):
    m, k = x.shape
    n = w.shape[1]
    n_pad = -(-n // V7X_LANES) * V7X_LANES
    if n_pad != n:
        w = jnp.pad(w, ((0, 0), (0, n_pad - n)))
    tm = _row_tile(m, tm_cap)
    tn = next(t for t in (1024, 512, V7X_MXU, V7X_LANES) if n_pad % t == 0)
    return pl.pallas_call(
        _mm_kernel,
        grid=(m // tm, n_pad // tn),
        in_specs=[pl.BlockSpec((tm, k), lambda i, j: (i, 0)),
                  pl.BlockSpec((k, tn), lambda i, j: (0, j))],
        out_specs=pl.BlockSpec((tm, tn), lambda i, j: (i, j)),
        out_shape=jax.ShapeDtypeStruct((m, n_pad), F32),
        compiler_params=_params("parallel", "parallel"),
        name="dense_matmul",
    )(x, w)


def _proj_t_kernel(w_ref, x_ref, o_ref):
    o_ref[0] = lax.dot_general(w_ref[...], x_ref[...].astype(BF16), NT_DIMS, preferred_element_type=F32)


def _proj_t(x, w_t_bf16, bsz, t_len, tm=512):
    n, k = w_t_bf16.shape
    nt = t_len // tm
    return pl.pallas_call(
        _proj_t_kernel,
        grid=(bsz, nt),
        in_specs=[pl.BlockSpec((n, k), lambda b, t: (0, 0)), pl.BlockSpec((tm, k), lambda b, t: (b * nt + t, 0))],
        out_specs=pl.BlockSpec((1, n, tm), lambda b, t: (b, 0, t)),
        out_shape=jax.ShapeDtypeStruct((bsz, n, t_len), F32),
        compiler_params=_params("parallel", "parallel"),
        name="proj_transposed",
    )(w_t_bf16, x)


def _post_mixer_kernel(n_parts, x_ref, *refs):
    lhs_refs, w_refs = refs[:n_parts], refs[n_parts:2 * n_parts]
    g_ref, b_ref, rwh_ref, rwl_ref, rb_ref, x1_ref, eid_ref, gate_ref = refs[2 * n_parts:]
    acc = DN_ALPHA * x_ref[...]
    for lhs_ref, w_ref in zip(lhs_refs, w_refs):
        acc = acc + jnp.dot(lhs_ref[...].astype(BF16), w_ref[...], preferred_element_type=F32)
    xc = acc - jnp.mean(acc, axis=-1, keepdims=True)
    var = jnp.mean(xc * xc, axis=-1, keepdims=True)
    x1 = xc * lax.rsqrt(var + LN_EPS) * g_ref[...] + b_ref[...]
    x1_ref[...] = x1
    x_hi = x1.astype(BF16)
    x_lo = (x1 - x_hi.astype(F32)).astype(BF16)
    dot = lambda a, w_ref: jnp.dot(a, w_ref[...], preferred_element_type=F32)
    logits = dot(x_hi, rwh_ref) + (dot(x_lo, rwh_ref) + dot(x_hi, rwl_ref)) + rb_ref[...]
    lane = lax.broadcasted_iota(I32, logits.shape, 1)
    eids = jnp.zeros(logits.shape, I32)
    vals = []
    for k in range(TOP_K):
        top = jnp.max(logits, axis=-1, keepdims=True)
        idx = jnp.min(jnp.where(logits == top, lane, V7X_LANES), axis=-1, keepdims=True)
        vals.append(top)
        eids = jnp.where(lane == k, idx, eids)
        logits = jnp.where(lane == idx, -jnp.inf, logits)
    exps = [jnp.exp(v - vals[0]) for v in vals]
    denom = exps[0] + exps[1] + exps[2] + exps[3]
    gates = jnp.zeros(logits.shape, F32)
    for k in range(TOP_K):
        gates = jnp.where(lane == k, exps[k] / denom, gates)
    eid_ref[...] = eids
    gate_ref[...] = gates


def _post_mixer(x, parts, g, b, rw, rb):
    m, d = x.shape
    tm = _row_tile(m, 640)
    row = lambda i: (i, 0)
    fixed = lambda i: (0, 0)
    rw_hi = rw.astype(BF16)
    rw_lo = (rw - rw_hi.astype(F32)).astype(BF16)
    lhs_specs = [pl.BlockSpec((tm, lhs.shape[1]), row) for lhs, _ in parts]
    w_specs = [pl.BlockSpec(w.shape, fixed) for _, w in parts]
    return pl.pallas_call(
        functools.partial(_post_mixer_kernel, len(parts)),
        grid=(m // tm,),
        in_specs=[pl.BlockSpec((tm, d), row)] + lhs_specs + w_specs + [
                  pl.BlockSpec((1, d), fixed), pl.BlockSpec((1, d), fixed),
                  pl.BlockSpec((d, V7X_LANES), fixed), pl.BlockSpec((d, V7X_LANES), fixed),
                  pl.BlockSpec((1, V7X_LANES), fixed)],
        out_specs=[pl.BlockSpec((tm, d), row), pl.BlockSpec((tm, V7X_LANES), row),
                   pl.BlockSpec((tm, V7X_LANES), row)],
        out_shape=[jax.ShapeDtypeStruct((m, d), F32), jax.ShapeDtypeStruct((m, V7X_LANES), I32),
                   jax.ShapeDtypeStruct((m, V7X_LANES), F32)],
        compiler_params=_params("parallel"),
        name="post_mixer",
    )(x, *[lhs for lhs, _ in parts], *[w for _, w in parts], g, b, rw_hi, rw_lo, rb)


def _rank_kernel(eid_ref, rank_ref, cnt_ref, carry_ref):
    i = pl.program_id(0)

    @pl.when(i == 0)
    def _():
        carry_ref[...] = jnp.zeros_like(carry_ref)

    eids = eid_ref[...]
    tm = eids.shape[0]
    lane = lax.broadcasted_iota(I32, eids.shape, 1)
    sel = [jnp.sum(jnp.where(lane == k, eids, 0), axis=-1, keepdims=True) for k in range(TOP_K)]
    onehot = jnp.zeros(eids.shape, F32)
    for k in range(TOP_K):
        onehot = onehot + (lane == sel[k]).astype(F32)
    r, c = _iota2((tm, tm))
    before = (c < r).astype(BF16)
    prior = jnp.dot(before, onehot.astype(BF16), preferred_element_type=F32) + carry_ref[...]
    ranks = jnp.zeros(eids.shape, F32)
    for k in range(TOP_K):
        rk = jnp.sum(jnp.where(lane == sel[k], prior, 0.0), axis=-1, keepdims=True)
        ranks = jnp.where(lane == k, rk, ranks)
    rank_ref[...] = ranks.astype(I32)
    total = carry_ref[...] + jnp.sum(onehot, axis=0, keepdims=True)
    carry_ref[...] = total
    cnt_ref[...] = total.astype(I32)


def _route_ranks(eids):
    m = eids.shape[0]
    tm = _row_tile(m, 640)
    return pl.pallas_call(
        _rank_kernel,
        grid=(m // tm,),
        in_specs=[pl.BlockSpec((tm, V7X_LANES), lambda i: (i, 0))],
        out_specs=[pl.BlockSpec((tm, V7X_LANES), lambda i: (i, 0)), pl.BlockSpec((1, V7X_LANES), lambda i: (0, 0))],
        out_shape=[jax.ShapeDtypeStruct((m, V7X_LANES), I32), jax.ShapeDtypeStruct((1, V7X_LANES), I32)],
        scratch_shapes=[pltpu.VMEM((1, V7X_LANES), F32)],
        compiler_params=_params("arbitrary"),
        name="route_ranks",
    )(eids)


def _row_copy(src_ref, s, dst_ref, d, sem):
    return pltpu.make_async_copy(src_ref.at[pl.ds(s, 1)], dst_ref.at[pl.ds(d, 1)], sem)


def _dispatch_kernel(pos_ref, x_ref, out_ref, sem):
    i = pl.program_id(0)
    tm = x_ref.shape[0]
    base = i * tm * TOP_K

    def start(t, carry):
        for k in range(TOP_K):
            _row_copy(x_ref, t, out_ref, pos_ref[base + t * TOP_K + k], sem).start(priority=k % 2)
        return carry

    def wait(t, carry):
        for k in range(TOP_K):
            _row_copy(x_ref, t, out_ref, pos_ref[base + t * TOP_K + k], sem).wait()
        return carry

    lax.fori_loop(0, tm, start, 0, unroll=DMA_UNROLL)
    lax.fori_loop(0, tm, wait, 0, unroll=DMA_UNROLL)


def _dispatch_rows(x1, pos_flat, n_rows):
    m, d = x1.shape
    tm = _row_tile(m, 1024)
    return pl.pallas_call(
        _dispatch_kernel,
        grid_spec=pltpu.PrefetchScalarGridSpec(
            num_scalar_prefetch=1,
            grid=(m // tm,),
            in_specs=[pl.BlockSpec((tm, d), lambda i, pos: (i, 0))],
            out_specs=pl.BlockSpec(memory_space=pl.ANY),
            scratch_shapes=[pltpu.SemaphoreType.DMA(())],
        ),
        out_shape=jax.ShapeDtypeStruct((n_rows, d), x1.dtype),
        compiler_params=_params("arbitrary"),
        name="moe_dispatch",
    )(pos_flat, x1)


def _expert_weight_prep(w1_ref, w2_ref, w1g_ref, w1u_ref, w2b_ref):
    r, c = _iota2((V7X_MXU, V7X_MXU))
    half = V7X_MXU // 2
    src = jnp.where(c < half, 2 * c, 2 * (c - half) + 1)
    perm = jnp.where(r == src, 1.0, 0.0).astype(BF16)
    for j in range(w1_ref.shape[3] // V7X_MXU):
        blk = w1_ref[0, 0, :, j * V7X_MXU:(j + 1) * V7X_MXU].astype(BF16)
        res = jnp.dot(blk, perm, preferred_element_type=F32).astype(BF16)
        w1g_ref[:, j * half:(j + 1) * half] = res[:, :half]
        w1u_ref[:, j * half:(j + 1) * half] = res[:, half:]
    w2b_ref[...] = w2_ref[0, 0].astype(BF16)


def _ffn_kernel(te_ref, tf_ref, tr_ref, nv_ref, x_ref, w1_ref, b1g_ref, b1u_ref, w2_ref, b2_ref, o_ref,
                w1g_ref, w1u_ref, w2b_ref):
    del tf_ref
    i = pl.program_id(0)
    live = i < nv_ref[0]

    @pl.when(live & ((i == 0) | (te_ref[i] != te_ref[jnp.maximum(i - 1, 0)])))
    def _():
        _expert_weight_prep(w1_ref, w2_ref, w1g_ref, w1u_ref, w2b_ref)

    @pl.when(live)
    def _():
        row = lax.broadcasted_iota(I32, x_ref.shape, 0)
        x = jnp.where(row < tr_ref[i], x_ref[...], 0.0).astype(BF16)
        hg = jnp.dot(x, w1g_ref[...], preferred_element_type=F32) + b1g_ref[0]
        hu = jnp.dot(x, w1u_ref[...], preferred_element_type=F32) + b1u_ref[0]
        gate = jnp.minimum(hg, SWIGLU_LIMIT)
        up = jnp.clip(hu, -SWIGLU_LIMIT, SWIGLU_LIMIT)
        act = (up + 1.0) * gate * jax.nn.sigmoid(SWIGLU_ALPHA * gate)
        o_ref[...] = jnp.dot(act.astype(BF16), w2b_ref[...], preferred_element_type=F32) + b2_ref[0]


def _expert_ffn(x_rows, tile_expert, tile_fetch, tile_rows, n_valid, w1_all, b1g, b1u, w2_all, layer, b2):
    rows, d = x_rows.shape
    tm = EXPERT_TILE
    rmap = lambda i, te, tf, tr, nv: (jnp.minimum(i, nv[0] - 1), 0)
    wmap = lambda i, te, tf, tr, nv: (te[i], 0, 0)
    lmap = lambda i, te, tf, tr, nv: (layer, tf[i], 0, 0)
    return pl.pallas_call(
        _ffn_kernel,
        grid_spec=pltpu.PrefetchScalarGridSpec(
            num_scalar_prefetch=4,
            grid=(rows // tm,),
            in_specs=[pl.BlockSpec((tm, d), rmap),
                      pl.BlockSpec((1, 1, d, 2 * D_FF), lmap),
                      pl.BlockSpec((1, 1, D_FF), wmap), pl.BlockSpec((1, 1, D_FF), wmap),
                      pl.BlockSpec((1, 1, D_FF, d), lmap),
                      pl.BlockSpec((1, 1, d), wmap)],
            out_specs=pl.BlockSpec((tm, d), rmap),
            scratch_shapes=[pltpu.VMEM((d, D_FF), BF16), pltpu.VMEM((d, D_FF), BF16), pltpu.VMEM((D_FF, d), BF16)],
        ),
        out_shape=jax.ShapeDtypeStruct((rows, d), F32),
        compiler_params=_params("arbitrary"),
        name="moe_expert_ffn",
    )(tile_expert, tile_fetch, tile_rows, n_valid, x_rows, w1_all, b1g, b1u, w2_all, b2)


def _combine_kernel(pos_ref, y_ref, gate_ref, x_ref, g_ref, b_ref, o_ref, *rest):
    ob_ref = rest[0] if len(rest) == 3 else None
    buf_ref, sem = rest[-2:]
    i = pl.program_id(0)
    tm = x_ref.shape[0]
    base = i * tm * TOP_K

    def start(t, carry):
        for k in range(TOP_K):
            _row_copy(y_ref, pos_ref[base + t * TOP_K + k], buf_ref.at[k], t, sem).start(priority=k % 2)
        return carry

    def wait(t, carry):
        for k in range(TOP_K):
            _row_copy(y_ref, pos_ref[base + t * TOP_K + k], buf_ref.at[k], t, sem).wait()
        return carry

    lax.fori_loop(0, tm, start, 0, unroll=DMA_UNROLL)
    lax.fori_loop(0, tm, wait, 0, unroll=DMA_UNROLL)
    gates = gate_ref[...]
    acc = DN_ALPHA * x_ref[...]
    for k in range(TOP_K):
        acc = acc + gates[:, k:k + 1] * buf_ref[k]
    xc = acc - jnp.mean(acc, axis=-1, keepdims=True)
    var = jnp.mean(xc * xc, axis=-1, keepdims=True)
    out = xc * lax.rsqrt(var + LN_EPS) * g_ref[...] + b_ref[...]
    o_ref[...] = out
    if ob_ref is not None:
        ob_ref[...] = out.astype(BF16)


def _combine_ln(y_rows, pos_flat, gates, x1, g, b, emit_bf16):
    m, d = x1.shape
    tm = _row_tile(m, 320)
    row = lambda i, pos: (i, 0)
    fixed = lambda i, pos: (0, 0)
    return pl.pallas_call(
        _combine_kernel,
        grid_spec=pltpu.PrefetchScalarGridSpec(
            num_scalar_prefetch=1,
            grid=(m // tm,),
            in_specs=[pl.BlockSpec(memory_space=pl.ANY), pl.BlockSpec((tm, V7X_LANES), row),
                      pl.BlockSpec((tm, d), row), pl.BlockSpec((1, d), fixed), pl.BlockSpec((1, d), fixed)],
            out_specs=[pl.BlockSpec((tm, d), row)] * (2 if emit_bf16 else 1),
            scratch_shapes=[pltpu.VMEM((TOP_K, tm, d), F32), pltpu.SemaphoreType.DMA(())],
        ),
        out_shape=[jax.ShapeDtypeStruct((m, d), F32)] + ([jax.ShapeDtypeStruct((m, d), BF16)] if emit_bf16 else []),
        compiler_params=_params("arbitrary"),
        name="moe_combine_ln",
    )(pos_flat, y_rows, gates, x1, g, b)


def _moe_ln(x1, eids, gates, w1_all, b1g, b1u, w2_all, layer, b2, g, b, emit_bf16):
    m, d = x1.shape
    ranks, counts = _route_ranks(eids)
    counts = counts[0, :N_EXPERTS]
    padded = (counts + EXPERT_TILE - 1) // EXPERT_TILE * EXPERT_TILE
    ends = jnp.cumsum(padded)
    gstart = ends - padded
    n_tiles = -(-m * TOP_K // EXPERT_TILE) + N_EXPERTS
    tile_start = jnp.arange(n_tiles, dtype=I32) * EXPERT_TILE
    tile_expert = jnp.minimum(jnp.sum((tile_start[:, None] >= ends[None, :]).astype(I32), axis=1), N_EXPERTS - 1)
    onehot_t = tile_expert[:, None] == jnp.arange(N_EXPERTS, dtype=I32)[None, :]
    used = jnp.sum(jnp.where(onehot_t, (gstart + counts)[None, :], 0), axis=1)
    tile_rows = jnp.clip(used - tile_start, 0, EXPERT_TILE).astype(I32)
    n_valid = (ends[-1:] // EXPERT_TILE).astype(I32)
    first = jnp.concatenate([jnp.ones((1,), bool), tile_expert[1:] != tile_expert[:-1]])
    group_end = jnp.sum(jnp.where(onehot_t, ends[None, :], 0), axis=1) // EXPERT_TILE
    next_expert = jnp.sum(jnp.where(jnp.minimum(group_end, n_tiles - 1)[:, None] == jnp.arange(n_tiles)[None, :],
                                    tile_expert[None, :], 0), axis=1)
    tile_fetch = jnp.where(first, tile_expert, next_expert).astype(I32)
    sel = eids[:, :TOP_K, None] == jnp.arange(N_EXPERTS, dtype=I32)[None, None, :]
    pos_flat = (jnp.sum(jnp.where(sel, gstart[None, None, :], 0), axis=-1) + ranks[:, :TOP_K]).reshape(-1).astype(I32)
    x_rows = _dispatch_rows(x1, pos_flat, n_tiles * EXPERT_TILE)
    y_rows = _expert_ffn(x_rows, tile_expert, tile_fetch, tile_rows, n_valid, w1_all, b1g, b1u, w2_all, layer, b2)
    return _combine_ln(y_rows, pos_flat, gates, x1, g, b, emit_bf16)


def _fox_prep_kernel(t_ref, bias_ref, lf_ref, ccol_ref):
    t_len = t_ref.shape[0]
    r, c = _iota2((V7X_LANES, V7X_LANES))
    tril = jnp.where(c <= r, 1.0, 0.0)
    carry = jnp.zeros((1, V7X_LANES), F32)
    for blk in range(t_len // V7X_LANES):
        rows = slice(blk * V7X_LANES, (blk + 1) * V7X_LANES)
        lf = _log_sigmoid(t_ref[rows, :] + bias_ref[...])
        lf_ref[rows, :] = lf
        cs = jnp.dot(tril, lf, precision=HIGHEST, preferred_element_type=F32) + carry
        ccol_ref[rows, :] = cs
        carry = cs[V7X_LANES - 1:, :]


def _fox_prep(tail, bias_row, bsz, t_len):
    n_p = bsz * t_len
    blk = pl.BlockSpec((t_len, V7X_LANES), lambda b: (b, 0))
    out = jax.ShapeDtypeStruct((n_p, V7X_LANES), F32)
    return pl.pallas_call(
        _fox_prep_kernel,
        grid=(bsz,),
        in_specs=[blk, pl.BlockSpec((1, V7X_LANES), lambda b: (0, 0))],
        out_specs=[blk, blk],
        out_shape=[out, out],
        compiler_params=_params("parallel"),
        name="fox_prep",
    )(tail, bias_row)


def _fox_attn_kernel(q_ref, k_ref, v_ref, ccol_ref, o_ref, kb_ref, vt_ref, cb0_ref, cb1_ref, m0_ref, m1_ref, l0_ref, l1_ref,
                     acc0_ref, acc1_ref, s0_ref, s1_ref):
    pair = pl.program_id(1)
    tq = ATTN_BLOCK
    t_len = k_ref.shape[0]
    cb_refs, m_refs, l_refs, acc_refs = (cb0_ref, cb1_ref), (m0_ref, m1_ref), (l0_ref, l1_ref), (acc0_ref, acc1_ref)
    s_refs = (s0_ref, s1_ref)

    kb_ref[...] = k_ref[...].astype(BF16)
    sr, sc = _iota2((V7X_LANES, V7X_LANES))
    for j in range(2):
        sel = jnp.where(sr == 2 * pair + j, 1.0, 0.0).astype(BF16)
        cb_refs[j][...] = _expand_heads(ccol_ref[...], sel)
    vt_ref[...] = v_ref[0].astype(BF16)
    feat = lax.broadcasted_iota(I32, (V7X_LANES, tq), 0)
    key_id, qry_id = _iota2((tq, tq))

    def scores(kb, qh_j, j):
        rows = slice(kb * tq, (kb + 1) * tq)
        return (jnp.dot(kb_ref[rows, :], qh_j, preferred_element_type=F32)
                - jnp.concatenate([cb_refs[j][rows, :]] * (tq // V7X_LANES), axis=1))

    def update(kb, j, s):
        m_old = m_refs[j][...]
        m_new = jnp.maximum(m_old, jnp.max(s, axis=0, keepdims=True))
        alpha = jnp.exp(m_old - m_new)
        pe = jnp.exp(s - m_new)
        l_refs[j][...] = alpha * l_refs[j][...] + jnp.sum(pe, axis=0, keepdims=True)
        acc_refs[j][...] = alpha * acc_refs[j][...] + jnp.dot(vt_ref[:, kb * tq:(kb + 1) * tq], pe.astype(BF16),
                                                              preferred_element_type=F32)
        m_refs[j][...] = m_new

    for qi in range(t_len // tq):
        q_rows = slice(qi * tq, (qi + 1) * tq)
        qt = (q_ref[q_rows, :] * (DH_B ** -0.5)).T
        qh = [jnp.where((feat // DH_B) == j, qt, 0.0).astype(BF16) for j in range(2)]
        for j in range(2):
            m_refs[j][...] = jnp.full(m_refs[j].shape, -jnp.inf, F32)
            l_refs[j][...] = jnp.zeros_like(l_refs[j])
            acc_refs[j][...] = jnp.zeros_like(acc_refs[j])
            s_refs[j][...] = scores(0, qh[j], j)
        for kb in range(qi):
            for j in range(2):
                s = s_refs[j][...]
                s_refs[j][...] = scores(kb + 1, qh[j], j)
                update(kb, j, s)
        for j in range(2):
            update(qi, j, jnp.where(key_id <= qry_id, s_refs[j][...], -jnp.inf))
        out_t = jnp.where((feat // DH_B) == 0, acc0_ref[...] / l0_ref[...], acc1_ref[...] / l1_ref[...])
        o_ref[q_rows, :] = out_t.T.astype(o_ref.dtype)


def _fox_attention(h_main, kv_t, ccol, bsz, t_len):
    tq = ATTN_BLOCK
    qcol, kcol = (4 * D_A) // V7X_LANES, (4 * D_A + D_B) // V7X_LANES
    cbs, row, acc = pltpu.VMEM((t_len, V7X_LANES), F32), pltpu.VMEM((1, tq), F32), pltpu.VMEM((V7X_LANES, tq), F32)
    seq = lambda col: pl.BlockSpec((t_len, V7X_LANES), lambda b, p: (b, col + p))
    return pl.pallas_call(
        _fox_attn_kernel,
        grid=(bsz, H_B // 2),
        in_specs=[seq(qcol), seq(kcol),
                  pl.BlockSpec((1, V7X_LANES, t_len), lambda b, p: (b, D_B // V7X_LANES + p, 0)),
                  pl.BlockSpec((t_len, V7X_LANES), lambda b, p: (b, 0))],
        out_specs=pl.BlockSpec((t_len, V7X_LANES), lambda b, p: (b, p)),
        out_shape=jax.ShapeDtypeStruct((h_main.shape[0], D_B), BF16),
        scratch_shapes=[pltpu.VMEM((t_len, V7X_LANES), BF16), pltpu.VMEM((V7X_LANES, t_len), BF16),
                        cbs, cbs, row, row, row, row, acc, acc, pltpu.VMEM((tq, tq), F32), pltpu.VMEM((tq, tq), F32)],
        compiler_params=_params("parallel", "parallel"),
        name="fox_attention",
    )(h_main, h_main, kv_t, ccol)


def _fox_decode_kernel(pt_ref, q_ref, kn_ref, vn_ref, t_ref, bias_ref, *rest):
    np_ = DECODE_PAGES
    k_refs, v_refs, lf_refs = rest[:np_], rest[np_:2 * np_], rest[2 * np_:3 * np_]
    all_ref, o_ref, lfo_ref, m_ref, l_ref, acc_ref, carry_ref, rows_ref = rest[3 * np_:]
    del all_ref
    b = pl.program_id(0)
    j = pl.program_id(1)
    q = q_ref[0] * (DH_B ** -0.5)
    hrow, hlane = _iota2((H_B, D_B))
    own = hlane // DH_B == hrow
    qmat = jnp.where(own, jnp.broadcast_to(q, (H_B, D_B)), 0.0)
    lf_new = _log_sigmoid(t_ref[0] + bias_ref[...])

    @pl.when(j == 0)
    def _():
        m_ref[...] = jnp.sum(qmat * kn_ref[0], axis=-1, keepdims=True)
        l_ref[...] = jnp.ones_like(l_ref)
        acc_ref[...] = jnp.broadcast_to(vn_ref[0], acc_ref.shape)
        carry_ref[...] = jnp.zeros_like(carry_ref)
        lfo_ref[0] = lf_new

    r8, c8 = _iota2((H_B, H_B))
    cn = jnp.sum(jnp.where(r8 == c8, jnp.broadcast_to(lf_new[:, :H_B], (H_B, H_B)), 0.0), axis=-1, keepdims=True)
    kr, kc = _iota2((PAGE_SIZE, PAGE_SIZE))
    later = jnp.where(kr > kc, 1.0, 0.0).astype(BF16)
    qb = qmat.astype(BF16)
    lfts = [lf_refs[i][0] for i in range(np_)]
    carries = [None] * np_
    run = carry_ref[...]
    for i in reversed(range(np_)):
        carries[i] = run
        run = run + jnp.sum(lfts[i], axis=-1, keepdims=True)
    carry_ref[...] = run
    scores = []
    for i in range(np_):
        lf = lfts[i]
        hi = lf.astype(BF16)
        r1 = lf - hi.astype(F32)
        mid = r1.astype(BF16)
        lo = (r1 - mid.astype(F32)).astype(BF16)
        parts = jnp.dot(jnp.concatenate([hi, mid, lo], axis=0), later, preferred_element_type=F32)
        suffix = parts[:H_B] + parts[H_B:2 * H_B] + parts[2 * H_B:]
        k2 = k_refs[i][0].reshape(D_B, PAGE_SIZE).astype(BF16)
        scores.append(jnp.dot(qb, k2, preferred_element_type=F32) + (suffix + (cn + carries[i])))
    m_old = m_ref[...]
    m_new = m_old
    for s in scores:
        m_new = jnp.maximum(m_new, jnp.max(s, axis=-1, keepdims=True))
    alpha = jnp.exp(m_old - m_new)
    l_new = alpha * l_ref[...]
    acc = alpha * acc_ref[...]
    for i, s in enumerate(scores):
        pe = jnp.exp(s - m_new)
        l_new = l_new + jnp.sum(pe, axis=-1, keepdims=True)
        v2 = v_refs[i][0].reshape(D_B, PAGE_SIZE).astype(BF16)
        acc = acc + lax.dot_general(pe.astype(BF16), v2, NT_DIMS, preferred_element_type=F32)
    m_ref[...] = m_new
    l_ref[...] = l_new
    acc_ref[...] = acc

    @pl.when(j == pl.num_programs(1) - 1)
    def _():
        o = jnp.where(own, acc / l_new, 0.0)
        rows_ref[pl.ds(b, 1), :] = jnp.sum(o, axis=0, keepdims=True)

    @pl.when((j == pl.num_programs(1) - 1) & (b == pl.num_programs(0) - 1))
    def _():
        o_ref[...] = rows_ref[...].astype(o_ref.dtype)


def _fox_decode(hs_main, hs_tail, bias_row, cache_kt, cache_vt, logf_t, page_table, ob_all):
    bd = hs_main.shape[0]
    n_p = ob_all.shape[0] - bd
    n_pages = page_table.shape[1]
    steps = n_pages // DECODE_PAGES
    qcol, kcol, vcol = (4 * D_A) // D_B, (4 * D_A + D_B) // D_B, (4 * D_A + 2 * D_B) // D_B

    def page(i, nd):
        return lambda b, j, pt: (pt[b * n_pages + (steps - 1 - j) * DECODE_PAGES + i],) + (0,) * nd

    tok = lambda col: pl.BlockSpec((1, 1, D_B), lambda b, j, pt: (b, 0, col))
    in_specs = [tok(qcol), tok(kcol), tok(vcol),
                pl.BlockSpec((1, 1, V7X_LANES), lambda b, j, pt: (b, 0, 0)),
                pl.BlockSpec((1, V7X_LANES), lambda b, j, pt: (0, 0))]
    in_specs += [pl.BlockSpec((1, H_B, DH_B, PAGE_SIZE), page(i, 3)) for i in range(DECODE_PAGES)]
    in_specs += [pl.BlockSpec((1, H_B, DH_B, PAGE_SIZE), page(i, 3)) for i in range(DECODE_PAGES)]
    in_specs += [pl.BlockSpec((1, H_B, PAGE_SIZE), page(i, 2)) for i in range(DECODE_PAGES)]
    in_specs += [pl.BlockSpec(memory_space=pl.ANY)]
    n_in = len(in_specs) + 1
    return pl.pallas_call(
        _fox_decode_kernel,
        grid_spec=pltpu.PrefetchScalarGridSpec(
            num_scalar_prefetch=1,
            grid=(bd, steps),
            in_specs=in_specs,
            out_specs=[_token_rows_spec(n_p, bd, D_B),
                       pl.BlockSpec((1, 1, V7X_LANES), lambda b, j, pt: (b, 0, 0))],
            scratch_shapes=[pltpu.VMEM((H_B, 1), F32), pltpu.VMEM((H_B, 1), F32), pltpu.VMEM((H_B, D_B), F32),
                            pltpu.VMEM((H_B, 1), F32), pltpu.VMEM((bd, D_B), F32)],
        ),
        out_shape=[jax.ShapeDtypeStruct(ob_all.shape, BF16), jax.ShapeDtypeStruct((bd, 1, V7X_LANES), F32)],
        input_output_aliases={n_in - 1: 0},
        compiler_params=_params("arbitrary", "arbitrary"),
        name="fox_decode",
    )(page_table.reshape(-1), hs_main, hs_main, hs_main, hs_tail, bias_row,
      *([cache_kt] * DECODE_PAGES), *([cache_vt] * DECODE_PAGES), *([logf_t] * DECODE_PAGES), ob_all)


def _hgrn_gates(q, z, lb):
    qa = _silu(q)
    logf = jnp.log(lb + (1.0 - lb) * jax.nn.sigmoid(z))
    ka = (1.0 - lb) * jax.nn.sigmoid(-z)
    return qa, ka, logf


def _hgrn_out(o, g, nw):
    o = o * lax.rsqrt(jnp.mean(o * o, axis=-1, keepdims=True) + RMS_EPS)
    return o * nw * _silu(g)


def _hgrn_kernel(q_ref, f_ref, i_ref, g_ref, lb_ref, nw_ref, o_ref, s_ref, qd_ref, dec_ref, oc_ref, u_ref):
    t_len = q_ref.shape[0]
    cs = CHUNK_A
    per_blk = V7X_LANES // cs
    n_blk = t_len // V7X_LANES
    lb = lb_ref[0]
    r, c = _iota2((V7X_LANES, V7X_LANES))
    same = (r // cs) == (c // cs)
    causal = same & (c <= r)
    sums = jnp.concatenate([jnp.where(causal, 1.0, 0.0), jnp.where(same, 1.0, 0.0)], axis=0).astype(BF16)

    def intra(blk, carry):
        start = pl.multiple_of(blk * V7X_LANES, V7X_LANES)
        rows = pl.ds(start, V7X_LANES)
        qa, ka, logf = _hgrn_gates(q_ref[rows, :], f_ref[rows, :], lb)
        both = _select_sum(sums, logf)
        b, gtot = both[:V7X_LANES], both[V7X_LANES:]
        qd = (qa * jnp.exp(b)).astype(BF16)
        kd = (ka * jnp.exp(-b)).astype(BF16)
        ke = (ka * jnp.exp(gtot - b)).astype(BF16)
        v = i_ref[rows, :].astype(BF16)
        qd_ref[rows, :] = qd
        sc = jnp.where(causal, lax.dot_general(qd, kd, NT_DIMS, preferred_element_type=F32), 0.0)
        oc_ref[rows, :] = jnp.dot(sc.astype(BF16), v, preferred_element_type=F32)
        dec = jnp.exp(gtot)
        for k in range(per_blk):
            sub = slice(k * cs, (k + 1) * cs)
            ci = blk * per_blk + k
            u_ref[ci] = lax.dot_general(v[sub], ke[sub], TN_DIMS, preferred_element_type=F32)
            dec_ref[pl.ds(ci, 1), :] = dec[k * cs:k * cs + 1, :]
        return carry

    lax.fori_loop(0, n_blk, intra, 0, unroll=4)

    def scan(ci, st):
        u = u_ref[ci]
        u_ref[ci] = st
        return st * dec_ref[pl.ds(ci, 1), :] + u

    st_last = lax.fori_loop(0, t_len // cs, scan, jnp.zeros((DV_A, DK_A), F32), unroll=4)
    s_ref[0, 0] = st_last.T

    def inter(blk, carry):
        start = pl.multiple_of(blk * V7X_LANES, V7X_LANES)
        rows = pl.ds(start, V7X_LANES)
        qd = qd_ref[rows, :]
        parts = [lax.dot_general(qd[k * cs:(k + 1) * cs], u_ref[blk * per_blk + k].astype(BF16), NT_DIMS,
                                 preferred_element_type=F32) for k in range(per_blk)]
        o = oc_ref[rows, :] + jnp.concatenate(parts, axis=0)
        o_ref[rows, :] = _hgrn_out(o, g_ref[rows, :], nw_ref[...]).astype(o_ref.dtype)
        return carry

    lax.fori_loop(0, n_blk, inter, 0, unroll=4)


def _hgrn_prompt(h_main, lb3, nw_row, bsz, t_len):
    n_p = bsz * t_len
    n_chunks = t_len // CHUNK_A
    col = lambda grp: (lambda b, h: (b, grp * H_A + h))
    return pl.pallas_call(
        _hgrn_kernel,
        grid=(bsz, H_A),
        in_specs=[pl.BlockSpec((t_len, DK_A), col(0)), pl.BlockSpec((t_len, DK_A), col(1)),
                  pl.BlockSpec((t_len, DV_A), col(2)), pl.BlockSpec((t_len, DV_A), col(3)),
                  pl.BlockSpec((1, 1, DK_A), lambda b, h: (h, 0, 0)), pl.BlockSpec((1, DV_A), lambda b, h: (0, 0))],
        out_specs=[pl.BlockSpec((t_len, DV_A), lambda b, h: (b, h)),
                   pl.BlockSpec((1, 1, DK_A, DV_A), lambda b, h: (b, h, 0, 0))],
        out_shape=[jax.ShapeDtypeStruct((h_main.shape[0], D_A), BF16),
                   jax.ShapeDtypeStruct((bsz, H_A, DK_A, DV_A), F32)],
        scratch_shapes=[pltpu.VMEM((t_len, DK_A), BF16), pltpu.VMEM((n_chunks, DK_A), F32),
                        pltpu.VMEM((t_len, DV_A), F32), pltpu.VMEM((n_chunks, DV_A, DK_A), F32)],
        compiler_params=_params("parallel", "parallel"),
        name="hgrn_prompt",
    )(h_main, h_main, h_main, h_main, lb3, nw_row)


def _hgrn_step_kernel(h_ref, lb_ref, nw_ref, s0_ref, all_ref, o_ref, s_ref, rows_ref):
    del all_ref
    b = pl.program_id(0)
    outs = []
    for h in range(H_A):
        grp = lambda g: h_ref[0, :, g * D_A + h * DK_A: g * D_A + (h + 1) * DK_A]
        qa, ka, logf = _hgrn_gates(grp(0), grp(1), lb_ref[h])
        v = grp(2)
        s_new = _to_column(jnp.exp(logf)) * s0_ref[0, h] + _to_column(ka) * v
        s_ref[0, h] = s_new
        o = jnp.sum(_to_column(qa) * s_new, axis=0, keepdims=True)
        outs.append(_hgrn_out(o, grp(3), nw_ref[...]))
    rows_ref[pl.ds(b, 1), :] = jnp.concatenate(outs, axis=-1)

    @pl.when(b == pl.num_programs(0) - 1)
    def _():
        o_ref[...] = rows_ref[...].astype(o_ref.dtype)


def _hgrn_step(hs_main, lb3, nw_row, s0, oa_all):
    bd = hs_main.shape[0]
    n_p = oa_all.shape[0] - bd
    return pl.pallas_call(
        _hgrn_step_kernel,
        grid=(bd,),
        in_specs=[pl.BlockSpec((1, 1, 4 * D_A), lambda b: (b, 0, 0)), pl.BlockSpec((H_A, 1, DK_A), lambda b: (0, 0, 0)),
                  pl.BlockSpec((1, DV_A), lambda b: (0, 0)), pl.BlockSpec((1, H_A, DK_A, DV_A), lambda b: (b, 0, 0, 0)),
                  pl.BlockSpec(memory_space=pl.ANY)],
        out_specs=[_token_rows_spec(n_p, bd, D_A),
                   pl.BlockSpec((1, H_A, DK_A, DV_A), lambda b: (b, 0, 0, 0))],
        out_shape=[jax.ShapeDtypeStruct(oa_all.shape, BF16), jax.ShapeDtypeStruct(s0.shape, F32)],
        scratch_shapes=[pltpu.VMEM((bd, D_A), F32)],
        input_output_aliases={4: 0},
        compiler_params=_params("arbitrary"),
        name="hgrn_step",
    )(hs_main, lb3, nw_row, s0, oa_all)


def _conv_silu(cur, prev, w, b):
    row8 = lax.broadcasted_iota(I32, prev.shape, 0)
    acc = b + cur * w[CONV_W - 1:CONV_W, :]
    for s in range(1, CONV_W):
        sh = pltpu.roll(cur, s, 0)
        head = jnp.where(row8 < s, pltpu.roll(prev, s, 0), sh[:V7X_SUBLANES, :])
        shifted = jnp.concatenate([head, sh[V7X_SUBLANES:, :]], axis=0)
        acc = acc + shifted * w[CONV_W - 1 - s:CONV_W - s, :]
    return _silu(acc)


def _gated_group_norm(y, z, nw):
    y = y * _silu(z)
    parts = []
    for g in range(N_GROUPS_C):
        seg = y[:, g * GROUP_W:(g + 1) * GROUP_W]
        parts.append(seg * lax.rsqrt(jnp.mean(seg * seg, axis=-1, keepdims=True) + RMS_EPS))
    return jnp.concatenate(parts, axis=-1) * nw


def _ssd_kernel(z_ref, x_ref, bc_ref, dt_ref, cwx_ref, cbx_ref, cwbc_ref, cbbc_ref, dtb_ref, alog_ref, dsk_ref,
                nw_ref, y_ref, hs_ref, tailx_ref, tailbc_ref, ht_ref, e_ref, yacc_ref, xw_ref):
    tb = pl.program_id(1)
    tt = x_ref.shape[0]
    pair_w = 2 * HEADDIM_C
    heads_per_group = H_C // N_GROUPS_C

    @pl.when(tb == 0)
    def _():
        tailx_ref[...] = jnp.zeros_like(tailx_ref)
        tailbc_ref[...] = jnp.zeros_like(tailbc_ref)
        ht_ref[...] = jnp.zeros_like(ht_ref)
        er, ec = _iota2(e_ref.shape)
        e_ref[...] = jnp.where(ec // HEADDIM_C == er, 1.0, 0.0).astype(BF16)

    x_raw = x_ref[...]
    bc_raw = bc_ref[...]
    xs = _conv_silu(x_raw, tailx_ref[...], cwx_ref[...], cbx_ref[...])
    bcv = _conv_silu(bc_raw, tailbc_ref[...], cwbc_ref[...], cbbc_ref[...])
    tailx_ref[...] = x_raw[tt - V7X_SUBLANES:, :]
    tailbc_ref[...] = bc_raw[tt - V7X_SUBLANES:, :]

    lane = lax.broadcasted_iota(I32, (tt, V7X_LANES), 1)
    dt = jnp.where(lane < H_C, _softplus(dt_ref[...] + dtb_ref[...]), 0.0)
    a = -jnp.exp(alog_ref[...])
    r, c = _iota2((tt, tt))
    causal = c <= r
    cum = jnp.dot(jnp.where(causal, 1.0, 0.0), dt * a, precision=HIGHEST, preferred_element_type=F32)
    cum_t = cum.T
    xdt = xs * _expand_heads(dt, e_ref[...])
    low = (lane % pair_w) < HEADDIM_C

    for g in range(N_GROUPS_C):
        b_g = bcv[:, g * D_STATE_C:(g + 1) * D_STATE_C]
        c_g = bcv[:, D_BC + g * D_STATE_C:D_BC + (g + 1) * D_STATE_C].astype(BF16)
        cb = lax.dot_general(c_g, b_g.astype(BF16), NT_DIMS, preferred_element_type=F32)
        y_inter = jnp.dot(c_g, ht_ref[g].astype(BF16), preferred_element_type=F32)
        decs = []
        for pr in range(heads_per_group // 2):
            slab = slice(g * GROUP_W + pr * pair_w, g * GROUP_W + (pr + 1) * pair_w)
            xdt_slab = xdt[:, slab]
            ys, es, tes = [], [], []
            for j in range(2):
                head = g * heads_per_group + pr * 2 + j
                colb = jnp.broadcast_to(cum[:, head:head + 1], (tt, tt))
                decay = jnp.exp(jnp.where(causal, colb - cum_t[head:head + 1, :], -jnp.inf))
                ys.append(jnp.dot((cb * decay).astype(BF16), xdt_slab.astype(BF16), preferred_element_type=F32))
                es.append(jnp.exp(colb))
                tes.append(jnp.exp(colb[tt - 1:, :] - colb))
            e_pair = jnp.where(low, es[0], es[1])
            yacc_ref[:, slab] = (jnp.where(low, ys[0], ys[1]) + e_pair * y_inter[:, pr * pair_w:(pr + 1) * pair_w]
                                 + dsk_ref[:, slab] * xs[:, slab])
            xw_ref[:, pr * pair_w:(pr + 1) * pair_w] = xdt_slab * jnp.where(low, tes[0], tes[1])
            decs.append(e_pair[tt - 1:, :])
        dec_row = jnp.concatenate(decs, axis=-1)
        ht_ref[g] = ht_ref[g] * dec_row + jnp.dot(b_g.T.astype(BF16), xw_ref[...].astype(BF16),
                                                  preferred_element_type=F32)

    y_ref[...] = _gated_group_norm(yacc_ref[...], z_ref[...], nw_ref[...]).astype(y_ref.dtype)

    @pl.when(tb == pl.num_programs(1) - 1)
    def _():
        for g in range(N_GROUPS_C):
            for q in range(GROUP_W // V7X_LANES):
                rows = slice(g * GROUP_W + q * V7X_LANES, g * GROUP_W + (q + 1) * V7X_LANES)
                hs_ref[0, rows, :] = ht_ref[g][:, q * V7X_LANES:(q + 1) * V7X_LANES].T


def _ssd_prompt(h_main, h_tail, conv_w, conv_b, dtb_row, alog_row, dsk_row, nw_row, bsz, t_len):
    n_p = bsz * t_len
    tt = SSD_BLOCK
    nt = t_len // tt
    rowmap = lambda col: (lambda b, t: (b * nt + t, col))
    fixed = lambda b, t: (0, 0)
    cwx, cwbc = conv_w[:, :D_INNER], conv_w[:, D_INNER:]
    cbx, cbbc = conv_b[None, :D_INNER], conv_b[None, D_INNER:]
    return pl.pallas_call(
        _ssd_kernel,
        grid=(bsz, nt),
        in_specs=[pl.BlockSpec((tt, D_INNER), rowmap(0)), pl.BlockSpec((tt, D_INNER), rowmap(1)),
                  pl.BlockSpec((tt, 2 * D_BC), rowmap(2 * D_INNER // (2 * D_BC))),
                  pl.BlockSpec((tt, V7X_LANES), rowmap(0)),
                  pl.BlockSpec((CONV_W, D_INNER), fixed), pl.BlockSpec((1, D_INNER), fixed),
                  pl.BlockSpec((CONV_W, 2 * D_BC), fixed), pl.BlockSpec((1, 2 * D_BC), fixed),
                  pl.BlockSpec((1, V7X_LANES), fixed), pl.BlockSpec((1, V7X_LANES), fixed),
                  pl.BlockSpec((1, D_INNER), fixed), pl.BlockSpec((1, D_INNER), fixed)],
        out_specs=[pl.BlockSpec((tt, D_INNER), rowmap(0)),
                   pl.BlockSpec((1, D_INNER, D_STATE_C), lambda b, t: (b, 0, 0))],
        out_shape=[jax.ShapeDtypeStruct((h_main.shape[0], D_INNER), BF16),
                   jax.ShapeDtypeStruct((bsz, D_INNER, D_STATE_C), F32)],
        scratch_shapes=[pltpu.VMEM((V7X_SUBLANES, D_INNER), F32), pltpu.VMEM((V7X_SUBLANES, 2 * D_BC), F32),
                        pltpu.VMEM((N_GROUPS_C, D_STATE_C, GROUP_W), F32), pltpu.VMEM((V7X_LANES, D_INNER), BF16),
                        pltpu.VMEM((tt, D_INNER), F32), pltpu.VMEM((tt, GROUP_W), F32)],
        compiler_params=_params("parallel", "arbitrary"),
        name="ssd_prompt",
    )(h_main, h_main, h_main, h_tail, cwx, cbx, cwbc, cbbc, dtb_row, alog_row, dsk_row, nw_row)


def _ssd_step_kernel(h_ref, t_ref, cs_ref, cw_ref, cb_ref, dtb_ref, alog_ref, dsk_ref, nw_ref, h0_ref, all_ref,
                     y_ref, hn_ref, rows_ref):
    del all_ref
    b = pl.program_id(0)
    z = h_ref[0, :, :D_INNER]
    xbc_new = h_ref[0, :, D_INNER:]
    cw = cw_ref[...]
    conv = cb_ref[...] + xbc_new * cw[CONV_W - 1:CONV_W, :]
    for j in range(CONV_W - 1):
        conv = conv + cs_ref[0, j:j + 1, :] * cw[j:j + 1, :]
    xbc = _silu(conv)
    xs = xbc[:, :D_INNER]
    lane = lax.broadcasted_iota(I32, (1, V7X_LANES), 1)
    dt = jnp.where(lane < H_C, _softplus(t_ref[0] + dtb_ref[...]), 0.0)
    da = jnp.exp(dt * -jnp.exp(alog_ref[...]))
    er, ec = _iota2((V7X_LANES, D_INNER))
    expand = jnp.where(ec // HEADDIM_C == er, 1.0, 0.0).astype(BF16)
    rows8 = lambda v: jnp.broadcast_to(v, (V7X_SUBLANES, V7X_LANES))
    dt_x = _expand_heads(rows8(dt), expand)[:1, :]
    da_x = _expand_heads(rows8(da), expand)[:1, :]
    xdt = xs * dt_x
    y_parts = []
    for q in range(D_INNER // V7X_LANES):
        g = q // (GROUP_W // V7X_LANES)
        lanes = slice(q * V7X_LANES, (q + 1) * V7X_LANES)
        b_g = xbc[:, D_INNER + g * D_STATE_C:D_INNER + (g + 1) * D_STATE_C]
        c_g = xbc[:, D_INNER + D_BC + g * D_STATE_C:D_INNER + D_BC + (g + 1) * D_STATE_C]
        h_new = _to_column(da_x[:, lanes]) * h0_ref[0, lanes, :] + _to_column(xdt[:, lanes]) * b_g
        hn_ref[0, lanes, :] = h_new
        y_parts.append(_to_row(jnp.sum(h_new * c_g, axis=-1, keepdims=True)))
    y = jnp.concatenate(y_parts, axis=-1) + dsk_ref[...] * xs
    rows_ref[pl.ds(b, 1), :] = _gated_group_norm(y, z, nw_ref[...])

    @pl.when(b == pl.num_programs(0) - 1)
    def _():
        y_ref[...] = rows_ref[...].astype(y_ref.dtype)


def _ssd_step(hs_main, hs_tail, conv_state, conv_w, conv_b, dtb_row, alog_row, dsk_row, nw_row, h0, y_all):
    bd = hs_main.shape[0]
    n_p = y_all.shape[0] - bd
    fixed = lambda b: (0, 0)
    tok = lambda b: (b, 0, 0)
    return pl.pallas_call(
        _ssd_step_kernel,
        grid=(bd,),
        in_specs=[pl.BlockSpec((1, 1, ODD_MAIN), tok), pl.BlockSpec((1, 1, V7X_LANES), tok),
                  pl.BlockSpec((1, CONV_W - 1, CONV_DIM), tok),
                  pl.BlockSpec((CONV_W, CONV_DIM), fixed), pl.BlockSpec((1, CONV_DIM), fixed),
                  pl.BlockSpec((1, V7X_LANES), fixed), pl.BlockSpec((1, V7X_LANES), fixed),
                  pl.BlockSpec((1, D_INNER), fixed), pl.BlockSpec((1, D_INNER), fixed),
                  pl.BlockSpec((1, D_INNER, D_STATE_C), tok), pl.BlockSpec(memory_space=pl.ANY)],
        out_specs=[_token_rows_spec(n_p, bd, D_INNER), pl.BlockSpec((1, D_INNER, D_STATE_C), tok)],
        out_shape=[jax.ShapeDtypeStruct(y_all.shape, BF16), jax.ShapeDtypeStruct((bd, D_INNER, D_STATE_C), F32)],
        scratch_shapes=[pltpu.VMEM((bd, D_INNER), F32)],
        input_output_aliases={10: 0},
        compiler_params=_params("arbitrary"),
        name="ssd_step",
    )(hs_main, hs_tail, conv_state, conv_w, conv_b[None], dtb_row, alog_row, dsk_row, nw_row, h0, y_all)


def _pad_lanes(v):
    return jnp.pad(v, (0, V7X_LANES - v.shape[0]))[None]


def kernel(x_prompt, x_sample, cache_k, cache_v, cache_logf, page_table, state_hgrn, state_ssm, state_conv,
           w_in_even, hgrn_lower_bound, hgrn_norm_w, fox_f_bias, w_out_even,
           w_in_odd, conv_w, conv_b, dt_bias, a_log, d_skip, ssm_norm_w, w_out_odd,
           ln1_g, ln1_b, ln2_g, ln2_b, router_w, router_b, exp_w1, exp_b1, exp_w2, exp_b2):
    lb_all = jnp.cumsum(jax.nn.softmax(hgrn_lower_bound, axis=0), axis=0)
    bp, t_p, d = x_prompt.shape
    bd, t_d, _ = x_sample.shape
    assert t_d == 1
    n_p = bp * t_p
    x_all = jnp.concatenate([x_prompt.reshape(n_p, d), x_sample.reshape(bd, d)], axis=0)
    x_bf = x_all.astype(BF16)
    outs = {}
    for l in range(DEPTH):
        i = l // 2
        if l % 2 == 0:
            h_main = _matmul(x_bf, w_in_even[i][:, :EVEN_MAIN].astype(BF16))
            h_tail = _matmul(x_bf, w_in_even[i][:, EVEN_MAIN:].astype(BF16))
            hs_main, hs_tail = h_main[n_p:, None, :], h_tail[n_p:, None, :]
            lb3 = lb_all[i].reshape(H_A, 1, DK_A)
            nw_row = hgrn_norm_w[i][None]
            bias_row = _pad_lanes(fox_f_bias[i])
            lf_p, ccol = _fox_prep(h_tail, bias_row, bp, t_p)
            kcol, vcol = 4 * D_A + D_B, 4 * D_A + 2 * D_B
            kv_t = _proj_t(x_bf, w_in_even[i][:, kcol:kcol + 2 * D_B].T.astype(BF16), bp, t_p)
            ob_p = _fox_attention(h_main, kv_t, ccol, bp, t_p)
            oa_p, hg_p = _hgrn_prompt(h_main, lb3, nw_row, bp, t_p)
            oa_all, hg_s = _hgrn_step(hs_main, lb3, nw_row, state_hgrn[i], oa_p)
            ob_all, lf_s = _fox_decode(hs_main, hs_tail, bias_row, jnp.transpose(cache_k[i], (0, 2, 3, 1)),
                                       jnp.transpose(cache_v[i], (0, 2, 3, 1)), jnp.swapaxes(cache_logf[i], 1, 2),
                                       page_table, ob_p)
            w_out = w_out_even[i].astype(BF16)
            parts = [(oa_all, w_out[:D_A]), (ob_all, w_out[D_A:])]
            heads_last = lambda a: jnp.transpose(a.reshape(bp, H_B, DH_B, t_p), (0, 3, 1, 2))
            vals = (("kp", heads_last(kv_t[:, :D_B])),
                    ("ks", h_main[n_p:, kcol:kcol + D_B].reshape(bd, t_d, H_B, DH_B)),
                    ("vp", heads_last(kv_t[:, D_B:])),
                    ("vs", h_main[n_p:, vcol:vcol + D_B].reshape(bd, t_d, H_B, DH_B)),
                    ("lfp", lf_p[:, :H_B].reshape(bp, t_p, H_B)), ("lfs", lf_s[:, :, :H_B]),
                    ("hgp", hg_p), ("hgs", hg_s))
        else:
            h_main = _matmul(x_bf, w_in_odd[i][:, :ODD_MAIN].astype(BF16))
            h_tail = _matmul(x_bf, w_in_odd[i][:, ODD_MAIN:].astype(BF16))
            hs_main, hs_tail = h_main[n_p:, None, :], h_tail[n_p:, None, :]
            dtb_row, alog_row = _pad_lanes(dt_bias[i]), _pad_lanes(a_log[i])
            dsk_row = jnp.repeat(d_skip[i], HEADDIM_C)[None]
            nw_row = ssm_norm_w[i][None]
            y_p, ss_p = _ssd_prompt(h_main, h_tail, conv_w[i], conv_b[i], dtb_row, alog_row, dsk_row, nw_row, bp, t_p)
            y_all, ss_s = _ssd_step(hs_main, hs_tail, state_conv[i], conv_w[i], conv_b[i], dtb_row, alog_row, dsk_row,
                                    nw_row, state_ssm[i].reshape(bd, D_INNER, D_STATE_C), y_p)
            parts = [(y_all, w_out_odd[i].astype(BF16))]
            tail_rows = jnp.stack([h_main[(b + 1) * t_p - (CONV_W - 1):(b + 1) * t_p, D_INNER:] for b in range(bp)])
            vals = (("ssp", ss_p.reshape(bp, H_C, HEADDIM_C, D_STATE_C)),
                    ("sss", ss_s.reshape(bd, H_C, HEADDIM_C, D_STATE_C)),
                    ("cvp", tail_rows),
                    ("cvs", jnp.concatenate([state_conv[i][:, 1:], hs_main[:, :, D_INNER:]], axis=1)))
        for name, val in vals:
            outs.setdefault(name, []).append(val)
        rw = jnp.pad(router_w[l], ((0, 0), (0, V7X_LANES - N_EXPERTS)))
        rb = jnp.concatenate([router_b[l], jnp.full((V7X_LANES - N_EXPERTS,), -jnp.inf, F32)])[None]
        x1, eids, gates = _post_mixer(x_all, parts, ln1_g[l][None], ln1_b[l][None], rw, rb)
        b1g = exp_b1[l][:, None, 0::2]
        b1u = exp_b1[l][:, None, 1::2]
        more = l + 1 < DEPTH
        res = _moe_ln(x1, eids, gates, exp_w1, b1g, b1u, exp_w2, l, exp_b2[l][:, None, :],
                      ln2_g[l][None], ln2_b[l][None], more)
        x_all = res[0]
        x_bf = res[1] if more else None
    st = {k: jnp.stack(v) for k, v in outs.items()}
    return (x_all[:n_p].reshape(bp, t_p, d), x_all[n_p:].reshape(bd, t_d, d),
            st["kp"], st["ks"], st["vp"], st["vs"], st["lfp"], st["lfs"],
            st["hgp"], st["hgs"], st["ssp"], st["sss"], st["cvp"], st["cvs"])
```

```python
import functools

import jax
import jax.numpy as jnp
import numpy as np
from jax import lax
from jax.experimental import pallas as pl
from jax.experimental.pallas import tpu as pltpu

F32 = jnp.float32
BF16 = jnp.bfloat16
I32 = jnp.int32
HIGHEST = lax.Precision.HIGHEST

D_MODEL = 1024
DEPTH = 2
PAGE_SIZE = 128
H_A, DK_A, DV_A, CHUNK_A = 4, 128, 128, 32
H_B, DH_B = 8, 64
D_A = H_A * DK_A
D_B = H_B * DH_B
EVEN_MAIN = 4 * D_A + 3 * D_B
D_INNER = 2 * D_MODEL
HEADDIM_C = 64
H_C = D_INNER // HEADDIM_C
N_GROUPS_C = 4
D_STATE_C = 128
D_BC = N_GROUPS_C * D_STATE_C
GROUP_W = D_INNER // N_GROUPS_C
CONV_W = 4
CONV_DIM = D_INNER + 2 * D_BC
ODD_MAIN = D_INNER + CONV_DIM
N_EXPERTS = 32
TOP_K = 4
D_FF = D_MODEL
SWIGLU_LIMIT = 7.0
SWIGLU_ALPHA = 1.702
DN_ALPHA = (2 * DEPTH) ** 0.25
LN_EPS = 1e-5
RMS_EPS = 1e-6

V7X_LANES = 128
V7X_SUBLANES = 8
V7X_MXU = 256
VMEM_LIMIT = 56 * 1024 * 1024
EXPERT_TILE = 512
ATTN_BLOCK = 256
SSD_BLOCK = 128
DECODE_PAGES = 16
DMA_UNROLL = 16
MATMUL_ROW_CAP = 1024
TOKEN_ROW_CAP = 640

NT_DIMS = (((1,), (1,)), ((), ()))
TN_DIMS = (((0,), (0,)), ((), ()))


def _row_tile(m, cap):
    best = 0
    for t in range(16, cap + 1, 16):
        if m % t == 0:
            best = t
    assert best, (m, cap)
    return best


def _token_rows_spec(n_rows, bd, width):
    assert n_rows % bd == 0
    return pl.BlockSpec((bd, width), lambda *_: (n_rows // bd, 0))


def _params(*sem):
    return pltpu.CompilerParams(dimension_semantics=sem, vmem_limit_bytes=VMEM_LIMIT)


def _silu(x):
    return x * jax.nn.sigmoid(x)


def _softplus(x):
    return jnp.maximum(x, 0.0) + jnp.log1p(jnp.exp(-jnp.abs(x)))


def _log_sigmoid(x):
    return -_softplus(-x)


def _iota2(shape):
    return lax.broadcasted_iota(I32, shape, 0), lax.broadcasted_iota(I32, shape, 1)


def _to_column(row_vec):
    n = row_vec.shape[1]
    r, c = _iota2((n, n))
    return jnp.sum(jnp.where(r == c, jnp.broadcast_to(row_vec, (n, n)), 0.0), axis=1, keepdims=True)


def _to_row(col_vec):
    n = col_vec.shape[0]
    r, c = _iota2((n, n))
    return jnp.sum(jnp.where(r == c, jnp.broadcast_to(col_vec, (n, n)), 0.0), axis=0, keepdims=True)


def _expand_heads(v, e_bf16):
    hi = v.astype(BF16)
    r1 = v - hi.astype(F32)
    mid = r1.astype(BF16)
    lo = (r1 - mid.astype(F32)).astype(BF16)
    dot = lambda a: jnp.dot(a, e_bf16, preferred_element_type=F32)
    return dot(hi) + dot(mid) + dot(lo)


def _select_sum(m01_bf16, x):
    hi = x.astype(BF16)
    r1 = x - hi.astype(F32)
    mid = r1.astype(BF16)
    lo = (r1 - mid.astype(F32)).astype(BF16)
    dot = lambda a: jnp.dot(m01_bf16, a, preferred_element_type=F32)
    return dot(hi) + dot(mid) + dot(lo)


def _mm_kernel(x_ref, w_ref, o_ref):
    o_ref[...] = jnp.dot(x_ref[...].astype(BF16), w_ref[...].astype(BF16), preferred_element_type=F32)


def _matmul(x, w, tm_cap=MATMUL_ROW_CAP):
    m, k = x.shape
    n = w.shape[1]
    n_pad = -(-n // V7X_LANES) * V7X_LANES
    if n_pad != n:
        w = jnp.pad(w, ((0, 0), (0, n_pad - n)))
    tm = _row_tile(m, tm_cap)
    tn = next(t for t in (1024, 512, V7X_MXU, V7X_LANES) if n_pad % t == 0)
    return pl.pallas_call(
        _mm_kernel,
        grid=(m // tm, n_pad // tn),
        in_specs=[pl.BlockSpec((tm, k), lambda i, j: (i, 0)),
                  pl.BlockSpec((k, tn), lambda i, j: (0, j))],
        out_specs=pl.BlockSpec((tm, tn), lambda i, j: (i, j)),
        out_shape=jax.ShapeDtypeStruct((m, n_pad), F32),
        compiler_params=_params("parallel", "parallel"),
        name="dense_matmul",
    )(x, w)


def _proj_t_kernel(w_ref, x_ref, o_ref):
    o_ref[0] = lax.dot_general(w_ref[...], x_ref[...].astype(BF16), NT_DIMS, preferred_element_type=F32)


def _proj_t(x, w_t_bf16, bsz, t_len, tm=512):
    n, k = w_t_bf16.shape
    nt = t_len // tm
    return pl.pallas_call(
        _proj_t_kernel,
        grid=(bsz, nt),
        in_specs=[pl.BlockSpec((n, k), lambda b, t: (0, 0)), pl.BlockSpec((tm, k), lambda b, t: (b * nt + t, 0))],
        out_specs=pl.BlockSpec((1, n, tm), lambda b, t: (b, 0, t)),
        out_shape=jax.ShapeDtypeStruct((bsz, n, t_len), F32),
        compiler_params=_params("parallel", "parallel"),
        name="proj_transposed",
    )(w_t_bf16, x)


def _post_mixer_kernel(n_parts, x_ref, *refs):
    lhs_refs, w_refs = refs[:n_parts], refs[n_parts:2 * n_parts]
    g_ref, b_ref, rwh_ref, rwl_ref, rb_ref, x1_ref, eid_ref, gate_ref = refs[2 * n_parts:]
    acc = DN_ALPHA * x_ref[...]
    for lhs_ref, w_ref in zip(lhs_refs, w_refs):
        acc = acc + jnp.dot(lhs_ref[...].astype(BF16), w_ref[...], preferred_element_type=F32)
    xc = acc - jnp.mean(acc, axis=-1, keepdims=True)
    var = jnp.mean(xc * xc, axis=-1, keepdims=True)
    x1 = xc * lax.rsqrt(var + LN_EPS) * g_ref[...] + b_ref[...]
    x1_ref[...] = x1
    x_hi = x1.astype(BF16)
    x_lo = (x1 - x_hi.astype(F32)).astype(BF16)
    dot = lambda a, w_ref: jnp.dot(a, w_ref[...], preferred_element_type=F32)
    logits = dot(x_hi, rwh_ref) + (dot(x_lo, rwh_ref) + dot(x_hi, rwl_ref)) + rb_ref[...]
    lane = lax.broadcasted_iota(I32, logits.shape, 1)
    eids = jnp.zeros(logits.shape, I32)
    vals = []
    for k in range(TOP_K):
        top = jnp.max(logits, axis=-1, keepdims=True)
        idx = jnp.min(jnp.where(logits == top, lane, V7X_LANES), axis=-1, keepdims=True)
        vals.append(top)
        eids = jnp.where(lane == k, idx, eids)
        logits = jnp.where(lane == idx, -jnp.inf, logits)
    exps = [jnp.exp(v - vals[0]) for v in vals]
    denom = exps[0] + exps[1] + exps[2] + exps[3]
    gates = jnp.zeros(logits.shape, F32)
    for k in range(TOP_K):
        gates = jnp.where(lane == k, exps[k] / denom, gates)
    eid_ref[...] = eids
    gate_ref[...] = gates


def _post_mixer(x, parts, g, b, rw, rb):
    m, d = x.shape
    tm = _row_tile(m, TOKEN_ROW_CAP)
    row = lambda i: (i, 0)
    fixed = lambda i: (0, 0)
    rw_hi = rw.astype(BF16)
    rw_lo = (rw - rw_hi.astype(F32)).astype(BF16)
    lhs_specs = [pl.BlockSpec((tm, lhs.shape[1]), row) for lhs, _ in parts]
    w_specs = [pl.BlockSpec(w.shape, fixed) for _, w in parts]
    return pl.pallas_call(
        functools.partial(_post_mixer_kernel, len(parts)),
        grid=(m // tm,),
        in_specs=[pl.BlockSpec((tm, d), row)] + lhs_specs + w_specs + [
                  pl.BlockSpec((1, d), fixed), pl.BlockSpec((1, d), fixed),
                  pl.BlockSpec((d, V7X_LANES), fixed), pl.BlockSpec((d, V7X_LANES), fixed),
                  pl.BlockSpec((1, V7X_LANES), fixed)],
        out_specs=[pl.BlockSpec((tm, d), row), pl.BlockSpec((tm, V7X_LANES), row),
                   pl.BlockSpec((tm, V7X_LANES), row)],
        out_shape=[jax.ShapeDtypeStruct((m, d), F32), jax.ShapeDtypeStruct((m, V7X_LANES), I32),
                   jax.ShapeDtypeStruct((m, V7X_LANES), F32)],
        compiler_params=_params("parallel"),
        name="post_mixer",
    )(x, *[lhs for lhs, _ in parts], *[w for _, w in parts], g, b, rw_hi, rw_lo, rb)


def _rank_kernel(eid_ref, rank_ref, cnt_ref, carry_ref):
    i = pl.program_id(0)

    @pl.when(i == 0)
    def _():
        carry_ref[...] = jnp.zeros_like(carry_ref)

    eids = eid_ref[...]
    tm = eids.shape[0]
    lane = lax.broadcasted_iota(I32, eids.shape, 1)
    sel = [jnp.sum(jnp.where(lane == k, eids, 0), axis=-1, keepdims=True) for k in range(TOP_K)]
    onehot = jnp.zeros(eids.shape, F32)
    for k in range(TOP_K):
        onehot = onehot + (lane == sel[k]).astype(F32)
    r, c = _iota2((tm, tm))
    before = (c < r).astype(BF16)
    prior = jnp.dot(before, onehot.astype(BF16), preferred_element_type=F32) + carry_ref[...]
    ranks = jnp.zeros(eids.shape, F32)
    for k in range(TOP_K):
        rk = jnp.sum(jnp.where(lane == sel[k], prior, 0.0), axis=-1, keepdims=True)
        ranks = jnp.where(lane == k, rk, ranks)
    rank_ref[...] = ranks.astype(I32)
    total = carry_ref[...] + jnp.sum(onehot, axis=0, keepdims=True)
    carry_ref[...] = total
    cnt_ref[...] = total.astype(I32)


def _route_ranks(eids):
    m = eids.shape[0]
    tm = _row_tile(m, TOKEN_ROW_CAP)
    return pl.pallas_call(
        _rank_kernel,
        grid=(m // tm,),
        in_specs=[pl.BlockSpec((tm, V7X_LANES), lambda i: (i, 0))],
        out_specs=[pl.BlockSpec((tm, V7X_LANES), lambda i: (i, 0)), pl.BlockSpec((1, V7X_LANES), lambda i: (0, 0))],
        out_shape=[jax.ShapeDtypeStruct((m, V7X_LANES), I32), jax.ShapeDtypeStruct((1, V7X_LANES), I32)],
        scratch_shapes=[pltpu.VMEM((1, V7X_LANES), F32)],
        compiler_params=_params("arbitrary"),
        name="route_ranks",
    )(eids)


def _row_copy(src_ref, s, dst_ref, d, sem):
    return pltpu.make_async_copy(src_ref.at[pl.ds(s, 1)], dst_ref.at[pl.ds(d, 1)], sem)


def _dispatch_kernel(pos_ref, x_ref, out_ref, sem):
    i = pl.program_id(0)
    tm = x_ref.shape[0]
    base = i * tm * TOP_K

    def start(t, carry):
        for k in range(TOP_K):
            _row_copy(x_ref, t, out_ref, pos_ref[base + t * TOP_K + k], sem).start(priority=k % 2)
        return carry

    def wait(t, carry):
        for k in range(TOP_K):
            _row_copy(x_ref, t, out_ref, pos_ref[base + t * TOP_K + k], sem).wait()
        return carry

    lax.fori_loop(0, tm, start, 0, unroll=DMA_UNROLL)
    lax.fori_loop(0, tm, wait, 0, unroll=DMA_UNROLL)


def _dispatch_rows(x1, pos_flat, n_rows):
    m, d = x1.shape
    tm = _row_tile(m, MATMUL_ROW_CAP)
    return pl.pallas_call(
        _dispatch_kernel,
        grid_spec=pltpu.PrefetchScalarGridSpec(
            num_scalar_prefetch=1,
            grid=(m // tm,),
            in_specs=[pl.BlockSpec((tm, d), lambda i, pos: (i, 0))],
            out_specs=pl.BlockSpec(memory_space=pl.ANY),
            scratch_shapes=[pltpu.SemaphoreType.DMA(())],
        ),
        out_shape=jax.ShapeDtypeStruct((n_rows, d), x1.dtype),
        compiler_params=_params("arbitrary"),
        name="moe_dispatch",
    )(pos_flat, x1)


def _expert_weight_prep(w1_ref, w2_ref, w1g_ref, w1u_ref, w2b_ref):
    r, c = _iota2((V7X_MXU, V7X_MXU))
    half = V7X_MXU // 2
    src = jnp.where(c < half, 2 * c, 2 * (c - half) + 1)
    perm = jnp.where(r == src, 1.0, 0.0).astype(BF16)
    for j in range(w1_ref.shape[3] // V7X_MXU):
        blk = w1_ref[0, 0, :, j * V7X_MXU:(j + 1) * V7X_MXU].astype(BF16)
        res = jnp.dot(blk, perm, preferred_element_type=F32).astype(BF16)
        w1g_ref[:, j * half:(j + 1) * half] = res[:, :half]
        w1u_ref[:, j * half:(j + 1) * half] = res[:, half:]
    w2b_ref[...] = w2_ref[0, 0].astype(BF16)


def _ffn_kernel(te_ref, tf_ref, tr_ref, nv_ref, x_ref, w1_ref, b1g_ref, b1u_ref, w2_ref, b2_ref, o_ref,
                w1g_ref, w1u_ref, w2b_ref):
    del tf_ref
    i = pl.program_id(0)
    live = i < nv_ref[0]

    @pl.when(live & ((i == 0) | (te_ref[i] != te_ref[jnp.maximum(i - 1, 0)])))
    def _():
        _expert_weight_prep(w1_ref, w2_ref, w1g_ref, w1u_ref, w2b_ref)

    @pl.when(live)
    def _():
        row = lax.broadcasted_iota(I32, x_ref.shape, 0)
        x = jnp.where(row < tr_ref[i], x_ref[...], 0.0).astype(BF16)
        hg = jnp.dot(x, w1g_ref[...], preferred_element_type=F32) + b1g_ref[0]
        hu = jnp.dot(x, w1u_ref[...], preferred_element_type=F32) + b1u_ref[0]
        gate = jnp.minimum(hg, SWIGLU_LIMIT)
        up = jnp.clip(hu, -SWIGLU_LIMIT, SWIGLU_LIMIT)
        act = (up + 1.0) * gate * jax.nn.sigmoid(SWIGLU_ALPHA * gate)
        o_ref[...] = jnp.dot(act.astype(BF16), w2b_ref[...], preferred_element_type=F32) + b2_ref[0]


def _expert_ffn(x_rows, tile_expert, tile_fetch, tile_rows, n_valid, w1_all, b1g, b1u, w2_all, layer, b2):
    rows, d = x_rows.shape
    tm = EXPERT_TILE
    rmap = lambda i, te, tf, tr, nv: (jnp.minimum(i, nv[0] - 1), 0)
    wmap = lambda i, te, tf, tr, nv: (te[i], 0, 0)
    lmap = lambda i, te, tf, tr, nv: (layer, tf[i], 0, 0)
    return pl.pallas_call(
        _ffn_kernel,
        grid_spec=pltpu.PrefetchScalarGridSpec(
            num_scalar_prefetch=4,
            grid=(rows // tm,),
            in_specs=[pl.BlockSpec((tm, d), rmap),
                      pl.BlockSpec((1, 1, d, 2 * D_FF), lmap),
                      pl.BlockSpec((1, 1, D_FF), wmap), pl.BlockSpec((1, 1, D_FF), wmap),
                      pl.BlockSpec((1, 1, D_FF, d), lmap),
                      pl.BlockSpec((1, 1, d), wmap)],
            out_specs=pl.BlockSpec((tm, d), rmap),
            scratch_shapes=[pltpu.VMEM((d, D_FF), BF16), pltpu.VMEM((d, D_FF), BF16), pltpu.VMEM((D_FF, d), BF16)],
        ),
        out_shape=jax.ShapeDtypeStruct((rows, d), F32),
        compiler_params=_params("arbitrary"),
        name="moe_expert_ffn",
    )(tile_expert, tile_fetch, tile_rows, n_valid, x_rows, w1_all, b1g, b1u, w2_all, b2)


def _combine_kernel(pos_ref, y_ref, gate_ref, x_ref, g_ref, b_ref, o_ref, *rest):
    ob_ref = rest[0] if len(rest) == 3 else None
    buf_ref, sem = rest[-2:]
    i = pl.program_id(0)
    tm = x_ref.shape[0]
    base = i * tm * TOP_K

    def start(t, carry):
        for k in range(TOP_K):
            _row_copy(y_ref, pos_ref[base + t * TOP_K + k], buf_ref.at[k], t, sem).start(priority=k % 2)
        return carry

    def wait(t, carry):
        for k in range(TOP_K):
            _row_copy(y_ref, pos_ref[base + t * TOP_K + k], buf_ref.at[k], t, sem).wait()
        return carry

    lax.fori_loop(0, tm, start, 0, unroll=DMA_UNROLL)
    lax.fori_loop(0, tm, wait, 0, unroll=DMA_UNROLL)
    gates = gate_ref[...]
    acc = DN_ALPHA * x_ref[...]
    for k in range(TOP_K):
        acc = acc + gates[:, k:k + 1] * buf_ref[k]
    xc = acc - jnp.mean(acc, axis=-1, keepdims=True)
    var = jnp.mean(xc * xc, axis=-1, keepdims=True)
    out = xc * lax.rsqrt(var + LN_EPS) * g_ref[...] + b_ref[...]
    o_ref[...] = out
    if ob_ref is not None:
        ob_ref[...] = out.astype(BF16)


def _combine_ln(y_rows, pos_flat, gates, x1, g, b, emit_bf16):
    m, d = x1.shape
    tm = _row_tile(m, TOKEN_ROW_CAP)
    row = lambda i, pos: (i, 0)
    fixed = lambda i, pos: (0, 0)
    return pl.pallas_call(
        _combine_kernel,
        grid_spec=pltpu.PrefetchScalarGridSpec(
            num_scalar_prefetch=1,
            grid=(m // tm,),
            in_specs=[pl.BlockSpec(memory_space=pl.ANY), pl.BlockSpec((tm, V7X_LANES), row),
                      pl.BlockSpec((tm, d), row), pl.BlockSpec((1, d), fixed), pl.BlockSpec((1, d), fixed)],
            out_specs=[pl.BlockSpec((tm, d), row)] * (2 if emit_bf16 else 1),
            scratch_shapes=[pltpu.VMEM((TOP_K, tm, d), F32), pltpu.SemaphoreType.DMA(())],
        ),
        out_shape=[jax.ShapeDtypeStruct((m, d), F32)] + ([jax.ShapeDtypeStruct((m, d), BF16)] if emit_bf16 else []),
        compiler_params=_params("arbitrary"),
        name="moe_combine_ln",
    )(pos_flat, y_rows, gates, x1, g, b)


def _moe_ln(x1, eids, gates, w1_all, b1g, b1u, w2_all, layer, b2, g, b, emit_bf16):
    m, d = x1.shape
    ranks, counts = _route_ranks(eids)
    counts = counts[0, :N_EXPERTS]
    padded = (counts + EXPERT_TILE - 1) // EXPERT_TILE * EXPERT_TILE
    ends = jnp.cumsum(padded)
    gstart = ends - padded
    n_tiles = -(-m * TOP_K // EXPERT_TILE) + N_EXPERTS
    tile_start = jnp.arange(n_tiles, dtype=I32) * EXPERT_TILE
    tile_expert = jnp.minimum(jnp.sum((tile_start[:, None] >= ends[None, :]).astype(I32), axis=1), N_EXPERTS - 1)
    onehot_t = tile_expert[:, None] == jnp.arange(N_EXPERTS, dtype=I32)[None, :]
    used = jnp.sum(jnp.where(onehot_t, (gstart + counts)[None, :], 0), axis=1)
    tile_rows = jnp.clip(used - tile_start, 0, EXPERT_TILE).astype(I32)
    n_valid = (ends[-1:] // EXPERT_TILE).astype(I32)
    first = jnp.concatenate([jnp.ones((1,), bool), tile_expert[1:] != tile_expert[:-1]])
    group_end = jnp.sum(jnp.where(onehot_t, ends[None, :], 0), axis=1) // EXPERT_TILE
    next_expert = jnp.sum(jnp.where(jnp.minimum(group_end, n_tiles - 1)[:, None] == jnp.arange(n_tiles)[None, :],
                                    tile_expert[None, :], 0), axis=1)
    tile_fetch = jnp.where(first, tile_expert, next_expert).astype(I32)
    sel = eids[:, :TOP_K, None] == jnp.arange(N_EXPERTS, dtype=I32)[None, None, :]
    pos_flat = (jnp.sum(jnp.where(sel, gstart[None, None, :], 0), axis=-1) + ranks[:, :TOP_K]).reshape(-1).astype(I32)
    x_rows = _dispatch_rows(x1, pos_flat, n_tiles * EXPERT_TILE)
    y_rows = _expert_ffn(x_rows, tile_expert, tile_fetch, tile_rows, n_valid, w1_all, b1g, b1u, w2_all, layer, b2)
    return _combine_ln(y_rows, pos_flat, gates, x1, g, b, emit_bf16)


def _fox_prep_kernel(t_ref, bias_ref, lf_ref, ccol_ref):
    t_len = t_ref.shape[0]
    r, c = _iota2((V7X_LANES, V7X_LANES))
    tril = jnp.where(c <= r, 1.0, 0.0)
    carry = jnp.zeros((1, V7X_LANES), F32)
    for blk in range(t_len // V7X_LANES):
        rows = slice(blk * V7X_LANES, (blk + 1) * V7X_LANES)
        lf = _log_sigmoid(t_ref[rows, :] + bias_ref[...])
        lf_ref[rows, :] = lf
        cs = jnp.dot(tril, lf, precision=HIGHEST, preferred_element_type=F32) + carry
        ccol_ref[rows, :] = cs
        carry = cs[V7X_LANES - 1:, :]


def _fox_prep(tail, bias_row, bsz, t_len):
    n_p = bsz * t_len
    blk = pl.BlockSpec((t_len, V7X_LANES), lambda b: (b, 0))
    out = jax.ShapeDtypeStruct((n_p, V7X_LANES), F32)
    return pl.pallas_call(
        _fox_prep_kernel,
        grid=(bsz,),
        in_specs=[blk, pl.BlockSpec((1, V7X_LANES), lambda b: (0, 0))],
        out_specs=[blk, blk],
        out_shape=[out, out],
        compiler_params=_params("parallel"),
        name="fox_prep",
    )(tail, bias_row)


def _fox_attn_kernel(q_ref, k_ref, v_ref, ccol_ref, o_ref, kb_ref, vt_ref, cb0_ref, cb1_ref, m0_ref, m1_ref, l0_ref, l1_ref,
                     acc0_ref, acc1_ref, s0_ref, s1_ref):
    pair = pl.program_id(1)
    tq = ATTN_BLOCK
    t_len = k_ref.shape[0]
    cb_refs, m_refs, l_refs, acc_refs = (cb0_ref, cb1_ref), (m0_ref, m1_ref), (l0_ref, l1_ref), (acc0_ref, acc1_ref)
    s_refs = (s0_ref, s1_ref)

    kb_ref[...] = k_ref[...].astype(BF16)
    sr, sc = _iota2((V7X_LANES, V7X_LANES))
    for j in range(2):
        sel = jnp.where(sr == 2 * pair + j, 1.0, 0.0).astype(BF16)
        cb_refs[j][...] = _expand_heads(ccol_ref[...], sel)
    vt_ref[...] = v_ref[0].astype(BF16)
    feat = lax.broadcasted_iota(I32, (V7X_LANES, tq), 0)
    key_id, qry_id = _iota2((tq, tq))

    def scores(kb, qh_j, j):
        rows = slice(kb * tq, (kb + 1) * tq)
        return (jnp.dot(kb_ref[rows, :], qh_j, preferred_element_type=F32)
                - jnp.concatenate([cb_refs[j][rows, :]] * (tq // V7X_LANES), axis=1))

    def update(kb, j, s):
        m_old = m_refs[j][...]
        m_new = jnp.maximum(m_old, jnp.max(s, axis=0, keepdims=True))
        alpha = jnp.exp(m_old - m_new)
        pe = jnp.exp(s - m_new)
        l_refs[j][...] = alpha * l_refs[j][...] + jnp.sum(pe, axis=0, keepdims=True)
        acc_refs[j][...] = alpha * acc_refs[j][...] + jnp.dot(vt_ref[:, kb * tq:(kb + 1) * tq], pe.astype(BF16),
                                                              preferred_element_type=F32)
        m_refs[j][...] = m_new

    for qi in range(t_len // tq):
        q_rows = slice(qi * tq, (qi + 1) * tq)
        qt = (q_ref[q_rows, :] * (DH_B ** -0.5)).T
        qh = [jnp.where((feat // DH_B) == j, qt, 0.0).astype(BF16) for j in range(2)]
        for j in range(2):
            m_refs[j][...] = jnp.full(m_refs[j].shape, -jnp.inf, F32)
            l_refs[j][...] = jnp.zeros_like(l_refs[j])
            acc_refs[j][...] = jnp.zeros_like(acc_refs[j])
            s_refs[j][...] = scores(0, qh[j], j)
        for kb in range(qi):
            for j in range(2):
                s = s_refs[j][...]
                s_refs[j][...] = scores(kb + 1, qh[j], j)
                update(kb, j, s)
        for j in range(2):
            update(qi, j, jnp.where(key_id <= qry_id, s_refs[j][...], -jnp.inf))
        out_t = jnp.where((feat // DH_B) == 0, acc0_ref[...] / l0_ref[...], acc1_ref[...] / l1_ref[...])
        o_ref[q_rows, :] = out_t.T.astype(o_ref.dtype)


def _fox_attention(h_main, kv_t, ccol, bsz, t_len):
    tq = ATTN_BLOCK
    qcol, kcol = (4 * D_A) // V7X_LANES, (4 * D_A + D_B) // V7X_LANES
    cbs, row, acc = pltpu.VMEM((t_len, V7X_LANES), F32), pltpu.VMEM((1, tq), F32), pltpu.VMEM((V7X_LANES, tq), F32)
    seq = lambda col: pl.BlockSpec((t_len, V7X_LANES), lambda b, p: (b, col + p))
    return pl.pallas_call(
        _fox_attn_kernel,
        grid=(bsz, H_B // 2),
        in_specs=[seq(qcol), seq(kcol),
                  pl.BlockSpec((1, V7X_LANES, t_len), lambda b, p: (b, D_B // V7X_LANES + p, 0)),
                  pl.BlockSpec((t_len, V7X_LANES), lambda b, p: (b, 0))],
        out_specs=pl.BlockSpec((t_len, V7X_LANES), lambda b, p: (b, p)),
        out_shape=jax.ShapeDtypeStruct((h_main.shape[0], D_B), BF16),
        scratch_shapes=[pltpu.VMEM((t_len, V7X_LANES), BF16), pltpu.VMEM((V7X_LANES, t_len), BF16),
                        cbs, cbs, row, row, row, row, acc, acc, pltpu.VMEM((tq, tq), F32), pltpu.VMEM((tq, tq), F32)],
        compiler_params=_params("parallel", "parallel"),
        name="fox_attention",
    )(h_main, h_main, kv_t, ccol)


def _fox_decode_kernel(pt_ref, q_ref, kn_ref, vn_ref, t_ref, bias_ref, *rest):
    np_ = DECODE_PAGES
    k_refs, v_refs, lf_refs = rest[:np_], rest[np_:2 * np_], rest[2 * np_:3 * np_]
    all_ref, o_ref, lfo_ref, m_ref, l_ref, acc_ref, carry_ref, rows_ref = rest[3 * np_:]
    del all_ref
    b = pl.program_id(0)
    j = pl.program_id(1)
    q = q_ref[0] * (DH_B ** -0.5)
    hrow, hlane = _iota2((H_B, D_B))
    own = hlane // DH_B == hrow
    qmat = jnp.where(own, jnp.broadcast_to(q, (H_B, D_B)), 0.0)
    lf_new = _log_sigmoid(t_ref[0] + bias_ref[...])

    @pl.when(j == 0)
    def _():
        m_ref[...] = jnp.sum(qmat * kn_ref[0], axis=-1, keepdims=True)
        l_ref[...] = jnp.ones_like(l_ref)
        acc_ref[...] = jnp.broadcast_to(vn_ref[0], acc_ref.shape)
        carry_ref[...] = jnp.zeros_like(carry_ref)
        lfo_ref[0] = lf_new

    r8, c8 = _iota2((H_B, H_B))
    cn = jnp.sum(jnp.where(r8 == c8, jnp.broadcast_to(lf_new[:, :H_B], (H_B, H_B)), 0.0), axis=-1, keepdims=True)
    kr, kc = _iota2((PAGE_SIZE, PAGE_SIZE))
    later = jnp.where(kr > kc, 1.0, 0.0).astype(BF16)
    qb = qmat.astype(BF16)
    lfts = [lf_refs[i][0] for i in range(np_)]
    carries = [None] * np_
    run = carry_ref[...]
    for i in reversed(range(np_)):
        carries[i] = run
        run = run + jnp.sum(lfts[i], axis=-1, keepdims=True)
    carry_ref[...] = run
    scores = []
    for i in range(np_):
        lf = lfts[i]
        hi = lf.astype(BF16)
        r1 = lf - hi.astype(F32)
        mid = r1.astype(BF16)
        lo = (r1 - mid.astype(F32)).astype(BF16)
        parts = jnp.dot(jnp.concatenate([hi, mid, lo], axis=0), later, preferred_element_type=F32)
        suffix = parts[:H_B] + parts[H_B:2 * H_B] + parts[2 * H_B:]
        k2 = k_refs[i][0].reshape(D_B, PAGE_SIZE).astype(BF16)
        scores.append(jnp.dot(qb, k2, preferred_element_type=F32) + (suffix + (cn + carries[i])))
    m_old = m_ref[...]
    m_new = m_old
    for s in scores:
        m_new = jnp.maximum(m_new, jnp.max(s, axis=-1, keepdims=True))
    alpha = jnp.exp(m_old - m_new)
    l_new = alpha * l_ref[...]
    acc = alpha * acc_ref[...]
    for i, s in enumerate(scores):
        pe = jnp.exp(s - m_new)
        l_new = l_new + jnp.sum(pe, axis=-1, keepdims=True)
        v2 = v_refs[i][0].reshape(D_B, PAGE_SIZE).astype(BF16)
        acc = acc + lax.dot_general(pe.astype(BF16), v2, NT_DIMS, preferred_element_type=F32)
    m_ref[...] = m_new
    l_ref[...] = l_new
    acc_ref[...] = acc

    @pl.when(j == pl.num_programs(1) - 1)
    def _():
        o = jnp.where(own, acc / l_new, 0.0)
        rows_ref[pl.ds(b, 1), :] = jnp.sum(o, axis=0, keepdims=True)

    @pl.when((j == pl.num_programs(1) - 1) & (b == pl.num_programs(0) - 1))
    def _():
        o_ref[...] = rows_ref[...].astype(o_ref.dtype)


def _fox_decode(hs_main, hs_tail, bias_row, cache_kt, cache_vt, logf_t, page_table, ob_all):
    bd = hs_main.shape[0]
    n_p = ob_all.shape[0] - bd
    n_pages = page_table.shape[1]
    steps = n_pages // DECODE_PAGES
    qcol, kcol, vcol = (4 * D_A) // D_B, (4 * D_A + D_B) // D_B, (4 * D_A + 2 * D_B) // D_B

    def page(i, nd):
        return lambda b, j, pt: (pt[b * n_pages + (steps - 1 - j) * DECODE_PAGES + i],) + (0,) * nd

    tok = lambda col: pl.BlockSpec((1, 1, D_B), lambda b, j, pt: (b, 0, col))
    in_specs = [tok(qcol), tok(kcol), tok(vcol),
                pl.BlockSpec((1, 1, V7X_LANES), lambda b, j, pt: (b, 0, 0)),
                pl.BlockSpec((1, V7X_LANES), lambda b, j, pt: (0, 0))]
    in_specs += [pl.BlockSpec((1, H_B, DH_B, PAGE_SIZE), page(i, 3)) for i in range(DECODE_PAGES)]
    in_specs += [pl.BlockSpec((1, H_B, DH_B, PAGE_SIZE), page(i, 3)) for i in range(DECODE_PAGES)]
    in_specs += [pl.BlockSpec((1, H_B, PAGE_SIZE), page(i, 2)) for i in range(DECODE_PAGES)]
    in_specs += [pl.BlockSpec(memory_space=pl.ANY)]
    n_in = len(in_specs) + 1
    return pl.pallas_call(
        _fox_decode_kernel,
        grid_spec=pltpu.PrefetchScalarGridSpec(
            num_scalar_prefetch=1,
            grid=(bd, steps),
            in_specs=in_specs,
            out_specs=[_token_rows_spec(n_p, bd, D_B),
                       pl.BlockSpec((1, 1, V7X_LANES), lambda b, j, pt: (b, 0, 0))],
            scratch_shapes=[pltpu.VMEM((H_B, 1), F32), pltpu.VMEM((H_B, 1), F32), pltpu.VMEM((H_B, D_B), F32),
                            pltpu.VMEM((H_B, 1), F32), pltpu.VMEM((bd, D_B), F32)],
        ),
        out_shape=[jax.ShapeDtypeStruct(ob_all.shape, BF16), jax.ShapeDtypeStruct((bd, 1, V7X_LANES), F32)],
        input_output_aliases={n_in - 1: 0},
        compiler_params=_params("arbitrary", "arbitrary"),
        name="fox_decode",
    )(page_table.reshape(-1), hs_main, hs_main, hs_main, hs_tail, bias_row,
      *([cache_kt] * DECODE_PAGES), *([cache_vt] * DECODE_PAGES), *([logf_t] * DECODE_PAGES), ob_all)


def _hgrn_gates(q, z, lb):
    qa = _silu(q)
    logf = jnp.log(lb + (1.0 - lb) * jax.nn.sigmoid(z))
    ka = (1.0 - lb) * jax.nn.sigmoid(-z)
    return qa, ka, logf


def _hgrn_out(o, g, nw):
    o = o * lax.rsqrt(jnp.mean(o * o, axis=-1, keepdims=True) + RMS_EPS)
    return o * nw * _silu(g)


def _hgrn_kernel(q_ref, f_ref, i_ref, g_ref, lb_ref, nw_ref, o_ref, s_ref, qd_ref, dec_ref, oc_ref, u_ref):
    t_len = q_ref.shape[0]
    cs = CHUNK_A
    per_blk = V7X_LANES // cs
    n_blk = t_len // V7X_LANES
    lb = lb_ref[0]
    r, c = _iota2((V7X_LANES, V7X_LANES))
    same = (r // cs) == (c // cs)
    causal = same & (c <= r)
    sums = jnp.concatenate([jnp.where(causal, 1.0, 0.0), jnp.where(same, 1.0, 0.0)], axis=0).astype(BF16)

    def intra(blk, carry):
        start = pl.multiple_of(blk * V7X_LANES, V7X_LANES)
        rows = pl.ds(start, V7X_LANES)
        qa, ka, logf = _hgrn_gates(q_ref[rows, :], f_ref[rows, :], lb)
        both = _select_sum(sums, logf)
        b, gtot = both[:V7X_LANES], both[V7X_LANES:]
        qd = (qa * jnp.exp(b)).astype(BF16)
        kd = (ka * jnp.exp(-b)).astype(BF16)
        ke = (ka * jnp.exp(gtot - b)).astype(BF16)
        v = i_ref[rows, :].astype(BF16)
        qd_ref[rows, :] = qd
        sc = jnp.where(causal, lax.dot_general(qd, kd, NT_DIMS, preferred_element_type=F32), 0.0)
        oc_ref[rows, :] = jnp.dot(sc.astype(BF16), v, preferred_element_type=F32)
        dec = jnp.exp(gtot)
        for k in range(per_blk):
            sub = slice(k * cs, (k + 1) * cs)
            ci = blk * per_blk + k
            u_ref[ci] = lax.dot_general(v[sub], ke[sub], TN_DIMS, preferred_element_type=F32)
            dec_ref[pl.ds(ci, 1), :] = dec[k * cs:k * cs + 1, :]
        return carry

    lax.fori_loop(0, n_blk, intra, 0, unroll=4)

    def scan(ci, st):
        u = u_ref[ci]
        u_ref[ci] = st
        return st * dec_ref[pl.ds(ci, 1), :] + u

    st_last = lax.fori_loop(0, t_len // cs, scan, jnp.zeros((DV_A, DK_A), F32), unroll=4)
    s_ref[0, 0] = st_last.T

    def inter(blk, carry):
        start = pl.multiple_of(blk * V7X_LANES, V7X_LANES)
        rows = pl.ds(start, V7X_LANES)
        qd = qd_ref[rows, :]
        parts = [lax.dot_general(qd[k * cs:(k + 1) * cs], u_ref[blk * per_blk + k].astype(BF16), NT_DIMS,
                                 preferred_element_type=F32) for k in range(per_blk)]
        o = oc_ref[rows, :] + jnp.concatenate(parts, axis=0)
        o_ref[rows, :] = _hgrn_out(o, g_ref[rows, :], nw_ref[...]).astype(o_ref.dtype)
        return carry

    lax.fori_loop(0, n_blk, inter, 0, unroll=4)


def _hgrn_prompt(h_main, lb3, nw_row, bsz, t_len):
    n_p = bsz * t_len
    n_chunks = t_len // CHUNK_A
    col = lambda grp: (lambda b, h: (b, grp * H_A + h))
    return pl.pallas_call(
        _hgrn_kernel,
        grid=(bsz, H_A),
        in_specs=[pl.BlockSpec((t_len, DK_A), col(0)), pl.BlockSpec((t_len, DK_A), col(1)),
                  pl.BlockSpec((t_len, DV_A), col(2)), pl.BlockSpec((t_len, DV_A), col(3)),
                  pl.BlockSpec((1, 1, DK_A), lambda b, h: (h, 0, 0)), pl.BlockSpec((1, DV_A), lambda b, h: (0, 0))],
        out_specs=[pl.BlockSpec((t_len, DV_A), lambda b, h: (b, h)),
                   pl.BlockSpec((1, 1, DK_A, DV_A), lambda b, h: (b, h, 0, 0))],
        out_shape=[jax.ShapeDtypeStruct((h_main.shape[0], D_A), BF16),
                   jax.ShapeDtypeStruct((bsz, H_A, DK_A, DV_A), F32)],
        scratch_shapes=[pltpu.VMEM((t_len, DK_A), BF16), pltpu.VMEM((n_chunks, DK_A), F32),
                        pltpu.VMEM((t_len, DV_A), F32), pltpu.VMEM((n_chunks, DV_A, DK_A), F32)],
        compiler_params=_params("parallel", "parallel"),
        name="hgrn_prompt",
    )(h_main, h_main, h_main, h_main, lb3, nw_row)


def _hgrn_step_kernel(h_ref, lb_ref, nw_ref, s0_ref, all_ref, o_ref, s_ref, rows_ref):
    del all_ref
    b = pl.program_id(0)
    outs = []
    for h in range(H_A):
        grp = lambda g: h_ref[0, :, g * D_A + h * DK_A: g * D_A + (h + 1) * DK_A]
        qa, ka, logf = _hgrn_gates(grp(0), grp(1), lb_ref[h])
        v = grp(2)
        s_new = _to_column(jnp.exp(logf)) * s0_ref[0, h] + _to_column(ka) * v
        s_ref[0, h] = s_new
        o = jnp.sum(_to_column(qa) * s_new, axis=0, keepdims=True)
        outs.append(_hgrn_out(o, grp(3), nw_ref[...]))
    rows_ref[pl.ds(b, 1), :] = jnp.concatenate(outs, axis=-1)

    @pl.when(b == pl.num_programs(0) - 1)
    def _():
        o_ref[...] = rows_ref[...].astype(o_ref.dtype)


def _hgrn_step(hs_main, lb3, nw_row, s0, oa_all):
    bd = hs_main.shape[0]
    n_p = oa_all.shape[0] - bd
    return pl.pallas_call(
        _hgrn_step_kernel,
        grid=(bd,),
        in_specs=[pl.BlockSpec((1, 1, 4 * D_A), lambda b: (b, 0, 0)), pl.BlockSpec((H_A, 1, DK_A), lambda b: (0, 0, 0)),
                  pl.BlockSpec((1, DV_A), lambda b: (0, 0)), pl.BlockSpec((1, H_A, DK_A, DV_A), lambda b: (b, 0, 0, 0)),
                  pl.BlockSpec(memory_space=pl.ANY)],
        out_specs=[_token_rows_spec(n_p, bd, D_A),
                   pl.BlockSpec((1, H_A, DK_A, DV_A), lambda b: (b, 0, 0, 0))],
        out_shape=[jax.ShapeDtypeStruct(oa_all.shape, BF16), jax.ShapeDtypeStruct(s0.shape, F32)],
        scratch_shapes=[pltpu.VMEM((bd, D_A), F32)],
        input_output_aliases={4: 0},
        compiler_params=_params("arbitrary"),
        name="hgrn_step",
    )(hs_main, lb3, nw_row, s0, oa_all)


def _conv_silu(cur, prev, w, b):
    row8 = lax.broadcasted_iota(I32, prev.shape, 0)
    acc = b + cur * w[CONV_W - 1:CONV_W, :]
    for s in range(1, CONV_W):
        sh = pltpu.roll(cur, s, 0)
        head = jnp.where(row8 < s, pltpu.roll(prev, s, 0), sh[:V7X_SUBLANES, :])
        shifted = jnp.concatenate([head, sh[V7X_SUBLANES:, :]], axis=0)
        acc = acc + shifted * w[CONV_W - 1 - s:CONV_W - s, :]
    return _silu(acc)


def _gated_group_norm(y, z, nw):
    y = y * _silu(z)
    parts = []
    for g in range(N_GROUPS_C):
        seg = y[:, g * GROUP_W:(g + 1) * GROUP_W]
        parts.append(seg * lax.rsqrt(jnp.mean(seg * seg, axis=-1, keepdims=True) + RMS_EPS))
    return jnp.concatenate(parts, axis=-1) * nw


def _ssd_kernel(z_ref, x_ref, bc_ref, dt_ref, cwx_ref, cbx_ref, cwbc_ref, cbbc_ref, dtb_ref, alog_ref, dsk_ref,
                nw_ref, y_ref, hs_ref, tailx_ref, tailbc_ref, ht_ref, e_ref, yacc_ref, xw_ref):
    tb = pl.program_id(1)
    tt = x_ref.shape[0]
    pair_w = 2 * HEADDIM_C
    heads_per_group = H_C // N_GROUPS_C

    @pl.when(tb == 0)
    def _():
        tailx_ref[...] = jnp.zeros_like(tailx_ref)
        tailbc_ref[...] = jnp.zeros_like(tailbc_ref)
        ht_ref[...] = jnp.zeros_like(ht_ref)
        er, ec = _iota2(e_ref.shape)
        e_ref[...] = jnp.where(ec // HEADDIM_C == er, 1.0, 0.0).astype(BF16)

    x_raw = x_ref[...]
    bc_raw = bc_ref[...]
    xs = _conv_silu(x_raw, tailx_ref[...], cwx_ref[...], cbx_ref[...])
    bcv = _conv_silu(bc_raw, tailbc_ref[...], cwbc_ref[...], cbbc_ref[...])
    tailx_ref[...] = x_raw[tt - V7X_SUBLANES:, :]
    tailbc_ref[...] = bc_raw[tt - V7X_SUBLANES:, :]

    lane = lax.broadcasted_iota(I32, (tt, V7X_LANES), 1)
    dt = jnp.where(lane < H_C, _softplus(dt_ref[...] + dtb_ref[...]), 0.0)
    a = -jnp.exp(alog_ref[...])
    r, c = _iota2((tt, tt))
    causal = c <= r
    cum = jnp.dot(jnp.where(causal, 1.0, 0.0), dt * a, precision=HIGHEST, preferred_element_type=F32)
    cum_t = cum.T
    xdt = xs * _expand_heads(dt, e_ref[...])
    low = (lane % pair_w) < HEADDIM_C

    for g in range(N_GROUPS_C):
        b_g = bcv[:, g * D_STATE_C:(g + 1) * D_STATE_C]
        c_g = bcv[:, D_BC + g * D_STATE_C:D_BC + (g + 1) * D_STATE_C].astype(BF16)
        cb = lax.dot_general(c_g, b_g.astype(BF16), NT_DIMS, preferred_element_type=F32)
        y_inter = jnp.dot(c_g, ht_ref[g].astype(BF16), preferred_element_type=F32)
        decs = []
        for pr in range(heads_per_group // 2):
            slab = slice(g * GROUP_W + pr * pair_w, g * GROUP_W + (pr + 1) * pair_w)
            xdt_slab = xdt[:, slab]
            ys, es, tes = [], [], []
            for j in range(2):
                head = g * heads_per_group + pr * 2 + j
                colb = jnp.broadcast_to(cum[:, head:head + 1], (tt, tt))
                decay = jnp.exp(jnp.where(causal, colb - cum_t[head:head + 1, :], -jnp.inf))
                ys.append(jnp.dot((cb * decay).astype(BF16), xdt_slab.astype(BF16), preferred_element_type=F32))
                es.append(jnp.exp(colb))
                tes.append(jnp.exp(colb[tt - 1:, :] - colb))
            e_pair = jnp.where(low, es[0], es[1])
            yacc_ref[:, slab] = (jnp.where(low, ys[0], ys[1]) + e_pair * y_inter[:, pr * pair_w:(pr + 1) * pair_w]
                                 + dsk_ref[:, slab] * xs[:, slab])
            xw_ref[:, pr * pair_w:(pr + 1) * pair_w] = xdt_slab * jnp.where(low, tes[0], tes[1])
            decs.append(e_pair[tt - 1:, :])
        dec_row = jnp.concatenate(decs, axis=-1)
        ht_ref[g] = ht_ref[g] * dec_row + jnp.dot(b_g.T.astype(BF16), xw_ref[...].astype(BF16),
                                                  preferred_element_type=F32)

    y_ref[...] = _gated_group_norm(yacc_ref[...], z_ref[...], nw_ref[...]).astype(y_ref.dtype)

    @pl.when(tb == pl.num_programs(1) - 1)
    def _():
        for g in range(N_GROUPS_C):
            for q in range(GROUP_W // V7X_LANES):
                rows = slice(g * GROUP_W + q * V7X_LANES, g * GROUP_W + (q + 1) * V7X_LANES)
                hs_ref[0, rows, :] = ht_ref[g][:, q * V7X_LANES:(q + 1) * V7X_LANES].T


def _ssd_prompt(h_main, h_tail, conv_w, conv_b, dtb_row, alog_row, dsk_row, nw_row, bsz, t_len):
    n_p = bsz * t_len
    tt = SSD_BLOCK
    nt = t_len // tt
    rowmap = lambda col: (lambda b, t: (b * nt + t, col))
    fixed = lambda b, t: (0, 0)
    cwx, cwbc = conv_w[:, :D_INNER], conv_w[:, D_INNER:]
    cbx, cbbc = conv_b[None, :D_INNER], conv_b[None, D_INNER:]
    return pl.pallas_call(
        _ssd_kernel,
        grid=(bsz, nt),
        in_specs=[pl.BlockSpec((tt, D_INNER), rowmap(0)), pl.BlockSpec((tt, D_INNER), rowmap(1)),
                  pl.BlockSpec((tt, 2 * D_BC), rowmap(2 * D_INNER // (2 * D_BC))),
                  pl.BlockSpec((tt, V7X_LANES), rowmap(0)),
                  pl.BlockSpec((CONV_W, D_INNER), fixed), pl.BlockSpec((1, D_INNER), fixed),
                  pl.BlockSpec((CONV_W, 2 * D_BC), fixed), pl.BlockSpec((1, 2 * D_BC), fixed),
                  pl.BlockSpec((1, V7X_LANES), fixed), pl.BlockSpec((1, V7X_LANES), fixed),
                  pl.BlockSpec((1, D_INNER), fixed), pl.BlockSpec((1, D_INNER), fixed)],
        out_specs=[pl.BlockSpec((tt, D_INNER), rowmap(0)),
                   pl.BlockSpec((1, D_INNER, D_STATE_C), lambda b, t: (b, 0, 0))],
        out_shape=[jax.ShapeDtypeStruct((h_main.shape[0], D_INNER), BF16),
                   jax.ShapeDtypeStruct((bsz, D_INNER, D_STATE_C), F32)],
        scratch_shapes=[pltpu.VMEM((V7X_SUBLANES, D_INNER), F32), pltpu.VMEM((V7X_SUBLANES, 2 * D_BC), F32),
                        pltpu.VMEM((N_GROUPS_C, D_STATE_C, GROUP_W), F32), pltpu.VMEM((V7X_LANES, D_INNER), BF16),
                        pltpu.VMEM((tt, D_INNER), F32), pltpu.VMEM((tt, GROUP_W), F32)],
        compiler_params=_params("parallel", "arbitrary"),
        name="ssd_prompt",
    )(h_main, h_main, h_main, h_tail, cwx, cbx, cwbc, cbbc, dtb_row, alog_row, dsk_row, nw_row)


def _ssd_step_kernel(h_ref, t_ref, cs_ref, cw_ref, cb_ref, dtb_ref, alog_ref, dsk_ref, nw_ref, h0_ref, all_ref,
                     y_ref, hn_ref, rows_ref):
    del all_ref
    b = pl.program_id(0)
    z = h_ref[0, :, :D_INNER]
    xbc_new = h_ref[0, :, D_INNER:]
    cw = cw_ref[...]
    conv = cb_ref[...] + xbc_new * cw[CONV_W - 1:CONV_W, :]
    for j in range(CONV_W - 1):
        conv = conv + cs_ref[0, j:j + 1, :] * cw[j:j + 1, :]
    xbc = _silu(conv)
    xs = xbc[:, :D_INNER]
    lane = lax.broadcasted_iota(I32, (1, V7X_LANES), 1)
    dt = jnp.where(lane < H_C, _softplus(t_ref[0] + dtb_ref[...]), 0.0)
    da = jnp.exp(dt * -jnp.exp(alog_ref[...]))
    er, ec = _iota2((V7X_LANES, D_INNER))
    expand = jnp.where(ec // HEADDIM_C == er, 1.0, 0.0).astype(BF16)
    rows8 = lambda v: jnp.broadcast_to(v, (V7X_SUBLANES, V7X_LANES))
    dt_x = _expand_heads(rows8(dt), expand)[:1, :]
    da_x = _expand_heads(rows8(da), expand)[:1, :]
    xdt = xs * dt_x
    y_parts = []
    for q in range(D_INNER // V7X_LANES):
        g = q // (GROUP_W // V7X_LANES)
        lanes = slice(q * V7X_LANES, (q + 1) * V7X_LANES)
        b_g = xbc[:, D_INNER + g * D_STATE_C:D_INNER + (g + 1) * D_STATE_C]
        c_g = xbc[:, D_INNER + D_BC + g * D_STATE_C:D_INNER + D_BC + (g + 1) * D_STATE_C]
        h_new = _to_column(da_x[:, lanes]) * h0_ref[0, lanes, :] + _to_column(xdt[:, lanes]) * b_g
        hn_ref[0, lanes, :] = h_new
        y_parts.append(_to_row(jnp.sum(h_new * c_g, axis=-1, keepdims=True)))
    y = jnp.concatenate(y_parts, axis=-1) + dsk_ref[...] * xs
    rows_ref[pl.ds(b, 1), :] = _gated_group_norm(y, z, nw_ref[...])

    @pl.when(b == pl.num_programs(0) - 1)
    def _():
        y_ref[...] = rows_ref[...].astype(y_ref.dtype)


def _ssd_step(hs_main, hs_tail, conv_state, conv_w, conv_b, dtb_row, alog_row, dsk_row, nw_row, h0, y_all):
    bd = hs_main.shape[0]
    n_p = y_all.shape[0] - bd
    fixed = lambda b: (0, 0)
    tok = lambda b: (b, 0, 0)
    return pl.pallas_call(
        _ssd_step_kernel,
        grid=(bd,),
        in_specs=[pl.BlockSpec((1, 1, ODD_MAIN), tok), pl.BlockSpec((1, 1, V7X_LANES), tok),
                  pl.BlockSpec((1, CONV_W - 1, CONV_DIM), tok),
                  pl.BlockSpec((CONV_W, CONV_DIM), fixed), pl.BlockSpec((1, CONV_DIM), fixed),
                  pl.BlockSpec((1, V7X_LANES), fixed), pl.BlockSpec((1, V7X_LANES), fixed),
                  pl.BlockSpec((1, D_INNER), fixed), pl.BlockSpec((1, D_INNER), fixed),
                  pl.BlockSpec((1, D_INNER, D_STATE_C), tok), pl.BlockSpec(memory_space=pl.ANY)],
        out_specs=[_token_rows_spec(n_p, bd, D_INNER), pl.BlockSpec((1, D_INNER, D_STATE_C), tok)],
        out_shape=[jax.ShapeDtypeStruct(y_all.shape, BF16), jax.ShapeDtypeStruct((bd, D_INNER, D_STATE_C), F32)],
        scratch_shapes=[pltpu.VMEM((bd, D_INNER), F32)],
        input_output_aliases={10: 0},
        compiler_params=_params("arbitrary"),
        name="ssd_step",
    )(hs_main, hs_tail, conv_state, conv_w, conv_b[None], dtb_row, alog_row, dsk_row, nw_row, h0, y_all)


def _pad_lanes(v):
    return jnp.pad(v, (0, V7X_LANES - v.shape[0]))[None]


def kernel(x_prompt, x_sample, cache_k, cache_v, cache_logf, page_table, state_hgrn, state_ssm, state_conv,
           w_in_even, hgrn_lower_bound, hgrn_norm_w, fox_f_bias, w_out_even,
           w_in_odd, conv_w, conv_b, dt_bias, a_log, d_skip, ssm_norm_w, w_out_odd,
           ln1_g, ln1_b, ln2_g, ln2_b, router_w, router_b, exp_w1, exp_b1, exp_w2, exp_b2):
    lb_all = jnp.cumsum(jax.nn.softmax(hgrn_lower_bound, axis=0), axis=0)
    bp, t_p, d = x_prompt.shape
    bd, t_d, _ = x_sample.shape
    assert t_d == 1
    n_p = bp * t_p
    x_all = jnp.concatenate([x_prompt.reshape(n_p, d), x_sample.reshape(bd, d)], axis=0)
    x_bf = x_all.astype(BF16)
    outs = {}
    for l in range(DEPTH):
        i = l // 2
        if l % 2 == 0:
            h_main = _matmul(x_bf, w_in_even[i][:, :EVEN_MAIN].astype(BF16))
            h_tail = _matmul(x_bf, w_in_even[i][:, EVEN_MAIN:].astype(BF16))
            hs_main, hs_tail = h_main[n_p:, None, :], h_tail[n_p:, None, :]
            lb3 = lb_all[i].reshape(H_A, 1, DK_A)
            nw_row = hgrn_norm_w[i][None]
            bias_row = _pad_lanes(fox_f_bias[i])
            lf_p, ccol = _fox_prep(h_tail, bias_row, bp, t_p)
            kcol, vcol = 4 * D_A + D_B, 4 * D_A + 2 * D_B
            kv_t = _proj_t(x_bf, w_in_even[i][:, kcol:kcol + 2 * D_B].T.astype(BF16), bp, t_p)
            ob_p = _fox_attention(h_main, kv_t, ccol, bp, t_p)
            oa_p, hg_p = _hgrn_prompt(h_main, lb3, nw_row, bp, t_p)
            oa_all, hg_s = _hgrn_step(hs_main, lb3, nw_row, state_hgrn[i], oa_p)
            ob_all, lf_s = _fox_decode(hs_main, hs_tail, bias_row, jnp.transpose(cache_k[i], (0, 2, 3, 1)),
                                       jnp.transpose(cache_v[i], (0, 2, 3, 1)), jnp.swapaxes(cache_logf[i], 1, 2),
                                       page_table, ob_p)
            w_out = w_out_even[i].astype(BF16)
            parts = [(oa_all, w_out[:D_A]), (ob_all, w_out[D_A:])]
            heads_last = lambda a: jnp.transpose(a.reshape(bp, H_B, DH_B, t_p), (0, 3, 1, 2))
            vals = (("kp", heads_last(kv_t[:, :D_B])),
                    ("ks", h_main[n_p:, kcol:kcol + D_B].reshape(bd, t_d, H_B, DH_B)),
                    ("vp", heads_last(kv_t[:, D_B:])),
                    ("vs", h_main[n_p:, vcol:vcol + D_B].reshape(bd, t_d, H_B, DH_B)),
                    ("lfp", lf_p[:, :H_B].reshape(bp, t_p, H_B)), ("lfs", lf_s[:, :, :H_B]),
                    ("hgp", hg_p), ("hgs", hg_s))
        else:
            h_main = _matmul(x_bf, w_in_odd[i][:, :ODD_MAIN].astype(BF16))
            h_tail = _matmul(x_bf, w_in_odd[i][:, ODD_MAIN:].astype(BF16))
            hs_main, hs_tail = h_main[n_p:, None, :], h_tail[n_p:, None, :]
            dtb_row, alog_row = _pad_lanes(dt_bias[i]), _pad_lanes(a_log[i])
            dsk_row = jnp.repeat(d_skip[i], HEADDIM_C)[None]
            nw_row = ssm_norm_w[i][None]
            y_p, ss_p = _ssd_prompt(h_main, h_tail, conv_w[i], conv_b[i], dtb_row, alog_row, dsk_row, nw_row, bp, t_p)
            y_all, ss_s = _ssd_step(hs_main, hs_tail, state_conv[i], conv_w[i], conv_b[i], dtb_row, alog_row, dsk_row,
                                    nw_row, state_ssm[i].reshape(bd, D_INNER, D_STATE_C), y_p)
            parts = [(y_all, w_out_odd[i].astype(BF16))]
            tail_rows = jnp.stack([h_main[(b + 1) * t_p - (CONV_W - 1):(b + 1) * t_p, D_INNER:] for b in range(bp)])
            vals = (("ssp", ss_p.reshape(bp, H_C, HEADDIM_C, D_STATE_C)),
                    ("sss", ss_s.reshape(bd, H_C, HEADDIM_C, D_STATE_C)),
                    ("cvp", tail_rows),
                    ("cvs", jnp.concatenate([state_conv[i][:, 1:], hs_main[:, :, D_INNER:]], axis=1)))
        for name, val in vals:
            outs.setdefault(name, []).append(val)
        rw = jnp.pad(router_w[l], ((0, 0), (0, V7X_LANES - N_EXPERTS)))
        rb = jnp.concatenate([router_b[l], jnp.full((V7X_LANES - N_EXPERTS,), -jnp.inf, F32)])[None]
        x1, eids, gates = _post_mixer(x_all, parts, ln1_g[l][None], ln1_b[l][None], rw, rb)
        b1g = exp_b1[l][:, None, 0::2]
        b1u = exp_b1[l][:, None, 1::2]
        more = l + 1 < DEPTH
        res = _moe_ln(x1, eids, gates, exp_w1, b1g, b1u, exp_w2, l, exp_b2[l][:, None, :],
                      ln2_g[l][None], ln2_b[l][None], more)
        x_all = res[0]
        x_bf = res[1] if more else None
    st = {k: jnp.stack(v) for k, v in outs.items()}
    return (x_all[:n_p].reshape(bp, t_p, d), x_all[n_p:].reshape(bd, t_d, d),
            st["kp"], st["ks"], st["vp"], st["vs"], st["lfp"], st["lfs"],
            st["hgp"], st["hgs"], st["ssp"], st["sss"], st["cvp"], st["cvs"])
```

```python
import functools

import jax
import jax.numpy as jnp
import numpy as np
from jax import lax
from jax.experimental import pallas as pl
from jax.experimental.pallas import tpu as pltpu

F32 = jnp.float32
BF16 = jnp.bfloat16
I32 = jnp.int32
HIGHEST = lax.Precision.HIGHEST

D_MODEL = 1024
DEPTH = 2
PAGE_SIZE = 128
H_A, DK_A, DV_A, CHUNK_A = 4, 128, 128, 32
H_B, DH_B = 8, 64
D_A = H_A * DK_A
D_B = H_B * DH_B
EVEN_MAIN = 4 * D_A + 3 * D_B
D_INNER = 2 * D_MODEL
HEADDIM_C = 64
H_C = D_INNER // HEADDIM_C
N_GROUPS_C = 4
D_STATE_C = 128
D_BC = N_GROUPS_C * D_STATE_C
GROUP_W = D_INNER // N_GROUPS_C
CONV_W = 4
CONV_DIM = D_INNER + 2 * D_BC
ODD_MAIN = D_INNER + CONV_DIM
N_EXPERTS = 32
TOP_K = 4
D_FF = D_MODEL
SWIGLU_LIMIT = 7.0
SWIGLU_ALPHA = 1.702
DN_ALPHA = (2 * DEPTH) ** 0.25
LN_EPS = 1e-5
RMS_EPS = 1e-6

V7X_LANES = 128
V7X_SUBLANES = 8
V7X_MXU = 256
VMEM_LIMIT = 56 * 1024 * 1024
EXPERT_TILE = 512
ATTN_BLOCK = 256
SSD_BLOCK = 128
DECODE_PAGES = 16
DMA_UNROLL = 16
MATMUL_ROW_CAP = 1024
TOKEN_ROW_CAP = 640

NT_DIMS = (((1,), (1,)), ((), ()))
TN_DIMS = (((0,), (0,)), ((), ()))


def _row_tile(m, cap):
    best = 0
    for t in range(16, cap + 1, 16):
        if m % t == 0:
            best = t
    assert best, (m, cap)
    return best


def _token_rows_spec(n_rows, bd, width):
    assert n_rows % bd == 0
    return pl.BlockSpec((bd, width), lambda *_: (n_rows // bd, 0))


def _params(*sem):
    return pltpu.CompilerParams(dimension_semantics=sem, vmem_limit_bytes=VMEM_LIMIT)


def _silu(x):
    return x * jax.nn.sigmoid(x)


def _softplus(x):
    return jnp.maximum(x, 0.0) + jnp.log1p(jnp.exp(-jnp.abs(x)))


def _log_sigmoid(x):
    return -_softplus(-x)


def _iota2(shape):
    return lax.broadcasted_iota(I32, shape, 0), lax.broadcasted_iota(I32, shape, 1)


def _to_column(row_vec):
    n = row_vec.shape[1]
    r, c = _iota2((n, n))
    return jnp.sum(jnp.where(r == c, jnp.broadcast_to(row_vec, (n, n)), 0.0), axis=1, keepdims=True)


def _to_row(col_vec):
    n = col_vec.shape[0]
    r, c = _iota2((n, n))
    return jnp.sum(jnp.where(r == c, jnp.broadcast_to(col_vec, (n, n)), 0.0), axis=0, keepdims=True)


def _expand_heads(v, e_bf16):
    hi = v.astype(BF16)
    r1 = v - hi.astype(F32)
    mid = r1.astype(BF16)
    lo = (r1 - mid.astype(F32)).astype(BF16)
    dot = lambda a: jnp.dot(a, e_bf16, preferred_element_type=F32)
    return dot(hi) + dot(mid) + dot(lo)


def _select_sum(m01_bf16, x):
    hi = x.astype(BF16)
    r1 = x - hi.astype(F32)
    mid = r1.astype(BF16)
    lo = (r1 - mid.astype(F32)).astype(BF16)
    dot = lambda a: jnp.dot(m01_bf16, a, preferred_element_type=F32)
    return dot(hi) + dot(mid) + dot(lo)


def _mm_kernel(x_ref, w_ref, o_ref):
    o_ref[...] = jnp.dot(x_ref[...].astype(BF16), w_ref[...].astype(BF16), preferred_element_type=F32)


def _matmul(x, w, tm_cap=MATMUL_ROW_CAP):
    m, k = x.shape
    n = w.shape[1]
    n_pad = -(-n // V7X_LANES) * V7X_LANES
    if n_pad != n:
        w = jnp.pad(w, ((0, 0), (0, n_pad - n)))
    tm = _row_tile(m, tm_cap)
    tn = next(t for t in (1024, 512, V7X_MXU, V7X_LANES) if n_pad % t == 0)
    return pl.pallas_call(
        _mm_kernel,
        grid=(m // tm, n_pad // tn),
        in_specs=[pl.BlockSpec((tm, k), lambda i, j: (i, 0)),
                  pl.BlockSpec((k, tn), lambda i, j: (0, j))],
        out_specs=pl.BlockSpec((tm, tn), lambda i, j: (i, j)),
        out_shape=jax.ShapeDtypeStruct((m, n_pad), F32),
        compiler_params=_params("parallel", "parallel"),
        name="dense_matmul",
    )(x, w)


def _proj_t_kernel(w_ref, x_ref, o_ref):
    o_ref[0] = lax.dot_general(w_ref[...], x_ref[...].astype(BF16), NT_DIMS, preferred_element_type=F32)


def _proj_t(x, w_t_bf16, bsz, t_len, tm=512):
    n, k = w_t_bf16.shape
    nt = t_len // tm
    return pl.pallas_call(
        _proj_t_kernel,
        grid=(bsz, nt),
        in_specs=[pl.BlockSpec((n, k), lambda b, t: (0, 0)), pl.BlockSpec((tm, k), lambda b, t: (b * nt + t, 0))],
        out_specs=pl.BlockSpec((1, n, tm), lambda b, t: (b, 0, t)),
        out_shape=jax.ShapeDtypeStruct((bsz, n, t_len), F32),
        compiler_params=_params("parallel", "parallel"),
        name="proj_transposed",
    )(w_t_bf16, x)


def _post_mixer_kernel(n_parts, x_ref, *refs):
    lhs_refs, w_refs = refs[:n_parts], refs[n_parts:2 * n_parts]
    g_ref, b_ref, rwh_ref, rwl_ref, rb_ref, x1_ref, eid_ref, gate_ref, rank_ref, cnt_ref, carry_ref = refs[2 * n_parts:]
    acc = DN_ALPHA * x_ref[...]
    for lhs_ref, w_ref in zip(lhs_refs, w_refs):
        acc = acc + jnp.dot(lhs_ref[...].astype(BF16), w_ref[...], preferred_element_type=F32)
    xc = acc - jnp.mean(acc, axis=-1, keepdims=True)
    var = jnp.mean(xc * xc, axis=-1, keepdims=True)
    x1 = xc * lax.rsqrt(var + LN_EPS) * g_ref[...] + b_ref[...]
    x1_ref[...] = x1
    x_hi = x1.astype(BF16)
    x_lo = (x1 - x_hi.astype(F32)).astype(BF16)
    dot = lambda a, w_ref: jnp.dot(a, w_ref[...], preferred_element_type=F32)
    logits = dot(x_hi, rwh_ref) + (dot(x_lo, rwh_ref) + dot(x_hi, rwl_ref)) + rb_ref[...]
    lane = lax.broadcasted_iota(I32, logits.shape, 1)
    eids = jnp.zeros(logits.shape, I32)
    vals = []
    onehot = jnp.zeros(logits.shape, F32)
    sel = []
    for k in range(TOP_K):
        top = jnp.max(logits, axis=-1, keepdims=True)
        idx = jnp.min(jnp.where(logits == top, lane, V7X_LANES), axis=-1, keepdims=True)
        vals.append(top)
        sel.append(idx)
        onehot = onehot + (lane == idx).astype(F32)
        eids = jnp.where(lane == k, idx, eids)
        logits = jnp.where(lane == idx, -jnp.inf, logits)
    exps = [jnp.exp(v - vals[0]) for v in vals]
    denom = exps[0] + exps[1] + exps[2] + exps[3]
    gates = jnp.zeros(logits.shape, F32)
    for k in range(TOP_K):
        gates = jnp.where(lane == k, exps[k] / denom, gates)
    eid_ref[...] = eids
    gate_ref[...] = gates
    @pl.when(pl.program_id(0) == 0)
    def _():
        carry_ref[...] = jnp.zeros_like(carry_ref)

    tm = logits.shape[0]
    r, c = _iota2((tm, tm))
    prior = jnp.dot((c < r).astype(BF16), onehot.astype(BF16), preferred_element_type=F32) + carry_ref[...]
    ranks = jnp.zeros(logits.shape, F32)
    for k in range(TOP_K):
        rk = jnp.sum(jnp.where(lane == sel[k], prior, 0.0), axis=-1, keepdims=True)
        ranks = jnp.where(lane == k, rk, ranks)
    rank_ref[...] = ranks.astype(I32)
    total = carry_ref[...] + jnp.sum(onehot, axis=0, keepdims=True)
    carry_ref[...] = total
    cnt_ref[...] = total.astype(I32)


def _post_mixer(x, parts, g, b, rw, rb):
    m, d = x.shape
    tm = _row_tile(m, TOKEN_ROW_CAP)
    row = lambda i: (i, 0)
    fixed = lambda i: (0, 0)
    rw_hi = rw.astype(BF16)
    rw_lo = (rw - rw_hi.astype(F32)).astype(BF16)
    lhs_specs = [pl.BlockSpec((tm, lhs.shape[1]), row) for lhs, _ in parts]
    w_specs = [pl.BlockSpec(w.shape, fixed) for _, w in parts]
    return pl.pallas_call(
        functools.partial(_post_mixer_kernel, len(parts)),
        grid=(m // tm,),
        in_specs=[pl.BlockSpec((tm, d), row)] + lhs_specs + w_specs + [
                  pl.BlockSpec((1, d), fixed), pl.BlockSpec((1, d), fixed),
                  pl.BlockSpec((d, V7X_LANES), fixed), pl.BlockSpec((d, V7X_LANES), fixed),
                  pl.BlockSpec((1, V7X_LANES), fixed)],
        out_specs=[pl.BlockSpec((tm, d), row), pl.BlockSpec((tm, V7X_LANES), row),
                   pl.BlockSpec((tm, V7X_LANES), row), pl.BlockSpec((tm, V7X_LANES), row),
                   pl.BlockSpec((1, V7X_LANES), fixed)],
        out_shape=[jax.ShapeDtypeStruct((m, d), F32), jax.ShapeDtypeStruct((m, V7X_LANES), I32),
                   jax.ShapeDtypeStruct((m, V7X_LANES), F32), jax.ShapeDtypeStruct((m, V7X_LANES), I32),
                   jax.ShapeDtypeStruct((1, V7X_LANES), I32)],
        scratch_shapes=[pltpu.VMEM((1, V7X_LANES), F32)],
        compiler_params=_params("arbitrary"),
        name="post_mixer",
    )(x, *[lhs for lhs, _ in parts], *[w for _, w in parts], g, b, rw_hi, rw_lo, rb)


def _rank_kernel(eid_ref, rank_ref, cnt_ref, carry_ref):
    i = pl.program_id(0)

    @pl.when(i == 0)
    def _():
        carry_ref[...] = jnp.zeros_like(carry_ref)

    eids = eid_ref[...]
    tm = eids.shape[0]
    lane = lax.broadcasted_iota(I32, eids.shape, 1)
    sel = [jnp.sum(jnp.where(lane == k, eids, 0), axis=-1, keepdims=True) for k in range(TOP_K)]
    onehot = jnp.zeros(eids.shape, F32)
    for k in range(TOP_K):
        onehot = onehot + (lane == sel[k]).astype(F32)
    r, c = _iota2((tm, tm))
    before = (c < r).astype(BF16)
    prior = jnp.dot(before, onehot.astype(BF16), preferred_element_type=F32) + carry_ref[...]
    ranks = jnp.zeros(eids.shape, F32)
    for k in range(TOP_K):
        rk = jnp.sum(jnp.where(lane == sel[k], prior, 0.0), axis=-1, keepdims=True)
        ranks = jnp.where(lane == k, rk, ranks)
    rank_ref[...] = ranks.astype(I32)
    total = carry_ref[...] + jnp.sum(onehot, axis=0, keepdims=True)
    carry_ref[...] = total
    cnt_ref[...] = total.astype(I32)


def _route_ranks(eids):
    m = eids.shape[0]
    tm = _row_tile(m, TOKEN_ROW_CAP)
    return pl.pallas_call(
        _rank_kernel,
        grid=(m // tm,),
        in_specs=[pl.BlockSpec((tm, V7X_LANES), lambda i: (i, 0))],
        out_specs=[pl.BlockSpec((tm, V7X_LANES), lambda i: (i, 0)), pl.BlockSpec((1, V7X_LANES), lambda i: (0, 0))],
        out_shape=[jax.ShapeDtypeStruct((m, V7X_LANES), I32), jax.ShapeDtypeStruct((1, V7X_LANES), I32)],
        scratch_shapes=[pltpu.VMEM((1, V7X_LANES), F32)],
        compiler_params=_params("arbitrary"),
        name="route_ranks",
    )(eids)


def _row_copy(src_ref, s, dst_ref, d, sem):
    return pltpu.make_async_copy(src_ref.at[pl.ds(s, 1)], dst_ref.at[pl.ds(d, 1)], sem)


def _dispatch_kernel(pos_ref, x_ref, out_ref, sem):
    i = pl.program_id(0)
    tm = x_ref.shape[0]
    base = i * tm * TOP_K

    def start(t, carry):
        for k in range(TOP_K):
            _row_copy(x_ref, t, out_ref, pos_ref[base + t * TOP_K + k], sem).start(priority=k % 2)
        return carry

    def wait(t, carry):
        for k in range(TOP_K):
            _row_copy(x_ref, t, out_ref, pos_ref[base + t * TOP_K + k], sem).wait()
        return carry

    lax.fori_loop(0, tm, start, 0, unroll=DMA_UNROLL)
    lax.fori_loop(0, tm, wait, 0, unroll=DMA_UNROLL)


def _dispatch_rows(x1, pos_flat, n_rows):
    m, d = x1.shape
    tm = _row_tile(m, MATMUL_ROW_CAP)
    return pl.pallas_call(
        _dispatch_kernel,
        grid_spec=pltpu.PrefetchScalarGridSpec(
            num_scalar_prefetch=1,
            grid=(m // tm,),
            in_specs=[pl.BlockSpec((tm, d), lambda i, pos: (i, 0))],
            out_specs=pl.BlockSpec(memory_space=pl.ANY),
            scratch_shapes=[pltpu.SemaphoreType.DMA(())],
        ),
        out_shape=jax.ShapeDtypeStruct((n_rows, d), x1.dtype),
        compiler_params=_params("arbitrary"),
        name="moe_dispatch",
    )(pos_flat, x1)


def _expert_weight_prep(w1_ref, w2_ref, w1g_ref, w1u_ref, w2b_ref):
    r, c = _iota2((V7X_MXU, V7X_MXU))
    half = V7X_MXU // 2
    src = jnp.where(c < half, 2 * c, 2 * (c - half) + 1)
    perm = jnp.where(r == src, 1.0, 0.0).astype(BF16)
    for j in range(w1_ref.shape[3] // V7X_MXU):
        blk = w1_ref[0, 0, :, j * V7X_MXU:(j + 1) * V7X_MXU].astype(BF16)
        res = jnp.dot(blk, perm, preferred_element_type=F32).astype(BF16)
        w1g_ref[:, j * half:(j + 1) * half] = res[:, :half]
        w1u_ref[:, j * half:(j + 1) * half] = res[:, half:]
    w2b_ref[...] = w2_ref[0, 0].astype(BF16)


def _ffn_kernel(te_ref, tf_ref, tr_ref, nv_ref, x_ref, w1_ref, b1g_ref, b1u_ref, w2_ref, b2_ref, o_ref,
                w1g_ref, w1u_ref, w2b_ref):
    del tf_ref
    i = pl.program_id(0)
    live = i < nv_ref[0]

    @pl.when(live & ((i == 0) | (te_ref[i] != te_ref[jnp.maximum(i - 1, 0)])))
    def _():
        _expert_weight_prep(w1_ref, w2_ref, w1g_ref, w1u_ref, w2b_ref)

    @pl.when(live)
    def _():
        row = lax.broadcasted_iota(I32, x_ref.shape, 0)
        x = jnp.where(row < tr_ref[i], x_ref[...], 0.0).astype(BF16)
        hg = jnp.dot(x, w1g_ref[...], preferred_element_type=F32) + b1g_ref[0]
        hu = jnp.dot(x, w1u_ref[...], preferred_element_type=F32) + b1u_ref[0]
        gate = jnp.minimum(hg, SWIGLU_LIMIT)
        up = jnp.clip(hu, -SWIGLU_LIMIT, SWIGLU_LIMIT)
        act = (up + 1.0) * gate * jax.nn.sigmoid(SWIGLU_ALPHA * gate)
        o_ref[...] = jnp.dot(act.astype(BF16), w2b_ref[...], preferred_element_type=F32) + b2_ref[0]


def _expert_ffn(x_rows, tile_expert, tile_fetch, tile_rows, n_valid, w1_all, b1g, b1u, w2_all, layer, b2):
    rows, d = x_rows.shape
    tm = EXPERT_TILE
    rmap = lambda i, te, tf, tr, nv: (jnp.minimum(i, nv[0] - 1), 0)
    wmap = lambda i, te, tf, tr, nv: (te[i], 0, 0)
    lmap = lambda i, te, tf, tr, nv: (layer, tf[i], 0, 0)
    return pl.pallas_call(
        _ffn_kernel,
        grid_spec=pltpu.PrefetchScalarGridSpec(
            num_scalar_prefetch=4,
            grid=(rows // tm,),
            in_specs=[pl.BlockSpec((tm, d), rmap),
                      pl.BlockSpec((1, 1, d, 2 * D_FF), lmap),
                      pl.BlockSpec((1, 1, D_FF), wmap), pl.BlockSpec((1, 1, D_FF), wmap),
                      pl.BlockSpec((1, 1, D_FF, d), lmap),
                      pl.BlockSpec((1, 1, d), wmap)],
            out_specs=pl.BlockSpec((tm, d), rmap),
            scratch_shapes=[pltpu.VMEM((d, D_FF), BF16), pltpu.VMEM((d, D_FF), BF16), pltpu.VMEM((D_FF, d), BF16)],
        ),
        out_shape=jax.ShapeDtypeStruct((rows, d), F32),
        compiler_params=_params("arbitrary"),
        name="moe_expert_ffn",
    )(tile_expert, tile_fetch, tile_rows, n_valid, x_rows, w1_all, b1g, b1u, w2_all, b2)


def _combine_kernel(pos_ref, y_ref, gate_ref, x_ref, g_ref, b_ref, o_ref, *rest):
    ob_ref = rest[0] if len(rest) == 3 else None
    buf_ref, sem = rest[-2:]
    i = pl.program_id(0)
    tm = x_ref.shape[0]
    base = i * tm * TOP_K

    def start(t, carry):
        for k in range(TOP_K):
            _row_copy(y_ref, pos_ref[base + t * TOP_K + k], buf_ref.at[k], t, sem).start(priority=k % 2)
        return carry

    def wait(t, carry):
        for k in range(TOP_K):
            _row_copy(y_ref, pos_ref[base + t * TOP_K + k], buf_ref.at[k], t, sem).wait()
        return carry

    lax.fori_loop(0, tm, start, 0, unroll=DMA_UNROLL)
    lax.fori_loop(0, tm, wait, 0, unroll=DMA_UNROLL)
    gates = gate_ref[...]
    acc = DN_ALPHA * x_ref[...]
    for k in range(TOP_K):
        acc = acc + gates[:, k:k + 1] * buf_ref[k]
    xc = acc - jnp.mean(acc, axis=-1, keepdims=True)
    var = jnp.mean(xc * xc, axis=-1, keepdims=True)
    out = xc * lax.rsqrt(var + LN_EPS) * g_ref[...] + b_ref[...]
    o_ref[...] = out
    if ob_ref is not None:
        ob_ref[...] = out.astype(BF16)


def _combine_ln(y_rows, pos_flat, gates, x1, g, b, emit_bf16):
    m, d = x1.shape
    tm = _row_tile(m, TOKEN_ROW_CAP)
    row = lambda i, pos: (i, 0)
    fixed = lambda i, pos: (0, 0)
    return pl.pallas_call(
        _combine_kernel,
        grid_spec=pltpu.PrefetchScalarGridSpec(
            num_scalar_prefetch=1,
            grid=(m // tm,),
            in_specs=[pl.BlockSpec(memory_space=pl.ANY), pl.BlockSpec((tm, V7X_LANES), row),
                      pl.BlockSpec((tm, d), row), pl.BlockSpec((1, d), fixed), pl.BlockSpec((1, d), fixed)],
            out_specs=[pl.BlockSpec((tm, d), row)] * (2 if emit_bf16 else 1),
            scratch_shapes=[pltpu.VMEM((TOP_K, tm, d), F32), pltpu.SemaphoreType.DMA(())],
        ),
        out_shape=[jax.ShapeDtypeStruct((m, d), F32)] + ([jax.ShapeDtypeStruct((m, d), BF16)] if emit_bf16 else []),
        compiler_params=_params("arbitrary"),
        name="moe_combine_ln",
    )(pos_flat, y_rows, gates, x1, g, b)


def _moe_ln(x1, eids, gates, ranks, counts, w1_all, b1g, b1u, w2_all, layer, b2, g, b, emit_bf16):
    m, d = x1.shape
    counts = counts[0, :N_EXPERTS]
    padded = (counts + EXPERT_TILE - 1) // EXPERT_TILE * EXPERT_TILE
    ends = jnp.cumsum(padded)
    gstart = ends - padded
    n_tiles = -(-m * TOP_K // EXPERT_TILE) + N_EXPERTS
    tile_start = jnp.arange(n_tiles, dtype=I32) * EXPERT_TILE
    tile_expert = jnp.minimum(jnp.sum((tile_start[:, None] >= ends[None, :]).astype(I32), axis=1), N_EXPERTS - 1)
    onehot_t = tile_expert[:, None] == jnp.arange(N_EXPERTS, dtype=I32)[None, :]
    used = jnp.sum(jnp.where(onehot_t, (gstart + counts)[None, :], 0), axis=1)
    tile_rows = jnp.clip(used - tile_start, 0, EXPERT_TILE).astype(I32)
    n_valid = (ends[-1:] // EXPERT_TILE).astype(I32)
    first = jnp.concatenate([jnp.ones((1,), bool), tile_expert[1:] != tile_expert[:-1]])
    group_end = jnp.sum(jnp.where(onehot_t, ends[None, :], 0), axis=1) // EXPERT_TILE
    next_expert = jnp.sum(jnp.where(jnp.minimum(group_end, n_tiles - 1)[:, None] == jnp.arange(n_tiles)[None, :],
                                    tile_expert[None, :], 0), axis=1)
    tile_fetch = jnp.where(first, tile_expert, next_expert).astype(I32)
    sel = eids[:, :TOP_K, None] == jnp.arange(N_EXPERTS, dtype=I32)[None, None, :]
    pos_flat = (jnp.sum(jnp.where(sel, gstart[None, None, :], 0), axis=-1) + ranks[:, :TOP_K]).reshape(-1).astype(I32)
    x_rows = _dispatch_rows(x1, pos_flat, n_tiles * EXPERT_TILE)
    y_rows = _expert_ffn(x_rows, tile_expert, tile_fetch, tile_rows, n_valid, w1_all, b1g, b1u, w2_all, layer, b2)
    return _combine_ln(y_rows, pos_flat, gates, x1, g, b, emit_bf16)


def _fox_prep_kernel(t_ref, bias_ref, lf_ref, ccol_ref):
    t_len = t_ref.shape[0]
    r, c = _iota2((V7X_LANES, V7X_LANES))
    tril = jnp.where(c <= r, 1.0, 0.0)
    carry = jnp.zeros((1, V7X_LANES), F32)
    for blk in range(t_len // V7X_LANES):
        rows = slice(blk * V7X_LANES, (blk + 1) * V7X_LANES)
        lf = _log_sigmoid(t_ref[rows, :] + bias_ref[...])
        lf_ref[rows, :] = lf
        cs = jnp.dot(tril, lf, precision=HIGHEST, preferred_element_type=F32) + carry
        ccol_ref[rows, :] = cs
        carry = cs[V7X_LANES - 1:, :]


def _fox_prep(tail, bias_row, bsz, t_len):
    n_p = bsz * t_len
    blk = pl.BlockSpec((t_len, V7X_LANES), lambda b: (b, 0))
    out = jax.ShapeDtypeStruct((n_p, V7X_LANES), F32)
    return pl.pallas_call(
        _fox_prep_kernel,
        grid=(bsz,),
        in_specs=[blk, pl.BlockSpec((1, V7X_LANES), lambda b: (0, 0))],
        out_specs=[blk, blk],
        out_shape=[out, out],
        compiler_params=_params("parallel"),
        name="fox_prep",
    )(tail, bias_row)


def _fox_attn_kernel(q_ref, k_ref, v_ref, ccol_ref, o_ref, kb_ref, vt_ref, cb0_ref, cb1_ref, m0_ref, m1_ref, l0_ref, l1_ref,
                     acc0_ref, acc1_ref, s0_ref, s1_ref):
    pair = pl.program_id(1)
    tq = ATTN_BLOCK
    t_len = k_ref.shape[0]
    cb_refs, m_refs, l_refs, acc_refs = (cb0_ref, cb1_ref), (m0_ref, m1_ref), (l0_ref, l1_ref), (acc0_ref, acc1_ref)
    s_refs = (s0_ref, s1_ref)

    kb_ref[...] = k_ref[...].astype(BF16)
    sr, sc = _iota2((V7X_LANES, V7X_LANES))
    for j in range(2):
        sel = jnp.where(sr == 2 * pair + j, 1.0, 0.0).astype(BF16)
        cb_refs[j][...] = _expand_heads(ccol_ref[...], sel)
    vt_ref[...] = v_ref[0].astype(BF16)
    feat = lax.broadcasted_iota(I32, (V7X_LANES, tq), 0)
    key_id, qry_id = _iota2((tq, tq))

    def scores(kb, qh_j, j):
        rows = slice(kb * tq, (kb + 1) * tq)
        return (jnp.dot(kb_ref[rows, :], qh_j, preferred_element_type=F32)
                - jnp.concatenate([cb_refs[j][rows, :]] * (tq // V7X_LANES), axis=1))

    def update(kb, j, s):
        m_old = m_refs[j][...]
        m_new = jnp.maximum(m_old, jnp.max(s, axis=0, keepdims=True))
        alpha = jnp.exp(m_old - m_new)
        pe = jnp.exp(s - m_new)
        l_refs[j][...] = alpha * l_refs[j][...] + jnp.sum(pe, axis=0, keepdims=True)
        acc_refs[j][...] = alpha * acc_refs[j][...] + jnp.dot(vt_ref[:, kb * tq:(kb + 1) * tq], pe.astype(BF16),
                                                              preferred_element_type=F32)
        m_refs[j][...] = m_new

    for qi in range(t_len // tq):
        q_rows = slice(qi * tq, (qi + 1) * tq)
        qt = (q_ref[q_rows, :] * (DH_B ** -0.5)).T
        qh = [jnp.where((feat // DH_B) == j, qt, 0.0).astype(BF16) for j in range(2)]
        for j in range(2):
            m_refs[j][...] = jnp.full(m_refs[j].shape, -jnp.inf, F32)
            l_refs[j][...] = jnp.zeros_like(l_refs[j])
            acc_refs[j][...] = jnp.zeros_like(acc_refs[j])
            s_refs[j][...] = scores(0, qh[j], j)
        for kb in range(qi):
            for j in range(2):
                s = s_refs[j][...]
                s_refs[j][...] = scores(kb + 1, qh[j], j)
                update(kb, j, s)
        for j in range(2):
            update(qi, j, jnp.where(key_id <= qry_id, s_refs[j][...], -jnp.inf))
        out_t = jnp.where((feat // DH_B) == 0, acc0_ref[...] / l0_ref[...], acc1_ref[...] / l1_ref[...])
        o_ref[q_rows, :] = out_t.T.astype(o_ref.dtype)


def _fox_attention(h_main, kv_t, ccol, bsz, t_len):
    tq = ATTN_BLOCK
    qcol, kcol = (4 * D_A) // V7X_LANES, (4 * D_A + D_B) // V7X_LANES
    cbs, row, acc = pltpu.VMEM((t_len, V7X_LANES), F32), pltpu.VMEM((1, tq), F32), pltpu.VMEM((V7X_LANES, tq), F32)
    seq = lambda col: pl.BlockSpec((t_len, V7X_LANES), lambda b, p: (b, col + p))
    return pl.pallas_call(
        _fox_attn_kernel,
        grid=(bsz, H_B // 2),
        in_specs=[seq(qcol), seq(kcol),
                  pl.BlockSpec((1, V7X_LANES, t_len), lambda b, p: (b, D_B // V7X_LANES + p, 0)),
                  pl.BlockSpec((t_len, V7X_LANES), lambda b, p: (b, 0))],
        out_specs=pl.BlockSpec((t_len, V7X_LANES), lambda b, p: (b, p)),
        out_shape=jax.ShapeDtypeStruct((h_main.shape[0], D_B), BF16),
        scratch_shapes=[pltpu.VMEM((t_len, V7X_LANES), BF16), pltpu.VMEM((V7X_LANES, t_len), BF16),
                        cbs, cbs, row, row, row, row, acc, acc, pltpu.VMEM((tq, tq), F32), pltpu.VMEM((tq, tq), F32)],
        compiler_params=_params("parallel", "parallel"),
        name="fox_attention",
    )(h_main, h_main, kv_t, ccol)


def _fox_decode_kernel(pt_ref, q_ref, kn_ref, vn_ref, t_ref, bias_ref, *rest):
    np_ = DECODE_PAGES
    k_refs, v_refs, lf_refs = rest[:np_], rest[np_:2 * np_], rest[2 * np_:3 * np_]
    all_ref, o_ref, lfo_ref, m_ref, l_ref, acc_ref, carry_ref, rows_ref = rest[3 * np_:]
    del all_ref
    b = pl.program_id(0)
    j = pl.program_id(1)
    q = q_ref[0] * (DH_B ** -0.5)
    hrow, hlane = _iota2((H_B, D_B))
    own = hlane // DH_B == hrow
    qmat = jnp.where(own, jnp.broadcast_to(q, (H_B, D_B)), 0.0)
    lf_new = _log_sigmoid(t_ref[0] + bias_ref[...])

    @pl.when(j == 0)
    def _():
        m_ref[...] = jnp.sum(qmat * kn_ref[0], axis=-1, keepdims=True)
        l_ref[...] = jnp.ones_like(l_ref)
        acc_ref[...] = jnp.broadcast_to(vn_ref[0], acc_ref.shape)
        carry_ref[...] = jnp.zeros_like(carry_ref)
        lfo_ref[0] = lf_new

    r8, c8 = _iota2((H_B, H_B))
    cn = jnp.sum(jnp.where(r8 == c8, jnp.broadcast_to(lf_new[:, :H_B], (H_B, H_B)), 0.0), axis=-1, keepdims=True)
    kr, kc = _iota2((PAGE_SIZE, PAGE_SIZE))
    later = jnp.where(kr > kc, 1.0, 0.0).astype(BF16)
    qb = qmat.astype(BF16)
    lfts = [lf_refs[i][0] for i in range(np_)]
    carries = [None] * np_
    run = carry_ref[...]
    for i in reversed(range(np_)):
        carries[i] = run
        run = run + jnp.sum(lfts[i], axis=-1, keepdims=True)
    carry_ref[...] = run
    scores = []
    for i in range(np_):
        lf = lfts[i]
        hi = lf.astype(BF16)
        r1 = lf - hi.astype(F32)
        mid = r1.astype(BF16)
        lo = (r1 - mid.astype(F32)).astype(BF16)
        parts = jnp.dot(jnp.concatenate([hi, mid, lo], axis=0), later, preferred_element_type=F32)
        suffix = parts[:H_B] + parts[H_B:2 * H_B] + parts[2 * H_B:]
        k2 = k_refs[i][0].reshape(D_B, PAGE_SIZE).astype(BF16)
        scores.append(jnp.dot(qb, k2, preferred_element_type=F32) + (suffix + (cn + carries[i])))
    m_old = m_ref[...]
    m_new = m_old
    for s in scores:
        m_new = jnp.maximum(m_new, jnp.max(s, axis=-1, keepdims=True))
    alpha = jnp.exp(m_old - m_new)
    l_new = alpha * l_ref[...]
    acc = alpha * acc_ref[...]
    for i, s in enumerate(scores):
        pe = jnp.exp(s - m_new)
        l_new = l_new + jnp.sum(pe, axis=-1, keepdims=True)
        v2 = v_refs[i][0].reshape(D_B, PAGE_SIZE).astype(BF16)
        acc = acc + lax.dot_general(pe.astype(BF16), v2, NT_DIMS, preferred_element_type=F32)
    m_ref[...] = m_new
    l_ref[...] = l_new
    acc_ref[...] = acc

    @pl.when(j == pl.num_programs(1) - 1)
    def _():
        o = jnp.where(own, acc / l_new, 0.0)
        rows_ref[pl.ds(b, 1), :] = jnp.sum(o, axis=0, keepdims=True)

    @pl.when((j == pl.num_programs(1) - 1) & (b == pl.num_programs(0) - 1))
    def _():
        o_ref[...] = rows_ref[...].astype(o_ref.dtype)


def _fox_decode(hs_main, hs_tail, bias_row, cache_kt, cache_vt, logf_t, page_table, ob_all):
    bd = hs_main.shape[0]
    n_p = ob_all.shape[0] - bd
    n_pages = page_table.shape[1]
    steps = n_pages // DECODE_PAGES
    qcol, kcol, vcol = (4 * D_A) // D_B, (4 * D_A + D_B) // D_B, (4 * D_A + 2 * D_B) // D_B

    def page(i, nd):
        return lambda b, j, pt: (pt[b * n_pages + (steps - 1 - j) * DECODE_PAGES + i],) + (0,) * nd

    tok = lambda col: pl.BlockSpec((1, 1, D_B), lambda b, j, pt: (b, 0, col))
    in_specs = [tok(qcol), tok(kcol), tok(vcol),
                pl.BlockSpec((1, 1, V7X_LANES), lambda b, j, pt: (b, 0, 0)),
                pl.BlockSpec((1, V7X_LANES), lambda b, j, pt: (0, 0))]
    in_specs += [pl.BlockSpec((1, H_B, DH_B, PAGE_SIZE), page(i, 3)) for i in range(DECODE_PAGES)]
    in_specs += [pl.BlockSpec((1, H_B, DH_B, PAGE_SIZE), page(i, 3)) for i in range(DECODE_PAGES)]
    in_specs += [pl.BlockSpec((1, H_B, PAGE_SIZE), page(i, 2)) for i in range(DECODE_PAGES)]
    in_specs += [pl.BlockSpec(memory_space=pl.ANY)]
    n_in = len(in_specs) + 1
    return pl.pallas_call(
        _fox_decode_kernel,
        grid_spec=pltpu.PrefetchScalarGridSpec(
            num_scalar_prefetch=1,
            grid=(bd, steps),
            in_specs=in_specs,
            out_specs=[_token_rows_spec(n_p, bd, D_B),
                       pl.BlockSpec((1, 1, V7X_LANES), lambda b, j, pt: (b, 0, 0))],
            scratch_shapes=[pltpu.VMEM((H_B, 1), F32), pltpu.VMEM((H_B, 1), F32), pltpu.VMEM((H_B, D_B), F32),
                            pltpu.VMEM((H_B, 1), F32), pltpu.VMEM((bd, D_B), F32)],
        ),
        out_shape=[jax.ShapeDtypeStruct(ob_all.shape, BF16), jax.ShapeDtypeStruct((bd, 1, V7X_LANES), F32)],
        input_output_aliases={n_in - 1: 0},
        compiler_params=_params("arbitrary", "arbitrary"),
        name="fox_decode",
    )(page_table.reshape(-1), hs_main, hs_main, hs_main, hs_tail, bias_row,
      *([cache_kt] * DECODE_PAGES), *([cache_vt] * DECODE_PAGES), *([logf_t] * DECODE_PAGES), ob_all)


def _hgrn_gates(q, z, lb):
    qa = _silu(q)
    logf = jnp.log(lb + (1.0 - lb) * jax.nn.sigmoid(z))
    ka = (1.0 - lb) * jax.nn.sigmoid(-z)
    return qa, ka, logf


def _hgrn_out(o, g, nw):
    o = o * lax.rsqrt(jnp.mean(o * o, axis=-1, keepdims=True) + RMS_EPS)
    return o * nw * _silu(g)


def _hgrn_kernel(q_ref, f_ref, i_ref, g_ref, lb_ref, nw_ref, o_ref, s_ref, qd_ref, dec_ref, oc_ref, u_ref):
    t_len = q_ref.shape[0]
    cs = CHUNK_A
    per_blk = V7X_LANES // cs
    n_blk = t_len // V7X_LANES
    lb = lb_ref[0]
    r, c = _iota2((V7X_LANES, V7X_LANES))
    same = (r // cs) == (c // cs)
    causal = same & (c <= r)
    sums = jnp.concatenate([jnp.where(causal, 1.0, 0.0), jnp.where(same, 1.0, 0.0)], axis=0).astype(BF16)

    def intra(blk, carry):
        start = pl.multiple_of(blk * V7X_LANES, V7X_LANES)
        rows = pl.ds(start, V7X_LANES)
        qa, ka, logf = _hgrn_gates(q_ref[rows, :], f_ref[rows, :], lb)
        both = _select_sum(sums, logf)
        b, gtot = both[:V7X_LANES], both[V7X_LANES:]
        qd = (qa * jnp.exp(b)).astype(BF16)
        kd = (ka * jnp.exp(-b)).astype(BF16)
        ke = (ka * jnp.exp(gtot - b)).astype(BF16)
        v = i_ref[rows, :].astype(BF16)
        qd_ref[rows, :] = qd
        sc = jnp.where(causal, lax.dot_general(qd, kd, NT_DIMS, preferred_element_type=F32), 0.0)
        oc_ref[rows, :] = jnp.dot(sc.astype(BF16), v, preferred_element_type=F32)
        dec = jnp.exp(gtot)
        for k in range(per_blk):
            sub = slice(k * cs, (k + 1) * cs)
            ci = blk * per_blk + k
            u_ref[ci] = lax.dot_general(v[sub], ke[sub], TN_DIMS, preferred_element_type=F32)
            dec_ref[pl.ds(ci, 1), :] = dec[k * cs:k * cs + 1, :]
        return carry

    lax.fori_loop(0, n_blk, intra, 0, unroll=4)

    def scan(ci, st):
        u = u_ref[ci]
        u_ref[ci] = st
        return st * dec_ref[pl.ds(ci, 1), :] + u

    st_last = lax.fori_loop(0, t_len // cs, scan, jnp.zeros((DV_A, DK_A), F32), unroll=4)
    s_ref[0, 0] = st_last.T

    def inter(blk, carry):
        start = pl.multiple_of(blk * V7X_LANES, V7X_LANES)
        rows = pl.ds(start, V7X_LANES)
        qd = qd_ref[rows, :]
        parts = [lax.dot_general(qd[k * cs:(k + 1) * cs], u_ref[blk * per_blk + k].astype(BF16), NT_DIMS,
                                 preferred_element_type=F32) for k in range(per_blk)]
        o = oc_ref[rows, :] + jnp.concatenate(parts, axis=0)
        o_ref[rows, :] = _hgrn_out(o, g_ref[rows, :], nw_ref[...]).astype(o_ref.dtype)
        return carry

    lax.fori_loop(0, n_blk, inter, 0, unroll=4)


def _hgrn_prompt(h_main, lb3, nw_row, bsz, t_len):
    n_p = bsz * t_len
    n_chunks = t_len // CHUNK_A
    col = lambda grp: (lambda b, h: (b, grp * H_A + h))
    return pl.pallas_call(
        _hgrn_kernel,
        grid=(bsz, H_A),
        in_specs=[pl.BlockSpec((t_len, DK_A), col(0)), pl.BlockSpec((t_len, DK_A), col(1)),
                  pl.BlockSpec((t_len, DV_A), col(2)), pl.BlockSpec((t_len, DV_A), col(3)),
                  pl.BlockSpec((1, 1, DK_A), lambda b, h: (h, 0, 0)), pl.BlockSpec((1, DV_A), lambda b, h: (0, 0))],
        out_specs=[pl.BlockSpec((t_len, DV_A), lambda b, h: (b, h)),
                   pl.BlockSpec((1, 1, DK_A, DV_A), lambda b, h: (b, h, 0, 0))],
        out_shape=[jax.ShapeDtypeStruct((h_main.shape[0], D_A), BF16),
                   jax.ShapeDtypeStruct((bsz, H_A, DK_A, DV_A), F32)],
        scratch_shapes=[pltpu.VMEM((t_len, DK_A), BF16), pltpu.VMEM((n_chunks, DK_A), F32),
                        pltpu.VMEM((t_len, DV_A), F32), pltpu.VMEM((n_chunks, DV_A, DK_A), F32)],
        compiler_params=_params("parallel", "parallel"),
        name="hgrn_prompt",
    )(h_main, h_main, h_main, h_main, lb3, nw_row)


def _hgrn_step_kernel(h_ref, lb_ref, nw_ref, s0_ref, all_ref, o_ref, s_ref, rows_ref):
    del all_ref
    b = pl.program_id(0)
    outs = []
    for h in range(H_A):
        grp = lambda g: h_ref[0, :, g * D_A + h * DK_A: g * D_A + (h + 1) * DK_A]
        qa, ka, logf = _hgrn_gates(grp(0), grp(1), lb_ref[h])
        v = grp(2)
        s_new = _to_column(jnp.exp(logf)) * s0_ref[0, h] + _to_column(ka) * v
        s_ref[0, h] = s_new
        o = jnp.sum(_to_column(qa) * s_new, axis=0, keepdims=True)
        outs.append(_hgrn_out(o, grp(3), nw_ref[...]))
    rows_ref[pl.ds(b, 1), :] = jnp.concatenate(outs, axis=-1)

    @pl.when(b == pl.num_programs(0) - 1)
    def _():
        o_ref[...] = rows_ref[...].astype(o_ref.dtype)


def _hgrn_step(hs_main, lb3, nw_row, s0, oa_all):
    bd = hs_main.shape[0]
    n_p = oa_all.shape[0] - bd
    return pl.pallas_call(
        _hgrn_step_kernel,
        grid=(bd,),
        in_specs=[pl.BlockSpec((1, 1, 4 * D_A), lambda b: (b, 0, 0)), pl.BlockSpec((H_A, 1, DK_A), lambda b: (0, 0, 0)),
                  pl.BlockSpec((1, DV_A), lambda b: (0, 0)), pl.BlockSpec((1, H_A, DK_A, DV_A), lambda b: (b, 0, 0, 0)),
                  pl.BlockSpec(memory_space=pl.ANY)],
        out_specs=[_token_rows_spec(n_p, bd, D_A),
                   pl.BlockSpec((1, H_A, DK_A, DV_A), lambda b: (b, 0, 0, 0))],
        out_shape=[jax.ShapeDtypeStruct(oa_all.shape, BF16), jax.ShapeDtypeStruct(s0.shape, F32)],
        scratch_shapes=[pltpu.VMEM((bd, D_A), F32)],
        input_output_aliases={4: 0},
        compiler_params=_params("arbitrary"),
        name="hgrn_step",
    )(hs_main, lb3, nw_row, s0, oa_all)


def _conv_silu(cur, prev, w, b):
    row8 = lax.broadcasted_iota(I32, prev.shape, 0)
    acc = b + cur * w[CONV_W - 1:CONV_W, :]
    for s in range(1, CONV_W):
        sh = pltpu.roll(cur, s, 0)
        head = jnp.where(row8 < s, pltpu.roll(prev, s, 0), sh[:V7X_SUBLANES, :])
        shifted = jnp.concatenate([head, sh[V7X_SUBLANES:, :]], axis=0)
        acc = acc + shifted * w[CONV_W - 1 - s:CONV_W - s, :]
    return _silu(acc)


def _gated_group_norm(y, z, nw):
    y = y * _silu(z)
    parts = []
    for g in range(N_GROUPS_C):
        seg = y[:, g * GROUP_W:(g + 1) * GROUP_W]
        parts.append(seg * lax.rsqrt(jnp.mean(seg * seg, axis=-1, keepdims=True) + RMS_EPS))
    return jnp.concatenate(parts, axis=-1) * nw


def _ssd_kernel(z_ref, x_ref, bc_ref, dt_ref, cwx_ref, cbx_ref, cwbc_ref, cbbc_ref, dtb_ref, alog_ref, dsk_ref,
                nw_ref, y_ref, hs_ref, tailx_ref, tailbc_ref, ht_ref, e_ref, yacc_ref, xw_ref):
    tb = pl.program_id(1)
    tt = x_ref.shape[0]
    pair_w = 2 * HEADDIM_C
    heads_per_group = H_C // N_GROUPS_C

    @pl.when(tb == 0)
    def _():
        tailx_ref[...] = jnp.zeros_like(tailx_ref)
        tailbc_ref[...] = jnp.zeros_like(tailbc_ref)
        ht_ref[...] = jnp.zeros_like(ht_ref)
        er, ec = _iota2(e_ref.shape)
        e_ref[...] = jnp.where(ec // HEADDIM_C == er, 1.0, 0.0).astype(BF16)

    x_raw = x_ref[...]
    bc_raw = bc_ref[...]
    xs = _conv_silu(x_raw, tailx_ref[...], cwx_ref[...], cbx_ref[...])
    bcv = _conv_silu(bc_raw, tailbc_ref[...], cwbc_ref[...], cbbc_ref[...])
    tailx_ref[...] = x_raw[tt - V7X_SUBLANES:, :]
    tailbc_ref[...] = bc_raw[tt - V7X_SUBLANES:, :]

    lane = lax.broadcasted_iota(I32, (tt, V7X_LANES), 1)
    dt = jnp.where(lane < H_C, _softplus(dt_ref[...] + dtb_ref[...]), 0.0)
    a = -jnp.exp(alog_ref[...])
    r, c = _iota2((tt, tt))
    causal = c <= r
    cum = jnp.dot(jnp.where(causal, 1.0, 0.0), dt * a, precision=HIGHEST, preferred_element_type=F32)
    cum_t = cum.T
    xdt = xs * _expand_heads(dt, e_ref[...])
    low = (lane % pair_w) < HEADDIM_C

    for g in range(N_GROUPS_C):
        b_g = bcv[:, g * D_STATE_C:(g + 1) * D_STATE_C]
        c_g = bcv[:, D_BC + g * D_STATE_C:D_BC + (g + 1) * D_STATE_C].astype(BF16)
        cb = lax.dot_general(c_g, b_g.astype(BF16), NT_DIMS, preferred_element_type=F32)
        y_inter = jnp.dot(c_g, ht_ref[g].astype(BF16), preferred_element_type=F32)
        decs = []
        for pr in range(heads_per_group // 2):
            slab = slice(g * GROUP_W + pr * pair_w, g * GROUP_W + (pr + 1) * pair_w)
            xdt_slab = xdt[:, slab]
            ys, es, tes = [], [], []
            for j in range(2):
                head = g * heads_per_group + pr * 2 + j
                colb = jnp.broadcast_to(cum[:, head:head + 1], (tt, tt))
                decay = jnp.exp(jnp.where(causal, colb - cum_t[head:head + 1, :], -jnp.inf))
                ys.append(jnp.dot((cb * decay).astype(BF16), xdt_slab.astype(BF16), preferred_element_type=F32))
                es.append(jnp.exp(colb))
                tes.append(jnp.exp(colb[tt - 1:, :] - colb))
            e_pair = jnp.where(low, es[0], es[1])
            yacc_ref[:, slab] = (jnp.where(low, ys[0], ys[1]) + e_pair * y_inter[:, pr * pair_w:(pr + 1) * pair_w]
                                 + dsk_ref[:, slab] * xs[:, slab])
            xw_ref[:, pr * pair_w:(pr + 1) * pair_w] = xdt_slab * jnp.where(low, tes[0], tes[1])
            decs.append(e_pair[tt - 1:, :])
        dec_row = jnp.concatenate(decs, axis=-1)
        ht_ref[g] = ht_ref[g] * dec_row + jnp.dot(b_g.T.astype(BF16), xw_ref[...].astype(BF16),
                                                  preferred_element_type=F32)

    y_ref[...] = _gated_group_norm(yacc_ref[...], z_ref[...], nw_ref[...]).astype(y_ref.dtype)

    @pl.when(tb == pl.num_programs(1) - 1)
    def _():
        for g in range(N_GROUPS_C):
            for q in range(GROUP_W // V7X_LANES):
                rows = slice(g * GROUP_W + q * V7X_LANES, g * GROUP_W + (q + 1) * V7X_LANES)
                hs_ref[0, rows, :] = ht_ref[g][:, q * V7X_LANES:(q + 1) * V7X_LANES].T


def _ssd_prompt(h_main, h_tail, conv_w, conv_b, dtb_row, alog_row, dsk_row, nw_row, bsz, t_len):
    n_p = bsz * t_len
    tt = SSD_BLOCK
    nt = t_len // tt
    rowmap = lambda col: (lambda b, t: (b * nt + t, col))
    fixed = lambda b, t: (0, 0)
    cwx, cwbc = conv_w[:, :D_INNER], conv_w[:, D_INNER:]
    cbx, cbbc = conv_b[None, :D_INNER], conv_b[None, D_INNER:]
    return pl.pallas_call(
        _ssd_kernel,
        grid=(bsz, nt),
        in_specs=[pl.BlockSpec((tt, D_INNER), rowmap(0)), pl.BlockSpec((tt, D_INNER), rowmap(1)),
                  pl.BlockSpec((tt, 2 * D_BC), rowmap(2 * D_INNER // (2 * D_BC))),
                  pl.BlockSpec((tt, V7X_LANES), rowmap(0)),
                  pl.BlockSpec((CONV_W, D_INNER), fixed), pl.BlockSpec((1, D_INNER), fixed),
                  pl.BlockSpec((CONV_W, 2 * D_BC), fixed), pl.BlockSpec((1, 2 * D_BC), fixed),
                  pl.BlockSpec((1, V7X_LANES), fixed), pl.BlockSpec((1, V7X_LANES), fixed),
                  pl.BlockSpec((1, D_INNER), fixed), pl.BlockSpec((1, D_INNER), fixed)],
        out_specs=[pl.BlockSpec((tt, D_INNER), rowmap(0)),
                   pl.BlockSpec((1, D_INNER, D_STATE_C), lambda b, t: (b, 0, 0))],
        out_shape=[jax.ShapeDtypeStruct((h_main.shape[0], D_INNER), BF16),
                   jax.ShapeDtypeStruct((bsz, D_INNER, D_STATE_C), F32)],
        scratch_shapes=[pltpu.VMEM((V7X_SUBLANES, D_INNER), F32), pltpu.VMEM((V7X_SUBLANES, 2 * D_BC), F32),
                        pltpu.VMEM((N_GROUPS_C, D_STATE_C, GROUP_W), F32), pltpu.VMEM((V7X_LANES, D_INNER), BF16),
                        pltpu.VMEM((tt, D_INNER), F32), pltpu.VMEM((tt, GROUP_W), F32)],
        compiler_params=_params("parallel", "arbitrary"),
        name="ssd_prompt",
    )(h_main, h_main, h_main, h_tail, cwx, cbx, cwbc, cbbc, dtb_row, alog_row, dsk_row, nw_row)


def _ssd_step_kernel(h_ref, t_ref, cs_ref, cw_ref, cb_ref, dtb_ref, alog_ref, dsk_ref, nw_ref, h0_ref, all_ref,
                     y_ref, hn_ref, rows_ref):
    del all_ref
    b = pl.program_id(0)
    z = h_ref[0, :, :D_INNER]
    xbc_new = h_ref[0, :, D_INNER:]
    cw = cw_ref[...]
    conv = cb_ref[...] + xbc_new * cw[CONV_W - 1:CONV_W, :]
    for j in range(CONV_W - 1):
        conv = conv + cs_ref[0, j:j + 1, :] * cw[j:j + 1, :]
    xbc = _silu(conv)
    xs = xbc[:, :D_INNER]
    lane = lax.broadcasted_iota(I32, (1, V7X_LANES), 1)
    dt = jnp.where(lane < H_C, _softplus(t_ref[0] + dtb_ref[...]), 0.0)
    da = jnp.exp(dt * -jnp.exp(alog_ref[...]))
    er, ec = _iota2((V7X_LANES, D_INNER))
    expand = jnp.where(ec // HEADDIM_C == er, 1.0, 0.0).astype(BF16)
    rows8 = lambda v: jnp.broadcast_to(v, (V7X_SUBLANES, V7X_LANES))
    dt_x = _expand_heads(rows8(dt), expand)[:1, :]
    da_x = _expand_heads(rows8(da), expand)[:1, :]
    xdt = xs * dt_x
    y_parts = []
    for q in range(D_INNER // V7X_LANES):
        g = q // (GROUP_W // V7X_LANES)
        lanes = slice(q * V7X_LANES, (q + 1) * V7X_LANES)
        b_g = xbc[:, D_INNER + g * D_STATE_C:D_INNER + (g + 1) * D_STATE_C]
        c_g = xbc[:, D_INNER + D_BC + g * D_STATE_C:D_INNER + D_BC + (g + 1) * D_STATE_C]
        h_new = _to_column(da_x[:, lanes]) * h0_ref[0, lanes, :] + _to_column(xdt[:, lanes]) * b_g
        hn_ref[0, lanes, :] = h_new
        y_parts.append(_to_row(jnp.sum(h_new * c_g, axis=-1, keepdims=True)))
    y = jnp.concatenate(y_parts, axis=-1) + dsk_ref[...] * xs
    rows_ref[pl.ds(b, 1), :] = _gated_group_norm(y, z, nw_ref[...])

    @pl.when(b == pl.num_programs(0) - 1)
    def _():
        y_ref[...] = rows_ref[...].astype(y_ref.dtype)


def _ssd_step(hs_main, hs_tail, conv_state, conv_w, conv_b, dtb_row, alog_row, dsk_row, nw_row, h0, y_all):
    bd = hs_main.shape[0]
    n_p = y_all.shape[0] - bd
    fixed = lambda b: (0, 0)
    tok = lambda b: (b, 0, 0)
    return pl.pallas_call(
        _ssd_step_kernel,
        grid=(bd,),
        in_specs=[pl.BlockSpec((1, 1, ODD_MAIN), tok), pl.BlockSpec((1, 1, V7X_LANES), tok),
                  pl.BlockSpec((1, CONV_W - 1, CONV_DIM), tok),
                  pl.BlockSpec((CONV_W, CONV_DIM), fixed), pl.BlockSpec((1, CONV_DIM), fixed),
                  pl.BlockSpec((1, V7X_LANES), fixed), pl.BlockSpec((1, V7X_LANES), fixed),
                  pl.BlockSpec((1, D_INNER), fixed), pl.BlockSpec((1, D_INNER), fixed),
                  pl.BlockSpec((1, D_INNER, D_STATE_C), tok), pl.BlockSpec(memory_space=pl.ANY)],
        out_specs=[_token_rows_spec(n_p, bd, D_INNER), pl.BlockSpec((1, D_INNER, D_STATE_C), tok)],
        out_shape=[jax.ShapeDtypeStruct(y_all.shape, BF16), jax.ShapeDtypeStruct((bd, D_INNER, D_STATE_C), F32)],
        scratch_shapes=[pltpu.VMEM((bd, D_INNER), F32)],
        input_output_aliases={10: 0},
        compiler_params=_params("arbitrary"),
        name="ssd_step",
    )(hs_main, hs_tail, conv_state, conv_w, conv_b[None], dtb_row, alog_row, dsk_row, nw_row, h0, y_all)


def _pad_lanes(v):
    return jnp.pad(v, (0, V7X_LANES - v.shape[0]))[None]


def kernel(x_prompt, x_sample, cache_k, cache_v, cache_logf, page_table, state_hgrn, state_ssm, state_conv,
           w_in_even, hgrn_lower_bound, hgrn_norm_w, fox_f_bias, w_out_even,
           w_in_odd, conv_w, conv_b, dt_bias, a_log, d_skip, ssm_norm_w, w_out_odd,
           ln1_g, ln1_b, ln2_g, ln2_b, router_w, router_b, exp_w1, exp_b1, exp_w2, exp_b2):
    lb_all = jnp.cumsum(jax.nn.softmax(hgrn_lower_bound, axis=0), axis=0)
    bp, t_p, d = x_prompt.shape
    bd, t_d, _ = x_sample.shape
    assert t_d == 1
    n_p = bp * t_p
    x_all = jnp.concatenate([x_prompt.reshape(n_p, d), x_sample.reshape(bd, d)], axis=0)
    x_bf = x_all.astype(BF16)
    outs = {}
    for l in range(DEPTH):
        i = l // 2
        if l % 2 == 0:
            h_main = _matmul(x_bf, w_in_even[i][:, :EVEN_MAIN].astype(BF16))
            h_tail = _matmul(x_bf, w_in_even[i][:, EVEN_MAIN:].astype(BF16))
            hs_main, hs_tail = h_main[n_p:, None, :], h_tail[n_p:, None, :]
            lb3 = lb_all[i].reshape(H_A, 1, DK_A)
            nw_row = hgrn_norm_w[i][None]
            bias_row = _pad_lanes(fox_f_bias[i])
            lf_p, ccol = _fox_prep(h_tail, bias_row, bp, t_p)
            kcol, vcol = 4 * D_A + D_B, 4 * D_A + 2 * D_B
            kv_t = _proj_t(x_bf, w_in_even[i][:, kcol:kcol + 2 * D_B].T.astype(BF16), bp, t_p)
            ob_p = _fox_attention(h_main, kv_t, ccol, bp, t_p)
            oa_p, hg_p = _hgrn_prompt(h_main, lb3, nw_row, bp, t_p)
            oa_all, hg_s = _hgrn_step(hs_main, lb3, nw_row, state_hgrn[i], oa_p)
            ob_all, lf_s = _fox_decode(hs_main, hs_tail, bias_row, jnp.transpose(cache_k[i], (0, 2, 3, 1)),
                                       jnp.transpose(cache_v[i], (0, 2, 3, 1)), jnp.swapaxes(cache_logf[i], 1, 2),
                                       page_table, ob_p)
            w_out = w_out_even[i].astype(BF16)
            parts = [(oa_all, w_out[:D_A]), (ob_all, w_out[D_A:])]
            heads_last = lambda a: jnp.transpose(a.reshape(bp, H_B, DH_B, t_p), (0, 3, 1, 2))
            vals = (("kp", heads_last(kv_t[:, :D_B])),
                    ("ks", h_main[n_p:, kcol:kcol + D_B].reshape(bd, t_d, H_B, DH_B)),
                    ("vp", heads_last(kv_t[:, D_B:])),
                    ("vs", h_main[n_p:, vcol:vcol + D_B].reshape(bd, t_d, H_B, DH_B)),
                    ("lfp", lf_p[:, :H_B].reshape(bp, t_p, H_B)), ("lfs", lf_s[:, :, :H_B]),
                    ("hgp", hg_p), ("hgs", hg_s))
        else:
            h_main = _matmul(x_bf, w_in_odd[i][:, :ODD_MAIN].astype(BF16))
            h_tail = _matmul(x_bf, w_in_odd[i][:, ODD_MAIN:].astype(BF16))
            hs_main, hs_tail = h_main[n_p:, None, :], h_tail[n_p:, None, :]
            dtb_row, alog_row = _pad_lanes(dt_bias[i]), _pad_lanes(a_log[i])
            dsk_row = jnp.repeat(d_skip[i], HEADDIM_C)[None]
            nw_row = ssm_norm_w[i][None]
            y_p, ss_p = _ssd_prompt(h_main, h_tail, conv_w[i], conv_b[i], dtb_row, alog_row, dsk_row, nw_row, bp, t_p)
            y_all, ss_s = _ssd_step(hs_main, hs_tail, state_conv[i], conv_w[i], conv_b[i], dtb_row, alog_row, dsk_row,
                                    nw_row, state_ssm[i].reshape(bd, D_INNER, D_STATE_C), y_p)
            parts = [(y_all, w_out_odd[i].astype(BF16))]
            tail_rows = jnp.stack([h_main[(b + 1) * t_p - (CONV_W - 1):(b + 1) * t_p, D_INNER:] for b in range(bp)])
            vals = (("ssp", ss_p.reshape(bp, H_C, HEADDIM_C, D_STATE_C)),
                    ("sss", ss_s.reshape(bd, H_C, HEADDIM_C, D_STATE_C)),
                    ("cvp", tail_rows),
                    ("cvs", jnp.concatenate([state_conv[i][:, 1:], hs_main[:, :, D_INNER:]], axis=1)))
        for name, val in vals:
            outs.setdefault(name, []).append(val)
        rw = jnp.pad(router_w[l], ((0, 0), (0, V7X_LANES - N_EXPERTS)))
        rb = jnp.concatenate([router_b[l], jnp.full((V7X_LANES - N_EXPERTS,), -jnp.inf, F32)])[None]
        x1, eids, gates, ranks, counts = _post_mixer(x_all, parts, ln1_g[l][None], ln1_b[l][None], rw, rb)
        b1g = exp_b1[l][:, None, 0::2]
        b1u = exp_b1[l][:, None, 1::2]
        more = l + 1 < DEPTH
        res = _moe_ln(x1, eids, gates, ranks, counts, exp_w1, b1g, b1u, exp_w2, l, exp_b2[l][:, None, :],
                      ln2_g[l][None], ln2_b[l][None], more)
        x_all = res[0]
        x_bf = res[1] if more else None
    st = {k: jnp.stack(v) for k, v in outs.items()}
    return (x_all[:n_p].reshape(bp, t_p, d), x_all[n_p:].reshape(bd, t_d, d),
            st["kp"], st["ks"], st["vp"], st["vs"], st["lfp"], st["lfs"],
            st["hgp"], st["hgs"], st["ssp"], st["sss"], st["cvp"], st["cvs"])
```
